```python
import jax, jax.numpy as jnp
from jax import lax
import numpy as np

D_MODEL = 1024
BATCH = 8
SEQ = 2048
DEPTH = 1

D_MIX = D_MODEL
D_POOL = D_MIX // 2
POOL_WINDOWS = (2, 4, 8, 16)
N_POOL_GROUPS = len(POOL_WINDOWS)
POOL_GROUP = D_POOL // N_POOL_GROUPS
D_ATTN = D_MIX - D_POOL
HEAD_DIM = 64
N_HEADS = D_ATTN // HEAD_DIM
DILATED_PATTERNS = ((128, 1), (512, 4), (2048, 16))
Q_BLOCK = 128
N_EXPERTS = 32
TOP_K = 4
D_EXPERT = D_MODEL
SWIGLU_LIMIT = 7.0
SWIGLU_ALPHA = 1.702
EPS = 1e-5
NEG_INF = -1e30

kernel_name = 'hybrid_pool_dilated_attn_moe'


def rmsnorm(x, g):
    xf = x.astype(jnp.float32)
    y = xf * lax.rsqrt(jnp.mean(xf * xf, axis=-1, keepdims=True) + EPS)
    return (y * g.astype(jnp.float32)).astype(x.dtype)


def causal_multiscale_pool(u, w_pool, pool_scale):
    B, S, _ = u.shape
    uf = u.astype(jnp.float32)
    cs = jnp.concatenate([jnp.zeros((B, 1, D_POOL), jnp.float32), jnp.cumsum(uf, axis=1)], axis=1)
    t = jnp.arange(S)
    groups = []
    for gi, w in enumerate(POOL_WINDOWS):
        lo, hi = gi * POOL_GROUP, (gi + 1) * POOL_GROUP
        start = jnp.maximum(t + 1 - w, 0)
        csg = cs[:, :, lo:hi]
        win_sum = csg[:, 1:] - csg[:, start]
        count = (t + 1 - start).astype(jnp.float32)[None, :, None]
        groups.append(win_sum / count - uf[:, :, lo:hi])
    pooled = jnp.stack(groups, axis=2)
    y = jnp.einsum('bsgc,gcd->bsgd', pooled, w_pool.astype(jnp.float32)).reshape(B, S, D_POOL)
    return (y * pool_scale.astype(jnp.float32)).astype(u.dtype)


def dilated_branch(q, k, v, window, dilation):
    B, H, S, hd = q.shape
    n_back = window // dilation
    Lc = S // dilation
    nb = -(-Lc // Q_BLOCK)
    Lp = nb * Q_BLOCK

    def compress(a):
        a = a.reshape(B, H, Lc, dilation, hd).transpose(0, 1, 3, 2, 4)
        return jnp.pad(a, ((0, 0), (0, 0), (0, 0), (0, Lp - Lc), (0, 0)))

    def slabs(a):
        a = jnp.pad(a, ((0, 0), (0, 0), (0, 0), (Q_BLOCK, 0), (0, 0)))
        a = a.reshape(B, H, dilation, nb + 1, Q_BLOCK, hd)
        return jnp.concatenate([a[:, :, :, :-1], a[:, :, :, 1:]], axis=4)

    qb = compress(q).reshape(B, H, dilation, nb, Q_BLOCK, hd)
    ks = slabs(compress(k))
    vs = slabs(compress(v))
    s = jnp.einsum('bhrnqe,bhrnke->bhrnqk', qb, ks, preferred_element_type=jnp.float32)
    qi = jnp.arange(Q_BLOCK)[:, None]
    kj = jnp.arange(2 * Q_BLOCK)[None, :]
    diff = qi + Q_BLOCK - kj
    band = (diff >= 0) & (diff <= n_back)
    key_pos = (jnp.arange(nb)[:, None, None] - 1) * Q_BLOCK + kj[None]
    mask = band[None] & (key_pos >= 0)
    s = jnp.where(mask, s, NEG_INF)
    m = jnp.max(s, axis=-1, keepdims=True)
    p = jnp.exp(s - m)
    l = jnp.sum(p, axis=-1)
    o = jnp.einsum('bhrnqk,bhrnke->bhrnqe', p, vs.astype(jnp.float32)) / l[..., None]

    def uncompress(a, tail):
        a = a.reshape((B, H, dilation, Lp) + tail)[:, :, :, :Lc]
        a = jnp.moveaxis(a, 2, 3)
        return a.reshape((B, H, S) + tail)

    return uncompress(o, (hd,)), uncompress(m[..., 0], ()), uncompress(l, ())


def dilated_attention(q, k, v):
    outs = [dilated_branch(q, k, v, w, d) for (w, d) in DILATED_PATTERNS]
    m_all = outs[0][1]
    for _, m_i, _ in outs[1:]:
        m_all = jnp.maximum(m_all, m_i)
    wts = [l_i * jnp.exp(m_i - m_all) for (_, m_i, l_i) in outs]
    total = wts[0] + wts[1] + wts[2]
    o = (wts[0][..., None] * outs[0][0] + wts[1][..., None] * outs[1][0]
         + wts[2][..., None] * outs[2][0]) / total[..., None]
    return o


def moe_ffn(h, w_router, b_router, w_gu, b_gu, w_down, b_down):
    B, S, D = h.shape
    t = h.reshape(B * S, D)
    logits = (jnp.dot(t, w_router, preferred_element_type=jnp.float32) + b_router.astype(jnp.float32))
    top_vals, top_idx = lax.top_k(logits, TOP_K)
    gates = jax.nn.softmax(top_vals, axis=-1)
    combine = jnp.sum(jax.nn.one_hot(top_idx, N_EXPERTS, dtype=jnp.float32) * gates[..., None], axis=1)
    out = jnp.zeros((B * S, D), jnp.float32)
    for e in range(N_EXPERTS):
        gu = jnp.dot(t, w_gu[e], preferred_element_type=jnp.float32) + b_gu[e].astype(jnp.float32)
        gate = jnp.minimum(gu[:, :D_EXPERT], SWIGLU_LIMIT)
        lin = jnp.clip(gu[:, D_EXPERT:], -SWIGLU_LIMIT, SWIGLU_LIMIT)
        act = gate * jax.nn.sigmoid(SWIGLU_ALPHA * gate) * (lin + 1.0)
        y = jnp.dot(act.astype(h.dtype), w_down[e], preferred_element_type=jnp.float32) + b_down[e].astype(jnp.float32)
        out = out + combine[:, e:e + 1] * y
    return out.reshape(B, S, D).astype(h.dtype)


def setup_inputs(seed: int = 0) -> dict:
    key = jax.random.key(seed)
    ks = jax.random.split(key, 14)
    f32 = jnp.float32
    d_in = D_POOL + 3 * D_ATTN
    nrm = lambda k, shape, scale: jax.random.normal(k, shape, f32) * scale
    return {
        'x': nrm(ks[0], (BATCH, SEQ, D_MODEL), 1.0),
        'g_mix': 1.0 + nrm(ks[1], (DEPTH, D_MODEL), 0.02),
        'w_in': nrm(ks[2], (DEPTH, D_MODEL, d_in), D_MODEL ** -0.5),
        'w_pool': nrm(ks[3], (DEPTH, N_POOL_GROUPS, POOL_GROUP, POOL_GROUP), POOL_GROUP ** -0.5),
        'pool_scale': 1.0 + nrm(ks[4], (DEPTH, D_POOL), 0.02),
        'w_out': nrm(ks[5], (DEPTH, D_MIX, D_MODEL), D_MIX ** -0.5),
        'g_ffn': 1.0 + nrm(ks[6], (DEPTH, D_MODEL), 0.02),
        'w_router': nrm(ks[7], (DEPTH, D_MODEL, N_EXPERTS), D_MODEL ** -0.5),
        'b_router': nrm(ks[8], (DEPTH, N_EXPERTS), 0.01),
        'w_gu': nrm(ks[9], (DEPTH, N_EXPERTS, D_MODEL, 2 * D_EXPERT), D_MODEL ** -0.5),
        'b_gu': nrm(ks[10], (DEPTH, N_EXPERTS, 2 * D_EXPERT), 0.01),
        'w_down': nrm(ks[11], (DEPTH, N_EXPERTS, D_EXPERT, D_MODEL), D_EXPERT ** -0.5),
        'b_down': nrm(ks[12], (DEPTH, N_EXPERTS, D_MODEL), 0.01),
        'g_final': 1.0 + nrm(ks[13], (D_MODEL,), 0.02),
    }


def reference(x, g_mix, w_in, w_pool, pool_scale, w_out, g_ffn, w_router, b_router,
              w_gu, b_gu, w_down, b_down, g_final):
    B, S, _ = x.shape

    def heads(a):
        return a.reshape(B, S, N_HEADS, HEAD_DIM).transpose(0, 2, 1, 3)

    for layer in range(DEPTH):
        h = rmsnorm(x, g_mix[layer])
        proj = jnp.dot(h, w_in[layer], preferred_element_type=jnp.float32).astype(x.dtype)
        u = proj[..., :D_POOL]
        q = heads(proj[..., D_POOL:D_POOL + D_ATTN]) * (HEAD_DIM ** -0.5)
        k = heads(proj[..., D_POOL + D_ATTN:D_POOL + 2 * D_ATTN])
        v = heads(proj[..., D_POOL + 2 * D_ATTN:])
        y_pool = causal_multiscale_pool(u, w_pool[layer], pool_scale[layer])
        y_attn = dilated_attention(q, k, v).transpose(0, 2, 1, 3).reshape(B, S, D_ATTN).astype(x.dtype)
        mixed = jnp.concatenate([y_pool, y_attn], axis=-1)
        x = x + jnp.dot(mixed, w_out[layer], preferred_element_type=jnp.float32).astype(x.dtype)
        h2 = rmsnorm(x, g_ffn[layer])
        x = x + moe_ffn(h2, w_router[layer], b_router[layer], w_gu[layer], b_gu[layer],
                        w_down[layer], b_down[layer])
    return rmsnorm(x, g_final)
```

```python
import functools

import jax
import jax.numpy as jnp
from jax import lax
from jax.experimental import pallas as pl
from jax.experimental.pallas import tpu as pltpu

F32 = jnp.float32
BF16 = jnp.bfloat16
I32 = jnp.int32

D_MODEL = 1024
D_POOL = 512
D_ATTN = 512
POOL_WINDOWS = (2, 4, 8, 16)
POOL_GROUP = 128
HEAD_DIM = 64
N_EXPERTS = 32
TOP_K = 4
SWIGLU_LIMIT = 7.0
SWIGLU_ALPHA = 1.702
EPS = 1e-5
NEG_INF = -1e30

LANES = 128
QB = 128
DIL_MID = 4
DIL_FAR = 16
ROW_TILE = 512
MOE_TILE = 256
VMEM_LIMIT = 56 * 1024 * 1024


def _rms(x, g):
    ms = jnp.mean(x * x, axis=-1, keepdims=True)
    return x * lax.rsqrt(ms + EPS) * g


def _inproj_kernel(x_ref, g_ref, w_ref, u_ref, qkv_ref, w_bf):
    @pl.when(pl.program_id(0) == 0)
    def _():
        w_bf[...] = w_ref[...].astype(BF16)

    h = _rms(x_ref[...], g_ref[...]).astype(BF16)
    proj = jnp.dot(h, w_bf[...], preferred_element_type=F32)
    u_ref[...] = proj[:, :D_POOL]
    q = proj[:, D_POOL:D_POOL + D_ATTN] * (HEAD_DIM ** -0.5)
    qkv_ref[:, :D_ATTN] = q.astype(BF16)
    qkv_ref[:, D_ATTN:] = proj[:, D_POOL + D_ATTN:].astype(BF16)


def _inproj(x2, g, w):
    t = x2.shape[0]
    d_in = w.shape[1]
    return pl.pallas_call(
        _inproj_kernel,
        grid=(t // ROW_TILE,),
        in_specs=[
            pl.BlockSpec((ROW_TILE, D_MODEL), lambda i: (i, 0)),
            pl.BlockSpec((1, D_MODEL), lambda i: (0, 0)),
            pl.BlockSpec((D_MODEL, d_in), lambda i: (0, 0)),
        ],
        out_specs=[
            pl.BlockSpec((ROW_TILE, D_POOL), lambda i: (i, 0)),
            pl.BlockSpec((ROW_TILE, 3 * D_ATTN), lambda i: (i, 0)),
        ],
        out_shape=[
            jax.ShapeDtypeStruct((t, D_POOL), F32),
            jax.ShapeDtypeStruct((t, 3 * D_ATTN), BF16),
        ],
        scratch_shapes=[pltpu.VMEM((D_MODEL, d_in), BF16)],
        compiler_params=pltpu.CompilerParams(
            dimension_semantics=("arbitrary",), vmem_limit_bytes=VMEM_LIMIT),
        name="inproj",
    )(x2, g, w)


POOL_PAD = 16


def _pool_kernel(u_ref, wp_ref, sc_ref, y_ref, pad_ref):
    s = u_ref.shape[0]
    row = lax.broadcasted_iota(I32, (s, 1), 0)
    pad_ref[0:POOL_PAD, :] = jnp.zeros((POOL_PAD, POOL_GROUP), F32)
    for g, w in enumerate(POOL_WINDOWS):
        lo, hi = g * POOL_GROUP, (g + 1) * POOL_GROUP
        e = u_ref[:, lo:hi]
        acc = e
        span = 1
        while span < w:
            pad_ref[POOL_PAD:, :] = acc
            acc = acc + pad_ref[pl.ds(POOL_PAD - span, s), :]
            span *= 2
        count = jnp.minimum(row + 1, w).astype(F32)
        pooled = acc / count - e
        y = jnp.dot(pooled.astype(BF16), wp_ref[g].astype(BF16), preferred_element_type=F32)
        y_ref[:, lo:hi] = (y * sc_ref[:, lo:hi]).astype(BF16)


def _pool(u, w_pool, pool_scale, batch, seq):
    return pl.pallas_call(
        _pool_kernel,
        grid=(batch,),
        in_specs=[
            pl.BlockSpec((seq, D_POOL), lambda b: (b, 0)),
            pl.BlockSpec((len(POOL_WINDOWS), POOL_GROUP, POOL_GROUP), lambda b: (0, 0, 0)),
            pl.BlockSpec((1, D_POOL), lambda b: (0, 0)),
        ],
        out_specs=pl.BlockSpec((seq, D_POOL), lambda b: (b, 0)),
        out_shape=jax.ShapeDtypeStruct((batch * seq, D_POOL), BF16),
        scratch_shapes=[pltpu.VMEM((seq + POOL_PAD, POOL_GROUP), F32)],
        compiler_params=pltpu.CompilerParams(
            dimension_semantics=("arbitrary",), vmem_limit_bytes=VMEM_LIMIT),
        name="pool",
    )(u, w_pool, pool_scale)


def _head_pair_block(q, k, v, mask, lane_h0):
    outs = []
    for h in range(2):
        sel = lane_h0 if h == 0 else jnp.logical_not(lane_h0)
        qh = jnp.where(sel, q, jnp.zeros_like(q))
        s = lax.dot_general(qh, k, (((1,), (1,)), ((), ())), preferred_element_type=F32)
        s = jnp.where(mask, s, NEG_INF)
        m = jnp.max(s, axis=-1, keepdims=True)
        p = jnp.exp(s - m)
        l = jnp.sum(p, axis=-1, keepdims=True)
        o = jnp.dot(p.astype(BF16), v, preferred_element_type=F32)
        outs.append((o, m, l))
    o = jnp.where(lane_h0, outs[0][0], outs[1][0])
    m = jnp.where(lane_h0, outs[0][1], outs[1][1])
    l = jnp.where(lane_h0, outs[0][2], outs[1][2])
    return o, m, l


def _merge(o_a, m_a, l_a, o_b, m_b, l_b):
    m = jnp.maximum(m_a, m_b)
    ea = jnp.exp(m_a - m)
    eb = jnp.exp(m_b - m)
    return o_a * ea + o_b * eb, m, l_a * ea + l_b * eb


def _attn_kernel(q_ref, k_ref, v_ref, o_ref, stage, qp, kp, vp, o23, m23, l23, on, mn, ln):
    s = q_ref.shape[0]
    n_chunk = DIL_FAR
    rows = s // n_chunk
    sub = rows // DIL_MID
    lane_h0 = lax.broadcasted_iota(I32, (1, LANES), 1) < HEAD_DIM

    for src, dst in ((q_ref, qp), (k_ref, kp), (v_ref, vp)):
        stage[...] = src[...].astype(F32)
        for r in range(n_chunk):
            dst[r * rows:(r + 1) * rows, :] = stage[pl.ds(r, rows, stride=n_chunk), :].astype(BF16)

    qi = lax.broadcasted_iota(I32, (QB, QB), 0)
    kj = lax.broadcasted_iota(I32, (QB, QB), 1)
    causal = qi >= kj

    def far_body(r, carry):
        r0 = pl.multiple_of(r * rows, rows)
        o, m, l = _head_pair_block(qp[pl.ds(r0, rows), :], kp[pl.ds(r0, rows), :],
                                   vp[pl.ds(r0, rows), :], causal, lane_h0)
        o23[pl.ds(r0, rows), :] = o
        m23[pl.ds(r0, rows), :] = m
        l23[pl.ds(r0, rows), :] = l
        return carry

    lax.fori_loop(0, n_chunk, far_body, 0)

    def mid_index(n_key_sub):
        i_q = lax.broadcasted_iota(I32, (QB, DIL_MID * n_key_sub), 0)
        i_k = lax.broadcasted_iota(I32, (QB, DIL_MID * n_key_sub), 1)
        jq, aq = i_q // sub, i_q % sub
        jk, ak = i_k // n_key_sub, i_k % n_key_sub
        return jq, aq, jk, ak

    jq, aq, jk, ak = mid_index(sub)
    d0 = DIL_FAR * (aq - ak) + DIL_MID * (jq - jk)
    mask_mid0 = d0 >= 0
    jq, aq, jk, ak = mid_index(2 * sub)
    d1 = DIL_FAR * (aq + sub - ak) + DIL_MID * (jq - jk)
    mask_mid = (d1 >= 0) & (d1 <= DIL_MID * QB)

    def mid_tile(r4, a0, k0, nk, mask):
        def at(j, off, n):
            start = (DIL_MID * j + r4) * rows + off
            return pl.ds(start if isinstance(start, int) else pl.multiple_of(start, sub), n)

        q = jnp.concatenate([qp[at(j, a0, sub), :] for j in range(DIL_MID)], axis=0)
        k = jnp.concatenate([kp[at(j, k0, nk), :] for j in range(DIL_MID)], axis=0)
        v = jnp.concatenate([vp[at(j, k0, nk), :] for j in range(DIL_MID)], axis=0)
        o, m, l = _head_pair_block(q, k, v, mask, lane_h0)
        for j in range(DIL_MID):
            dst = at(j, a0, sub)
            src = slice(j * sub, (j + 1) * sub)
            oo, mm, ll = _merge(o23[dst, :], m23[dst, :], l23[dst, :], o[src], m[src], l[src])
            o23[dst, :] = oo
            m23[dst, :] = mm
            l23[dst, :] = ll

    for r4 in range(DIL_MID):
        mid_tile(r4, 0, 0, sub, mask_mid0)

        def mid_body(a_blk, carry, r4=r4):
            a0 = pl.multiple_of(a_blk * sub, sub)
            mid_tile(r4, a0, a0 - sub, 2 * sub, mask_mid)
            return carry

        lax.fori_loop(1, rows // sub, mid_body, 0)

    for src, dst in ((o23, on), (m23, mn), (l23, ln)):
        for r in range(n_chunk):
            stage[pl.ds(r, rows, stride=n_chunk), :] = src[r * rows:(r + 1) * rows, :]
        dst[...] = stage[...]

    qi2 = lax.broadcasted_iota(I32, (QB, 2 * QB), 0)
    kj2 = lax.broadcasted_iota(I32, (QB, 2 * QB), 1)
    dn = qi2 + QB - kj2
    mask_near = (dn >= 0) & (dn <= QB)

    def near_finish(r0, o, m, l):
        dst = pl.ds(r0, QB)
        oo, _, ll = _merge(on[dst, :], mn[dst, :], ln[dst, :], o, m, l)
        o_ref[dst, :] = (oo / ll).astype(o_ref.dtype)

    o, m, l = _head_pair_block(q_ref[0:QB, :], k_ref[0:QB, :], v_ref[0:QB, :], causal, lane_h0)
    near_finish(0, o, m, l)

    def near_body(n, carry):
        r0 = pl.multiple_of(n * QB, QB)
        o, m, l = _head_pair_block(q_ref[pl.ds(r0, QB), :], k_ref[pl.ds(r0 - QB, 2 * QB), :],
                                   v_ref[pl.ds(r0 - QB, 2 * QB), :], mask_near, lane_h0)
        near_finish(r0, o, m, l)
        return carry

    lax.fori_loop(1, s // QB, near_body, 0)


def _attention(qkv, batch, seq):
    n_pair = D_ATTN // LANES
    blk = (seq, LANES)
    f32_scr = pltpu.VMEM(blk, F32)
    bf_scr = pltpu.VMEM(blk, BF16)
    return pl.pallas_call(
        _attn_kernel,
        grid=(batch, n_pair),
        in_specs=[
            pl.BlockSpec(blk, lambda b, h: (b, h)),
            pl.BlockSpec(blk, lambda b, h: (b, n_pair + h)),
            pl.BlockSpec(blk, lambda b, h: (b, 2 * n_pair + h)),
        ],
        out_specs=pl.BlockSpec(blk, lambda b, h: (b, h)),
        out_shape=jax.ShapeDtypeStruct((batch * seq, D_ATTN), BF16),
        scratch_shapes=[f32_scr, bf_scr, bf_scr, bf_scr,
                        f32_scr, f32_scr, f32_scr, f32_scr, f32_scr, f32_scr],
        compiler_params=pltpu.CompilerParams(
            dimension_semantics=("arbitrary", "arbitrary"), vmem_limit_bytes=VMEM_LIMIT),
        name="dilated_attn",
    )(qkv, qkv, qkv)


META_EIDX = 0
META_GATE = 4
META_RANK = 8


def _outproj_kernel(x_ref, yp_ref, ya_ref, wo_ref, g_ref, wr_ref, br_ref,
                    x1_ref, h2_ref, meta_ref, cnt_ref, wo_bf, carry):
    @pl.when(pl.program_id(0) == 0)
    def _():
        wo_bf[...] = wo_ref[...].astype(BF16)
        carry[...] = jnp.zeros_like(carry)

    tm = x_ref.shape[0]
    x1 = (x_ref[...]
          + jnp.dot(yp_ref[...], wo_bf[:D_POOL, :], preferred_element_type=F32)
          + jnp.dot(ya_ref[...], wo_bf[D_POOL:, :], preferred_element_type=F32))
    x1_ref[...] = x1
    h2 = _rms(x1, g_ref[...])
    h2_ref[...] = h2.astype(BF16)

    logits = jnp.dot(h2, wr_ref[...], preferred_element_type=F32,
                     precision=lax.Precision.HIGHEST) + br_ref[...]
    lane = lax.broadcasted_iota(I32, (tm, LANES), 1)
    work = logits
    idxs, vals = [], []
    for _ in range(TOP_K):
        mx = jnp.max(work, axis=-1, keepdims=True)
        idx = jnp.min(jnp.where(work == mx, lane, LANES), axis=-1, keepdims=True)
        idxs.append(idx)
        vals.append(mx)
        work = jnp.where(lane == idx, NEG_INF * 2.0, work)
    exps = [jnp.exp(v - vals[0]) for v in vals]
    den = exps[0] + exps[1] + exps[2] + exps[3]

    onehot = jnp.zeros((tm, LANES), F32)
    for idx in idxs:
        onehot = onehot + (lane == idx).astype(F32)
    ti = lax.broadcasted_iota(I32, (tm, tm), 0)
    tj = lax.broadcasted_iota(I32, (tm, tm), 1)
    before = (tj < ti).astype(BF16)
    rank_e = carry[...] + jnp.dot(before, onehot.astype(BF16), preferred_element_type=F32)
    carry[...] = carry[...] + jnp.sum(onehot, axis=0, keepdims=True)
    cnt_ref[...] = carry[...]

    meta = jnp.zeros((tm, LANES), F32)
    for k in range(TOP_K):
        rank_k = jnp.sum(jnp.where(lane == idxs[k], rank_e, 0.0), axis=-1, keepdims=True)
        meta = jnp.where(lane == META_EIDX + k, idxs[k].astype(F32), meta)
        meta = jnp.where(lane == META_GATE + k, exps[k] / den, meta)
        meta = jnp.where(lane == META_RANK + k, rank_k, meta)
    meta_ref[...] = meta


def _outproj(x2, y_pool, y_attn, w_out, g, w_router_pad, b_router_pad):
    t = x2.shape[0]
    row = lambda i: (i, 0)
    const = lambda i: (0, 0)
    return pl.pallas_call(
        _outproj_kernel,
        grid=(t // ROW_TILE,),
        in_specs=[
            pl.BlockSpec((ROW_TILE, D_MODEL), row),
            pl.BlockSpec((ROW_TILE, D_POOL), row),
            pl.BlockSpec((ROW_TILE, D_ATTN), row),
            pl.BlockSpec((D_MODEL, D_MODEL), const),
            pl.BlockSpec((1, D_MODEL), const),
            pl.BlockSpec((D_MODEL, LANES), const),
            pl.BlockSpec((1, LANES), const),
        ],
        out_specs=[
            pl.BlockSpec((ROW_TILE, D_MODEL), row),
            pl.BlockSpec((ROW_TILE, D_MODEL), row),
            pl.BlockSpec((ROW_TILE, LANES), row),
            pl.BlockSpec((1, LANES), const),
        ],
        out_shape=[
            jax.ShapeDtypeStruct((t, D_MODEL), F32),
            jax.ShapeDtypeStruct((t, D_MODEL), BF16),
            jax.ShapeDtypeStruct((t, LANES), F32),
            jax.ShapeDtypeStruct((1, LANES), F32),
        ],
        scratch_shapes=[pltpu.VMEM((D_MODEL, D_MODEL), BF16), pltpu.VMEM((1, LANES), F32)],
        compiler_params=pltpu.CompilerParams(
            dimension_semantics=("arbitrary",), vmem_limit_bytes=VMEM_LIMIT),
        name="outproj_router",
    )(x2, y_pool, y_attn, w_out, g, w_router_pad, b_router_pad)


def _moe_kernel(te_ref, nv_ref, xs_ref, wgu_ref, bgu_ref, wd_ref, bd_ref, y_ref, wgu_bf, wd_bf):
    i = pl.program_id(0)

    @pl.when(i < nv_ref[0])
    def _():
        prev = te_ref[jnp.maximum(i - 1, 0)]
        new_expert = jnp.logical_or(i == 0, te_ref[i] != prev)

        @pl.when(new_expert)
        def _():
            wgu_bf[...] = wgu_ref[0].astype(BF16)
            wd_bf[...] = wd_ref[0].astype(BF16)

        d_e = wd_bf.shape[0]
        gu = jnp.dot(xs_ref[...], wgu_bf[...], preferred_element_type=F32) + bgu_ref[0]
        gate = jnp.minimum(gu[:, :d_e], SWIGLU_LIMIT)
        lin = jnp.clip(gu[:, d_e:], -SWIGLU_LIMIT, SWIGLU_LIMIT)
        act = gate * jax.nn.sigmoid(SWIGLU_ALPHA * gate) * (lin + 1.0)
        y = jnp.dot(act.astype(BF16), wd_bf[...], preferred_element_type=F32) + bd_ref[0]
        y_ref[...] = y


def _moe(tile_expert, n_valid, xs, w_gu, b_gu, w_down, b_down):
    p_max = xs.shape[0]
    n_tiles = p_max // MOE_TILE
    d_e = w_down.shape[1]

    def row(i, te, nv):
        return (jnp.minimum(i, nv[0] - 1), 0)

    def expert(i, te, nv):
        return (te[jnp.minimum(i, nv[0] - 1)], 0, 0)

    grid_spec = pltpu.PrefetchScalarGridSpec(
        num_scalar_prefetch=2,
        grid=(n_tiles,),
        in_specs=[
            pl.BlockSpec((MOE_TILE, D_MODEL), row),
            pl.BlockSpec((1, D_MODEL, 2 * d_e), expert),
            pl.BlockSpec((1, 1, 2 * d_e), expert),
            pl.BlockSpec((1, d_e, D_MODEL), expert),
            pl.BlockSpec((1, 1, D_MODEL), expert),
        ],
        out_specs=pl.BlockSpec((MOE_TILE, D_MODEL), row),
        scratch_shapes=[pltpu.VMEM((D_MODEL, 2 * d_e), BF16), pltpu.VMEM((d_e, D_MODEL), BF16)],
    )
    return pl.pallas_call(
        _moe_kernel,
        grid_spec=grid_spec,
        out_shape=jax.ShapeDtypeStruct((p_max, D_MODEL), F32),
        compiler_params=pltpu.CompilerParams(
            dimension_semantics=("arbitrary",), vmem_limit_bytes=VMEM_LIMIT),
        name="moe_ffn",
    )(tile_expert, n_valid, xs, w_gu, b_gu, w_down, b_down)


def _final_kernel(x1_ref, yk_ref, meta_ref, g_ref, o_ref):
    tm = x1_ref.shape[0]
    lane = lax.broadcasted_iota(I32, (tm, LANES), 1)
    meta = meta_ref[...]
    acc = x1_ref[...]
    for k in range(TOP_K):
        gate_k = jnp.sum(jnp.where(lane == META_GATE + k, meta, 0.0), axis=-1, keepdims=True)
        acc = acc + gate_k * yk_ref[k]
    o_ref[...] = _rms(acc, g_ref[...])


def _final(x1, yk, meta, g):
    t = x1.shape[0]
    tm = 256
    row = lambda i: (i, 0)
    return pl.pallas_call(
        _final_kernel,
        grid=(t // tm,),
        in_specs=[
            pl.BlockSpec((tm, D_MODEL), row),
            pl.BlockSpec((TOP_K, tm, D_MODEL), lambda i: (0, i, 0)),
            pl.BlockSpec((tm, LANES), row),
            pl.BlockSpec((1, D_MODEL), lambda i: (0, 0)),
        ],
        out_specs=pl.BlockSpec((tm, D_MODEL), row),
        out_shape=jax.ShapeDtypeStruct((t, D_MODEL), F32),
        compiler_params=pltpu.CompilerParams(
            dimension_semantics=("arbitrary",), vmem_limit_bytes=VMEM_LIMIT),
        name="combine_final",
    )(x1, yk, meta, g)


def kernel(x, g_mix, w_in, w_pool, pool_scale, w_out, g_ffn, w_router, b_router,
           w_gu, b_gu, w_down, b_down, g_final):
    batch, seq, d = x.shape
    t = batch * seq
    x2 = x.reshape(t, d)

    u, qkv = _inproj(x2, g_mix[0].reshape(1, d), w_in[0])
    y_pool = _pool(u, w_pool[0], pool_scale[0].reshape(1, D_POOL), batch, seq)
    y_attn = _attention(qkv, batch, seq)

    wr_pad = jnp.zeros((d, LANES), F32).at[:, :N_EXPERTS].set(w_router[0])
    br_pad = jnp.full((1, LANES), NEG_INF, F32).at[0, :N_EXPERTS].set(b_router[0])
    x1, h2, meta, cnt = _outproj(x2, y_pool, y_attn, w_out[0], g_ffn[0].reshape(1, d),
                                 wr_pad, br_pad)

    eidx = meta[:, META_EIDX:META_EIDX + TOP_K].astype(I32)
    rank = meta[:, META_RANK:META_RANK + TOP_K].astype(I32)
    counts = cnt[0, :N_EXPERTS].astype(I32)
    padded = ((counts + MOE_TILE - 1) // MOE_TILE) * MOE_TILE
    ends = jnp.cumsum(padded)
    offsets = ends - padded
    onehot = eidx[..., None] == jnp.arange(N_EXPERTS, dtype=I32)
    pos = rank + jnp.sum(jnp.where(onehot, offsets, 0), axis=-1)
    p_max = t * TOP_K + N_EXPERTS * MOE_TILE
    n_tiles = p_max // MOE_TILE
    tile_start = jnp.arange(n_tiles, dtype=I32) * MOE_TILE
    tile_expert = jnp.minimum(
        jnp.sum((tile_start[:, None] >= ends[None, :]).astype(I32), axis=1), N_EXPERTS - 1)
    n_valid = (ends[-1] // MOE_TILE).reshape(1).astype(I32)

    xs = jnp.zeros((p_max, d), BF16).at[pos.reshape(-1)].set(jnp.repeat(h2, TOP_K, axis=0))
    y_sorted = _moe(tile_expert, n_valid, xs, w_gu[0], b_gu[0].reshape(N_EXPERTS, 1, -1),
                    w_down[0], b_down[0].reshape(N_EXPERTS, 1, -1))
    yk = y_sorted[pos.T]

    out = _final(x1, yk, meta, g_final.reshape(1, d))
    return out.reshape(batch, seq, d)
```

```python
import jax
import jax.numpy as jnp
from jax import lax
from jax.experimental import pallas as pl
from jax.experimental.pallas import tpu as pltpu

F32 = jnp.float32
BF16 = jnp.bfloat16
I32 = jnp.int32

D_MODEL = 1024
D_POOL = 512
D_ATTN = 512
POOL_WINDOWS = (2, 4, 8, 16)
POOL_GROUP = 128
HEAD_DIM = 64
N_EXPERTS = 32
TOP_K = 4
SWIGLU_LIMIT = 7.0
SWIGLU_ALPHA = 1.702
EPS = 1e-5
NEG_INF = -1e30

LANES = 128
QB = 128
DIL_MID = 4
DIL_FAR = 16
ROW_TILE = 512
MOE_TILE = 256
DISPATCH_TILE = 256
COMBINE_TILE = 128
VMEM_LIMIT = 56 * 1024 * 1024


def _rms(x, g):
    ms = jnp.mean(x * x, axis=-1, keepdims=True)
    return x * lax.rsqrt(ms + EPS) * g


def _inproj_kernel(x_ref, g_ref, w_ref, u_ref, qkv_ref, w_bf):
    @pl.when(pl.program_id(0) == 0)
    def _():
        w_bf[...] = w_ref[...].astype(BF16)

    h = _rms(x_ref[...], g_ref[...]).astype(BF16)
    proj = jnp.dot(h, w_bf[...], preferred_element_type=F32)
    u_ref[...] = proj[:, :D_POOL]
    q = proj[:, D_POOL:D_POOL + D_ATTN] * (HEAD_DIM ** -0.5)
    qkv_ref[:, :D_ATTN] = q.astype(BF16)
    qkv_ref[:, D_ATTN:] = proj[:, D_POOL + D_ATTN:].astype(BF16)


def _inproj(x2, g, w):
    t = x2.shape[0]
    d_in = w.shape[1]
    return pl.pallas_call(
        _inproj_kernel,
        grid=(t // ROW_TILE,),
        in_specs=[
            pl.BlockSpec((ROW_TILE, D_MODEL), lambda i: (i, 0)),
            pl.BlockSpec((1, D_MODEL), lambda i: (0, 0)),
            pl.BlockSpec((D_MODEL, d_in), lambda i: (0, 0)),
        ],
        out_specs=[
            pl.BlockSpec((ROW_TILE, D_POOL), lambda i: (i, 0)),
            pl.BlockSpec((ROW_TILE, 3 * D_ATTN), lambda i: (i, 0)),
        ],
        out_shape=[
            jax.ShapeDtypeStruct((t, D_POOL), F32),
            jax.ShapeDtypeStruct((t, 3 * D_ATTN), BF16),
        ],
        scratch_shapes=[pltpu.VMEM((D_MODEL, d_in), BF16)],
        compiler_params=pltpu.CompilerParams(
            dimension_semantics=("arbitrary",), vmem_limit_bytes=VMEM_LIMIT),
        name="inproj",
    )(x2, g, w)


POOL_PAD = 16


def _pool_kernel(u_ref, wp_ref, sc_ref, y_ref, pad_ref):
    s = u_ref.shape[0]
    row = lax.broadcasted_iota(I32, (s, 1), 0)
    pad_ref[0:POOL_PAD, :] = jnp.zeros((POOL_PAD, POOL_GROUP), F32)
    for g, w in enumerate(POOL_WINDOWS):
        lo, hi = g * POOL_GROUP, (g + 1) * POOL_GROUP
        e = u_ref[:, lo:hi]
        acc = e
        span = 1
        while span < w:
            pad_ref[POOL_PAD:, :] = acc
            acc = acc + pad_ref[pl.ds(POOL_PAD - span, s), :]
            span *= 2
        count = jnp.minimum(row + 1, w).astype(F32)
        pooled = acc / count - e
        y = jnp.dot(pooled.astype(BF16), wp_ref[g].astype(BF16), preferred_element_type=F32)
        y_ref[:, lo:hi] = (y * sc_ref[:, lo:hi]).astype(BF16)


def _pool(u, w_pool, pool_scale, batch, seq):
    return pl.pallas_call(
        _pool_kernel,
        grid=(batch,),
        in_specs=[
            pl.BlockSpec((seq, D_POOL), lambda b: (b, 0)),
            pl.BlockSpec((len(POOL_WINDOWS), POOL_GROUP, POOL_GROUP), lambda b: (0, 0, 0)),
            pl.BlockSpec((1, D_POOL), lambda b: (0, 0)),
        ],
        out_specs=pl.BlockSpec((seq, D_POOL), lambda b: (b, 0)),
        out_shape=jax.ShapeDtypeStruct((batch * seq, D_POOL), BF16),
        scratch_shapes=[pltpu.VMEM((seq + POOL_PAD, POOL_GROUP), F32)],
        compiler_params=pltpu.CompilerParams(
            dimension_semantics=("arbitrary",), vmem_limit_bytes=VMEM_LIMIT),
        name="pool",
    )(u, w_pool, pool_scale)


def _pair_block(q, k, v_ext, mask2, lane_h0):
    zero = jnp.zeros_like(q)
    q2 = jnp.concatenate([jnp.where(lane_h0, q, zero), jnp.where(lane_h0, zero, q)], axis=0)
    s = lax.dot_general(q2, k, (((1,), (1,)), ((), ())), preferred_element_type=F32)
    s = jnp.where(mask2, s, NEG_INF)
    m = jnp.max(s, axis=-1, keepdims=True)
    p = jnp.exp(s - m).astype(BF16)
    ol = jnp.dot(p, v_ext, preferred_element_type=F32)
    o = jnp.where(lane_h0, ol[:QB, :LANES], ol[QB:, :LANES])
    l = jnp.where(lane_h0, ol[:QB, LANES:], ol[QB:, LANES:])
    mb = jnp.where(lane_h0, m[:QB], m[QB:])
    return o, mb, l


def _merge(o_a, m_a, l_a, o_b, m_b, l_b):
    m = jnp.maximum(m_a, m_b)
    ea = jnp.exp(m_a - m)
    eb = jnp.exp(m_b - m)
    return o_a * ea + o_b * eb, m, l_a * ea + l_b * eb


def _both_heads(mask):
    return jnp.concatenate([mask, mask], axis=0)


def _attn_kernel(q_ref, k_ref, v_ref, o_ref, stage, qp, kp, vpx, vnx, o23, m23, l23, on, mn, ln):
    s = q_ref.shape[0]
    n_chunk = DIL_FAR
    rows = s // n_chunk
    sub = rows // DIL_MID
    lane_h0 = lax.broadcasted_iota(I32, (1, LANES), 1) < HEAD_DIM

    ones = jnp.ones((s, LANES), BF16)
    vnx[:, :LANES] = v_ref[...]
    vnx[:, LANES:] = ones
    vpx[:, LANES:] = ones
    for src, dst in ((q_ref, qp), (k_ref, kp), (v_ref, vpx)):
        stage[...] = src[...].astype(F32)
        for r in range(n_chunk):
            dst[r * rows:(r + 1) * rows, 0:LANES] = (
                stage[pl.ds(r, rows, stride=n_chunk), :].astype(BF16))

    qi = lax.broadcasted_iota(I32, (QB, QB), 0)
    kj = lax.broadcasted_iota(I32, (QB, QB), 1)
    causal = _both_heads(qi >= kj)

    far_unroll = 4

    def far_body(it, carry):
        for u in range(far_unroll):
            r0 = pl.multiple_of((it * far_unroll + u) * rows, rows)
            blk = pl.ds(r0, rows)
            o, m, l = _pair_block(qp[blk, :], kp[blk, :], vpx[blk, :], causal, lane_h0)
            o23[blk, :] = o
            m23[blk, :] = m
            l23[blk, :] = l
        return carry

    lax.fori_loop(0, n_chunk // far_unroll, far_body, 0)

    def mid_index(n_key_sub):
        i_q = lax.broadcasted_iota(I32, (QB, DIL_MID * n_key_sub), 0)
        i_k = lax.broadcasted_iota(I32, (QB, DIL_MID * n_key_sub), 1)
        return i_q // sub, i_q % sub, i_k // n_key_sub, i_k % n_key_sub

    jq, aq, jk, ak = mid_index(sub)
    d0 = DIL_FAR * (aq - ak) + DIL_MID * (jq - jk)
    mask_mid0 = _both_heads(d0 >= 0)
    jq, aq, jk, ak = mid_index(2 * sub)
    d1 = DIL_FAR * (aq + sub - ak) + DIL_MID * (jq - jk)
    mask_mid = _both_heads((d1 >= 0) & (d1 <= DIL_MID * QB))

    def mid_tile(r4, a0, k0, nk, mask):
        def at(j, off, n):
            start = (DIL_MID * j + r4) * rows + off
            return pl.ds(start if isinstance(start, int) else pl.multiple_of(start, sub), n)

        q = jnp.concatenate([qp[at(j, a0, sub), :] for j in range(DIL_MID)], axis=0)
        k = jnp.concatenate([kp[at(j, k0, nk), :] for j in range(DIL_MID)], axis=0)
        v = jnp.concatenate([vpx[at(j, k0, nk), :] for j in range(DIL_MID)], axis=0)
        o, m, l = _pair_block(q, k, v, mask, lane_h0)
        for j in range(DIL_MID):
            dst = at(j, a0, sub)
            src = slice(j * sub, (j + 1) * sub)
            oo, mm, ll = _merge(o23[dst, :], m23[dst, :], l23[dst, :], o[src], m[src], l[src])
            o23[dst, :] = oo
            m23[dst, :] = mm
            l23[dst, :] = ll

    for r4 in range(DIL_MID):
        mid_tile(r4, 0, 0, sub, mask_mid0)

    def mid_body(a_blk, carry):
        a0 = pl.multiple_of(a_blk * sub, sub)
        for r4 in range(DIL_MID):
            mid_tile(r4, a0, a0 - sub, 2 * sub, mask_mid)
        return carry

    lax.fori_loop(1, rows // sub, mid_body, 0)

    for src, dst in ((o23, on), (m23, mn), (l23, ln)):
        for r in range(n_chunk):
            dst[pl.ds(r, rows, stride=n_chunk), :] = src[r * rows:(r + 1) * rows, :]

    qi2 = lax.broadcasted_iota(I32, (QB, 2 * QB), 0)
    kj2 = lax.broadcasted_iota(I32, (QB, 2 * QB), 1)
    dn = qi2 + QB - kj2
    mask_near = _both_heads((dn >= 0) & (dn <= QB))

    def near_finish(dst, o, m, l):
        oo, _, ll = _merge(on[dst, :], mn[dst, :], ln[dst, :], o, m, l)
        o_ref[dst, :] = (oo / ll).astype(o_ref.dtype)

    first = pl.ds(0, QB)
    o, m, l = _pair_block(q_ref[first, :], k_ref[first, :], vnx[first, :], causal, lane_h0)
    near_finish(first, o, m, l)

    near_unroll = 3

    def near_body(it, carry):
        for u in range(near_unroll):
            r0 = pl.multiple_of((1 + it * near_unroll + u) * QB, QB)
            keys = pl.ds(r0 - QB, 2 * QB)
            o, m, l = _pair_block(q_ref[pl.ds(r0, QB), :], k_ref[keys, :], vnx[keys, :],
                                  mask_near, lane_h0)
            near_finish(pl.ds(r0, QB), o, m, l)
        return carry

    lax.fori_loop(0, (s // QB - 1) // near_unroll, near_body, 0)


def _attention(qkv, batch, seq):
    n_pair = D_ATTN // LANES
    blk = (seq, LANES)
    f32_scr = pltpu.VMEM(blk, F32)
    bf_scr = pltpu.VMEM(blk, BF16)
    bfx_scr = pltpu.VMEM((seq, 2 * LANES), BF16)
    return pl.pallas_call(
        _attn_kernel,
        grid=(batch, n_pair),
        in_specs=[
            pl.BlockSpec(blk, lambda b, h: (b, h)),
            pl.BlockSpec(blk, lambda b, h: (b, n_pair + h)),
            pl.BlockSpec(blk, lambda b, h: (b, 2 * n_pair + h)),
        ],
        out_specs=pl.BlockSpec(blk, lambda b, h: (b, h)),
        out_shape=jax.ShapeDtypeStruct((batch * seq, D_ATTN), BF16),
        scratch_shapes=[f32_scr, bf_scr, bf_scr, bfx_scr, bfx_scr,
                        f32_scr, f32_scr, f32_scr, f32_scr, f32_scr, f32_scr],
        compiler_params=pltpu.CompilerParams(
            dimension_semantics=("arbitrary", "arbitrary"), vmem_limit_bytes=VMEM_LIMIT),
        name="dilated_attn",
    )(qkv, qkv, qkv)


META_ROWS = 16
META_EIDX, META_GATE, META_RANK = 0, 4, 8


def _outproj_kernel(x_ref, yp_ref, ya_ref, wo_ref, g_ref, wrt_ref, brt_ref,
                    x1_ref, h2_ref, meta_ref, cnt_ref, wo_bf, before, carry):
    tm = x_ref.shape[0]

    @pl.when(pl.program_id(0) == 0)
    def _():
        wo_bf[...] = wo_ref[...].astype(BF16)
        carry[...] = jnp.zeros_like(carry)
        ti = lax.broadcasted_iota(I32, (tm, tm), 0)
        tj = lax.broadcasted_iota(I32, (tm, tm), 1)
        before[...] = (ti < tj).astype(BF16)

    x1 = (x_ref[...]
          + jnp.dot(yp_ref[...], wo_bf[:D_POOL, :], preferred_element_type=F32)
          + jnp.dot(ya_ref[...], wo_bf[D_POOL:, :], preferred_element_type=F32))
    x1_ref[...] = x1
    h2 = _rms(x1, g_ref[...])
    h2_ref[...] = h2

    logits_t = lax.dot_general(wrt_ref[...].astype(BF16), h2.astype(BF16),
                               (((1,), (1,)), ((), ())), preferred_element_type=F32)
    logits_t = logits_t + brt_ref[:, 0:1]
    eid = lax.broadcasted_iota(I32, (N_EXPERTS, tm), 0)
    work = logits_t
    idxs, vals = [], []
    for _ in range(TOP_K):
        mx = jnp.max(work, axis=0, keepdims=True)
        idx = jnp.min(jnp.where(work == mx, eid, N_EXPERTS), axis=0, keepdims=True)
        idxs.append(idx)
        vals.append(mx)
        work = jnp.where(eid == idx, -jnp.inf, work)
    exps = [jnp.exp(v - vals[0]) for v in vals]
    den = exps[0] + exps[1] + exps[2] + exps[3]

    onehot = jnp.zeros((N_EXPERTS, tm), F32)
    for idx in idxs:
        onehot = onehot + (eid == idx).astype(F32)
    rank_e = carry[:, 0:1] + jnp.dot(onehot.astype(BF16), before[...],
                                     preferred_element_type=F32)
    carry[...] = carry[...] + jnp.sum(onehot, axis=1, keepdims=True)
    cnt_ref[...] = carry[...]

    mrow = lax.broadcasted_iota(I32, (META_ROWS, tm), 0)
    meta = jnp.zeros((META_ROWS, tm), F32)
    for k in range(TOP_K):
        rank_k = jnp.sum(jnp.where(eid == idxs[k], rank_e, 0.0), axis=0, keepdims=True)
        meta = jnp.where(mrow == META_EIDX + k, idxs[k].astype(F32), meta)
        meta = jnp.where(mrow == META_GATE + k, exps[k] / den, meta)
        meta = jnp.where(mrow == META_RANK + k, rank_k, meta)
    meta_ref[...] = meta


def _outproj(x2, y_pool, y_attn, w_out, g, w_router_t, b_router_t):
    t = x2.shape[0]
    row = lambda i: (i, 0)
    const = lambda i: (0, 0)
    return pl.pallas_call(
        _outproj_kernel,
        grid=(t // ROW_TILE,),
        in_specs=[
            pl.BlockSpec((ROW_TILE, D_MODEL), row),
            pl.BlockSpec((ROW_TILE, D_POOL), row),
            pl.BlockSpec((ROW_TILE, D_ATTN), row),
            pl.BlockSpec((D_MODEL, D_MODEL), const),
            pl.BlockSpec((1, D_MODEL), const),
            pl.BlockSpec((N_EXPERTS, D_MODEL), const),
            pl.BlockSpec((N_EXPERTS, LANES), const),
        ],
        out_specs=[
            pl.BlockSpec((ROW_TILE, D_MODEL), row),
            pl.BlockSpec((ROW_TILE, D_MODEL), row),
            pl.BlockSpec((META_ROWS, ROW_TILE), lambda i: (0, i)),
            pl.BlockSpec((N_EXPERTS, LANES), const),
        ],
        out_shape=[
            jax.ShapeDtypeStruct((t, D_MODEL), F32),
            jax.ShapeDtypeStruct((t, D_MODEL), F32),
            jax.ShapeDtypeStruct((META_ROWS, t), F32),
            jax.ShapeDtypeStruct((N_EXPERTS, LANES), F32),
        ],
        scratch_shapes=[pltpu.VMEM((D_MODEL, D_MODEL), BF16),
                        pltpu.VMEM((ROW_TILE, ROW_TILE), BF16),
                        pltpu.VMEM((N_EXPERTS, LANES), F32)],
        compiler_params=pltpu.CompilerParams(
            dimension_semantics=("arbitrary",), vmem_limit_bytes=VMEM_LIMIT),
        name="outproj_router",
    )(x2, y_pool, y_attn, w_out, g, w_router_t, b_router_t)


def _row(ref, r):
    return ref.at[pl.ds(r, 1), :]


def _dispatch_kernel(pad_lo, pad_hi, nv_ref, pos_ref, h2_hbm, zero_hbm, xs_hbm, sem, pad_sem):
    i = pl.program_id(0)
    n = pl.num_programs(0)
    tq = DISPATCH_TILE
    base = i * tq

    def wait_rows(n_rows, s):
        def body(_, c):
            pltpu.make_async_copy(_row(h2_hbm, 0), _row(xs_hbm, 0), s).wait()
            return c
        lax.fori_loop(0, n_rows, body, 0)

    @pl.when(i == 0)
    def _():
        def tile_of(j):
            return xs_hbm.at[pl.ds(pl.multiple_of(j * MOE_TILE, MOE_TILE), MOE_TILE), :]

        def per_expert(e, total):
            def body(p, c):
                pltpu.make_async_copy(_row(zero_hbm, 0), _row(xs_hbm, p), pad_sem).start()
                return c
            lax.fori_loop(pad_lo[e], pad_hi[e], body, 0)
            return total + (pad_hi[e] - pad_lo[e])
        total = lax.fori_loop(0, N_EXPERTS, per_expert, jnp.int32(0))

        def tail(j, c):
            pltpu.make_async_copy(zero_hbm, tile_of(j), pad_sem).start()
            return c
        n_tiles = xs_hbm.shape[0] // MOE_TILE
        lax.fori_loop(nv_ref[0], n_tiles, tail, 0)

        def drain(_, c):
            pltpu.make_async_copy(_row(zero_hbm, 0), _row(xs_hbm, 0), pad_sem).wait()
            return c
        lax.fori_loop(0, total, drain, 0)

        def drain_tail(_, c):
            pltpu.make_async_copy(zero_hbm, tile_of(0), pad_sem).wait()
            return c
        lax.fori_loop(nv_ref[0], n_tiles, drain_tail, 0)

    slot = i % 2

    def issue(t, c):
        src = _row(h2_hbm, base + t)
        for k in range(TOP_K):
            pltpu.make_async_copy(src, _row(xs_hbm, pos_ref[0, 0, TOP_K * t + k]),
                                  sem.at[slot]).start()
        return c

    lax.fori_loop(0, tq, issue, 0, unroll=4)

    @pl.when(i > 0)
    def _():
        wait_rows(tq * TOP_K, sem.at[1 - slot])

    @pl.when(i == n - 1)
    def _():
        wait_rows(tq * TOP_K, sem.at[slot])


def _dispatch(pad_lo, pad_hi, n_valid, pos_flat, h2, p_max):
    t, d = h2.shape
    tq = DISPATCH_TILE
    pos3 = pos_flat.reshape(t // tq, 1, tq * TOP_K)
    zero_tile = jnp.zeros((MOE_TILE, d), h2.dtype)
    grid_spec = pltpu.PrefetchScalarGridSpec(
        num_scalar_prefetch=3,
        grid=(t // tq,),
        in_specs=[
            pl.BlockSpec((1, 1, tq * TOP_K), lambda i, lo, hi, nv: (i, 0, 0),
                         memory_space=pltpu.SMEM),
            pl.BlockSpec(memory_space=pl.ANY),
            pl.BlockSpec(memory_space=pl.ANY),
        ],
        out_specs=pl.BlockSpec(memory_space=pl.ANY),
        scratch_shapes=[pltpu.SemaphoreType.DMA((2,)), pltpu.SemaphoreType.DMA(())],
    )
    return pl.pallas_call(
        _dispatch_kernel,
        grid_spec=grid_spec,
        out_shape=jax.ShapeDtypeStruct((p_max, d), h2.dtype),
        compiler_params=pltpu.CompilerParams(dimension_semantics=("arbitrary",)),
        name="dispatch_rows",
    )(pad_lo, pad_hi, n_valid, pos3, h2, zero_tile)


def _moe_kernel(te_ref, nv_ref, xs_ref, wgu_ref, bgu_ref, wd_ref, bd_ref, y_ref, wgu_bf, wd_bf):
    i = pl.program_id(0)

    @pl.when(i < nv_ref[0])
    def _():
        prev = te_ref[jnp.maximum(i - 1, 0)]
        new_expert = jnp.logical_or(i == 0, te_ref[i] != prev)

        @pl.when(new_expert)
        def _():
            wgu_bf[...] = wgu_ref[0].astype(BF16)
            wd_bf[...] = wd_ref[0].astype(BF16)

        d_e = wd_bf.shape[0]
        gu = jnp.dot(xs_ref[...].astype(BF16), wgu_bf[...],
                     preferred_element_type=F32) + bgu_ref[0]
        gate = jnp.minimum(gu[:, :d_e], SWIGLU_LIMIT)
        lin = jnp.clip(gu[:, d_e:], -SWIGLU_LIMIT, SWIGLU_LIMIT)
        act = gate * jax.nn.sigmoid(SWIGLU_ALPHA * gate) * (lin + 1.0)
        y = jnp.dot(act.astype(BF16), wd_bf[...], preferred_element_type=F32) + bd_ref[0]
        y_ref[...] = y

    @pl.when(i >= nv_ref[0])
    def _():
        y_ref[...] = jnp.zeros_like(y_ref)


def _moe(tile_expert, n_valid, xs, w_gu, b_gu, w_down, b_down):
    p_max = xs.shape[0]
    n_tiles = p_max // MOE_TILE
    d_e = w_down.shape[1]

    def row(i, te, nv):
        return (jnp.minimum(i, nv[0] - 1), 0)

    def expert(i, te, nv):
        return (te[jnp.minimum(i, nv[0] - 1)], 0, 0)

    grid_spec = pltpu.PrefetchScalarGridSpec(
        num_scalar_prefetch=2,
        grid=(n_tiles,),
        in_specs=[
            pl.BlockSpec((MOE_TILE, D_MODEL), row),
            pl.BlockSpec((1, D_MODEL, 2 * d_e), expert),
            pl.BlockSpec((1, 1, 2 * d_e), expert),
            pl.BlockSpec((1, d_e, D_MODEL), expert),
            pl.BlockSpec((1, 1, D_MODEL), expert),
        ],
        out_specs=pl.BlockSpec((MOE_TILE, D_MODEL), lambda i, te, nv: (i, 0)),
        scratch_shapes=[pltpu.VMEM((D_MODEL, 2 * d_e), BF16), pltpu.VMEM((d_e, D_MODEL), BF16)],
    )
    return pl.pallas_call(
        _moe_kernel,
        grid_spec=grid_spec,
        out_shape=jax.ShapeDtypeStruct((p_max, D_MODEL), F32),
        compiler_params=pltpu.CompilerParams(
            dimension_semantics=("arbitrary",), vmem_limit_bytes=VMEM_LIMIT),
        name="moe_ffn",
    )(tile_expert, n_valid, xs, w_gu, b_gu, w_down, b_down)


def _combine_kernel(pos_cur, pos_nxt, x1_ref, gate_ref, g_ref, y_hbm, o_ref, buf, sem):
    i = pl.program_id(0)
    n = pl.num_programs(0)
    tq = COMBINE_TILE

    def issue(pos_ref, slot):
        def body(t, c):
            for k in range(TOP_K):
                pltpu.make_async_copy(_row(y_hbm, pos_ref[0, 0, TOP_K * t + k]),
                                      buf.at[slot, k, pl.ds(t, 1), :], sem.at[slot]).start()
            return c
        lax.fori_loop(0, tq, body, 0, unroll=4)

    @pl.when(i == 0)
    def _():
        issue(pos_cur, 0)

    @pl.when(i + 1 < n)
    def _():
        issue(pos_nxt, (i + 1) % 2)

    slot = i % 2

    def drain(_, c):
        pltpu.make_async_copy(_row(y_hbm, 0), buf.at[slot, 0, pl.ds(0, 1), :], sem.at[slot]).wait()
        return c

    lax.fori_loop(0, tq * TOP_K, drain, 0)

    acc = x1_ref[...]
    for k in range(TOP_K):
        acc = acc + gate_ref[:, k:k + 1] * buf[slot, k]
    o_ref[...] = _rms(acc, g_ref[...])


def _combine(pos_flat, x1, gates, g, y_sorted):
    t, d = x1.shape
    tq = COMBINE_TILE
    n = t // tq
    pos3 = pos_flat.reshape(n, 1, tq * TOP_K)
    row = lambda i: (i, 0)
    return pl.pallas_call(
        _combine_kernel,
        grid=(n,),
        in_specs=[
            pl.BlockSpec((1, 1, tq * TOP_K), lambda i: (i, 0, 0), memory_space=pltpu.SMEM),
            pl.BlockSpec((1, 1, tq * TOP_K), lambda i: (jnp.minimum(i + 1, n - 1), 0, 0),
                         memory_space=pltpu.SMEM),
            pl.BlockSpec((tq, d), row),
            pl.BlockSpec((tq, TOP_K), row),
            pl.BlockSpec((1, d), lambda i: (0, 0)),
            pl.BlockSpec(memory_space=pl.ANY),
        ],
        out_specs=pl.BlockSpec((tq, d), row),
        out_shape=jax.ShapeDtypeStruct((t, d), F32),
        scratch_shapes=[pltpu.VMEM((2, TOP_K, tq, d), F32), pltpu.SemaphoreType.DMA((2,))],
        compiler_params=pltpu.CompilerParams(
            dimension_semantics=("arbitrary",), vmem_limit_bytes=VMEM_LIMIT),
        name="combine_final",
    )(pos3, pos3, x1, gates, g, y_sorted)


def kernel(x, g_mix, w_in, w_pool, pool_scale, w_out, g_ffn, w_router, b_router,
           w_gu, b_gu, w_down, b_down, g_final):
    batch, seq, d = x.shape
    t = batch * seq
    x2 = x.reshape(t, d)

    u, qkv = _inproj(x2, g_mix[0].reshape(1, d), w_in[0])
    y_pool = _pool(u, w_pool[0], pool_scale[0].reshape(1, D_POOL), batch, seq)
    y_attn = _attention(qkv, batch, seq)

    wr_t = w_router[0].T
    br_t = jnp.broadcast_to(b_router[0].reshape(N_EXPERTS, 1), (N_EXPERTS, LANES))
    x1, h2, meta, cnt = _outproj(x2, y_pool, y_attn, w_out[0], g_ffn[0].reshape(1, d), wr_t, br_t)

    eidx = meta[META_EIDX:META_EIDX + TOP_K].T.astype(I32)
    rank = meta[META_RANK:META_RANK + TOP_K].T.astype(I32)
    gates = meta[META_GATE:META_GATE + TOP_K].T
    counts = cnt[:, 0].astype(I32)
    padded = ((counts + MOE_TILE - 1) // MOE_TILE) * MOE_TILE
    ends = jnp.cumsum(padded)
    offsets = ends - padded
    onehot = eidx[..., None] == jnp.arange(N_EXPERTS, dtype=I32)
    pos = rank + jnp.sum(jnp.where(onehot, offsets, 0), axis=-1)
    pos_flat = pos.reshape(-1)
    p_max = t * TOP_K + N_EXPERTS * MOE_TILE
    n_tiles = p_max // MOE_TILE
    tile_start = jnp.arange(n_tiles, dtype=I32) * MOE_TILE
    tile_expert = jnp.minimum(
        jnp.sum((tile_start[:, None] >= ends[None, :]).astype(I32), axis=1), N_EXPERTS - 1)
    n_valid = (ends[-1] // MOE_TILE).reshape(1).astype(I32)

    xs = _dispatch(offsets + counts, ends, n_valid, pos_flat, h2, p_max)
    y_sorted = _moe(tile_expert, n_valid, xs, w_gu[0], b_gu[0].reshape(N_EXPERTS, 1, -1),
                    w_down[0], b_down[0].reshape(N_EXPERTS, 1, -1))
    out = _combine(pos_flat, x1, gates, g_final.reshape(1, d), y_sorted)
    return out.reshape(batch, seq, d)
```

```python
import functools

import jax
import jax.numpy as jnp
from jax import lax
from jax.experimental import pallas as pl
from jax.experimental.pallas import tpu as pltpu
from jax.experimental.pallas import tpu_sc as plsc

F32 = jnp.float32
BF16 = jnp.bfloat16
I32 = jnp.int32

D_MODEL = 1024
D_POOL = 512
D_ATTN = 512
POOL_WINDOWS = (2, 4, 8, 16)
POOL_GROUP = 128
HEAD_DIM = 64
N_EXPERTS = 32
TOP_K = 4
SWIGLU_LIMIT = 7.0
SWIGLU_ALPHA = 1.702
EPS = 1e-5
NEG_INF = -1e30

LANES = 128
QB = 128
DIL_MID = 4
DIL_FAR = 16
ROW_TILE = 512
MOE_TILE = 256
COMBINE_TILE = 256
SC_WINDOW = 128
N_QUARTER = 4
D_QUARTER = D_MODEL // N_QUARTER
VMEM_LIMIT = 56 * 1024 * 1024


def _rms(x, g):
    ms = jnp.mean(x * x, axis=-1, keepdims=True)
    return x * lax.rsqrt(ms + EPS) * g


def _inproj_kernel(x_ref, g_ref, w_ref, u_ref, qkv_ref, w_bf):
    @pl.when(pl.program_id(0) == 0)
    def _():
        w_bf[...] = w_ref[...].astype(BF16)

    h = _rms(x_ref[...], g_ref[...]).astype(BF16)
    proj = jnp.dot(h, w_bf[...], preferred_element_type=F32)
    u_ref[...] = proj[:, :D_POOL]
    q = proj[:, D_POOL:D_POOL + D_ATTN] * (HEAD_DIM ** -0.5)
    qkv_ref[:, :D_ATTN] = q.astype(BF16)
    qkv_ref[:, D_ATTN:] = proj[:, D_POOL + D_ATTN:].astype(BF16)


def _inproj(x2, g, w):
    t = x2.shape[0]
    d_in = w.shape[1]
    return pl.pallas_call(
        _inproj_kernel,
        grid=(t // ROW_TILE,),
        in_specs=[
            pl.BlockSpec((ROW_TILE, D_MODEL), lambda i: (i, 0)),
            pl.BlockSpec((1, D_MODEL), lambda i: (0, 0)),
            pl.BlockSpec((D_MODEL, d_in), lambda i: (0, 0)),
        ],
        out_specs=[
            pl.BlockSpec((ROW_TILE, D_POOL), lambda i: (i, 0)),
            pl.BlockSpec((ROW_TILE, 3 * D_ATTN), lambda i: (i, 0)),
        ],
        out_shape=[
            jax.ShapeDtypeStruct((t, D_POOL), F32),
            jax.ShapeDtypeStruct((t, 3 * D_ATTN), BF16),
        ],
        scratch_shapes=[pltpu.VMEM((D_MODEL, d_in), BF16)],
        compiler_params=pltpu.CompilerParams(
            dimension_semantics=("arbitrary",), vmem_limit_bytes=VMEM_LIMIT),
        name="inproj",
    )(x2, g, w)


POOL_PAD = 16


def _pool_kernel(u_ref, wp_ref, sc_ref, y_ref, pad_ref):
    s = u_ref.shape[0]
    row = lax.broadcasted_iota(I32, (s, 1), 0)
    pad_ref[0:POOL_PAD, :] = jnp.zeros((POOL_PAD, POOL_GROUP), F32)
    for g, w in enumerate(POOL_WINDOWS):
        lo, hi = g * POOL_GROUP, (g + 1) * POOL_GROUP
        e = u_ref[:, lo:hi]
        acc = e
        span = 1
        while span < w:
            pad_ref[POOL_PAD:, :] = acc
            acc = acc + pad_ref[pl.ds(POOL_PAD - span, s), :]
            span *= 2
        count = jnp.minimum(row + 1, w).astype(F32)
        pooled = acc / count - e
        y = jnp.dot(pooled.astype(BF16), wp_ref[g].astype(BF16), preferred_element_type=F32)
        y_ref[:, lo:hi] = (y * sc_ref[:, lo:hi]).astype(BF16)


def _pool(u, w_pool, pool_scale, batch, seq):
    return pl.pallas_call(
        _pool_kernel,
        grid=(batch,),
        in_specs=[
            pl.BlockSpec((seq, D_POOL), lambda b: (b, 0)),
            pl.BlockSpec((len(POOL_WINDOWS), POOL_GROUP, POOL_GROUP), lambda b: (0, 0, 0)),
            pl.BlockSpec((1, D_POOL), lambda b: (0, 0)),
        ],
        out_specs=pl.BlockSpec((seq, D_POOL), lambda b: (b, 0)),
        out_shape=jax.ShapeDtypeStruct((batch * seq, D_POOL), BF16),
        scratch_shapes=[pltpu.VMEM((seq + POOL_PAD, POOL_GROUP), F32)],
        compiler_params=pltpu.CompilerParams(
            dimension_semantics=("arbitrary",), vmem_limit_bytes=VMEM_LIMIT),
        name="pool",
    )(u, w_pool, pool_scale)


def _pair_block(q, k, v_ext, mask2, lane_h0):
    zero = jnp.zeros_like(q)
    q2 = jnp.concatenate([jnp.where(lane_h0, q, zero), jnp.where(lane_h0, zero, q)], axis=0)
    s = lax.dot_general(q2, k, (((1,), (1,)), ((), ())), preferred_element_type=F32)
    s = jnp.where(mask2, s, NEG_INF)
    m = jnp.max(s, axis=-1, keepdims=True)
    p = jnp.exp(s - m).astype(BF16)
    ol = jnp.dot(p, v_ext, preferred_element_type=F32)
    o = jnp.where(lane_h0, ol[:QB, :LANES], ol[QB:, :LANES])
    l = jnp.where(lane_h0, ol[:QB, LANES:], ol[QB:, LANES:])
    mb = jnp.where(lane_h0, m[:QB], m[QB:])
    return o, mb, l


def _merge(o_a, m_a, l_a, o_b, m_b, l_b):
    m = jnp.maximum(m_a, m_b)
    ea = jnp.exp(m_a - m)
    eb = jnp.exp(m_b - m)
    return o_a * ea + o_b * eb, m, l_a * ea + l_b * eb


def _both_heads(mask):
    return jnp.concatenate([mask, mask], axis=0)


def _attn_kernel(q_ref, k_ref, v_ref, o_ref, stage, qp, kp, vpx, vnx, o23, m23, l23, on, mn, ln):
    s = q_ref.shape[0]
    n_chunk = DIL_FAR
    rows = s // n_chunk
    sub = rows // DIL_MID
    lane_h0 = lax.broadcasted_iota(I32, (1, LANES), 1) < HEAD_DIM

    ones = jnp.ones((s, LANES), BF16)
    vnx[:, :LANES] = v_ref[...]
    vnx[:, LANES:] = ones
    vpx[:, LANES:] = ones
    for src, dst in ((q_ref, qp), (k_ref, kp), (v_ref, vpx)):
        stage[...] = src[...].astype(F32)
        for r in range(n_chunk):
            dst[r * rows:(r + 1) * rows, 0:LANES] = (
                stage[pl.ds(r, rows, stride=n_chunk), :].astype(BF16))

    qi = lax.broadcasted_iota(I32, (QB, QB), 0)
    kj = lax.broadcasted_iota(I32, (QB, QB), 1)
    causal = _both_heads(qi >= kj)

    far_unroll = 4

    def far_body(it, carry):
        for u in range(far_unroll):
            r0 = pl.multiple_of((it * far_unroll + u) * rows, rows)
            blk = pl.ds(r0, rows)
            o, m, l = _pair_block(qp[blk, :], kp[blk, :], vpx[blk, :], causal, lane_h0)
            o23[blk, :] = o
            m23[blk, :] = m
            l23[blk, :] = l
        return carry

    lax.fori_loop(0, n_chunk // far_unroll, far_body, 0)

    def mid_index(n_key_sub):
        i_q = lax.broadcasted_iota(I32, (QB, DIL_MID * n_key_sub), 0)
        i_k = lax.broadcasted_iota(I32, (QB, DIL_MID * n_key_sub), 1)
        return i_q // sub, i_q % sub, i_k // n_key_sub, i_k % n_key_sub

    jq, aq, jk, ak = mid_index(sub)
    d0 = DIL_FAR * (aq - ak) + DIL_MID * (jq - jk)
    mask_mid0 = _both_heads(d0 >= 0)
    jq, aq, jk, ak = mid_index(2 * sub)
    d1 = DIL_FAR * (aq + sub - ak) + DIL_MID * (jq - jk)
    mask_mid = _both_heads((d1 >= 0) & (d1 <= DIL_MID * QB))

    def mid_tile(r4, a0, k0, nk, mask):
        def at(j, off, n):
            start = (DIL_MID * j + r4) * rows + off
            return pl.ds(start if isinstance(start, int) else pl.multiple_of(start, sub), n)

        q = jnp.concatenate([qp[at(j, a0, sub), :] for j in range(DIL_MID)], axis=0)
        k = jnp.concatenate([kp[at(j, k0, nk), :] for j in range(DIL_MID)], axis=0)
        v = jnp.concatenate([vpx[at(j, k0, nk), :] for j in range(DIL_MID)], axis=0)
        o, m, l = _pair_block(q, k, v, mask, lane_h0)
        for j in range(DIL_MID):
            dst = at(j, a0, sub)
            src = slice(j * sub, (j + 1) * sub)
            oo, mm, ll = _merge(o23[dst, :], m23[dst, :], l23[dst, :], o[src], m[src], l[src])
            o23[dst, :] = oo
            m23[dst, :] = mm
            l23[dst, :] = ll

    for r4 in range(DIL_MID):
        mid_tile(r4, 0, 0, sub, mask_mid0)

    def mid_body(a_blk, carry):
        a0 = pl.multiple_of(a_blk * sub, sub)
        for r4 in range(DIL_MID):
            mid_tile(r4, a0, a0 - sub, 2 * sub, mask_mid)
        return carry

    lax.fori_loop(1, rows // sub, mid_body, 0)

    for src, dst in ((o23, on), (m23, mn), (l23, ln)):
        for r in range(n_chunk):
            dst[pl.ds(r, rows, stride=n_chunk), :] = src[r * rows:(r + 1) * rows, :]

    qi2 = lax.broadcasted_iota(I32, (QB, 2 * QB), 0)
    kj2 = lax.broadcasted_iota(I32, (QB, 2 * QB), 1)
    dn = qi2 + QB - kj2
    mask_near = _both_heads((dn >= 0) & (dn <= QB))

    def near_finish(dst, o, m, l):
        oo, _, ll = _merge(on[dst, :], mn[dst, :], ln[dst, :], o, m, l)
        o_ref[dst, :] = (oo / ll).astype(o_ref.dtype)

    first = pl.ds(0, QB)
    o, m, l = _pair_block(q_ref[first, :], k_ref[first, :], vnx[first, :], causal, lane_h0)
    near_finish(first, o, m, l)

    near_unroll = 3

    def near_body(it, carry):
        for u in range(near_unroll):
            r0 = pl.multiple_of((1 + it * near_unroll + u) * QB, QB)
            keys = pl.ds(r0 - QB, 2 * QB)
            o, m, l = _pair_block(q_ref[pl.ds(r0, QB), :], k_ref[keys, :], vnx[keys, :],
                                  mask_near, lane_h0)
            near_finish(pl.ds(r0, QB), o, m, l)
        return carry

    lax.fori_loop(0, (s // QB - 1) // near_unroll, near_body, 0)


def _attention(qkv, batch, seq):
    n_pair = D_ATTN // LANES
    blk = (seq, LANES)
    f32_scr = pltpu.VMEM(blk, F32)
    bf_scr = pltpu.VMEM(blk, BF16)
    bfx_scr = pltpu.VMEM((seq, 2 * LANES), BF16)
    return pl.pallas_call(
        _attn_kernel,
        grid=(batch, n_pair),
        in_specs=[
            pl.BlockSpec(blk, lambda b, h: (b, h)),
            pl.BlockSpec(blk, lambda b, h: (b, n_pair + h)),
            pl.BlockSpec(blk, lambda b, h: (b, 2 * n_pair + h)),
        ],
        out_specs=pl.BlockSpec(blk, lambda b, h: (b, h)),
        out_shape=jax.ShapeDtypeStruct((batch * seq, D_ATTN), BF16),
        scratch_shapes=[f32_scr, bf_scr, bf_scr, bfx_scr, bfx_scr,
                        f32_scr, f32_scr, f32_scr, f32_scr, f32_scr, f32_scr],
        compiler_params=pltpu.CompilerParams(
            dimension_semantics=("arbitrary", "arbitrary"), vmem_limit_bytes=VMEM_LIMIT),
        name="dilated_attn",
    )(qkv, qkv, qkv)


META_ROWS = 16
META_EIDX, META_GATE, META_RANK = 0, 4, 8


def _outproj_kernel(x_ref, yp_ref, ya_ref, wo_ref, g_ref, wrt_ref, brt_ref,
                    x1_ref, h2_ref, meta_ref, cnt_ref, wo_bf, before, carry):
    tm = x_ref.shape[0]

    @pl.when(pl.program_id(0) == 0)
    def _():
        wo_bf[...] = wo_ref[...].astype(BF16)
        carry[...] = jnp.zeros_like(carry)
        ti = lax.broadcasted_iota(I32, (tm, tm), 0)
        tj = lax.broadcasted_iota(I32, (tm, tm), 1)
        before[...] = (ti < tj).astype(BF16)

    x1 = (x_ref[...]
          + jnp.dot(yp_ref[...], wo_bf[:D_POOL, :], preferred_element_type=F32)
          + jnp.dot(ya_ref[...], wo_bf[D_POOL:, :], preferred_element_type=F32))
    x1_ref[...] = x1
    h2 = _rms(x1, g_ref[...])
    h2_ref[...] = h2

    logits_t = lax.dot_general(wrt_ref[...].astype(BF16), h2.astype(BF16),
                               (((1,), (1,)), ((), ())), preferred_element_type=F32)
    logits_t = logits_t + brt_ref[:, 0:1]
    eid = lax.broadcasted_iota(I32, (N_EXPERTS, tm), 0)
    work = logits_t
    idxs, vals = [], []
    for _ in range(TOP_K):
        mx = jnp.max(work, axis=0, keepdims=True)
        idx = jnp.min(jnp.where(work == mx, eid, N_EXPERTS), axis=0, keepdims=True)
        idxs.append(idx)
        vals.append(mx)
        work = jnp.where(eid == idx, -jnp.inf, work)
    exps = [jnp.exp(v - vals[0]) for v in vals]
    den = exps[0] + exps[1] + exps[2] + exps[3]

    onehot = jnp.zeros((N_EXPERTS, tm), F32)
    for idx in idxs:
        onehot = onehot + (eid == idx).astype(F32)
    rank_e = carry[:, 0:1] + jnp.dot(onehot.astype(BF16), before[...],
                                     preferred_element_type=F32)
    carry[...] = carry[...] + jnp.sum(onehot, axis=1, keepdims=True)
    cnt_ref[...] = carry[...]

    mrow = lax.broadcasted_iota(I32, (META_ROWS, tm), 0)
    meta = jnp.zeros((META_ROWS, tm), F32)
    for k in range(TOP_K):
        rank_k = jnp.sum(jnp.where(eid == idxs[k], rank_e, 0.0), axis=0, keepdims=True)
        meta = jnp.where(mrow == META_EIDX + k, idxs[k].astype(F32), meta)
        meta = jnp.where(mrow == META_GATE + k, exps[k] / den, meta)
        meta = jnp.where(mrow == META_RANK + k, rank_k, meta)
    meta_ref[...] = meta


def _outproj(x2, y_pool, y_attn, w_out, g, w_router_t, b_router_t):
    t = x2.shape[0]
    row = lambda i: (i, 0)
    const = lambda i: (0, 0)
    return pl.pallas_call(
        _outproj_kernel,
        grid=(t // ROW_TILE,),
        in_specs=[
            pl.BlockSpec((ROW_TILE, D_MODEL), row),
            pl.BlockSpec((ROW_TILE, D_POOL), row),
            pl.BlockSpec((ROW_TILE, D_ATTN), row),
            pl.BlockSpec((D_MODEL, D_MODEL), const),
            pl.BlockSpec((1, D_MODEL), const),
            pl.BlockSpec((N_EXPERTS, D_MODEL), const),
            pl.BlockSpec((N_EXPERTS, LANES), const),
        ],
        out_specs=[
            pl.BlockSpec((ROW_TILE, D_MODEL), row),
            pl.BlockSpec((ROW_TILE, D_MODEL), row),
            pl.BlockSpec((META_ROWS, ROW_TILE), lambda i: (0, i)),
            pl.BlockSpec((N_EXPERTS, LANES), const),
        ],
        out_shape=[
            jax.ShapeDtypeStruct((t, D_MODEL), F32),
            jax.ShapeDtypeStruct((t, D_MODEL), F32),
            jax.ShapeDtypeStruct((META_ROWS, t), F32),
            jax.ShapeDtypeStruct((N_EXPERTS, LANES), F32),
        ],
        scratch_shapes=[pltpu.VMEM((D_MODEL, D_MODEL), BF16),
                        pltpu.VMEM((ROW_TILE, ROW_TILE), BF16),
                        pltpu.VMEM((N_EXPERTS, LANES), F32)],
        compiler_params=pltpu.CompilerParams(
            dimension_semantics=("arbitrary",), vmem_limit_bytes=VMEM_LIMIT),
        name="outproj_router",
    )(x2, y_pool, y_attn, w_out, g, w_router_t, b_router_t)


def _sc_mesh():
    return plsc.VectorSubcoreMesh(core_axis_name="core", subcore_axis_name="subcore")


def _sc_dispatch(h2, pos_rows, p_max):
    t = h2.shape[0]

    @functools.partial(
        pl.kernel, mesh=_sc_mesh(),
        out_type=[jax.ShapeDtypeStruct((p_max, D_QUARTER), h2.dtype)] * N_QUARTER)
    def run(h_hbm, *refs):
        pos_hbm, xs_hbm = refs[:TOP_K], refs[TOP_K:]
        for c in range(N_QUARTER):
            def body(x_vmem, *idx_vmem, dst=xs_hbm[c]):
                for iv in idx_vmem:
                    pltpu.sync_copy(x_vmem, dst.at[iv.at[0]])

            pltpu.emit_pipeline(
                body,
                grid=(t // SC_WINDOW,),
                in_specs=[pl.BlockSpec((SC_WINDOW, D_QUARTER), lambda i, c=c: (i, c))]
                + [pl.BlockSpec((1, SC_WINDOW), lambda i: (0, i))] * TOP_K,
                out_specs=[],
                core_axis_name=("core", "subcore"),
                dimension_semantics=(pltpu.PARALLEL,),
            )(h_hbm, *pos_hbm)

    return run(h2, *pos_rows)


def _sc_unpermute(y_quarters, idx_row):
    n = idx_row.shape[1]

    @functools.partial(
        pl.kernel, mesh=_sc_mesh(),
        out_type=[jax.ShapeDtypeStruct((n, D_QUARTER), y_quarters[0].dtype)] * N_QUARTER)
    def run(*refs):
        y_hbm, i_hbm, o_hbm = refs[:N_QUARTER], refs[N_QUARTER], refs[N_QUARTER + 1:]
        for c in range(N_QUARTER):
            def body(i_vmem, o_vmem, src=y_hbm[c]):
                pltpu.sync_copy(src.at[i_vmem.at[0]], o_vmem)

            pltpu.emit_pipeline(
                body,
                grid=(n // SC_WINDOW,),
                in_specs=[pl.BlockSpec((1, SC_WINDOW), lambda i: (0, i))],
                out_specs=[pl.BlockSpec((SC_WINDOW, D_QUARTER), lambda i: (i, 0))],
                core_axis_name=("core", "subcore"),
                dimension_semantics=(pltpu.PARALLEL,),
            )(i_hbm, o_hbm[c])

    return run(*y_quarters, idx_row)


def _moe_kernel(te_ref, nv_ref, rows_ref, *refs):
    xs_refs = refs[:N_QUARTER]
    wgu_ref, bgu_ref, wd_ref, bd_ref = refs[N_QUARTER:N_QUARTER + 4]
    y_refs = refs[N_QUARTER + 4:2 * N_QUARTER + 4]
    wgu_bf, wd_bf = refs[2 * N_QUARTER + 4:]
    i = pl.program_id(0)

    @pl.when(i < nv_ref[0])
    def _():
        prev = te_ref[jnp.maximum(i - 1, 0)]
        new_expert = jnp.logical_or(i == 0, te_ref[i] != prev)

        @pl.when(new_expert)
        def _():
            wgu_bf[...] = wgu_ref[0].astype(BF16)
            wd_bf[...] = wd_ref[0].astype(BF16)

        d_e = wd_bf.shape[0]
        x = jnp.concatenate([r[...] for r in xs_refs], axis=1)
        rid = lax.broadcasted_iota(I32, (MOE_TILE, 1), 0)
        x = jnp.where(rid < rows_ref[i], x, 0.0).astype(BF16)
        gu = jnp.dot(x, wgu_bf[...], preferred_element_type=F32) + bgu_ref[0]
        gate = jnp.minimum(gu[:, :d_e], SWIGLU_LIMIT)
        lin = jnp.clip(gu[:, d_e:], -SWIGLU_LIMIT, SWIGLU_LIMIT)
        act = gate * jax.nn.sigmoid(SWIGLU_ALPHA * gate) * (lin + 1.0)
        y = jnp.dot(act.astype(BF16), wd_bf[...], preferred_element_type=F32) + bd_ref[0]
        for c, y_ref in enumerate(y_refs):
            y_ref[...] = y[:, c * D_QUARTER:(c + 1) * D_QUARTER]

    @pl.when(i >= nv_ref[0])
    def _():
        for y_ref in y_refs:
            y_ref[...] = jnp.zeros_like(y_ref)


def _moe(tile_expert, n_valid, tile_rows, xs_quarters, w_gu, b_gu, w_down, b_down):
    p_max = xs_quarters[0].shape[0]
    n_tiles = p_max // MOE_TILE
    d_e = w_down.shape[1]

    def row(i, te, nv, tr):
        return (jnp.minimum(i, nv[0] - 1), 0)

    def expert(i, te, nv, tr):
        return (te[jnp.minimum(i, nv[0] - 1)], 0, 0)

    quarter = pl.BlockSpec((MOE_TILE, D_QUARTER), row)
    grid_spec = pltpu.PrefetchScalarGridSpec(
        num_scalar_prefetch=3,
        grid=(n_tiles,),
        in_specs=[quarter] * N_QUARTER + [
            pl.BlockSpec((1, D_MODEL, 2 * d_e), expert),
            pl.BlockSpec((1, 1, 2 * d_e), expert),
            pl.BlockSpec((1, d_e, D_MODEL), expert),
            pl.BlockSpec((1, 1, D_MODEL), expert),
        ],
        out_specs=[pl.BlockSpec((MOE_TILE, D_QUARTER), lambda i, te, nv, tr: (i, 0))] * N_QUARTER,
        scratch_shapes=[pltpu.VMEM((D_MODEL, 2 * d_e), BF16), pltpu.VMEM((d_e, D_MODEL), BF16)],
    )
    return pl.pallas_call(
        _moe_kernel,
        grid_spec=grid_spec,
        out_shape=[jax.ShapeDtypeStruct((p_max, D_QUARTER), F32)] * N_QUARTER,
        compiler_params=pltpu.CompilerParams(
            dimension_semantics=("arbitrary",), vmem_limit_bytes=VMEM_LIMIT),
        name="moe_ffn",
    )(tile_expert, n_valid, tile_rows, *xs_quarters, w_gu, b_gu, w_down, b_down)


def _combine_kernel(x1_ref, gate_ref, g_ref, *refs):
    yk_refs, o_ref = refs[:N_QUARTER], refs[N_QUARTER]
    acc = x1_ref[...]
    for k in range(TOP_K):
        y_k = jnp.concatenate([r[k] for r in yk_refs], axis=1)
        acc = acc + gate_ref[:, k:k + 1] * y_k
    o_ref[...] = _rms(acc, g_ref[...])


def _combine(x1, gates, g, yk_quarters):
    t, d = x1.shape
    tq = COMBINE_TILE
    row = lambda i: (i, 0)
    return pl.pallas_call(
        _combine_kernel,
        grid=(t // tq,),
        in_specs=[
            pl.BlockSpec((tq, d), row),
            pl.BlockSpec((tq, TOP_K), row),
            pl.BlockSpec((1, d), lambda i: (0, 0)),
        ] + [pl.BlockSpec((TOP_K, tq, D_QUARTER), lambda i: (0, i, 0))] * N_QUARTER,
        out_specs=pl.BlockSpec((tq, d), row),
        out_shape=jax.ShapeDtypeStruct((t, d), F32),
        compiler_params=pltpu.CompilerParams(
            dimension_semantics=("arbitrary",), vmem_limit_bytes=VMEM_LIMIT),
        name="combine_final",
    )(x1, gates, g, *yk_quarters)


def kernel(x, g_mix, w_in, w_pool, pool_scale, w_out, g_ffn, w_router, b_router,
           w_gu, b_gu, w_down, b_down, g_final):
    batch, seq, d = x.shape
    t = batch * seq
    x2 = x.reshape(t, d)

    u, qkv = _inproj(x2, g_mix[0].reshape(1, d), w_in[0])
    y_pool = _pool(u, w_pool[0], pool_scale[0].reshape(1, D_POOL), batch, seq)
    y_attn = _attention(qkv, batch, seq)

    wr_t = w_router[0].T
    br_t = jnp.broadcast_to(b_router[0].reshape(N_EXPERTS, 1), (N_EXPERTS, LANES))
    x1, h2, meta, cnt = _outproj(x2, y_pool, y_attn, w_out[0], g_ffn[0].reshape(1, d), wr_t, br_t)

    eidx = meta[META_EIDX:META_EIDX + TOP_K].astype(I32)
    rank = meta[META_RANK:META_RANK + TOP_K].astype(I32)
    gates = meta[META_GATE:META_GATE + TOP_K].T
    counts = cnt[:, 0].astype(I32)
    padded = ((counts + MOE_TILE - 1) // MOE_TILE) * MOE_TILE
    ends = jnp.cumsum(padded)
    offsets = ends - padded
    onehot = eidx[..., None] == jnp.arange(N_EXPERTS, dtype=I32)
    pos = rank + jnp.sum(jnp.where(onehot, offsets, 0), axis=-1)
    p_max = t * TOP_K + N_EXPERTS * MOE_TILE
    n_tiles = p_max // MOE_TILE
    tile_start = jnp.arange(n_tiles, dtype=I32) * MOE_TILE
    tile_expert = jnp.minimum(
        jnp.sum((tile_start[:, None] >= ends[None, :]).astype(I32), axis=1), N_EXPERTS - 1)
    tile_rows = jnp.clip((offsets + counts)[tile_expert] - tile_start, 0, MOE_TILE)
    n_valid = (ends[-1] // MOE_TILE).reshape(1).astype(I32)

    xs = _sc_dispatch(h2, [pos[k:k + 1] for k in range(TOP_K)], p_max)
    ys = _moe(tile_expert, n_valid, tile_rows, xs, w_gu[0], b_gu[0].reshape(N_EXPERTS, 1, -1),
              w_down[0], b_down[0].reshape(N_EXPERTS, 1, -1))
    yk = _sc_unpermute(ys, pos.reshape(1, TOP_K * t))
    yk = [q.reshape(TOP_K, t, D_QUARTER) for q in yk]
    out = _combine(x1, gates, g_final.reshape(1, d), yk)
    return out.reshape(batch, seq, d)
```

```python
import functools

import jax
import jax.numpy as jnp
from jax import lax
from jax.experimental import pallas as pl
from jax.experimental.pallas import tpu as pltpu
from jax.experimental.pallas import tpu_sc as plsc

F32 = jnp.float32
BF16 = jnp.bfloat16
I32 = jnp.int32

D_MODEL = 1024
D_POOL = 512
D_ATTN = 512
POOL_WINDOWS = (2, 4, 8, 16)
POOL_GROUP = 128
HEAD_DIM = 64
N_EXPERTS = 32
TOP_K = 4
SWIGLU_LIMIT = 7.0
SWIGLU_ALPHA = 1.702
EPS = 1e-5
NEG_INF = -1e30

LANES = 128
QB = 128
DIL_MID = 4
DIL_FAR = 16
ROW_TILE = 512
MOE_TILE = 256
COMBINE_TILE = 256
SC_WINDOW = 128
D_PACKED = D_MODEL // 2
N_PART = 2
D_PART = D_PACKED // N_PART


def _pack_row(x):
    hi = lax.bitcast_convert_type(x[:, :D_PACKED].astype(BF16).astype(F32), I32)
    lo = lax.bitcast_convert_type(x[:, D_PACKED:].astype(BF16).astype(F32), I32)
    return hi | lax.shift_right_logical(lo, 16)


def _unpack_row(w):
    hi = lax.bitcast_convert_type(w & jnp.int32(-65536), F32)
    lo = lax.bitcast_convert_type(lax.shift_left(w, 16), F32)
    return jnp.concatenate([hi, lo], axis=1)
VMEM_LIMIT = 56 * 1024 * 1024


def _rms(x, g):
    ms = jnp.mean(x * x, axis=-1, keepdims=True)
    return x * lax.rsqrt(ms + EPS) * g


def _inproj_kernel(x_ref, g_ref, w_ref, u_ref, qkv_ref, w_bf):
    @pl.when(pl.program_id(0) == 0)
    def _():
        w_bf[...] = w_ref[...].astype(BF16)

    h = _rms(x_ref[...], g_ref[...]).astype(BF16)
    proj = jnp.dot(h, w_bf[...], preferred_element_type=F32)
    u_ref[...] = proj[:, :D_POOL]
    q = proj[:, D_POOL:D_POOL + D_ATTN] * (HEAD_DIM ** -0.5)
    qkv_ref[:, :D_ATTN] = q.astype(BF16)
    qkv_ref[:, D_ATTN:] = proj[:, D_POOL + D_ATTN:].astype(BF16)


def _inproj(x2, g, w):
    t = x2.shape[0]
    d_in = w.shape[1]
    return pl.pallas_call(
        _inproj_kernel,
        grid=(t // ROW_TILE,),
        in_specs=[
            pl.BlockSpec((ROW_TILE, D_MODEL), lambda i: (i, 0)),
            pl.BlockSpec((1, D_MODEL), lambda i: (0, 0)),
            pl.BlockSpec((D_MODEL, d_in), lambda i: (0, 0)),
        ],
        out_specs=[
            pl.BlockSpec((ROW_TILE, D_POOL), lambda i: (i, 0)),
            pl.BlockSpec((ROW_TILE, 3 * D_ATTN), lambda i: (i, 0)),
        ],
        out_shape=[
            jax.ShapeDtypeStruct((t, D_POOL), F32),
            jax.ShapeDtypeStruct((t, 3 * D_ATTN), BF16),
        ],
        scratch_shapes=[pltpu.VMEM((D_MODEL, d_in), BF16)],
        compiler_params=pltpu.CompilerParams(
            dimension_semantics=("arbitrary",), vmem_limit_bytes=VMEM_LIMIT),
        name="inproj",
    )(x2, g, w)


POOL_PAD = 16


def _pool_kernel(u_ref, wp_ref, sc_ref, y_ref, pad_ref):
    s = u_ref.shape[0]
    row = lax.broadcasted_iota(I32, (s, 1), 0)
    pad_ref[0:POOL_PAD, :] = jnp.zeros((POOL_PAD, POOL_GROUP), F32)
    for g, w in enumerate(POOL_WINDOWS):
        lo, hi = g * POOL_GROUP, (g + 1) * POOL_GROUP
        e = u_ref[:, lo:hi]
        acc = e
        span = 1
        while span < w:
            pad_ref[POOL_PAD:, :] = acc
            acc = acc + pad_ref[pl.ds(POOL_PAD - span, s), :]
            span *= 2
        count = jnp.minimum(row + 1, w).astype(F32)
        pooled = acc / count - e
        y = jnp.dot(pooled.astype(BF16), wp_ref[g].astype(BF16), preferred_element_type=F32)
        y_ref[:, lo:hi] = (y * sc_ref[:, lo:hi]).astype(BF16)


def _pool(u, w_pool, pool_scale, batch, seq):
    return pl.pallas_call(
        _pool_kernel,
        grid=(batch,),
        in_specs=[
            pl.BlockSpec((seq, D_POOL), lambda b: (b, 0)),
            pl.BlockSpec((len(POOL_WINDOWS), POOL_GROUP, POOL_GROUP), lambda b: (0, 0, 0)),
            pl.BlockSpec((1, D_POOL), lambda b: (0, 0)),
        ],
        out_specs=pl.BlockSpec((seq, D_POOL), lambda b: (b, 0)),
        out_shape=jax.ShapeDtypeStruct((batch * seq, D_POOL), BF16),
        scratch_shapes=[pltpu.VMEM((seq + POOL_PAD, POOL_GROUP), F32)],
        compiler_params=pltpu.CompilerParams(
            dimension_semantics=("arbitrary",), vmem_limit_bytes=VMEM_LIMIT),
        name="pool",
    )(u, w_pool, pool_scale)


def _pair_block(q, k, v_ext, mask2, lane_h0):
    zero = jnp.zeros_like(q)
    q2 = jnp.concatenate([jnp.where(lane_h0, q, zero), jnp.where(lane_h0, zero, q)], axis=0)
    s = lax.dot_general(q2, k, (((1,), (1,)), ((), ())), preferred_element_type=F32)
    s = jnp.where(mask2, s, NEG_INF)
    m = jnp.max(s, axis=-1, keepdims=True)
    p = jnp.exp(s - m).astype(BF16)
    ol = jnp.dot(p, v_ext, preferred_element_type=F32)
    o = jnp.where(lane_h0, ol[:QB, :LANES], ol[QB:, :LANES])
    l = jnp.where(lane_h0, ol[:QB, LANES:], ol[QB:, LANES:])
    mb = jnp.where(lane_h0, m[:QB], m[QB:])
    return o, mb, l


def _merge(o_a, m_a, l_a, o_b, m_b, l_b):
    m = jnp.maximum(m_a, m_b)
    ea = jnp.exp(m_a - m)
    eb = jnp.exp(m_b - m)
    return o_a * ea + o_b * eb, m, l_a * ea + l_b * eb


def _both_heads(mask):
    return jnp.concatenate([mask, mask], axis=0)


def _attn_kernel(q_ref, k_ref, v_ref, o_ref, stage, qp, kp, vpx, vnx, o23, m23, l23, on, mn, ln):
    s = q_ref.shape[0]
    n_chunk = DIL_FAR
    rows = s // n_chunk
    sub = rows // DIL_MID
    lane_h0 = lax.broadcasted_iota(I32, (1, LANES), 1) < HEAD_DIM

    ones = jnp.ones((s, LANES), BF16)
    vnx[:, :LANES] = v_ref[...]
    vnx[:, LANES:] = ones
    vpx[:, LANES:] = ones
    for src, dst in ((q_ref, qp), (k_ref, kp), (v_ref, vpx)):
        stage[...] = src[...].astype(F32)
        for r in range(n_chunk):
            dst[r * rows:(r + 1) * rows, 0:LANES] = (
                stage[pl.ds(r, rows, stride=n_chunk), :].astype(BF16))

    qi = lax.broadcasted_iota(I32, (QB, QB), 0)
    kj = lax.broadcasted_iota(I32, (QB, QB), 1)
    causal = _both_heads(qi >= kj)

    far_unroll = 4

    def far_body(it, carry):
        for u in range(far_unroll):
            r0 = pl.multiple_of((it * far_unroll + u) * rows, rows)
            blk = pl.ds(r0, rows)
            o, m, l = _pair_block(qp[blk, :], kp[blk, :], vpx[blk, :], causal, lane_h0)
            o23[blk, :] = o
            m23[blk, :] = m
            l23[blk, :] = l
        return carry

    lax.fori_loop(0, n_chunk // far_unroll, far_body, 0)

    def mid_index(n_key_sub):
        i_q = lax.broadcasted_iota(I32, (QB, DIL_MID * n_key_sub), 0)
        i_k = lax.broadcasted_iota(I32, (QB, DIL_MID * n_key_sub), 1)
        return i_q // sub, i_q % sub, i_k // n_key_sub, i_k % n_key_sub

    jq, aq, jk, ak = mid_index(sub)
    d0 = DIL_FAR * (aq - ak) + DIL_MID * (jq - jk)
    mask_mid0 = _both_heads(d0 >= 0)
    jq, aq, jk, ak = mid_index(2 * sub)
    d1 = DIL_FAR * (aq + sub - ak) + DIL_MID * (jq - jk)
    mask_mid = _both_heads((d1 >= 0) & (d1 <= DIL_MID * QB))

    def mid_tile(r4, a0, k0, nk, mask):
        def at(j, off, n):
            start = (DIL_MID * j + r4) * rows + off
            return pl.ds(start if isinstance(start, int) else pl.multiple_of(start, sub), n)

        q = jnp.concatenate([qp[at(j, a0, sub), :] for j in range(DIL_MID)], axis=0)
        k = jnp.concatenate([kp[at(j, k0, nk), :] for j in range(DIL_MID)], axis=0)
        v = jnp.concatenate([vpx[at(j, k0, nk), :] for j in range(DIL_MID)], axis=0)
        o, m, l = _pair_block(q, k, v, mask, lane_h0)
        for j in range(DIL_MID):
            dst = at(j, a0, sub)
            src = slice(j * sub, (j + 1) * sub)
            oo, mm, ll = _merge(o23[dst, :], m23[dst, :], l23[dst, :], o[src], m[src], l[src])
            o23[dst, :] = oo
            m23[dst, :] = mm
            l23[dst, :] = ll

    for r4 in range(DIL_MID):
        mid_tile(r4, 0, 0, sub, mask_mid0)

    def mid_body(a_blk, carry):
        a0 = pl.multiple_of(a_blk * sub, sub)
        for r4 in range(DIL_MID):
            mid_tile(r4, a0, a0 - sub, 2 * sub, mask_mid)
        return carry

    lax.fori_loop(1, rows // sub, mid_body, 0)

    for src, dst in ((o23, on), (m23, mn), (l23, ln)):
        for r in range(n_chunk):
            dst[pl.ds(r, rows, stride=n_chunk), :] = src[r * rows:(r + 1) * rows, :]

    qi2 = lax.broadcasted_iota(I32, (QB, 2 * QB), 0)
    kj2 = lax.broadcasted_iota(I32, (QB, 2 * QB), 1)
    dn = qi2 + QB - kj2
    mask_near = _both_heads((dn >= 0) & (dn <= QB))

    def near_finish(dst, o, m, l):
        oo, _, ll = _merge(on[dst, :], mn[dst, :], ln[dst, :], o, m, l)
        o_ref[dst, :] = (oo / ll).astype(o_ref.dtype)

    first = pl.ds(0, QB)
    o, m, l = _pair_block(q_ref[first, :], k_ref[first, :], vnx[first, :], causal, lane_h0)
    near_finish(first, o, m, l)

    near_unroll = 3

    def near_body(it, carry):
        for u in range(near_unroll):
            r0 = pl.multiple_of((1 + it * near_unroll + u) * QB, QB)
            keys = pl.ds(r0 - QB, 2 * QB)
            o, m, l = _pair_block(q_ref[pl.ds(r0, QB), :], k_ref[keys, :], vnx[keys, :],
                                  mask_near, lane_h0)
            near_finish(pl.ds(r0, QB), o, m, l)
        return carry

    lax.fori_loop(0, (s // QB - 1) // near_unroll, near_body, 0)


def _attention(qkv, batch, seq):
    n_pair = D_ATTN // LANES
    blk = (seq, LANES)
    f32_scr = pltpu.VMEM(blk, F32)
    bf_scr = pltpu.VMEM(blk, BF16)
    bfx_scr = pltpu.VMEM((seq, 2 * LANES), BF16)
    return pl.pallas_call(
        _attn_kernel,
        grid=(batch, n_pair),
        in_specs=[
            pl.BlockSpec(blk, lambda b, h: (b, h)),
            pl.BlockSpec(blk, lambda b, h: (b, n_pair + h)),
            pl.BlockSpec(blk, lambda b, h: (b, 2 * n_pair + h)),
        ],
        out_specs=pl.BlockSpec(blk, lambda b, h: (b, h)),
        out_shape=jax.ShapeDtypeStruct((batch * seq, D_ATTN), BF16),
        scratch_shapes=[f32_scr, bf_scr, bf_scr, bfx_scr, bfx_scr,
                        f32_scr, f32_scr, f32_scr, f32_scr, f32_scr, f32_scr],
        compiler_params=pltpu.CompilerParams(
            dimension_semantics=("arbitrary", "arbitrary"), vmem_limit_bytes=VMEM_LIMIT),
        name="dilated_attn",
    )(qkv, qkv, qkv)


META_ROWS = 16
META_EIDX, META_GATE, META_RANK = 0, 4, 8


def _outproj_kernel(x_ref, yp_ref, ya_ref, wo_ref, g_ref, wrt_ref, brt_ref,
                    x1_ref, h2_ref, meta_ref, cnt_ref, wo_bf, before, carry):
    tm = x_ref.shape[0]

    @pl.when(pl.program_id(0) == 0)
    def _():
        wo_bf[...] = wo_ref[...].astype(BF16)
        carry[...] = jnp.zeros_like(carry)
        ti = lax.broadcasted_iota(I32, (tm, tm), 0)
        tj = lax.broadcasted_iota(I32, (tm, tm), 1)
        before[...] = (ti < tj).astype(BF16)

    x1 = (x_ref[...]
          + jnp.dot(yp_ref[...], wo_bf[:D_POOL, :], preferred_element_type=F32)
          + jnp.dot(ya_ref[...], wo_bf[D_POOL:, :], preferred_element_type=F32))
    x1_ref[...] = x1
    h2 = _rms(x1, g_ref[...])
    h2_ref[...] = _pack_row(h2)

    logits_t = lax.dot_general(wrt_ref[...].astype(BF16), h2.astype(BF16),
                               (((1,), (1,)), ((), ())), preferred_element_type=F32)
    logits_t = logits_t + brt_ref[:, 0:1]
    eid = lax.broadcasted_iota(I32, (N_EXPERTS, tm), 0)
    work = logits_t
    idxs, vals = [], []
    for _ in range(TOP_K):
        mx = jnp.max(work, axis=0, keepdims=True)
        idx = jnp.min(jnp.where(work == mx, eid, N_EXPERTS), axis=0, keepdims=True)
        idxs.append(idx)
        vals.append(mx)
        work = jnp.where(eid == idx, -jnp.inf, work)
    exps = [jnp.exp(v - vals[0]) for v in vals]
    den = exps[0] + exps[1] + exps[2] + exps[3]

    onehot = jnp.zeros((N_EXPERTS, tm), F32)
    for idx in idxs:
        onehot = onehot + (eid == idx).astype(F32)
    rank_e = carry[:, 0:1] + jnp.dot(onehot.astype(BF16), before[...],
                                     preferred_element_type=F32)
    carry[...] = carry[...] + jnp.sum(onehot, axis=1, keepdims=True)
    cnt_ref[...] = carry[...]

    mrow = lax.broadcasted_iota(I32, (META_ROWS, tm), 0)
    meta = jnp.zeros((META_ROWS, tm), F32)
    for k in range(TOP_K):
        rank_k = jnp.sum(jnp.where(eid == idxs[k], rank_e, 0.0), axis=0, keepdims=True)
        meta = jnp.where(mrow == META_EIDX + k, idxs[k].astype(F32), meta)
        meta = jnp.where(mrow == META_GATE + k, exps[k] / den, meta)
        meta = jnp.where(mrow == META_RANK + k, rank_k, meta)
    meta_ref[...] = meta


def _outproj(x2, y_pool, y_attn, w_out, g, w_router_t, b_router_t):
    t = x2.shape[0]
    row = lambda i: (i, 0)
    const = lambda i: (0, 0)
    return pl.pallas_call(
        _outproj_kernel,
        grid=(t // ROW_TILE,),
        in_specs=[
            pl.BlockSpec((ROW_TILE, D_MODEL), row),
            pl.BlockSpec((ROW_TILE, D_POOL), row),
            pl.BlockSpec((ROW_TILE, D_ATTN), row),
            pl.BlockSpec((D_MODEL, D_MODEL), const),
            pl.BlockSpec((1, D_MODEL), const),
            pl.BlockSpec((N_EXPERTS, D_MODEL), const),
            pl.BlockSpec((N_EXPERTS, LANES), const),
        ],
        out_specs=[
            pl.BlockSpec((ROW_TILE, D_MODEL), row),
            pl.BlockSpec((ROW_TILE, D_PACKED), row),
            pl.BlockSpec((META_ROWS, ROW_TILE), lambda i: (0, i)),
            pl.BlockSpec((N_EXPERTS, LANES), const),
        ],
        out_shape=[
            jax.ShapeDtypeStruct((t, D_MODEL), F32),
            jax.ShapeDtypeStruct((t, D_PACKED), I32),
            jax.ShapeDtypeStruct((META_ROWS, t), F32),
            jax.ShapeDtypeStruct((N_EXPERTS, LANES), F32),
        ],
        scratch_shapes=[pltpu.VMEM((D_MODEL, D_MODEL), BF16),
                        pltpu.VMEM((ROW_TILE, ROW_TILE), BF16),
                        pltpu.VMEM((N_EXPERTS, LANES), F32)],
        compiler_params=pltpu.CompilerParams(
            dimension_semantics=("arbitrary",), vmem_limit_bytes=VMEM_LIMIT),
        name="outproj_router",
    )(x2, y_pool, y_attn, w_out, g, w_router_t, b_router_t)


def _sc_mesh():
    return plsc.VectorSubcoreMesh(core_axis_name="core", subcore_axis_name="subcore")


def _sc_dispatch(h2, pos_rows, p_max):
    t = h2.shape[0]

    @functools.partial(
        pl.kernel, mesh=_sc_mesh(),
        out_type=[jax.ShapeDtypeStruct((p_max, D_PART), h2.dtype)] * N_PART)
    def run(h_hbm, *refs):
        pos_hbm, xs_hbm = refs[:TOP_K], refs[TOP_K:]
        for c in range(N_PART):
            def body(x_vmem, *idx_vmem, dst=xs_hbm[c]):
                for iv in idx_vmem:
                    pltpu.sync_copy(x_vmem, dst.at[iv.at[0]])

            pltpu.emit_pipeline(
                body,
                grid=(t // SC_WINDOW,),
                in_specs=[pl.BlockSpec((SC_WINDOW, D_PART), lambda i, c=c: (i, c))]
                + [pl.BlockSpec((1, SC_WINDOW), lambda i: (0, i))] * TOP_K,
                out_specs=[],
                core_axis_name=("core", "subcore"),
                dimension_semantics=(pltpu.PARALLEL,),
            )(h_hbm, *pos_hbm)

    return run(h2, *pos_rows)


def _sc_unpermute(y_quarters, idx_row):
    n = idx_row.shape[1]

    @functools.partial(
        pl.kernel, mesh=_sc_mesh(),
        out_type=[jax.ShapeDtypeStruct((n, D_PART), y_quarters[0].dtype)] * N_PART)
    def run(*refs):
        y_hbm, i_hbm, o_hbm = refs[:N_PART], refs[N_PART], refs[N_PART + 1:]
        for c in range(N_PART):
            def body(i_vmem, o_vmem, src=y_hbm[c]):
                pltpu.sync_copy(src.at[i_vmem.at[0]], o_vmem)

            pltpu.emit_pipeline(
                body,
                grid=(n // SC_WINDOW,),
                in_specs=[pl.BlockSpec((1, SC_WINDOW), lambda i: (0, i))],
                out_specs=[pl.BlockSpec((SC_WINDOW, D_PART), lambda i: (i, 0))],
                core_axis_name=("core", "subcore"),
                dimension_semantics=(pltpu.PARALLEL,),
            )(i_hbm, o_hbm[c])

    return run(*y_quarters, idx_row)


def _moe_kernel(te_ref, nv_ref, rows_ref, nxt_ref, *refs):
    xs_refs = refs[:N_PART]
    bgu_ref, bd_ref, wgu_hbm, wd_hbm = refs[N_PART:N_PART + 4]
    y_refs = refs[N_PART + 4:2 * N_PART + 4]
    wgu_f32, wd_f32, wgu_bf, wd_bf, sem = refs[2 * N_PART + 4:]
    i = pl.program_id(0)

    def weight_copies(e):
        return (pltpu.make_async_copy(wgu_hbm.at[e], wgu_f32, sem.at[0]),
                pltpu.make_async_copy(wd_hbm.at[e], wd_f32, sem.at[1]))

    @pl.when(i < nv_ref[0])
    def _():
        prev = te_ref[jnp.maximum(i - 1, 0)]
        new_expert = jnp.logical_or(i == 0, te_ref[i] != prev)

        @pl.when(i == 0)
        def _():
            for cp in weight_copies(te_ref[0]):
                cp.start()

        @pl.when(new_expert)
        def _():
            for cp in weight_copies(te_ref[i]):
                cp.wait()
            wgu_bf[...] = wgu_f32[...].astype(BF16)
            wd_bf[...] = wd_f32[...].astype(BF16)

            @pl.when(nxt_ref[i] >= 0)
            def _():
                for cp in weight_copies(nxt_ref[i]):
                    cp.start()

        d_e = wd_bf.shape[0]
        x = _unpack_row(jnp.concatenate([r[...] for r in xs_refs], axis=1))
        rid = lax.broadcasted_iota(I32, (MOE_TILE, 1), 0)
        x = jnp.where(rid < rows_ref[i], x, 0.0).astype(BF16)
        gu = jnp.dot(x, wgu_bf[...], preferred_element_type=F32) + bgu_ref[0]
        gate = jnp.minimum(gu[:, :d_e], SWIGLU_LIMIT)
        lin = jnp.clip(gu[:, d_e:], -SWIGLU_LIMIT, SWIGLU_LIMIT)
        act = gate * jax.nn.sigmoid(SWIGLU_ALPHA * gate) * (lin + 1.0)
        y = jnp.dot(act.astype(BF16), wd_bf[...], preferred_element_type=F32) + bd_ref[0]
        yp = _pack_row(y)
        for c, y_ref in enumerate(y_refs):
            y_ref[...] = yp[:, c * D_PART:(c + 1) * D_PART]

    @pl.when(i >= nv_ref[0])
    def _():
        for y_ref in y_refs:
            y_ref[...] = jnp.zeros_like(y_ref)


def _moe(tile_expert, n_valid, tile_rows, next_expert, xs_parts, w_gu, b_gu, w_down, b_down):
    p_max = xs_parts[0].shape[0]
    n_tiles = p_max // MOE_TILE
    d_e = w_down.shape[1]

    def row(i, te, nv, tr, nx):
        return (jnp.minimum(i, nv[0] - 1), 0)

    def expert(i, te, nv, tr, nx):
        return (te[jnp.minimum(i, nv[0] - 1)], 0, 0)

    grid_spec = pltpu.PrefetchScalarGridSpec(
        num_scalar_prefetch=4,
        grid=(n_tiles,),
        in_specs=[pl.BlockSpec((MOE_TILE, D_PART), row)] * N_PART + [
            pl.BlockSpec((1, 1, 2 * d_e), expert),
            pl.BlockSpec((1, 1, D_MODEL), expert),
            pl.BlockSpec(memory_space=pl.ANY),
            pl.BlockSpec(memory_space=pl.ANY),
        ],
        out_specs=[pl.BlockSpec((MOE_TILE, D_PART), lambda i, te, nv, tr, nx: (i, 0))] * N_PART,
        scratch_shapes=[pltpu.VMEM((D_MODEL, 2 * d_e), F32), pltpu.VMEM((d_e, D_MODEL), F32),
                        pltpu.VMEM((D_MODEL, 2 * d_e), BF16), pltpu.VMEM((d_e, D_MODEL), BF16),
                        pltpu.SemaphoreType.DMA((2,))],
    )
    return pl.pallas_call(
        _moe_kernel,
        grid_spec=grid_spec,
        out_shape=[jax.ShapeDtypeStruct((p_max, D_PART), I32)] * N_PART,
        compiler_params=pltpu.CompilerParams(
            dimension_semantics=("arbitrary",), vmem_limit_bytes=VMEM_LIMIT),
        name="moe_ffn",
    )(tile_expert, n_valid, tile_rows, next_expert, *xs_parts, b_gu, b_down, w_gu, w_down)


def _combine_kernel(x1_ref, gate_ref, g_ref, *refs):
    yk_refs, o_ref = refs[:N_PART], refs[N_PART]
    acc = x1_ref[...]
    for k in range(TOP_K):
        y_k = _unpack_row(jnp.concatenate([r[k] for r in yk_refs], axis=1))
        acc = acc + gate_ref[:, k:k + 1] * y_k
    o_ref[...] = _rms(acc, g_ref[...])


def _combine(x1, gates, g, yk_quarters):
    t, d = x1.shape
    tq = COMBINE_TILE
    row = lambda i: (i, 0)
    return pl.pallas_call(
        _combine_kernel,
        grid=(t // tq,),
        in_specs=[
            pl.BlockSpec((tq, d), row),
            pl.BlockSpec((tq, TOP_K), row),
            pl.BlockSpec((1, d), lambda i: (0, 0)),
        ] + [pl.BlockSpec((TOP_K, tq, D_PART), lambda i: (0, i, 0))] * N_PART,
        out_specs=pl.BlockSpec((tq, d), row),
        out_shape=jax.ShapeDtypeStruct((t, d), F32),
        compiler_params=pltpu.CompilerParams(
            dimension_semantics=("arbitrary",), vmem_limit_bytes=VMEM_LIMIT),
        name="combine_final",
    )(x1, gates, g, *yk_quarters)


def kernel(x, g_mix, w_in, w_pool, pool_scale, w_out, g_ffn, w_router, b_router,
           w_gu, b_gu, w_down, b_down, g_final):
    batch, seq, d = x.shape
    t = batch * seq
    x2 = x.reshape(t, d)

    u, qkv = _inproj(x2, g_mix[0].reshape(1, d), w_in[0])
    y_pool = _pool(u, w_pool[0], pool_scale[0].reshape(1, D_POOL), batch, seq)
    y_attn = _attention(qkv, batch, seq)

    wr_t = w_router[0].T
    br_t = jnp.broadcast_to(b_router[0].reshape(N_EXPERTS, 1), (N_EXPERTS, LANES))
    x1, h2, meta, cnt = _outproj(x2, y_pool, y_attn, w_out[0], g_ffn[0].reshape(1, d), wr_t, br_t)

    eidx = meta[META_EIDX:META_EIDX + TOP_K].astype(I32)
    rank = meta[META_RANK:META_RANK + TOP_K].astype(I32)
    gates = meta[META_GATE:META_GATE + TOP_K].T
    counts = cnt[:, 0].astype(I32)
    padded = ((counts + MOE_TILE - 1) // MOE_TILE) * MOE_TILE
    ends = jnp.cumsum(padded)
    offsets = ends - padded
    onehot = eidx[..., None] == jnp.arange(N_EXPERTS, dtype=I32)
    pos = rank + jnp.sum(jnp.where(onehot, offsets, 0), axis=-1)
    p_max = t * TOP_K + N_EXPERTS * MOE_TILE
    n_tiles = p_max // MOE_TILE
    tile_start = jnp.arange(n_tiles, dtype=I32) * MOE_TILE
    tile_expert = jnp.minimum(
        jnp.sum((tile_start[:, None] >= ends[None, :]).astype(I32), axis=1), N_EXPERTS - 1)
    tile_rows = jnp.clip((offsets + counts)[tile_expert] - tile_start, 0, MOE_TILE)
    n_valid = (ends[-1] // MOE_TILE).reshape(1).astype(I32)
    e_ids = jnp.arange(N_EXPERTS, dtype=I32)
    later = (e_ids[None, :] > e_ids[:, None]) & (counts[None, :] > 0)
    next_nonempty = jnp.min(jnp.where(later, e_ids[None, :], N_EXPERTS), axis=1)
    next_expert = jnp.where(next_nonempty < N_EXPERTS, next_nonempty, -1)[tile_expert]

    xs = _sc_dispatch(h2, [pos[k:k + 1] for k in range(TOP_K)], p_max)
    ys = _moe(tile_expert, n_valid, tile_rows, next_expert, xs, w_gu[0],
              b_gu[0].reshape(N_EXPERTS, 1, -1), w_down[0], b_down[0].reshape(N_EXPERTS, 1, -1))
    yk = _sc_unpermute(ys, pos.reshape(1, TOP_K * t))
    yk = [q.reshape(TOP_K, t, D_PART) for q in yk]
    out = _combine(x1, gates, g_final.reshape(1, d), yk)
    return out.reshape(batch, seq, d)
```

```python
import functools

import jax
import jax.numpy as jnp
from jax import lax
from jax.experimental import pallas as pl
from jax.experimental.pallas import tpu as pltpu
from jax.experimental.pallas import tpu_sc as plsc

F32 = jnp.float32
BF16 = jnp.bfloat16
I32 = jnp.int32

D_MODEL = 1024
D_POOL = 512
D_ATTN = 512
POOL_WINDOWS = (2, 4, 8, 16)
POOL_GROUP = 128
HEAD_DIM = 64
N_EXPERTS = 32
TOP_K = 4
SWIGLU_LIMIT = 7.0
SWIGLU_ALPHA = 1.702
EPS = 1e-5
NEG_INF = -1e30

LANES = 128
QB = 128
DIL_MID = 4
DIL_FAR = 16
ROW_TILE = 512
MOE_TILE = 512
COMBINE_TILE = 256
SC_WINDOW = 128
D_PACKED = D_MODEL // 2
N_PART = 2
D_PART = D_PACKED // N_PART


def _pack_row(x):
    hi = lax.bitcast_convert_type(x[:, :D_PACKED].astype(BF16).astype(F32), I32)
    lo = lax.bitcast_convert_type(x[:, D_PACKED:].astype(BF16).astype(F32), I32)
    return hi | lax.shift_right_logical(lo, 16)


def _unpack_row(w):
    hi = lax.bitcast_convert_type(w & jnp.int32(-65536), F32)
    lo = lax.bitcast_convert_type(lax.shift_left(w, 16), F32)
    return jnp.concatenate([hi, lo], axis=1)
VMEM_LIMIT = 56 * 1024 * 1024


def _rms(x, g):
    ms = jnp.mean(x * x, axis=-1, keepdims=True)
    return x * lax.rsqrt(ms + EPS) * g


def _inproj_kernel(x_ref, g_ref, w_ref, u_ref, qkv_ref, w_bf):
    @pl.when(pl.program_id(0) == 0)
    def _():
        w_bf[...] = w_ref[...].astype(BF16)

    h = _rms(x_ref[...], g_ref[...]).astype(BF16)
    proj = jnp.dot(h, w_bf[...], preferred_element_type=F32)
    u_ref[...] = proj[:, :D_POOL]
    q = proj[:, D_POOL:D_POOL + D_ATTN] * (HEAD_DIM ** -0.5)
    qkv_ref[:, :D_ATTN] = q.astype(BF16)
    qkv_ref[:, D_ATTN:] = proj[:, D_POOL + D_ATTN:].astype(BF16)


def _inproj(x2, g, w):
    t = x2.shape[0]
    d_in = w.shape[1]
    return pl.pallas_call(
        _inproj_kernel,
        grid=(t // ROW_TILE,),
        in_specs=[
            pl.BlockSpec((ROW_TILE, D_MODEL), lambda i: (i, 0)),
            pl.BlockSpec((1, D_MODEL), lambda i: (0, 0)),
            pl.BlockSpec((D_MODEL, d_in), lambda i: (0, 0)),
        ],
        out_specs=[
            pl.BlockSpec((ROW_TILE, D_POOL), lambda i: (i, 0)),
            pl.BlockSpec((ROW_TILE, 3 * D_ATTN), lambda i: (i, 0)),
        ],
        out_shape=[
            jax.ShapeDtypeStruct((t, D_POOL), F32),
            jax.ShapeDtypeStruct((t, 3 * D_ATTN), BF16),
        ],
        scratch_shapes=[pltpu.VMEM((D_MODEL, d_in), BF16)],
        compiler_params=pltpu.CompilerParams(
            dimension_semantics=("arbitrary",), vmem_limit_bytes=VMEM_LIMIT),
        name="inproj",
    )(x2, g, w)


POOL_PAD = 16


def _pool_kernel(u_ref, wp_ref, sc_ref, y_ref, pad_ref):
    s = u_ref.shape[0]
    row = lax.broadcasted_iota(I32, (s, 1), 0)
    pad_ref[0:POOL_PAD, :] = jnp.zeros((POOL_PAD, POOL_GROUP), F32)
    for g, w in enumerate(POOL_WINDOWS):
        lo, hi = g * POOL_GROUP, (g + 1) * POOL_GROUP
        e = u_ref[:, lo:hi]
        acc = e
        span = 1
        while span < w:
            pad_ref[POOL_PAD:, :] = acc
            acc = acc + pad_ref[pl.ds(POOL_PAD - span, s), :]
            span *= 2
        count = jnp.minimum(row + 1, w).astype(F32)
        pooled = acc / count - e
        y = jnp.dot(pooled.astype(BF16), wp_ref[g].astype(BF16), preferred_element_type=F32)
        y_ref[:, lo:hi] = (y * sc_ref[:, lo:hi]).astype(BF16)


def _pool(u, w_pool, pool_scale, batch, seq):
    return pl.pallas_call(
        _pool_kernel,
        grid=(batch,),
        in_specs=[
            pl.BlockSpec((seq, D_POOL), lambda b: (b, 0)),
            pl.BlockSpec((len(POOL_WINDOWS), POOL_GROUP, POOL_GROUP), lambda b: (0, 0, 0)),
            pl.BlockSpec((1, D_POOL), lambda b: (0, 0)),
        ],
        out_specs=pl.BlockSpec((seq, D_POOL), lambda b: (b, 0)),
        out_shape=jax.ShapeDtypeStruct((batch * seq, D_POOL), BF16),
        scratch_shapes=[pltpu.VMEM((seq + POOL_PAD, POOL_GROUP), F32)],
        compiler_params=pltpu.CompilerParams(
            dimension_semantics=("arbitrary",), vmem_limit_bytes=VMEM_LIMIT),
        name="pool",
    )(u, w_pool, pool_scale)


def _pair_block(q, k, v_ext, mask2, lane_h0):
    zero = jnp.zeros_like(q)
    q2 = jnp.concatenate([jnp.where(lane_h0, q, zero), jnp.where(lane_h0, zero, q)], axis=0)
    s = lax.dot_general(q2, k, (((1,), (1,)), ((), ())), preferred_element_type=F32)
    s = jnp.where(mask2, s, NEG_INF)
    m = jnp.max(s, axis=-1, keepdims=True)
    p = jnp.exp(s - m).astype(BF16)
    ol = jnp.dot(p, v_ext, preferred_element_type=F32)
    o = jnp.where(lane_h0, ol[:QB, :LANES], ol[QB:, :LANES])
    l = jnp.where(lane_h0, ol[:QB, LANES:], ol[QB:, LANES:])
    mb = jnp.where(lane_h0, m[:QB], m[QB:])
    return o, mb, l


def _merge(o_a, m_a, l_a, o_b, m_b, l_b):
    m = jnp.maximum(m_a, m_b)
    ea = jnp.exp(m_a - m)
    eb = jnp.exp(m_b - m)
    return o_a * ea + o_b * eb, m, l_a * ea + l_b * eb


def _both_heads(mask):
    return jnp.concatenate([mask, mask], axis=0)


def _attn_kernel(q_ref, k_ref, v_ref, o_ref, qp, kp, vpx, vnx, o23, m23, l23, on, mn):
    s = q_ref.shape[0]
    n_chunk = DIL_FAR
    rows = s // n_chunk
    sub = rows // DIL_MID
    grp = n_chunk * n_chunk
    lane_h0 = lax.broadcasted_iota(I32, (1, LANES), 1) < HEAD_DIM

    @pl.when((pl.program_id(0) == 0) & (pl.program_id(1) == 0))
    def _():
        ones = jnp.ones((s, LANES), BF16)
        vnx[:, LANES:] = ones
        vpx[:, LANES:] = ones

    vnx[:, :LANES] = v_ref[...]

    pi = lax.broadcasted_iota(I32, (grp, grp), 0)
    pj = lax.broadcasted_iota(I32, (grp, grp), 1)
    swap = (pj == (pi % n_chunk) * n_chunk + pi // n_chunk).astype(BF16)
    for src, dst in ((q_ref, qp), (k_ref, kp), (v_ref, vpx)):
        for g in range(s // grp):
            y = jnp.dot(swap, src[g * grp:(g + 1) * grp, :],
                        preferred_element_type=F32).astype(BF16)
            for r in range(n_chunk):
                dst[r * rows + g * n_chunk:r * rows + (g + 1) * n_chunk, 0:LANES] = (
                    y[r * n_chunk:(r + 1) * n_chunk])

    qi = lax.broadcasted_iota(I32, (QB, QB), 0)
    kj = lax.broadcasted_iota(I32, (QB, QB), 1)
    causal = _both_heads(qi >= kj)

    far_unroll = 4

    def far_body(it, carry):
        for u in range(far_unroll):
            r0 = pl.multiple_of((it * far_unroll + u) * rows, rows)
            blk = pl.ds(r0, rows)
            o, m, l = _pair_block(qp[blk, :], kp[blk, :], vpx[blk, :], causal, lane_h0)
            o23[blk, :] = o
            m23[blk, :] = m
            l23[blk, :] = l
        return carry

    lax.fori_loop(0, n_chunk // far_unroll, far_body, 0)

    def mid_index(n_key_sub):
        i_q = lax.broadcasted_iota(I32, (QB, DIL_MID * n_key_sub), 0)
        i_k = lax.broadcasted_iota(I32, (QB, DIL_MID * n_key_sub), 1)
        return i_q // sub, i_q % sub, i_k // n_key_sub, i_k % n_key_sub

    jq, aq, jk, ak = mid_index(sub)
    d0 = DIL_FAR * (aq - ak) + DIL_MID * (jq - jk)
    mask_mid0 = _both_heads(d0 >= 0)
    jq, aq, jk, ak = mid_index(2 * sub)
    d1 = DIL_FAR * (aq + sub - ak) + DIL_MID * (jq - jk)
    mask_mid = _both_heads((d1 >= 0) & (d1 <= DIL_MID * QB))

    def mid_tile(r4, a0, k0, nk, mask):
        def at(j, off, n):
            start = (DIL_MID * j + r4) * rows + off
            return pl.ds(start if isinstance(start, int) else pl.multiple_of(start, sub), n)

        q = jnp.concatenate([qp[at(j, a0, sub), :] for j in range(DIL_MID)], axis=0)
        k = jnp.concatenate([kp[at(j, k0, nk), :] for j in range(DIL_MID)], axis=0)
        v = jnp.concatenate([vpx[at(j, k0, nk), :] for j in range(DIL_MID)], axis=0)
        o, m, l = _pair_block(q, k, v, mask, lane_h0)
        for j in range(DIL_MID):
            dst = at(j, a0, sub)
            src = slice(j * sub, (j + 1) * sub)
            oo, mm, ll = _merge(o23[dst, :], m23[dst, :], l23[dst, :], o[src], m[src], l[src])
            o23[dst, :] = oo
            m23[dst, :] = mm
            l23[dst, :] = ll

    for r4 in range(DIL_MID):
        mid_tile(r4, 0, 0, sub, mask_mid0)

    def mid_body(a_blk, carry):
        a0 = pl.multiple_of(a_blk * sub, sub)
        for r4 in range(DIL_MID):
            mid_tile(r4, a0, a0 - sub, 2 * sub, mask_mid)
        return carry

    lax.fori_loop(1, rows // sub, mid_body, 0)

    for g in range(s // grp):
        slabs = []
        for r in range(n_chunk):
            blk = slice(r * rows + g * n_chunk, r * rows + (g + 1) * n_chunk)
            slabs.append((o23[blk, :] / l23[blk, :]).astype(BF16))
        on[g * grp:(g + 1) * grp, :] = jnp.dot(swap, jnp.concatenate(slabs, axis=0),
                                               preferred_element_type=F32)
    for r in range(n_chunk):
        blk = slice(r * rows, (r + 1) * rows)
        mn[pl.ds(r, rows, stride=n_chunk), :] = m23[blk, :] + jnp.log(l23[blk, :])

    qi2 = lax.broadcasted_iota(I32, (QB, 2 * QB), 0)
    kj2 = lax.broadcasted_iota(I32, (QB, 2 * QB), 1)
    dn = qi2 + QB - kj2
    mask_near = _both_heads((dn >= 0) & (dn <= QB))

    def near_finish(dst, o, m, l):
        oo, _, ll = _merge(on[dst, :], mn[dst, :], 1.0, o, m, l)
        o_ref[dst, :] = (oo / ll).astype(o_ref.dtype)

    first = pl.ds(0, QB)
    o, m, l = _pair_block(q_ref[first, :], k_ref[first, :], vnx[first, :], causal, lane_h0)
    near_finish(first, o, m, l)

    near_unroll = 3

    def near_body(it, carry):
        for u in range(near_unroll):
            r0 = pl.multiple_of((1 + it * near_unroll + u) * QB, QB)
            keys = pl.ds(r0 - QB, 2 * QB)
            o, m, l = _pair_block(q_ref[pl.ds(r0, QB), :], k_ref[keys, :], vnx[keys, :],
                                  mask_near, lane_h0)
            near_finish(pl.ds(r0, QB), o, m, l)
        return carry

    lax.fori_loop(0, (s // QB - 1) // near_unroll, near_body, 0)


def _attention(qkv, batch, seq):
    n_pair = D_ATTN // LANES
    blk = (seq, LANES)
    f32_scr = pltpu.VMEM(blk, F32)
    bf_scr = pltpu.VMEM(blk, BF16)
    bfx_scr = pltpu.VMEM((seq, 2 * LANES), BF16)
    return pl.pallas_call(
        _attn_kernel,
        grid=(batch, n_pair),
        in_specs=[
            pl.BlockSpec(blk, lambda b, h: (b, h)),
            pl.BlockSpec(blk, lambda b, h: (b, n_pair + h)),
            pl.BlockSpec(blk, lambda b, h: (b, 2 * n_pair + h)),
        ],
        out_specs=pl.BlockSpec(blk, lambda b, h: (b, h)),
        out_shape=jax.ShapeDtypeStruct((batch * seq, D_ATTN), BF16),
        scratch_shapes=[bf_scr, bf_scr, bfx_scr, bfx_scr,
                        f32_scr, f32_scr, f32_scr, f32_scr, f32_scr],
        compiler_params=pltpu.CompilerParams(
            dimension_semantics=("arbitrary", "arbitrary"), vmem_limit_bytes=VMEM_LIMIT),
        name="dilated_attn",
    )(qkv, qkv, qkv)


META_ROWS = 16
META_EIDX, META_GATE, META_RANK = 0, 4, 8


def _outproj_kernel(x_ref, yp_ref, ya_ref, wo_ref, g_ref, wrt_ref, brt_ref,
                    x1_ref, h2_ref, meta_ref, cnt_ref, wo_bf, before, carry):
    tm = x_ref.shape[0]

    @pl.when(pl.program_id(0) == 0)
    def _():
        wo_bf[...] = wo_ref[...].astype(BF16)
        carry[...] = jnp.zeros_like(carry)
        ti = lax.broadcasted_iota(I32, (tm, tm), 0)
        tj = lax.broadcasted_iota(I32, (tm, tm), 1)
        before[...] = (ti < tj).astype(BF16)

    x1 = (x_ref[...]
          + jnp.dot(yp_ref[...], wo_bf[:D_POOL, :], preferred_element_type=F32)
          + jnp.dot(ya_ref[...], wo_bf[D_POOL:, :], preferred_element_type=F32))
    x1_ref[...] = x1
    h2 = _rms(x1, g_ref[...])
    h2_ref[...] = _pack_row(h2)

    logits_t = lax.dot_general(wrt_ref[...].astype(BF16), h2.astype(BF16),
                               (((1,), (1,)), ((), ())), preferred_element_type=F32)
    logits_t = logits_t + brt_ref[:, 0:1]
    eid = lax.broadcasted_iota(I32, (N_EXPERTS, tm), 0)
    work = logits_t
    idxs, vals = [], []
    for _ in range(TOP_K):
        mx = jnp.max(work, axis=0, keepdims=True)
        idx = jnp.min(jnp.where(work == mx, eid, N_EXPERTS), axis=0, keepdims=True)
        idxs.append(idx)
        vals.append(mx)
        work = jnp.where(eid == idx, -jnp.inf, work)
    exps = [jnp.exp(v - vals[0]) for v in vals]
    den = exps[0] + exps[1] + exps[2] + exps[3]

    onehot = jnp.zeros((N_EXPERTS, tm), F32)
    for idx in idxs:
        onehot = onehot + (eid == idx).astype(F32)
    rank_e = carry[:, 0:1] + jnp.dot(onehot.astype(BF16), before[...],
                                     preferred_element_type=F32)
    carry[...] = carry[...] + jnp.sum(onehot, axis=1, keepdims=True)
    cnt_ref[...] = carry[...]

    mrow = lax.broadcasted_iota(I32, (META_ROWS, tm), 0)
    meta = jnp.zeros((META_ROWS, tm), F32)
    for k in range(TOP_K):
        rank_k = jnp.sum(jnp.where(eid == idxs[k], rank_e, 0.0), axis=0, keepdims=True)
        meta = jnp.where(mrow == META_EIDX + k, idxs[k].astype(F32), meta)
        meta = jnp.where(mrow == META_GATE + k, exps[k] / den, meta)
        meta = jnp.where(mrow == META_RANK + k, rank_k, meta)
    meta_ref[...] = meta


def _outproj(x2, y_pool, y_attn, w_out, g, w_router_t, b_router_t):
    t = x2.shape[0]
    row = lambda i: (i, 0)
    const = lambda i: (0, 0)
    return pl.pallas_call(
        _outproj_kernel,
        grid=(t // ROW_TILE,),
        in_specs=[
            pl.BlockSpec((ROW_TILE, D_MODEL), row),
            pl.BlockSpec((ROW_TILE, D_POOL), row),
            pl.BlockSpec((ROW_TILE, D_ATTN), row),
            pl.BlockSpec((D_MODEL, D_MODEL), const),
            pl.BlockSpec((1, D_MODEL), const),
            pl.BlockSpec((N_EXPERTS, D_MODEL), const),
            pl.BlockSpec((N_EXPERTS, LANES), const),
        ],
        out_specs=[
            pl.BlockSpec((ROW_TILE, D_MODEL), row),
            pl.BlockSpec((ROW_TILE, D_PACKED), row),
            pl.BlockSpec((META_ROWS, ROW_TILE), lambda i: (0, i)),
            pl.BlockSpec((N_EXPERTS, LANES), const),
        ],
        out_shape=[
            jax.ShapeDtypeStruct((t, D_MODEL), F32),
            jax.ShapeDtypeStruct((t, D_PACKED), I32),
            jax.ShapeDtypeStruct((META_ROWS, t), F32),
            jax.ShapeDtypeStruct((N_EXPERTS, LANES), F32),
        ],
        scratch_shapes=[pltpu.VMEM((D_MODEL, D_MODEL), BF16),
                        pltpu.VMEM((ROW_TILE, ROW_TILE), BF16),
                        pltpu.VMEM((N_EXPERTS, LANES), F32)],
        compiler_params=pltpu.CompilerParams(
            dimension_semantics=("arbitrary",), vmem_limit_bytes=VMEM_LIMIT),
        name="outproj_router",
    )(x2, y_pool, y_attn, w_out, g, w_router_t, b_router_t)


def _sc_mesh():
    return plsc.VectorSubcoreMesh(core_axis_name="core", subcore_axis_name="subcore")


def _sc_dispatch(h2, pos_rows, p_max):
    t = h2.shape[0]

    @functools.partial(
        pl.kernel, mesh=_sc_mesh(),
        out_type=[jax.ShapeDtypeStruct((p_max, D_PART), h2.dtype)] * N_PART)
    def run(h_hbm, *refs):
        pos_hbm, xs_hbm = refs[:TOP_K], refs[TOP_K:]
        for c in range(N_PART):
            def body(x_vmem, *idx_vmem, dst=xs_hbm[c]):
                for iv in idx_vmem:
                    pltpu.sync_copy(x_vmem, dst.at[iv.at[0]])

            pltpu.emit_pipeline(
                body,
                grid=(t // SC_WINDOW,),
                in_specs=[pl.BlockSpec((SC_WINDOW, D_PART), lambda i, c=c: (i, c))]
                + [pl.BlockSpec((1, SC_WINDOW), lambda i: (0, i))] * TOP_K,
                out_specs=[],
                core_axis_name=("core", "subcore"),
                dimension_semantics=(pltpu.PARALLEL,),
            )(h_hbm, *pos_hbm)

    return run(h2, *pos_rows)


def _sc_unpermute(y_quarters, idx_row):
    n = idx_row.shape[1]

    @functools.partial(
        pl.kernel, mesh=_sc_mesh(),
        out_type=[jax.ShapeDtypeStruct((n, D_PART), y_quarters[0].dtype)] * N_PART)
    def run(*refs):
        y_hbm, i_hbm, o_hbm = refs[:N_PART], refs[N_PART], refs[N_PART + 1:]
        for c in range(N_PART):
            def body(i_vmem, o_vmem, src=y_hbm[c]):
                pltpu.sync_copy(src.at[i_vmem.at[0]], o_vmem)

            pltpu.emit_pipeline(
                body,
                grid=(n // SC_WINDOW,),
                in_specs=[pl.BlockSpec((1, SC_WINDOW), lambda i: (0, i))],
                out_specs=[pl.BlockSpec((SC_WINDOW, D_PART), lambda i: (i, 0))],
                core_axis_name=("core", "subcore"),
                dimension_semantics=(pltpu.PARALLEL,),
            )(i_hbm, o_hbm[c])

    return run(*y_quarters, idx_row)


def _moe_kernel(te_ref, nv_ref, rows_ref, nxt_ref, *refs):
    xs_refs = refs[:N_PART]
    bgu_ref, bd_ref, wgu_hbm, wd_hbm = refs[N_PART:N_PART + 4]
    y_refs = refs[N_PART + 4:2 * N_PART + 4]
    wgu_f32, wd_f32, wgu_bf, wd_bf, sem = refs[2 * N_PART + 4:]
    i = pl.program_id(0)

    def weight_copies(e):
        return (pltpu.make_async_copy(wgu_hbm.at[e], wgu_f32, sem.at[0]),
                pltpu.make_async_copy(wd_hbm.at[e], wd_f32, sem.at[1]))

    @pl.when(i < nv_ref[0])
    def _():
        prev = te_ref[jnp.maximum(i - 1, 0)]
        new_expert = jnp.logical_or(i == 0, te_ref[i] != prev)

        @pl.when(i == 0)
        def _():
            for cp in weight_copies(te_ref[0]):
                cp.start()

        @pl.when(new_expert)
        def _():
            for cp in weight_copies(te_ref[i]):
                cp.wait()
            wgu_bf[...] = wgu_f32[...].astype(BF16)
            wd_bf[...] = wd_f32[...].astype(BF16)

            @pl.when(nxt_ref[i] >= 0)
            def _():
                for cp in weight_copies(nxt_ref[i]):
                    cp.start()

        d_e = wd_bf.shape[0]
        x = _unpack_row(jnp.concatenate([r[...] for r in xs_refs], axis=1))
        rid = lax.broadcasted_iota(I32, (MOE_TILE, 1), 0)
        x = jnp.where(rid < rows_ref[i], x, 0.0).astype(BF16)
        gu = jnp.dot(x, wgu_bf[...], preferred_element_type=F32) + bgu_ref[0]
        gate = jnp.minimum(gu[:, :d_e], SWIGLU_LIMIT)
        lin = jnp.clip(gu[:, d_e:], -SWIGLU_LIMIT, SWIGLU_LIMIT)
        act = gate * jax.nn.sigmoid(SWIGLU_ALPHA * gate) * (lin + 1.0)
        y = jnp.dot(act.astype(BF16), wd_bf[...], preferred_element_type=F32) + bd_ref[0]
        yp = _pack_row(y)
        for c, y_ref in enumerate(y_refs):
            y_ref[...] = yp[:, c * D_PART:(c + 1) * D_PART]

    @pl.when(i >= nv_ref[0])
    def _():
        for y_ref in y_refs:
            y_ref[...] = jnp.zeros_like(y_ref)


def _moe(tile_expert, n_valid, tile_rows, next_expert, xs_parts, w_gu, b_gu, w_down, b_down):
    p_max = xs_parts[0].shape[0]
    n_tiles = p_max // MOE_TILE
    d_e = w_down.shape[1]

    def row(i, te, nv, tr, nx):
        return (jnp.minimum(i, nv[0] - 1), 0)

    def expert(i, te, nv, tr, nx):
        return (te[jnp.minimum(i, nv[0] - 1)], 0, 0)

    grid_spec = pltpu.PrefetchScalarGridSpec(
        num_scalar_prefetch=4,
        grid=(n_tiles,),
        in_specs=[pl.BlockSpec((MOE_TILE, D_PART), row)] * N_PART + [
            pl.BlockSpec((1, 1, 2 * d_e), expert),
            pl.BlockSpec((1, 1, D_MODEL), expert),
            pl.BlockSpec(memory_space=pl.ANY),
            pl.BlockSpec(memory_space=pl.ANY),
        ],
        out_specs=[pl.BlockSpec((MOE_TILE, D_PART), lambda i, te, nv, tr, nx: (i, 0))] * N_PART,
        scratch_shapes=[pltpu.VMEM((D_MODEL, 2 * d_e), F32), pltpu.VMEM((d_e, D_MODEL), F32),
                        pltpu.VMEM((D_MODEL, 2 * d_e), BF16), pltpu.VMEM((d_e, D_MODEL), BF16),
                        pltpu.SemaphoreType.DMA((2,))],
    )
    return pl.pallas_call(
        _moe_kernel,
        grid_spec=grid_spec,
        out_shape=[jax.ShapeDtypeStruct((p_max, D_PART), I32)] * N_PART,
        compiler_params=pltpu.CompilerParams(
            dimension_semantics=("arbitrary",), vmem_limit_bytes=VMEM_LIMIT),
        name="moe_ffn",
    )(tile_expert, n_valid, tile_rows, next_expert, *xs_parts, b_gu, b_down, w_gu, w_down)


def _combine_kernel(x1_ref, gate_ref, g_ref, *refs):
    yk_refs, o_ref = refs[:N_PART], refs[N_PART]
    acc = x1_ref[...]
    for k in range(TOP_K):
        y_k = _unpack_row(jnp.concatenate([r[k] for r in yk_refs], axis=1))
        acc = acc + gate_ref[:, k:k + 1] * y_k
    o_ref[...] = _rms(acc, g_ref[...])


def _combine(x1, gates, g, yk_quarters):
    t, d = x1.shape
    tq = COMBINE_TILE
    row = lambda i: (i, 0)
    return pl.pallas_call(
        _combine_kernel,
        grid=(t // tq,),
        in_specs=[
            pl.BlockSpec((tq, d), row),
            pl.BlockSpec((tq, TOP_K), row),
            pl.BlockSpec((1, d), lambda i: (0, 0)),
        ] + [pl.BlockSpec((TOP_K, tq, D_PART), lambda i: (0, i, 0))] * N_PART,
        out_specs=pl.BlockSpec((tq, d), row),
        out_shape=jax.ShapeDtypeStruct((t, d), F32),
        compiler_params=pltpu.CompilerParams(
            dimension_semantics=("arbitrary",), vmem_limit_bytes=VMEM_LIMIT),
        name="combine_final",
    )(x1, gates, g, *yk_quarters)


def kernel(x, g_mix, w_in, w_pool, pool_scale, w_out, g_ffn, w_router, b_router,
           w_gu, b_gu, w_down, b_down, g_final):
    batch, seq, d = x.shape
    t = batch * seq
    x2 = x.reshape(t, d)

    u, qkv = _inproj(x2, g_mix[0].reshape(1, d), w_in[0])
    y_pool = _pool(u, w_pool[0], pool_scale[0].reshape(1, D_POOL), batch, seq)
    y_attn = _attention(qkv, batch, seq)

    wr_t = w_router[0].T
    br_t = jnp.broadcast_to(b_router[0].reshape(N_EXPERTS, 1), (N_EXPERTS, LANES))
    x1, h2, meta, cnt = _outproj(x2, y_pool, y_attn, w_out[0], g_ffn[0].reshape(1, d), wr_t, br_t)

    eidx = meta[META_EIDX:META_EIDX + TOP_K].astype(I32)
    rank = meta[META_RANK:META_RANK + TOP_K].astype(I32)
    gates = meta[META_GATE:META_GATE + TOP_K].T
    counts = cnt[:, 0].astype(I32)
    padded = ((counts + MOE_TILE - 1) // MOE_TILE) * MOE_TILE
    ends = jnp.cumsum(padded)
    offsets = ends - padded
    onehot = eidx[..., None] == jnp.arange(N_EXPERTS, dtype=I32)
    pos = rank + jnp.sum(jnp.where(onehot, offsets, 0), axis=-1)
    p_max = t * TOP_K + N_EXPERTS * MOE_TILE
    n_tiles = p_max // MOE_TILE
    tile_start = jnp.arange(n_tiles, dtype=I32) * MOE_TILE
    tile_expert = jnp.minimum(
        jnp.sum((tile_start[:, None] >= ends[None, :]).astype(I32), axis=1), N_EXPERTS - 1)
    tile_rows = jnp.clip((offsets + counts)[tile_expert] - tile_start, 0, MOE_TILE)
    n_valid = (ends[-1] // MOE_TILE).reshape(1).astype(I32)
    e_ids = jnp.arange(N_EXPERTS, dtype=I32)
    later = (e_ids[None, :] > e_ids[:, None]) & (counts[None, :] > 0)
    next_nonempty = jnp.min(jnp.where(later, e_ids[None, :], N_EXPERTS), axis=1)
    next_expert = jnp.where(next_nonempty < N_EXPERTS, next_nonempty, -1)[tile_expert]

    xs = _sc_dispatch(h2, [pos[k:k + 1] for k in range(TOP_K)], p_max)
    ys = _moe(tile_expert, n_valid, tile_rows, next_expert, xs, w_gu[0],
              b_gu[0].reshape(N_EXPERTS, 1, -1), w_down[0], b_down[0].reshape(N_EXPERTS, 1, -1))
    yk = _sc_unpermute(ys, pos.reshape(1, TOP_K * t))
    yk = [q.reshape(TOP_K, t, D_PART) for q in yk]
    out = _combine(x1, gates, g_final.reshape(1, d), yk)
    return out.reshape(batch, seq, d)
```

```python
import functools

import jax
import jax.numpy as jnp
from jax import lax
from jax.experimental import pallas as pl
from jax.experimental.pallas import tpu as pltpu
from jax.experimental.pallas import tpu_sc as plsc

F32 = jnp.float32
BF16 = jnp.bfloat16
I32 = jnp.int32

D_MODEL = 1024
D_POOL = 512
D_ATTN = 512
POOL_WINDOWS = (2, 4, 8, 16)
POOL_GROUP = 128
HEAD_DIM = 64
N_EXPERTS = 32
TOP_K = 4
SWIGLU_LIMIT = 7.0
SWIGLU_ALPHA = 1.702
EPS = 1e-5
NEG_INF = -1e30
LOG2_E = 1.4426950408889634

LANES = 128
QB = 128
DIL_MID = 4
DIL_FAR = 16
ROW_TILE = 512
MOE_TILE = 512
COMBINE_TILE = 256
SC_WINDOW = 128
D_PACKED = D_MODEL // 2
N_PART = 2
D_PART = D_PACKED // N_PART


def _pack_row(x):
    hi = lax.bitcast_convert_type(x[:, :D_PACKED].astype(BF16).astype(F32), I32)
    lo = lax.bitcast_convert_type(x[:, D_PACKED:].astype(BF16).astype(F32), I32)
    return hi | lax.shift_right_logical(lo, 16)


def _unpack_row(w):
    hi = lax.bitcast_convert_type(w & jnp.int32(-65536), F32)
    lo = lax.bitcast_convert_type(lax.shift_left(w, 16), F32)
    return jnp.concatenate([hi, lo], axis=1)
VMEM_LIMIT = 56 * 1024 * 1024


def _rms(x, g):
    ms = jnp.mean(x * x, axis=-1, keepdims=True)
    return x * lax.rsqrt(ms + EPS) * g


def _inproj_kernel(x_ref, g_ref, w_ref, u_ref, qkv_ref, w_bf):
    @pl.when(pl.program_id(0) == 0)
    def _():
        w_bf[...] = w_ref[...].astype(BF16)

    h = _rms(x_ref[...], g_ref[...]).astype(BF16)
    proj = jnp.dot(h, w_bf[...], preferred_element_type=F32)
    u_ref[...] = proj[:, :D_POOL]
    q = proj[:, D_POOL:D_POOL + D_ATTN] * (HEAD_DIM ** -0.5 * LOG2_E)
    qkv_ref[:, :D_ATTN] = q.astype(BF16)
    qkv_ref[:, D_ATTN:] = proj[:, D_POOL + D_ATTN:].astype(BF16)


def _inproj(x2, g, w):
    t = x2.shape[0]
    d_in = w.shape[1]
    return pl.pallas_call(
        _inproj_kernel,
        grid=(t // ROW_TILE,),
        in_specs=[
            pl.BlockSpec((ROW_TILE, D_MODEL), lambda i: (i, 0)),
            pl.BlockSpec((1, D_MODEL), lambda i: (0, 0)),
            pl.BlockSpec((D_MODEL, d_in), lambda i: (0, 0)),
        ],
        out_specs=[
            pl.BlockSpec((ROW_TILE, D_POOL), lambda i: (i, 0)),
            pl.BlockSpec((ROW_TILE, 3 * D_ATTN), lambda i: (i, 0)),
        ],
        out_shape=[
            jax.ShapeDtypeStruct((t, D_POOL), F32),
            jax.ShapeDtypeStruct((t, 3 * D_ATTN), BF16),
        ],
        scratch_shapes=[pltpu.VMEM((D_MODEL, d_in), BF16)],
        compiler_params=pltpu.CompilerParams(
            dimension_semantics=("arbitrary",), vmem_limit_bytes=VMEM_LIMIT),
        name="inproj",
    )(x2, g, w)


POOL_PAD = 16


def _pool_kernel(u_ref, wp_ref, sc_ref, y_ref, pad_ref):
    s = u_ref.shape[0]
    row = lax.broadcasted_iota(I32, (s, 1), 0)
    pad_ref[0:POOL_PAD, :] = jnp.zeros((POOL_PAD, POOL_GROUP), F32)
    for g, w in enumerate(POOL_WINDOWS):
        lo, hi = g * POOL_GROUP, (g + 1) * POOL_GROUP
        e = u_ref[:, lo:hi]
        acc = e
        span = 1
        while span < w:
            pad_ref[POOL_PAD:, :] = acc
            acc = acc + pad_ref[pl.ds(POOL_PAD - span, s), :]
            span *= 2
        count = jnp.minimum(row + 1, w).astype(F32)
        pooled = acc / count - e
        y = jnp.dot(pooled.astype(BF16), wp_ref[g].astype(BF16), preferred_element_type=F32)
        y_ref[:, lo:hi] = (y * sc_ref[:, lo:hi]).astype(BF16)


def _pool(u, w_pool, pool_scale, batch, seq):
    return pl.pallas_call(
        _pool_kernel,
        grid=(batch,),
        in_specs=[
            pl.BlockSpec((seq, D_POOL), lambda b: (b, 0)),
            pl.BlockSpec((len(POOL_WINDOWS), POOL_GROUP, POOL_GROUP), lambda b: (0, 0, 0)),
            pl.BlockSpec((1, D_POOL), lambda b: (0, 0)),
        ],
        out_specs=pl.BlockSpec((seq, D_POOL), lambda b: (b, 0)),
        out_shape=jax.ShapeDtypeStruct((batch * seq, D_POOL), BF16),
        scratch_shapes=[pltpu.VMEM((seq + POOL_PAD, POOL_GROUP), F32)],
        compiler_params=pltpu.CompilerParams(
            dimension_semantics=("arbitrary",), vmem_limit_bytes=VMEM_LIMIT),
        name="pool",
    )(u, w_pool, pool_scale)


def _pair_block(q, k, v_ext, mask2, lane_h0):
    zero = jnp.zeros_like(q)
    q2 = jnp.concatenate([jnp.where(lane_h0, q, zero), jnp.where(lane_h0, zero, q)], axis=0)
    s = lax.dot_general(q2, k, (((1,), (1,)), ((), ())), preferred_element_type=F32)
    s = jnp.where(mask2, s, NEG_INF)
    m = jnp.max(s, axis=-1, keepdims=True)
    p = jnp.exp2(s - m).astype(BF16)
    ol = jnp.dot(p, v_ext, preferred_element_type=F32)
    o = jnp.where(lane_h0, ol[:QB, :LANES], ol[QB:, :LANES])
    l = jnp.where(lane_h0, ol[:QB, LANES:], ol[QB:, LANES:])
    mb = jnp.where(lane_h0, m[:QB], m[QB:])
    return o, mb, l


def _merge(o_a, m_a, l_a, o_b, m_b, l_b):
    m = jnp.maximum(m_a, m_b)
    ea = jnp.exp2(m_a - m)
    eb = jnp.exp2(m_b - m)
    return o_a * ea + o_b * eb, m, l_a * ea + l_b * eb


def _both_heads(mask):
    return jnp.concatenate([mask, mask], axis=0)


def _attn_kernel(q_ref, k_ref, v_ref, o_ref, qp, kp, vpx, vnx, o23, m23, l23, on, mn):
    s = q_ref.shape[0]
    n_chunk = DIL_FAR
    rows = s // n_chunk
    sub = rows // DIL_MID
    grp = n_chunk * n_chunk
    lane_h0 = lax.broadcasted_iota(I32, (1, LANES), 1) < HEAD_DIM

    @pl.when((pl.program_id(0) == 0) & (pl.program_id(1) == 0))
    def _():
        ones = jnp.ones((s, LANES), BF16)
        vnx[:, LANES:] = ones
        vpx[:, LANES:] = ones

    vnx[:, :LANES] = v_ref[...]

    pi = lax.broadcasted_iota(I32, (grp, grp), 0)
    pj = lax.broadcasted_iota(I32, (grp, grp), 1)
    swap = (pj == (pi % n_chunk) * n_chunk + pi // n_chunk).astype(BF16)
    for src, dst in ((q_ref, qp), (k_ref, kp), (v_ref, vpx)):
        for g in range(s // grp):
            y = jnp.dot(swap, src[g * grp:(g + 1) * grp, :],
                        preferred_element_type=F32).astype(BF16)
            for r in range(n_chunk):
                dst[r * rows + g * n_chunk:r * rows + (g + 1) * n_chunk, 0:LANES] = (
                    y[r * n_chunk:(r + 1) * n_chunk])

    qi = lax.broadcasted_iota(I32, (QB, QB), 0)
    kj = lax.broadcasted_iota(I32, (QB, QB), 1)
    causal = _both_heads(qi >= kj)

    far_unroll = 16

    def far_body(it, carry):
        for u in range(far_unroll):
            r0 = pl.multiple_of((it * far_unroll + u) * rows, rows)
            blk = pl.ds(r0, rows)
            o, m, l = _pair_block(qp[blk, :], kp[blk, :], vpx[blk, :], causal, lane_h0)
            o23[blk, :] = o
            m23[blk, :] = m
            l23[blk, :] = l
        return carry

    lax.fori_loop(0, n_chunk // far_unroll, far_body, 0)

    def mid_index(n_key_sub):
        i_q = lax.broadcasted_iota(I32, (QB, DIL_MID * n_key_sub), 0)
        i_k = lax.broadcasted_iota(I32, (QB, DIL_MID * n_key_sub), 1)
        return i_q // sub, i_q % sub, i_k // n_key_sub, i_k % n_key_sub

    jq, aq, jk, ak = mid_index(sub)
    d0 = DIL_FAR * (aq - ak) + DIL_MID * (jq - jk)
    mask_mid0 = _both_heads(d0 >= 0)
    jq, aq, jk, ak = mid_index(2 * sub)
    d1 = DIL_FAR * (aq + sub - ak) + DIL_MID * (jq - jk)
    mask_mid = _both_heads((d1 >= 0) & (d1 <= DIL_MID * QB))

    def mid_tile(r4, a0, k0, nk, mask):
        def at(j, off, n):
            start = (DIL_MID * j + r4) * rows + off
            return pl.ds(start if isinstance(start, int) else pl.multiple_of(start, sub), n)

        q = jnp.concatenate([qp[at(j, a0, sub), :] for j in range(DIL_MID)], axis=0)
        k = jnp.concatenate([kp[at(j, k0, nk), :] for j in range(DIL_MID)], axis=0)
        v = jnp.concatenate([vpx[at(j, k0, nk), :] for j in range(DIL_MID)], axis=0)
        o, m, l = _pair_block(q, k, v, mask, lane_h0)
        for j in range(DIL_MID):
            dst = at(j, a0, sub)
            src = slice(j * sub, (j + 1) * sub)
            oo, mm, ll = _merge(o23[dst, :], m23[dst, :], l23[dst, :], o[src], m[src], l[src])
            o23[dst, :] = oo
            m23[dst, :] = mm
            l23[dst, :] = ll

    for r4 in range(DIL_MID):
        mid_tile(r4, 0, 0, sub, mask_mid0)

    def mid_body(a_blk, carry):
        a0 = pl.multiple_of(a_blk * sub, sub)
        for r4 in range(DIL_MID):
            mid_tile(r4, a0, a0 - sub, 2 * sub, mask_mid)
        return carry

    lax.fori_loop(1, rows // sub, mid_body, 0, unroll=True)

    for g in range(s // grp):
        slabs = []
        for r in range(n_chunk):
            blk = slice(r * rows + g * n_chunk, r * rows + (g + 1) * n_chunk)
            slabs.append((o23[blk, :] / l23[blk, :]).astype(BF16))
        on[g * grp:(g + 1) * grp, :] = jnp.dot(swap, jnp.concatenate(slabs, axis=0),
                                               preferred_element_type=F32)
    for r in range(n_chunk):
        blk = slice(r * rows, (r + 1) * rows)
        mn[pl.ds(r, rows, stride=n_chunk), :] = m23[blk, :] + jnp.log2(l23[blk, :])

    qi2 = lax.broadcasted_iota(I32, (QB, 2 * QB), 0)
    kj2 = lax.broadcasted_iota(I32, (QB, 2 * QB), 1)
    dn = qi2 + QB - kj2
    mask_near = _both_heads((dn >= 0) & (dn <= QB))

    def near_finish(dst, o, m, l):
        oo, _, ll = _merge(on[dst, :], mn[dst, :], 1.0, o, m, l)
        o_ref[dst, :] = (oo / ll).astype(o_ref.dtype)

    first = pl.ds(0, QB)
    o, m, l = _pair_block(q_ref[first, :], k_ref[first, :], vnx[first, :], causal, lane_h0)
    near_finish(first, o, m, l)

    near_unroll = 15

    def near_body(it, carry):
        for u in range(near_unroll):
            r0 = pl.multiple_of((1 + it * near_unroll + u) * QB, QB)
            keys = pl.ds(r0 - QB, 2 * QB)
            o, m, l = _pair_block(q_ref[pl.ds(r0, QB), :], k_ref[keys, :], vnx[keys, :],
                                  mask_near, lane_h0)
            near_finish(pl.ds(r0, QB), o, m, l)
        return carry

    lax.fori_loop(0, (s // QB - 1) // near_unroll, near_body, 0)


def _attention(qkv, batch, seq):
    n_pair = D_ATTN // LANES
    blk = (seq, LANES)
    f32_scr = pltpu.VMEM(blk, F32)
    bf_scr = pltpu.VMEM(blk, BF16)
    bfx_scr = pltpu.VMEM((seq, 2 * LANES), BF16)
    return pl.pallas_call(
        _attn_kernel,
        grid=(batch, n_pair),
        in_specs=[
            pl.BlockSpec(blk, lambda b, h: (b, h)),
            pl.BlockSpec(blk, lambda b, h: (b, n_pair + h)),
            pl.BlockSpec(blk, lambda b, h: (b, 2 * n_pair + h)),
        ],
        out_specs=pl.BlockSpec(blk, lambda b, h: (b, h)),
        out_shape=jax.ShapeDtypeStruct((batch * seq, D_ATTN), BF16),
        scratch_shapes=[bf_scr, bf_scr, bfx_scr, bfx_scr,
                        f32_scr, f32_scr, f32_scr, f32_scr, f32_scr],
        compiler_params=pltpu.CompilerParams(
            dimension_semantics=("arbitrary", "arbitrary"), vmem_limit_bytes=VMEM_LIMIT),
        name="dilated_attn",
    )(qkv, qkv, qkv)


META_ROWS = 16
META_EIDX, META_GATE, META_RANK = 0, 4, 8


def _outproj_kernel(x_ref, yp_ref, ya_ref, wo_ref, g_ref, wrt_ref, brt_ref,
                    x1_ref, h2_ref, meta_ref, cnt_ref, wo_bf, before, carry):
    tm = x_ref.shape[0]

    @pl.when(pl.program_id(0) == 0)
    def _():
        wo_bf[...] = wo_ref[...].astype(BF16)
        carry[...] = jnp.zeros_like(carry)
        ti = lax.broadcasted_iota(I32, (tm, tm), 0)
        tj = lax.broadcasted_iota(I32, (tm, tm), 1)
        before[...] = (ti < tj).astype(BF16)

    x1 = (x_ref[...]
          + jnp.dot(yp_ref[...], wo_bf[:D_POOL, :], preferred_element_type=F32)
          + jnp.dot(ya_ref[...], wo_bf[D_POOL:, :], preferred_element_type=F32))
    x1_ref[...] = x1
    h2 = _rms(x1, g_ref[...])
    h2_ref[...] = _pack_row(h2)

    logits_t = lax.dot_general(wrt_ref[...].astype(BF16), h2.astype(BF16),
                               (((1,), (1,)), ((), ())), preferred_element_type=F32)
    logits_t = logits_t + brt_ref[:, 0:1]
    eid = lax.broadcasted_iota(I32, (N_EXPERTS, tm), 0)
    work = logits_t
    idxs, vals = [], []
    for _ in range(TOP_K):
        mx = jnp.max(work, axis=0, keepdims=True)
        idx = jnp.min(jnp.where(work == mx, eid, N_EXPERTS), axis=0, keepdims=True)
        idxs.append(idx)
        vals.append(mx)
        work = jnp.where(eid == idx, -jnp.inf, work)
    exps = [jnp.exp(v - vals[0]) for v in vals]
    den = exps[0] + exps[1] + exps[2] + exps[3]

    onehot = jnp.zeros((N_EXPERTS, tm), F32)
    for idx in idxs:
        onehot = onehot + (eid == idx).astype(F32)
    rank_e = carry[:, 0:1] + jnp.dot(onehot.astype(BF16), before[...],
                                     preferred_element_type=F32)
    carry[...] = carry[...] + jnp.sum(onehot, axis=1, keepdims=True)
    cnt_ref[...] = carry[...]

    mrow = lax.broadcasted_iota(I32, (META_ROWS, tm), 0)
    meta = jnp.zeros((META_ROWS, tm), F32)
    for k in range(TOP_K):
        rank_k = jnp.sum(jnp.where(eid == idxs[k], rank_e, 0.0), axis=0, keepdims=True)
        meta = jnp.where(mrow == META_EIDX + k, idxs[k].astype(F32), meta)
        meta = jnp.where(mrow == META_GATE + k, exps[k] / den, meta)
        meta = jnp.where(mrow == META_RANK + k, rank_k, meta)
    meta_ref[...] = meta


def _outproj(x2, y_pool, y_attn, w_out, g, w_router_t, b_router_t):
    t = x2.shape[0]
    row = lambda i: (i, 0)
    const = lambda i: (0, 0)
    return pl.pallas_call(
        _outproj_kernel,
        grid=(t // ROW_TILE,),
        in_specs=[
            pl.BlockSpec((ROW_TILE, D_MODEL), row),
            pl.BlockSpec((ROW_TILE, D_POOL), row),
            pl.BlockSpec((ROW_TILE, D_ATTN), row),
            pl.BlockSpec((D_MODEL, D_MODEL), const),
            pl.BlockSpec((1, D_MODEL), const),
            pl.BlockSpec((N_EXPERTS, D_MODEL), const),
            pl.BlockSpec((N_EXPERTS, LANES), const),
        ],
        out_specs=[
            pl.BlockSpec((ROW_TILE, D_MODEL), row),
            pl.BlockSpec((ROW_TILE, D_PACKED), row),
            pl.BlockSpec((META_ROWS, ROW_TILE), lambda i: (0, i)),
            pl.BlockSpec((N_EXPERTS, LANES), const),
        ],
        out_shape=[
            jax.ShapeDtypeStruct((t, D_MODEL), F32),
            jax.ShapeDtypeStruct((t, D_PACKED), I32),
            jax.ShapeDtypeStruct((META_ROWS, t), F32),
            jax.ShapeDtypeStruct((N_EXPERTS, LANES), F32),
        ],
        scratch_shapes=[pltpu.VMEM((D_MODEL, D_MODEL), BF16),
                        pltpu.VMEM((ROW_TILE, ROW_TILE), BF16),
                        pltpu.VMEM((N_EXPERTS, LANES), F32)],
        compiler_params=pltpu.CompilerParams(
            dimension_semantics=("arbitrary",), vmem_limit_bytes=VMEM_LIMIT),
        name="outproj_router",
    )(x2, y_pool, y_attn, w_out, g, w_router_t, b_router_t)


def _sc_mesh():
    return plsc.VectorSubcoreMesh(core_axis_name="core", subcore_axis_name="subcore")


def _sc_dispatch(h2, pos_rows, p_max):
    t = h2.shape[0]

    @functools.partial(
        pl.kernel, mesh=_sc_mesh(),
        out_type=[jax.ShapeDtypeStruct((p_max, D_PART), h2.dtype)] * N_PART)
    def run(h_hbm, *refs):
        pos_hbm, xs_hbm = refs[:TOP_K], refs[TOP_K:]
        for c in range(N_PART):
            def body(x_vmem, *idx_vmem, dst=xs_hbm[c]):
                for iv in idx_vmem:
                    pltpu.sync_copy(x_vmem, dst.at[iv.at[0]])

            pltpu.emit_pipeline(
                body,
                grid=(t // SC_WINDOW,),
                in_specs=[pl.BlockSpec((SC_WINDOW, D_PART), lambda i, c=c: (i, c))]
                + [pl.BlockSpec((1, SC_WINDOW), lambda i: (0, i))] * TOP_K,
                out_specs=[],
                core_axis_name=("core", "subcore"),
                dimension_semantics=(pltpu.PARALLEL,),
            )(h_hbm, *pos_hbm)

    return run(h2, *pos_rows)


def _sc_unpermute(y_quarters, idx_row):
    n = idx_row.shape[1]

    @functools.partial(
        pl.kernel, mesh=_sc_mesh(),
        out_type=[jax.ShapeDtypeStruct((n, D_PART), y_quarters[0].dtype)] * N_PART)
    def run(*refs):
        y_hbm, i_hbm, o_hbm = refs[:N_PART], refs[N_PART], refs[N_PART + 1:]
        for c in range(N_PART):
            def body(i_vmem, o_vmem, src=y_hbm[c]):
                pltpu.sync_copy(src.at[i_vmem.at[0]], o_vmem)

            pltpu.emit_pipeline(
                body,
                grid=(n // SC_WINDOW,),
                in_specs=[pl.BlockSpec((1, SC_WINDOW), lambda i: (0, i))],
                out_specs=[pl.BlockSpec((SC_WINDOW, D_PART), lambda i: (i, 0))],
                core_axis_name=("core", "subcore"),
                dimension_semantics=(pltpu.PARALLEL,),
            )(i_hbm, o_hbm[c])

    return run(*y_quarters, idx_row)


def _moe_kernel(te_ref, nv_ref, rows_ref, nxt_ref, *refs):
    xs_refs = refs[:N_PART]
    bgu_ref, bd_ref, wgu_hbm, wd_hbm = refs[N_PART:N_PART + 4]
    y_refs = refs[N_PART + 4:2 * N_PART + 4]
    wgu_f32, wd_f32, wgu_bf, wd_bf, sem = refs[2 * N_PART + 4:]
    i = pl.program_id(0)

    def weight_copies(e):
        return (pltpu.make_async_copy(wgu_hbm.at[e], wgu_f32, sem.at[0]),
                pltpu.make_async_copy(wd_hbm.at[e], wd_f32, sem.at[1]))

    @pl.when(i < nv_ref[0])
    def _():
        prev = te_ref[jnp.maximum(i - 1, 0)]
        new_expert = jnp.logical_or(i == 0, te_ref[i] != prev)

        @pl.when(i == 0)
        def _():
            for cp in weight_copies(te_ref[0]):
                cp.start()

        @pl.when(new_expert)
        def _():
            for cp in weight_copies(te_ref[i]):
                cp.wait()
            wgu_bf[...] = wgu_f32[...].astype(BF16)
            wd_bf[...] = wd_f32[...].astype(BF16)

            @pl.when(nxt_ref[i] >= 0)
            def _():
                for cp in weight_copies(nxt_ref[i]):
                    cp.start()

        d_e = wd_bf.shape[0]
        x = _unpack_row(jnp.concatenate([r[...] for r in xs_refs], axis=1))
        rid = lax.broadcasted_iota(I32, (MOE_TILE, 1), 0)
        x = jnp.where(rid < rows_ref[i], x, 0.0).astype(BF16)
        gu = jnp.dot(x, wgu_bf[...], preferred_element_type=F32) + bgu_ref[0]
        gate = jnp.minimum(gu[:, :d_e], SWIGLU_LIMIT)
        lin = jnp.clip(gu[:, d_e:], -SWIGLU_LIMIT, SWIGLU_LIMIT)
        act = gate * jax.nn.sigmoid(SWIGLU_ALPHA * gate) * (lin + 1.0)
        y = jnp.dot(act.astype(BF16), wd_bf[...], preferred_element_type=F32) + bd_ref[0]
        yp = _pack_row(y)
        for c, y_ref in enumerate(y_refs):
            y_ref[...] = yp[:, c * D_PART:(c + 1) * D_PART]

    @pl.when(i >= nv_ref[0])
    def _():
        for y_ref in y_refs:
            y_ref[...] = jnp.zeros_like(y_ref)


def _moe(tile_expert, n_valid, tile_rows, next_expert, xs_parts, w_gu, b_gu, w_down, b_down):
    p_max = xs_parts[0].shape[0]
    n_tiles = p_max // MOE_TILE
    d_e = w_down.shape[1]

    def row(i, te, nv, tr, nx):
        return (jnp.minimum(i, nv[0] - 1), 0)

    def expert(i, te, nv, tr, nx):
        return (te[jnp.minimum(i, nv[0] - 1)], 0, 0)

    grid_spec = pltpu.PrefetchScalarGridSpec(
        num_scalar_prefetch=4,
        grid=(n_tiles,),
        in_specs=[pl.BlockSpec((MOE_TILE, D_PART), row)] * N_PART + [
            pl.BlockSpec((1, 1, 2 * d_e), expert),
            pl.BlockSpec((1, 1, D_MODEL), expert),
            pl.BlockSpec(memory_space=pl.ANY),
            pl.BlockSpec(memory_space=pl.ANY),
        ],
        out_specs=[pl.BlockSpec((MOE_TILE, D_PART), lambda i, te, nv, tr, nx: (i, 0))] * N_PART,
        scratch_shapes=[pltpu.VMEM((D_MODEL, 2 * d_e), F32), pltpu.VMEM((d_e, D_MODEL), F32),
                        pltpu.VMEM((D_MODEL, 2 * d_e), BF16), pltpu.VMEM((d_e, D_MODEL), BF16),
                        pltpu.SemaphoreType.DMA((2,))],
    )
    return pl.pallas_call(
        _moe_kernel,
        grid_spec=grid_spec,
        out_shape=[jax.ShapeDtypeStruct((p_max, D_PART), I32)] * N_PART,
        compiler_params=pltpu.CompilerParams(
            dimension_semantics=("arbitrary",), vmem_limit_bytes=VMEM_LIMIT),
        name="moe_ffn",
    )(tile_expert, n_valid, tile_rows, next_expert, *xs_parts, b_gu, b_down, w_gu, w_down)


def _combine_kernel(x1_ref, gate_ref, g_ref, *refs):
    yk_refs, o_ref = refs[:N_PART], refs[N_PART]
    acc = x1_ref[...]
    for k in range(TOP_K):
        y_k = _unpack_row(jnp.concatenate([r[k] for r in yk_refs], axis=1))
        acc = acc + gate_ref[:, k:k + 1] * y_k
    o_ref[...] = _rms(acc, g_ref[...])


def _combine(x1, gates, g, yk_quarters):
    t, d = x1.shape
    tq = COMBINE_TILE
    row = lambda i: (i, 0)
    return pl.pallas_call(
        _combine_kernel,
        grid=(t // tq,),
        in_specs=[
            pl.BlockSpec((tq, d), row),
            pl.BlockSpec((tq, TOP_K), row),
            pl.BlockSpec((1, d), lambda i: (0, 0)),
        ] + [pl.BlockSpec((TOP_K, tq, D_PART), lambda i: (0, i, 0))] * N_PART,
        out_specs=pl.BlockSpec((tq, d), row),
        out_shape=jax.ShapeDtypeStruct((t, d), F32),
        compiler_params=pltpu.CompilerParams(
            dimension_semantics=("arbitrary",), vmem_limit_bytes=VMEM_LIMIT),
        name="combine_final",
    )(x1, gates, g, *yk_quarters)


def kernel(x, g_mix, w_in, w_pool, pool_scale, w_out, g_ffn, w_router, b_router,
           w_gu, b_gu, w_down, b_down, g_final):
    batch, seq, d = x.shape
    t = batch * seq
    x2 = x.reshape(t, d)

    u, qkv = _inproj(x2, g_mix[0].reshape(1, d), w_in[0])
    y_pool = _pool(u, w_pool[0], pool_scale[0].reshape(1, D_POOL), batch, seq)
    y_attn = _attention(qkv, batch, seq)

    wr_t = w_router[0].T
    br_t = jnp.broadcast_to(b_router[0].reshape(N_EXPERTS, 1), (N_EXPERTS, LANES))
    x1, h2, meta, cnt = _outproj(x2, y_pool, y_attn, w_out[0], g_ffn[0].reshape(1, d), wr_t, br_t)

    eidx = meta[META_EIDX:META_EIDX + TOP_K].astype(I32)
    rank = meta[META_RANK:META_RANK + TOP_K].astype(I32)
    gates = meta[META_GATE:META_GATE + TOP_K].T
    counts = cnt[:, 0].astype(I32)
    padded = ((counts + MOE_TILE - 1) // MOE_TILE) * MOE_TILE
    ends = jnp.cumsum(padded)
    offsets = ends - padded
    onehot = eidx[..., None] == jnp.arange(N_EXPERTS, dtype=I32)
    pos = rank + jnp.sum(jnp.where(onehot, offsets, 0), axis=-1)
    p_max = t * TOP_K + N_EXPERTS * MOE_TILE
    n_tiles = p_max // MOE_TILE
    tile_start = jnp.arange(n_tiles, dtype=I32) * MOE_TILE
    tile_expert = jnp.minimum(
        jnp.sum((tile_start[:, None] >= ends[None, :]).astype(I32), axis=1), N_EXPERTS - 1)
    tile_rows = jnp.clip((offsets + counts)[tile_expert] - tile_start, 0, MOE_TILE)
    n_valid = (ends[-1] // MOE_TILE).reshape(1).astype(I32)
    e_ids = jnp.arange(N_EXPERTS, dtype=I32)
    later = (e_ids[None, :] > e_ids[:, None]) & (counts[None, :] > 0)
    next_nonempty = jnp.min(jnp.where(later, e_ids[None, :], N_EXPERTS), axis=1)
    next_expert = jnp.where(next_nonempty < N_EXPERTS, next_nonempty, -1)[tile_expert]

    xs = _sc_dispatch(h2, [pos[k:k + 1] for k in range(TOP_K)], p_max)
    ys = _moe(tile_expert, n_valid, tile_rows, next_expert, xs, w_gu[0],
              b_gu[0].reshape(N_EXPERTS, 1, -1), w_down[0], b_down[0].reshape(N_EXPERTS, 1, -1))
    yk = _sc_unpermute(ys, pos.reshape(1, TOP_K * t))
    yk = [q.reshape(TOP_K, t, D_PART) for q in yk]
    out = _combine(x1, gates, g_final.reshape(1, d), yk)
    return out.reshape(batch, seq, d)
```

```python
import functools

import jax
import jax.numpy as jnp
from jax import lax
from jax.experimental import pallas as pl
from jax.experimental.pallas import tpu as pltpu
from jax.experimental.pallas import tpu_sc as plsc

F32 = jnp.float32
BF16 = jnp.bfloat16
I32 = jnp.int32

D_MODEL = 1024
D_POOL = 512
D_ATTN = 512
POOL_WINDOWS = (2, 4, 8, 16)
POOL_GROUP = 128
HEAD_DIM = 64
N_EXPERTS = 32
TOP_K = 4
SWIGLU_LIMIT = 7.0
SWIGLU_ALPHA = 1.702
EPS = 1e-5
NEG_INF = -1e30
LOG2_E = 1.4426950408889634

LANES = 128
QB = 128
DIL_MID = 4
DIL_FAR = 16
ROW_TILE = 512
MOE_TILE = 512
COMBINE_TILE = 256
COMBINE_CHUNKS = 2
SC_WINDOW = 128
D_PACKED = D_MODEL // 2
N_PART = 2
D_PART = D_PACKED // N_PART


def _pack_row(x):
    hi = lax.bitcast_convert_type(x[:, :D_PACKED].astype(BF16).astype(F32), I32)
    lo = lax.bitcast_convert_type(x[:, D_PACKED:].astype(BF16).astype(F32), I32)
    return hi | lax.shift_right_logical(lo, 16)


def _unpack_row(w):
    hi = lax.bitcast_convert_type(w & jnp.int32(-65536), F32)
    lo = lax.bitcast_convert_type(lax.shift_left(w, 16), F32)
    return jnp.concatenate([hi, lo], axis=1)
VMEM_LIMIT = 56 * 1024 * 1024


def _rms(x, g):
    ms = jnp.mean(x * x, axis=-1, keepdims=True)
    return x * lax.rsqrt(ms + EPS) * g


def _inproj_kernel(x_ref, g_ref, w_ref, u_ref, qkv_ref, w_bf):
    @pl.when(pl.program_id(0) == 0)
    def _():
        w_bf[...] = w_ref[...].astype(BF16)

    h = _rms(x_ref[...], g_ref[...]).astype(BF16)
    proj = jnp.dot(h, w_bf[...], preferred_element_type=F32)
    u_ref[...] = proj[:, :D_POOL]
    q = proj[:, D_POOL:D_POOL + D_ATTN] * (HEAD_DIM ** -0.5 * LOG2_E)
    qkv_ref[:, :D_ATTN] = q.astype(BF16)
    qkv_ref[:, D_ATTN:] = proj[:, D_POOL + D_ATTN:].astype(BF16)


def _inproj(x2, g, w):
    t = x2.shape[0]
    d_in = w.shape[1]
    return pl.pallas_call(
        _inproj_kernel,
        grid=(t // ROW_TILE,),
        in_specs=[
            pl.BlockSpec((ROW_TILE, D_MODEL), lambda i: (i, 0)),
            pl.BlockSpec((1, D_MODEL), lambda i: (0, 0)),
            pl.BlockSpec((D_MODEL, d_in), lambda i: (0, 0)),
        ],
        out_specs=[
            pl.BlockSpec((ROW_TILE, D_POOL), lambda i: (i, 0)),
            pl.BlockSpec((ROW_TILE, 3 * D_ATTN), lambda i: (i, 0)),
        ],
        out_shape=[
            jax.ShapeDtypeStruct((t, D_POOL), F32),
            jax.ShapeDtypeStruct((t, 3 * D_ATTN), BF16),
        ],
        scratch_shapes=[pltpu.VMEM((D_MODEL, d_in), BF16)],
        compiler_params=pltpu.CompilerParams(
            dimension_semantics=("arbitrary",), vmem_limit_bytes=VMEM_LIMIT),
        name="inproj",
    )(x2, g, w)


POOL_PAD = 16


def _pool_kernel(u_ref, wp_ref, sc_ref, y_ref, pad_ref):
    s = u_ref.shape[0]
    row = lax.broadcasted_iota(I32, (s, 1), 0)
    pad_ref[0:POOL_PAD, :] = jnp.zeros((POOL_PAD, POOL_GROUP), F32)
    for g, w in enumerate(POOL_WINDOWS):
        lo, hi = g * POOL_GROUP, (g + 1) * POOL_GROUP
        e = u_ref[:, lo:hi]
        acc = e
        span = 1
        while span < w:
            pad_ref[POOL_PAD:, :] = acc
            acc = acc + pad_ref[pl.ds(POOL_PAD - span, s), :]
            span *= 2
        count = jnp.minimum(row + 1, w).astype(F32)
        pooled = acc / count - e
        y = jnp.dot(pooled.astype(BF16), wp_ref[g].astype(BF16), preferred_element_type=F32)
        y_ref[:, lo:hi] = (y * sc_ref[:, lo:hi]).astype(BF16)


def _pool(u, w_pool, pool_scale, batch, seq):
    return pl.pallas_call(
        _pool_kernel,
        grid=(batch,),
        in_specs=[
            pl.BlockSpec((seq, D_POOL), lambda b: (b, 0)),
            pl.BlockSpec((len(POOL_WINDOWS), POOL_GROUP, POOL_GROUP), lambda b: (0, 0, 0)),
            pl.BlockSpec((1, D_POOL), lambda b: (0, 0)),
        ],
        out_specs=pl.BlockSpec((seq, D_POOL), lambda b: (b, 0)),
        out_shape=jax.ShapeDtypeStruct((batch * seq, D_POOL), BF16),
        scratch_shapes=[pltpu.VMEM((seq + POOL_PAD, POOL_GROUP), F32)],
        compiler_params=pltpu.CompilerParams(
            dimension_semantics=("arbitrary",), vmem_limit_bytes=VMEM_LIMIT),
        name="pool",
    )(u, w_pool, pool_scale)


def _pair_block(q, k, v_ext, mask2, lane_h0):
    zero = jnp.zeros_like(q)
    q2 = jnp.concatenate([jnp.where(lane_h0, q, zero), jnp.where(lane_h0, zero, q)], axis=0)
    s = lax.dot_general(q2, k, (((1,), (1,)), ((), ())), preferred_element_type=F32)
    s = jnp.where(mask2, s, NEG_INF)
    m = jnp.max(s, axis=-1, keepdims=True)
    p = jnp.exp2(s - m).astype(BF16)
    ol = jnp.dot(p, v_ext, preferred_element_type=F32)
    o = jnp.where(lane_h0, ol[:QB, :LANES], ol[QB:, :LANES])
    l = jnp.where(lane_h0, ol[:QB, LANES:], ol[QB:, LANES:])
    mb = jnp.where(lane_h0, m[:QB], m[QB:])
    return o, mb, l


def _merge(o_a, m_a, l_a, o_b, m_b, l_b):
    m = jnp.maximum(m_a, m_b)
    ea = jnp.exp2(m_a - m)
    eb = jnp.exp2(m_b - m)
    return o_a * ea + o_b * eb, m, l_a * ea + l_b * eb


def _both_heads(mask):
    return jnp.concatenate([mask, mask], axis=0)


def _attn_kernel(q_ref, k_ref, v_ref, o_ref, qp, kp, vpx, vnx, o23, m23, l23, on, mn):
    s = q_ref.shape[0]
    n_chunk = DIL_FAR
    rows = s // n_chunk
    sub = rows // DIL_MID
    grp = n_chunk * n_chunk
    lane_h0 = lax.broadcasted_iota(I32, (1, LANES), 1) < HEAD_DIM

    @pl.when((pl.program_id(0) == 0) & (pl.program_id(1) == 0))
    def _():
        ones = jnp.ones((s, LANES), BF16)
        vnx[:, LANES:] = ones
        vpx[:, LANES:] = ones

    vnx[:, :LANES] = v_ref[...]

    pi = lax.broadcasted_iota(I32, (grp, grp), 0)
    pj = lax.broadcasted_iota(I32, (grp, grp), 1)
    swap = (pj == (pi % n_chunk) * n_chunk + pi // n_chunk).astype(BF16)
    for src, dst in ((q_ref, qp), (k_ref, kp), (v_ref, vpx)):
        for g in range(s // grp):
            y = jnp.dot(swap, src[g * grp:(g + 1) * grp, :],
                        preferred_element_type=F32).astype(BF16)
            for r in range(n_chunk):
                dst[r * rows + g * n_chunk:r * rows + (g + 1) * n_chunk, 0:LANES] = (
                    y[r * n_chunk:(r + 1) * n_chunk])

    qi = lax.broadcasted_iota(I32, (QB, QB), 0)
    kj = lax.broadcasted_iota(I32, (QB, QB), 1)
    causal = _both_heads(qi >= kj)

    far_unroll = 16

    def far_body(it, carry):
        for u in range(far_unroll):
            r0 = pl.multiple_of((it * far_unroll + u) * rows, rows)
            blk = pl.ds(r0, rows)
            o, m, l = _pair_block(qp[blk, :], kp[blk, :], vpx[blk, :], causal, lane_h0)
            o23[blk, :] = o
            m23[blk, :] = m
            l23[blk, :] = l
        return carry

    lax.fori_loop(0, n_chunk // far_unroll, far_body, 0)

    def mid_index(n_key_sub):
        i_q = lax.broadcasted_iota(I32, (QB, DIL_MID * n_key_sub), 0)
        i_k = lax.broadcasted_iota(I32, (QB, DIL_MID * n_key_sub), 1)
        return i_q // sub, i_q % sub, i_k // n_key_sub, i_k % n_key_sub

    jq, aq, jk, ak = mid_index(sub)
    d0 = DIL_FAR * (aq - ak) + DIL_MID * (jq - jk)
    mask_mid0 = _both_heads(d0 >= 0)
    jq, aq, jk, ak = mid_index(2 * sub)
    d1 = DIL_FAR * (aq + sub - ak) + DIL_MID * (jq - jk)
    mask_mid = _both_heads((d1 >= 0) & (d1 <= DIL_MID * QB))

    def mid_tile(r4, a0, k0, nk, mask):
        def at(j, off, n):
            start = (DIL_MID * j + r4) * rows + off
            return pl.ds(start if isinstance(start, int) else pl.multiple_of(start, sub), n)

        q = jnp.concatenate([qp[at(j, a0, sub), :] for j in range(DIL_MID)], axis=0)
        k = jnp.concatenate([kp[at(j, k0, nk), :] for j in range(DIL_MID)], axis=0)
        v = jnp.concatenate([vpx[at(j, k0, nk), :] for j in range(DIL_MID)], axis=0)
        o, m, l = _pair_block(q, k, v, mask, lane_h0)
        for j in range(DIL_MID):
            dst = at(j, a0, sub)
            src = slice(j * sub, (j + 1) * sub)
            oo, mm, ll = _merge(o23[dst, :], m23[dst, :], l23[dst, :], o[src], m[src], l[src])
            o23[dst, :] = oo
            m23[dst, :] = mm
            l23[dst, :] = ll

    for r4 in range(DIL_MID):
        mid_tile(r4, 0, 0, sub, mask_mid0)

    def mid_body(a_blk, carry):
        a0 = pl.multiple_of(a_blk * sub, sub)
        for r4 in range(DIL_MID):
            mid_tile(r4, a0, a0 - sub, 2 * sub, mask_mid)
        return carry

    lax.fori_loop(1, rows // sub, mid_body, 0, unroll=True)

    for g in range(s // grp):
        slabs = []
        for r in range(n_chunk):
            blk = slice(r * rows + g * n_chunk, r * rows + (g + 1) * n_chunk)
            slabs.append((o23[blk, :] / l23[blk, :]).astype(BF16))
        on[g * grp:(g + 1) * grp, :] = jnp.dot(swap, jnp.concatenate(slabs, axis=0),
                                               preferred_element_type=F32)
    for r in range(n_chunk):
        blk = slice(r * rows, (r + 1) * rows)
        mn[pl.ds(r, rows, stride=n_chunk), :] = m23[blk, :] + jnp.log2(l23[blk, :])

    qi2 = lax.broadcasted_iota(I32, (QB, 2 * QB), 0)
    kj2 = lax.broadcasted_iota(I32, (QB, 2 * QB), 1)
    dn = qi2 + QB - kj2
    mask_near = _both_heads((dn >= 0) & (dn <= QB))

    def near_finish(dst, o, m, l):
        oo, _, ll = _merge(on[dst, :], mn[dst, :], 1.0, o, m, l)
        o_ref[dst, :] = (oo / ll).astype(o_ref.dtype)

    first = pl.ds(0, QB)
    o, m, l = _pair_block(q_ref[first, :], k_ref[first, :], vnx[first, :], causal, lane_h0)
    near_finish(first, o, m, l)

    near_unroll = 15

    def near_body(it, carry):
        for u in range(near_unroll):
            r0 = pl.multiple_of((1 + it * near_unroll + u) * QB, QB)
            keys = pl.ds(r0 - QB, 2 * QB)
            o, m, l = _pair_block(q_ref[pl.ds(r0, QB), :], k_ref[keys, :], vnx[keys, :],
                                  mask_near, lane_h0)
            near_finish(pl.ds(r0, QB), o, m, l)
        return carry

    lax.fori_loop(0, (s // QB - 1) // near_unroll, near_body, 0)


def _attention(qkv, batch, seq):
    n_pair = D_ATTN // LANES
    blk = (seq, LANES)
    f32_scr = pltpu.VMEM(blk, F32)
    bf_scr = pltpu.VMEM(blk, BF16)
    bfx_scr = pltpu.VMEM((seq, 2 * LANES), BF16)
    return pl.pallas_call(
        _attn_kernel,
        grid=(batch, n_pair),
        in_specs=[
            pl.BlockSpec(blk, lambda b, h: (b, h)),
            pl.BlockSpec(blk, lambda b, h: (b, n_pair + h)),
            pl.BlockSpec(blk, lambda b, h: (b, 2 * n_pair + h)),
        ],
        out_specs=pl.BlockSpec(blk, lambda b, h: (b, h)),
        out_shape=jax.ShapeDtypeStruct((batch * seq, D_ATTN), BF16),
        scratch_shapes=[bf_scr, bf_scr, bfx_scr, bfx_scr,
                        f32_scr, f32_scr, f32_scr, f32_scr, f32_scr],
        compiler_params=pltpu.CompilerParams(
            dimension_semantics=("arbitrary", "arbitrary"), vmem_limit_bytes=VMEM_LIMIT),
        name="dilated_attn",
    )(qkv, qkv, qkv)


META_ROWS = 16
META_EIDX, META_GATE, META_RANK = 0, 4, 8


def _outproj_kernel(x_ref, yp_ref, ya_ref, wo_ref, g_ref, wrt_ref, brt_ref,
                    x1_ref, h2_ref, meta_ref, cnt_ref, wo_bf, before, carry):
    tm = x_ref.shape[0]

    @pl.when(pl.program_id(0) == 0)
    def _():
        wo_bf[...] = wo_ref[...].astype(BF16)
        carry[...] = jnp.zeros_like(carry)
        ti = lax.broadcasted_iota(I32, (tm, tm), 0)
        tj = lax.broadcasted_iota(I32, (tm, tm), 1)
        before[...] = (ti < tj).astype(BF16)

    x1 = (x_ref[...]
          + jnp.dot(yp_ref[...], wo_bf[:D_POOL, :], preferred_element_type=F32)
          + jnp.dot(ya_ref[...], wo_bf[D_POOL:, :], preferred_element_type=F32))
    x1_ref[...] = x1
    h2 = _rms(x1, g_ref[...])
    h2_ref[...] = _pack_row(h2)

    logits_t = lax.dot_general(wrt_ref[...].astype(BF16), h2.astype(BF16),
                               (((1,), (1,)), ((), ())), preferred_element_type=F32)
    logits_t = logits_t + brt_ref[:, 0:1]
    eid = lax.broadcasted_iota(I32, (N_EXPERTS, tm), 0)
    work = logits_t
    idxs, vals = [], []
    for _ in range(TOP_K):
        mx = jnp.max(work, axis=0, keepdims=True)
        idx = jnp.min(jnp.where(work == mx, eid, N_EXPERTS), axis=0, keepdims=True)
        idxs.append(idx)
        vals.append(mx)
        work = jnp.where(eid == idx, -jnp.inf, work)
    exps = [jnp.exp(v - vals[0]) for v in vals]
    den = exps[0] + exps[1] + exps[2] + exps[3]

    onehot = jnp.zeros((N_EXPERTS, tm), F32)
    for idx in idxs:
        onehot = onehot + (eid == idx).astype(F32)
    rank_e = carry[:, 0:1] + jnp.dot(onehot.astype(BF16), before[...],
                                     preferred_element_type=F32)
    carry[...] = carry[...] + jnp.sum(onehot, axis=1, keepdims=True)
    cnt_ref[...] = carry[...]

    mrow = lax.broadcasted_iota(I32, (META_ROWS, tm), 0)
    meta = jnp.zeros((META_ROWS, tm), F32)
    for k in range(TOP_K):
        rank_k = jnp.sum(jnp.where(eid == idxs[k], rank_e, 0.0), axis=0, keepdims=True)
        meta = jnp.where(mrow == META_EIDX + k, idxs[k].astype(F32), meta)
        meta = jnp.where(mrow == META_GATE + k, exps[k] / den, meta)
        meta = jnp.where(mrow == META_RANK + k, rank_k, meta)
    meta_ref[...] = meta


def _outproj(x2, y_pool, y_attn, w_out, g, w_router_t, b_router_t):
    t = x2.shape[0]
    row = lambda i: (i, 0)
    const = lambda i: (0, 0)
    return pl.pallas_call(
        _outproj_kernel,
        grid=(t // ROW_TILE,),
        in_specs=[
            pl.BlockSpec((ROW_TILE, D_MODEL), row),
            pl.BlockSpec((ROW_TILE, D_POOL), row),
            pl.BlockSpec((ROW_TILE, D_ATTN), row),
            pl.BlockSpec((D_MODEL, D_MODEL), const),
            pl.BlockSpec((1, D_MODEL), const),
            pl.BlockSpec((N_EXPERTS, D_MODEL), const),
            pl.BlockSpec((N_EXPERTS, LANES), const),
        ],
        out_specs=[
            pl.BlockSpec((ROW_TILE, D_MODEL), row),
            pl.BlockSpec((ROW_TILE, D_PACKED), row),
            pl.BlockSpec((META_ROWS, ROW_TILE), lambda i: (0, i)),
            pl.BlockSpec((N_EXPERTS, LANES), const),
        ],
        out_shape=[
            jax.ShapeDtypeStruct((t, D_MODEL), F32),
            jax.ShapeDtypeStruct((t, D_PACKED), I32),
            jax.ShapeDtypeStruct((META_ROWS, t), F32),
            jax.ShapeDtypeStruct((N_EXPERTS, LANES), F32),
        ],
        scratch_shapes=[pltpu.VMEM((D_MODEL, D_MODEL), BF16),
                        pltpu.VMEM((ROW_TILE, ROW_TILE), BF16),
                        pltpu.VMEM((N_EXPERTS, LANES), F32)],
        compiler_params=pltpu.CompilerParams(
            dimension_semantics=("arbitrary",), vmem_limit_bytes=VMEM_LIMIT),
        name="outproj_router",
    )(x2, y_pool, y_attn, w_out, g, w_router_t, b_router_t)


def _sc_mesh():
    return plsc.VectorSubcoreMesh(core_axis_name="core", subcore_axis_name="subcore")


def _sc_dispatch(h2, pos_rows, p_max):
    t = h2.shape[0]

    @functools.partial(
        pl.kernel, mesh=_sc_mesh(),
        out_type=[jax.ShapeDtypeStruct((p_max, D_PART), h2.dtype)] * N_PART)
    def run(h_hbm, *refs):
        pos_hbm, xs_hbm = refs[:TOP_K], refs[TOP_K:]
        for c in range(N_PART):
            def body(x_vmem, *idx_vmem, dst=xs_hbm[c]):
                for iv in idx_vmem:
                    pltpu.sync_copy(x_vmem, dst.at[iv.at[0]])

            pltpu.emit_pipeline(
                body,
                grid=(t // SC_WINDOW,),
                in_specs=[pl.BlockSpec((SC_WINDOW, D_PART), lambda i, c=c: (i, c))]
                + [pl.BlockSpec((1, SC_WINDOW), lambda i: (0, i))] * TOP_K,
                out_specs=[],
                core_axis_name=("core", "subcore"),
                dimension_semantics=(pltpu.PARALLEL,),
            )(h_hbm, *pos_hbm)

    return run(h2, *pos_rows)


def _sc_unpermute(y_quarters, idx_row):
    n = idx_row.shape[1]

    @functools.partial(
        pl.kernel, mesh=_sc_mesh(),
        out_type=[jax.ShapeDtypeStruct((n, D_PART), y_quarters[0].dtype)] * N_PART)
    def run(*refs):
        y_hbm, i_hbm, o_hbm = refs[:N_PART], refs[N_PART], refs[N_PART + 1:]
        for c in range(N_PART):
            def body(i_vmem, o_vmem, src=y_hbm[c]):
                pltpu.sync_copy(src.at[i_vmem.at[0]], o_vmem)

            pltpu.emit_pipeline(
                body,
                grid=(n // SC_WINDOW,),
                in_specs=[pl.BlockSpec((1, SC_WINDOW), lambda i: (0, i))],
                out_specs=[pl.BlockSpec((SC_WINDOW, D_PART), lambda i: (i, 0))],
                core_axis_name=("core", "subcore"),
                dimension_semantics=(pltpu.PARALLEL,),
            )(i_hbm, o_hbm[c])

    return run(*y_quarters, idx_row)


def _moe_kernel(te_ref, nv_ref, rows_ref, nxt_ref, *refs):
    xs_refs = refs[:N_PART]
    bgu_ref, bd_ref, wgu_hbm, wd_hbm = refs[N_PART:N_PART + 4]
    y_refs = refs[N_PART + 4:2 * N_PART + 4]
    wgu_f32, wd_f32, wgu_bf, wd_bf, sem = refs[2 * N_PART + 4:]
    i = pl.program_id(0)

    def weight_copies(e):
        return (pltpu.make_async_copy(wgu_hbm.at[e], wgu_f32, sem.at[0]),
                pltpu.make_async_copy(wd_hbm.at[e], wd_f32, sem.at[1]))

    @pl.when(i < nv_ref[0])
    def _():
        prev = te_ref[jnp.maximum(i - 1, 0)]
        new_expert = jnp.logical_or(i == 0, te_ref[i] != prev)

        @pl.when(i == 0)
        def _():
            for cp in weight_copies(te_ref[0]):
                cp.start()

        @pl.when(new_expert)
        def _():
            for cp in weight_copies(te_ref[i]):
                cp.wait()
            wgu_bf[...] = wgu_f32[...].astype(BF16)
            wd_bf[...] = wd_f32[...].astype(BF16)

            @pl.when(nxt_ref[i] >= 0)
            def _():
                for cp in weight_copies(nxt_ref[i]):
                    cp.start(priority=1)

        d_e = wd_bf.shape[0]
        x = _unpack_row(jnp.concatenate([r[...] for r in xs_refs], axis=1))
        rid = lax.broadcasted_iota(I32, (MOE_TILE, 1), 0)
        x = jnp.where(rid < rows_ref[i], x, 0.0).astype(BF16)
        gu = jnp.dot(x, wgu_bf[...], preferred_element_type=F32) + bgu_ref[0]
        gate = jnp.minimum(gu[:, :d_e], SWIGLU_LIMIT)
        lin = jnp.clip(gu[:, d_e:], -SWIGLU_LIMIT, SWIGLU_LIMIT)
        act = gate * jax.nn.sigmoid(SWIGLU_ALPHA * gate) * (lin + 1.0)
        y = jnp.dot(act.astype(BF16), wd_bf[...], preferred_element_type=F32) + bd_ref[0]
        yp = _pack_row(y)
        for c, y_ref in enumerate(y_refs):
            y_ref[...] = yp[:, c * D_PART:(c + 1) * D_PART]

    @pl.when(i >= nv_ref[0])
    def _():
        for y_ref in y_refs:
            y_ref[...] = jnp.zeros_like(y_ref)


def _moe(tile_expert, n_valid, tile_rows, next_expert, xs_parts, w_gu, b_gu, w_down, b_down):
    p_max = xs_parts[0].shape[0]
    n_tiles = p_max // MOE_TILE
    d_e = w_down.shape[1]

    def row(i, te, nv, tr, nx):
        return (jnp.minimum(i, nv[0] - 1), 0)

    def expert(i, te, nv, tr, nx):
        return (te[jnp.minimum(i, nv[0] - 1)], 0, 0)

    grid_spec = pltpu.PrefetchScalarGridSpec(
        num_scalar_prefetch=4,
        grid=(n_tiles,),
        in_specs=[pl.BlockSpec((MOE_TILE, D_PART), row)] * N_PART + [
            pl.BlockSpec((1, 1, 2 * d_e), expert),
            pl.BlockSpec((1, 1, D_MODEL), expert),
            pl.BlockSpec(memory_space=pl.ANY),
            pl.BlockSpec(memory_space=pl.ANY),
        ],
        out_specs=[pl.BlockSpec((MOE_TILE, D_PART), lambda i, te, nv, tr, nx: (i, 0))] * N_PART,
        scratch_shapes=[pltpu.VMEM((D_MODEL, 2 * d_e), F32), pltpu.VMEM((d_e, D_MODEL), F32),
                        pltpu.VMEM((D_MODEL, 2 * d_e), BF16), pltpu.VMEM((d_e, D_MODEL), BF16),
                        pltpu.SemaphoreType.DMA((2,))],
    )
    return pl.pallas_call(
        _moe_kernel,
        grid_spec=grid_spec,
        out_shape=[jax.ShapeDtypeStruct((p_max, D_PART), I32)] * N_PART,
        compiler_params=pltpu.CompilerParams(
            dimension_semantics=("arbitrary",), vmem_limit_bytes=VMEM_LIMIT),
        name="moe_ffn",
    )(tile_expert, n_valid, tile_rows, next_expert, *xs_parts, b_gu, b_down, w_gu, w_down)


def _combine_kernel(x1_ref, gate_ref, g_ref, *refs):
    yk_refs, o_ref = refs[:N_PART], refs[-1]
    acc = x1_ref[...]
    for k in range(TOP_K):
        y_k = _unpack_row(jnp.concatenate([r[k] for r in yk_refs], axis=1))
        acc = acc + gate_ref[:, k:k + 1] * y_k
    o_ref[...] = _rms(acc, g_ref[...])


def _combine(x1, gates, g, yk_parts, chunk, out_so_far):
    t, d = x1.shape
    tq = COMBINE_TILE
    steps = yk_parts[0].shape[1] // tq
    base = chunk * steps
    row = lambda i: (base + i, 0)
    in_specs = [
        pl.BlockSpec((tq, d), row),
        pl.BlockSpec((tq, TOP_K), row),
        pl.BlockSpec((1, d), lambda i: (0, 0)),
    ] + [pl.BlockSpec((TOP_K, tq, D_PART), lambda i: (0, i, 0))] * N_PART
    args = [x1, gates, g, *yk_parts]
    aliases = {}
    if out_so_far is not None:
        in_specs.append(pl.BlockSpec(memory_space=pl.ANY))
        aliases = {len(args): 0}
        args.append(out_so_far)
    return pl.pallas_call(
        _combine_kernel,
        grid=(steps,),
        in_specs=in_specs,
        out_specs=pl.BlockSpec((tq, d), row),
        out_shape=jax.ShapeDtypeStruct((t, d), F32),
        input_output_aliases=aliases,
        compiler_params=pltpu.CompilerParams(
            dimension_semantics=("arbitrary",), vmem_limit_bytes=VMEM_LIMIT),
        name="combine_final",
    )(*args)


def kernel(x, g_mix, w_in, w_pool, pool_scale, w_out, g_ffn, w_router, b_router,
           w_gu, b_gu, w_down, b_down, g_final):
    batch, seq, d = x.shape
    t = batch * seq
    x2 = x.reshape(t, d)

    u, qkv = _inproj(x2, g_mix[0].reshape(1, d), w_in[0])
    y_pool = _pool(u, w_pool[0], pool_scale[0].reshape(1, D_POOL), batch, seq)
    y_attn = _attention(qkv, batch, seq)

    wr_t = w_router[0].T
    br_t = jnp.broadcast_to(b_router[0].reshape(N_EXPERTS, 1), (N_EXPERTS, LANES))
    x1, h2, meta, cnt = _outproj(x2, y_pool, y_attn, w_out[0], g_ffn[0].reshape(1, d), wr_t, br_t)

    eidx = meta[META_EIDX:META_EIDX + TOP_K].astype(I32)
    rank = meta[META_RANK:META_RANK + TOP_K].astype(I32)
    gates = meta[META_GATE:META_GATE + TOP_K].T
    counts = cnt[:, 0].astype(I32)
    padded = ((counts + MOE_TILE - 1) // MOE_TILE) * MOE_TILE
    ends = jnp.cumsum(padded)
    offsets = ends - padded
    e_ids = jnp.arange(N_EXPERTS, dtype=I32)
    hit = eidx[None] == e_ids[:, None, None]
    pos = rank + jnp.sum(jnp.where(hit, offsets[:, None, None], 0), axis=0)
    p_max = t * TOP_K + N_EXPERTS * MOE_TILE
    n_tiles = p_max // MOE_TILE
    tile_start = jnp.arange(n_tiles, dtype=I32) * MOE_TILE
    tile_expert = jnp.minimum(
        jnp.sum((tile_start[:, None] >= ends[None, :]).astype(I32), axis=1), N_EXPERTS - 1)
    tile_rows = jnp.clip((offsets + counts)[tile_expert] - tile_start, 0, MOE_TILE)
    n_valid = (ends[-1] // MOE_TILE).reshape(1).astype(I32)
    later = (e_ids[None, :] > e_ids[:, None]) & (counts[None, :] > 0)
    next_nonempty = jnp.min(jnp.where(later, e_ids[None, :], N_EXPERTS), axis=1)
    next_expert = jnp.where(next_nonempty < N_EXPERTS, next_nonempty, -1)[tile_expert]

    xs = _sc_dispatch(h2, [pos[k:k + 1] for k in range(TOP_K)], p_max)
    ys = _moe(tile_expert, n_valid, tile_rows, next_expert, xs, w_gu[0],
              b_gu[0].reshape(N_EXPERTS, 1, -1), w_down[0], b_down[0].reshape(N_EXPERTS, 1, -1))
    out = None
    tc = t // COMBINE_CHUNKS
    for c in range(COMBINE_CHUNKS):
        idx = pos[:, c * tc:(c + 1) * tc].reshape(1, TOP_K * tc)
        yk = [q.reshape(TOP_K, tc, D_PART) for q in _sc_unpermute(ys, idx)]
        out = _combine(x1, gates, g_final.reshape(1, d), yk, c, out)
    return out.reshape(batch, seq, d)
```

```python
import functools

import jax
import jax.numpy as jnp
from jax import lax
from jax.experimental import pallas as pl
from jax.experimental.pallas import tpu as pltpu
from jax.experimental.pallas import tpu_sc as plsc

F32 = jnp.float32
BF16 = jnp.bfloat16
I32 = jnp.int32

D_MODEL = 1024
D_POOL = 512
D_ATTN = 512
POOL_WINDOWS = (2, 4, 8, 16)
POOL_GROUP = 128
HEAD_DIM = 64
N_EXPERTS = 32
TOP_K = 4
SWIGLU_LIMIT = 7.0
SWIGLU_ALPHA = 1.702
EPS = 1e-5
NEG_INF = -1e30
LOG2_E = 1.4426950408889634

LANES = 128
QB = 128
DIL_MID = 4
DIL_FAR = 16
ROW_TILE = 512
MOE_TILE = 512
COMBINE_TILE = 256
SC_WINDOW = 128
D_PACKED = D_MODEL // 2
N_PART = 2
D_PART = D_PACKED // N_PART


def _pack_row(x):
    hi = lax.bitcast_convert_type(x[:, :D_PACKED].astype(BF16).astype(F32), I32)
    lo = lax.bitcast_convert_type(x[:, D_PACKED:].astype(BF16).astype(F32), I32)
    return hi | lax.shift_right_logical(lo, 16)


def _unpack_row(w):
    hi = lax.bitcast_convert_type(w & jnp.int32(-65536), F32)
    lo = lax.bitcast_convert_type(lax.shift_left(w, 16), F32)
    return jnp.concatenate([hi, lo], axis=1)
VMEM_LIMIT = 56 * 1024 * 1024


def _rms(x, g):
    ms = jnp.mean(x * x, axis=-1, keepdims=True)
    return x * lax.rsqrt(ms + EPS) * g


POOL_PAD = 16


def _inproj_kernel(x_ref, g_ref, w_ref, wp_ref, sc_ref, yp_ref, qkv_ref, w_bf, hist, pad_ref,
                   *, tiles_per_seq):
    i = pl.program_id(0)
    tm = x_ref.shape[0]

    @pl.when(i == 0)
    def _():
        w_bf[...] = w_ref[...].astype(BF16)
        pad_ref[0:POOL_PAD, :] = jnp.zeros((POOL_PAD, POOL_GROUP), F32)

    @pl.when(i % tiles_per_seq == 0)
    def _():
        hist[...] = jnp.zeros_like(hist)

    h = _rms(x_ref[...], g_ref[...]).astype(BF16)
    proj = jnp.dot(h, w_bf[...], preferred_element_type=F32)
    q = proj[:, D_POOL:D_POOL + D_ATTN] * (HEAD_DIM ** -0.5 * LOG2_E)
    qkv_ref[:, :D_ATTN] = q.astype(BF16)
    qkv_ref[:, D_ATTN:] = proj[:, D_POOL + D_ATTN:].astype(BF16)

    pos = (i % tiles_per_seq) * tm + lax.broadcasted_iota(I32, (tm, 1), 0)
    for g, w in enumerate(POOL_WINDOWS):
        lo, hi = g * POOL_GROUP, (g + 1) * POOL_GROUP
        e = proj[:, lo:hi]
        acc = jnp.concatenate([hist[g], e], axis=0)
        hist[g] = e[tm - POOL_PAD:, :]
        span = 1
        while span < w:
            pad_ref[POOL_PAD:, :] = acc
            acc = acc + pad_ref[pl.ds(POOL_PAD - span, POOL_PAD + tm), :]
            span *= 2
        count = jnp.minimum(pos + 1, w).astype(F32)
        pooled = acc[POOL_PAD:, :] / count - e
        y = jnp.dot(pooled.astype(BF16), wp_ref[g].astype(BF16), preferred_element_type=F32)
        yp_ref[:, lo:hi] = (y * sc_ref[:, lo:hi]).astype(BF16)


def _inproj(x2, g, w, w_pool, pool_scale, seq):
    t = x2.shape[0]
    d_in = w.shape[1]
    n_grp = len(POOL_WINDOWS)
    return pl.pallas_call(
        functools.partial(_inproj_kernel, tiles_per_seq=seq // ROW_TILE),
        grid=(t // ROW_TILE,),
        in_specs=[
            pl.BlockSpec((ROW_TILE, D_MODEL), lambda i: (i, 0)),
            pl.BlockSpec((1, D_MODEL), lambda i: (0, 0)),
            pl.BlockSpec((D_MODEL, d_in), lambda i: (0, 0)),
            pl.BlockSpec((n_grp, POOL_GROUP, POOL_GROUP), lambda i: (0, 0, 0)),
            pl.BlockSpec((1, D_POOL), lambda i: (0, 0)),
        ],
        out_specs=[
            pl.BlockSpec((ROW_TILE, D_POOL), lambda i: (i, 0)),
            pl.BlockSpec((ROW_TILE, 3 * D_ATTN), lambda i: (i, 0)),
        ],
        out_shape=[
            jax.ShapeDtypeStruct((t, D_POOL), BF16),
            jax.ShapeDtypeStruct((t, 3 * D_ATTN), BF16),
        ],
        scratch_shapes=[pltpu.VMEM((D_MODEL, d_in), BF16),
                        pltpu.VMEM((n_grp, POOL_PAD, POOL_GROUP), F32),
                        pltpu.VMEM((2 * POOL_PAD + ROW_TILE, POOL_GROUP), F32)],
        compiler_params=pltpu.CompilerParams(
            dimension_semantics=("arbitrary",), vmem_limit_bytes=VMEM_LIMIT),
        name="inproj_pool",
    )(x2, g, w, w_pool, pool_scale)


def _pair_block(q, k, v_ext, mask2, lane_h0):
    zero = jnp.zeros_like(q)
    q2 = jnp.concatenate([jnp.where(lane_h0, q, zero), jnp.where(lane_h0, zero, q)], axis=0)
    s = lax.dot_general(q2, k, (((1,), (1,)), ((), ())), preferred_element_type=F32)
    s = jnp.where(mask2, s, NEG_INF)
    m = jnp.max(s, axis=-1, keepdims=True)
    p = jnp.exp2(s - m).astype(BF16)
    ol = jnp.dot(p, v_ext, preferred_element_type=F32)
    o = jnp.where(lane_h0, ol[:QB, :LANES], ol[QB:, :LANES])
    l = jnp.where(lane_h0, ol[:QB, LANES:], ol[QB:, LANES:])
    mb = jnp.where(lane_h0, m[:QB], m[QB:])
    return o, mb, l


def _merge(o_a, m_a, l_a, o_b, m_b, l_b):
    m = jnp.maximum(m_a, m_b)
    ea = jnp.exp2(m_a - m)
    eb = jnp.exp2(m_b - m)
    return o_a * ea + o_b * eb, m, l_a * ea + l_b * eb


def _both_heads(mask):
    return jnp.concatenate([mask, mask], axis=0)


def _attn_kernel(q_ref, k_ref, v_ref, o_ref, qp, kp, vpx, vnx, o23, m23, l23, on, mn):
    s = q_ref.shape[0]
    n_chunk = DIL_FAR
    rows = s // n_chunk
    sub = rows // DIL_MID
    grp = n_chunk * n_chunk
    lane_h0 = lax.broadcasted_iota(I32, (1, LANES), 1) < HEAD_DIM

    @pl.when((pl.program_id(0) == 0) & (pl.program_id(1) == 0))
    def _():
        ones = jnp.ones((s, LANES), BF16)
        vnx[:, LANES:] = ones
        vpx[:, LANES:] = ones

    vnx[:, :LANES] = v_ref[...]

    pi = lax.broadcasted_iota(I32, (grp, grp), 0)
    pj = lax.broadcasted_iota(I32, (grp, grp), 1)
    swap = (pj == (pi % n_chunk) * n_chunk + pi // n_chunk).astype(BF16)
    for src, dst in ((q_ref, qp), (k_ref, kp), (v_ref, vpx)):
        for g in range(s // grp):
            y = jnp.dot(swap, src[g * grp:(g + 1) * grp, :],
                        preferred_element_type=F32).astype(BF16)
            for r in range(n_chunk):
                dst[r * rows + g * n_chunk:r * rows + (g + 1) * n_chunk, 0:LANES] = (
                    y[r * n_chunk:(r + 1) * n_chunk])

    qi = lax.broadcasted_iota(I32, (QB, QB), 0)
    kj = lax.broadcasted_iota(I32, (QB, QB), 1)
    causal = _both_heads(qi >= kj)

    far_unroll = 16

    def far_body(it, carry):
        for u in range(far_unroll):
            r0 = pl.multiple_of((it * far_unroll + u) * rows, rows)
            blk = pl.ds(r0, rows)
            o, m, l = _pair_block(qp[blk, :], kp[blk, :], vpx[blk, :], causal, lane_h0)
            o23[blk, :] = o
            m23[blk, :] = m
            l23[blk, :] = l
        return carry

    lax.fori_loop(0, n_chunk // far_unroll, far_body, 0)

    def mid_index(n_key_sub):
        i_q = lax.broadcasted_iota(I32, (QB, DIL_MID * n_key_sub), 0)
        i_k = lax.broadcasted_iota(I32, (QB, DIL_MID * n_key_sub), 1)
        return i_q // sub, i_q % sub, i_k // n_key_sub, i_k % n_key_sub

    jq, aq, jk, ak = mid_index(sub)
    d0 = DIL_FAR * (aq - ak) + DIL_MID * (jq - jk)
    mask_mid0 = _both_heads(d0 >= 0)
    jq, aq, jk, ak = mid_index(2 * sub)
    d1 = DIL_FAR * (aq + sub - ak) + DIL_MID * (jq - jk)
    mask_mid = _both_heads((d1 >= 0) & (d1 <= DIL_MID * QB))

    def mid_tile(r4, a0, k0, nk, mask):
        def at(j, off, n):
            start = (DIL_MID * j + r4) * rows + off
            return pl.ds(start if isinstance(start, int) else pl.multiple_of(start, sub), n)

        q = jnp.concatenate([qp[at(j, a0, sub), :] for j in range(DIL_MID)], axis=0)
        k = jnp.concatenate([kp[at(j, k0, nk), :] for j in range(DIL_MID)], axis=0)
        v = jnp.concatenate([vpx[at(j, k0, nk), :] for j in range(DIL_MID)], axis=0)
        o, m, l = _pair_block(q, k, v, mask, lane_h0)
        for j in range(DIL_MID):
            dst = at(j, a0, sub)
            src = slice(j * sub, (j + 1) * sub)
            oo, mm, ll = _merge(o23[dst, :], m23[dst, :], l23[dst, :], o[src], m[src], l[src])
            o23[dst, :] = oo
            m23[dst, :] = mm
            l23[dst, :] = ll

    for r4 in range(DIL_MID):
        mid_tile(r4, 0, 0, sub, mask_mid0)

    def mid_body(a_blk, carry):
        a0 = pl.multiple_of(a_blk * sub, sub)
        for r4 in range(DIL_MID):
            mid_tile(r4, a0, a0 - sub, 2 * sub, mask_mid)
        return carry

    lax.fori_loop(1, rows // sub, mid_body, 0, unroll=True)

    for g in range(s // grp):
        slabs = []
        for r in range(n_chunk):
            blk = slice(r * rows + g * n_chunk, r * rows + (g + 1) * n_chunk)
            slabs.append((o23[blk, :] / l23[blk, :]).astype(BF16))
        on[g * grp:(g + 1) * grp, :] = jnp.dot(swap, jnp.concatenate(slabs, axis=0),
                                               preferred_element_type=F32)
    for r in range(n_chunk):
        blk = slice(r * rows, (r + 1) * rows)
        mn[pl.ds(r, rows, stride=n_chunk), :] = m23[blk, :] + jnp.log2(l23[blk, :])

    qi2 = lax.broadcasted_iota(I32, (QB, 2 * QB), 0)
    kj2 = lax.broadcasted_iota(I32, (QB, 2 * QB), 1)
    dn = qi2 + QB - kj2
    mask_near = _both_heads((dn >= 0) & (dn <= QB))

    def near_finish(dst, o, m, l):
        oo, _, ll = _merge(on[dst, :], mn[dst, :], 1.0, o, m, l)
        o_ref[dst, :] = (oo / ll).astype(o_ref.dtype)

    first = pl.ds(0, QB)
    o, m, l = _pair_block(q_ref[first, :], k_ref[first, :], vnx[first, :], causal, lane_h0)
    near_finish(first, o, m, l)

    near_unroll = 15

    def near_body(it, carry):
        for u in range(near_unroll):
            r0 = pl.multiple_of((1 + it * near_unroll + u) * QB, QB)
            keys = pl.ds(r0 - QB, 2 * QB)
            o, m, l = _pair_block(q_ref[pl.ds(r0, QB), :], k_ref[keys, :], vnx[keys, :],
                                  mask_near, lane_h0)
            near_finish(pl.ds(r0, QB), o, m, l)
        return carry

    lax.fori_loop(0, (s // QB - 1) // near_unroll, near_body, 0)


def _attention(qkv, batch, seq):
    n_pair = D_ATTN // LANES
    blk = (seq, LANES)
    f32_scr = pltpu.VMEM(blk, F32)
    bf_scr = pltpu.VMEM(blk, BF16)
    bfx_scr = pltpu.VMEM((seq, 2 * LANES), BF16)
    return pl.pallas_call(
        _attn_kernel,
        grid=(batch, n_pair),
        in_specs=[
            pl.BlockSpec(blk, lambda b, h: (b, h)),
            pl.BlockSpec(blk, lambda b, h: (b, n_pair + h)),
            pl.BlockSpec(blk, lambda b, h: (b, 2 * n_pair + h)),
        ],
        out_specs=pl.BlockSpec(blk, lambda b, h: (b, h)),
        out_shape=jax.ShapeDtypeStruct((batch * seq, D_ATTN), BF16),
        scratch_shapes=[bf_scr, bf_scr, bfx_scr, bfx_scr,
                        f32_scr, f32_scr, f32_scr, f32_scr, f32_scr],
        compiler_params=pltpu.CompilerParams(
            dimension_semantics=("arbitrary", "arbitrary"), vmem_limit_bytes=VMEM_LIMIT),
        name="dilated_attn",
    )(qkv, qkv, qkv)


META_ROWS = 16
META_EIDX, META_GATE, META_RANK = 0, 4, 8


def _outproj_kernel(x_ref, yp_ref, ya_ref, wo_ref, g_ref, wrt_ref, brt_ref,
                    x1_ref, h2_ref, meta_ref, cnt_ref, wo_bf, before, carry):
    tm = x_ref.shape[0]

    @pl.when(pl.program_id(0) == 0)
    def _():
        wo_bf[...] = wo_ref[...].astype(BF16)
        carry[...] = jnp.zeros_like(carry)
        ti = lax.broadcasted_iota(I32, (tm, tm), 0)
        tj = lax.broadcasted_iota(I32, (tm, tm), 1)
        before[...] = (ti < tj).astype(BF16)

    x1 = (x_ref[...]
          + jnp.dot(yp_ref[...], wo_bf[:D_POOL, :], preferred_element_type=F32)
          + jnp.dot(ya_ref[...], wo_bf[D_POOL:, :], preferred_element_type=F32))
    x1_ref[...] = x1
    h2 = _rms(x1, g_ref[...])
    h2_ref[...] = _pack_row(h2)

    logits_t = lax.dot_general(wrt_ref[...].astype(BF16), h2.astype(BF16),
                               (((1,), (1,)), ((), ())), preferred_element_type=F32)
    logits_t = logits_t + brt_ref[:, 0:1]
    eid = lax.broadcasted_iota(I32, (N_EXPERTS, tm), 0)
    work = logits_t
    idxs, vals = [], []
    for _ in range(TOP_K):
        mx = jnp.max(work, axis=0, keepdims=True)
        idx = jnp.min(jnp.where(work == mx, eid, N_EXPERTS), axis=0, keepdims=True)
        idxs.append(idx)
        vals.append(mx)
        work = jnp.where(eid == idx, -jnp.inf, work)
    exps = [jnp.exp(v - vals[0]) for v in vals]
    den = exps[0] + exps[1] + exps[2] + exps[3]

    onehot = jnp.zeros((N_EXPERTS, tm), F32)
    for idx in idxs:
        onehot = onehot + (eid == idx).astype(F32)
    rank_e = carry[:, 0:1] + jnp.dot(onehot.astype(BF16), before[...],
                                     preferred_element_type=F32)
    carry[...] = carry[...] + jnp.sum(onehot, axis=1, keepdims=True)
    cnt_ref[...] = carry[...]

    mrow = lax.broadcasted_iota(I32, (META_ROWS, tm), 0)
    meta = jnp.zeros((META_ROWS, tm), F32)
    for k in range(TOP_K):
        rank_k = jnp.sum(jnp.where(eid == idxs[k], rank_e, 0.0), axis=0, keepdims=True)
        meta = jnp.where(mrow == META_EIDX + k, idxs[k].astype(F32), meta)
        meta = jnp.where(mrow == META_GATE + k, exps[k] / den, meta)
        meta = jnp.where(mrow == META_RANK + k, rank_k, meta)
    meta_ref[...] = meta


def _outproj(x2, y_pool, y_attn, w_out, g, w_router_t, b_router_t):
    t = x2.shape[0]
    row = lambda i: (i, 0)
    const = lambda i: (0, 0)
    return pl.pallas_call(
        _outproj_kernel,
        grid=(t // ROW_TILE,),
        in_specs=[
            pl.BlockSpec((ROW_TILE, D_MODEL), row),
            pl.BlockSpec((ROW_TILE, D_POOL), row),
            pl.BlockSpec((ROW_TILE, D_ATTN), row),
            pl.BlockSpec((D_MODEL, D_MODEL), const),
            pl.BlockSpec((1, D_MODEL), const),
            pl.BlockSpec((N_EXPERTS, D_MODEL), const),
            pl.BlockSpec((N_EXPERTS, LANES), const),
        ],
        out_specs=[
            pl.BlockSpec((ROW_TILE, D_MODEL), row),
            pl.BlockSpec((ROW_TILE, D_PACKED), row),
            pl.BlockSpec((META_ROWS, ROW_TILE), lambda i: (0, i)),
            pl.BlockSpec((N_EXPERTS, LANES), const),
        ],
        out_shape=[
            jax.ShapeDtypeStruct((t, D_MODEL), F32),
            jax.ShapeDtypeStruct((t, D_PACKED), I32),
            jax.ShapeDtypeStruct((META_ROWS, t), F32),
            jax.ShapeDtypeStruct((N_EXPERTS, LANES), F32),
        ],
        scratch_shapes=[pltpu.VMEM((D_MODEL, D_MODEL), BF16),
                        pltpu.VMEM((ROW_TILE, ROW_TILE), BF16),
                        pltpu.VMEM((N_EXPERTS, LANES), F32)],
        compiler_params=pltpu.CompilerParams(
            dimension_semantics=("arbitrary",), vmem_limit_bytes=VMEM_LIMIT),
        name="outproj_router",
    )(x2, y_pool, y_attn, w_out, g, w_router_t, b_router_t)


def _sc_mesh():
    return plsc.VectorSubcoreMesh(core_axis_name="core", subcore_axis_name="subcore")


def _sc_dispatch(h2, pos_rows, p_max):
    t = h2.shape[0]

    @functools.partial(
        pl.kernel, mesh=_sc_mesh(),
        out_type=[jax.ShapeDtypeStruct((p_max, D_PART), h2.dtype)] * N_PART)
    def run(h_hbm, *refs):
        pos_hbm, xs_hbm = refs[:TOP_K], refs[TOP_K:]
        for c in range(N_PART):
            def body(x_vmem, *idx_vmem, dst=xs_hbm[c]):
                for iv in idx_vmem:
                    pltpu.sync_copy(x_vmem, dst.at[iv.at[0]])

            pltpu.emit_pipeline(
                body,
                grid=(t // SC_WINDOW,),
                in_specs=[pl.BlockSpec((SC_WINDOW, D_PART), lambda i, c=c: (i, c))]
                + [pl.BlockSpec((1, SC_WINDOW), lambda i: (0, i))] * TOP_K,
                out_specs=[],
                core_axis_name=("core", "subcore"),
                dimension_semantics=(pltpu.PARALLEL,),
            )(h_hbm, *pos_hbm)

    return run(h2, *pos_rows)


def _sc_unpermute(y_quarters, idx_row):
    n = idx_row.shape[1]

    @functools.partial(
        pl.kernel, mesh=_sc_mesh(),
        out_type=[jax.ShapeDtypeStruct((n, D_PART), y_quarters[0].dtype)] * N_PART)
    def run(*refs):
        y_hbm, i_hbm, o_hbm = refs[:N_PART], refs[N_PART], refs[N_PART + 1:]
        for c in range(N_PART):
            def body(i_vmem, o_vmem, src=y_hbm[c]):
                pltpu.sync_copy(src.at[i_vmem.at[0]], o_vmem)

            pltpu.emit_pipeline(
                body,
                grid=(n // SC_WINDOW,),
                in_specs=[pl.BlockSpec((1, SC_WINDOW), lambda i: (0, i))],
                out_specs=[pl.BlockSpec((SC_WINDOW, D_PART), lambda i: (i, 0))],
                core_axis_name=("core", "subcore"),
                dimension_semantics=(pltpu.PARALLEL,),
            )(i_hbm, o_hbm[c])

    return run(*y_quarters, idx_row)


def _moe_kernel(te_ref, nv_ref, rows_ref, nxt_ref, *refs):
    xs_refs = refs[:N_PART]
    bgu_ref, bd_ref, wgu_hbm, wd_hbm = refs[N_PART:N_PART + 4]
    y_refs = refs[N_PART + 4:2 * N_PART + 4]
    wgu_f32, wd_f32, wgu_bf, wd_bf, sem = refs[2 * N_PART + 4:]
    i = pl.program_id(0)

    def weight_copies(e):
        return (pltpu.make_async_copy(wgu_hbm.at[e], wgu_f32, sem.at[0]),
                pltpu.make_async_copy(wd_hbm.at[e], wd_f32, sem.at[1]))

    @pl.when(i < nv_ref[0])
    def _():
        prev = te_ref[jnp.maximum(i - 1, 0)]
        new_expert = jnp.logical_or(i == 0, te_ref[i] != prev)

        @pl.when(i == 0)
        def _():
            for cp in weight_copies(te_ref[0]):
                cp.start()

        @pl.when(new_expert)
        def _():
            for cp in weight_copies(te_ref[i]):
                cp.wait()
            wgu_bf[...] = wgu_f32[...].astype(BF16)
            wd_bf[...] = wd_f32[...].astype(BF16)

            @pl.when(nxt_ref[i] >= 0)
            def _():
                for cp in weight_copies(nxt_ref[i]):
                    cp.start()

        d_e = wd_bf.shape[0]

        def ffn(n_rows):
            x = _unpack_row(jnp.concatenate([r[0:n_rows, :] for r in xs_refs], axis=1))
            rid = lax.broadcasted_iota(I32, (n_rows, 1), 0)
            x = jnp.where(rid < rows_ref[i], x, 0.0).astype(BF16)
            gu = jnp.dot(x, wgu_bf[...], preferred_element_type=F32) + bgu_ref[0]
            gate = jnp.minimum(gu[:, :d_e], SWIGLU_LIMIT)
            lin = jnp.clip(gu[:, d_e:], -SWIGLU_LIMIT, SWIGLU_LIMIT)
            act = gate * jax.nn.sigmoid(SWIGLU_ALPHA * gate) * (lin + 1.0)
            y = jnp.dot(act.astype(BF16), wd_bf[...], preferred_element_type=F32) + bd_ref[0]
            yp = _pack_row(y)
            for c, y_ref in enumerate(y_refs):
                y_ref[0:n_rows, :] = yp[:, c * D_PART:(c + 1) * D_PART]
                if n_rows < MOE_TILE:
                    y_ref[n_rows:, :] = jnp.zeros((MOE_TILE - n_rows, D_PART), I32)

        half = MOE_TILE // 2

        @pl.when(rows_ref[i] > half)
        def _():
            ffn(MOE_TILE)

        @pl.when(rows_ref[i] <= half)
        def _():
            ffn(half)

    @pl.when(i >= nv_ref[0])
    def _():
        for y_ref in y_refs:
            y_ref[...] = jnp.zeros_like(y_ref)


def _moe(tile_expert, n_valid, tile_rows, next_expert, xs_parts, w_gu, b_gu, w_down, b_down):
    p_max = xs_parts[0].shape[0]
    n_tiles = p_max // MOE_TILE
    d_e = w_down.shape[1]

    def row(i, te, nv, tr, nx):
        return (jnp.minimum(i, nv[0] - 1), 0)

    def expert(i, te, nv, tr, nx):
        return (te[jnp.minimum(i, nv[0] - 1)], 0, 0)

    grid_spec = pltpu.PrefetchScalarGridSpec(
        num_scalar_prefetch=4,
        grid=(n_tiles,),
        in_specs=[pl.BlockSpec((MOE_TILE, D_PART), row)] * N_PART + [
            pl.BlockSpec((1, 1, 2 * d_e), expert),
            pl.BlockSpec((1, 1, D_MODEL), expert),
            pl.BlockSpec(memory_space=pl.ANY),
            pl.BlockSpec(memory_space=pl.ANY),
        ],
        out_specs=[pl.BlockSpec((MOE_TILE, D_PART), lambda i, te, nv, tr, nx: (i, 0))] * N_PART,
        scratch_shapes=[pltpu.VMEM((D_MODEL, 2 * d_e), F32), pltpu.VMEM((d_e, D_MODEL), F32),
                        pltpu.VMEM((D_MODEL, 2 * d_e), BF16), pltpu.VMEM((d_e, D_MODEL), BF16),
                        pltpu.SemaphoreType.DMA((2,))],
    )
    return pl.pallas_call(
        _moe_kernel,
        grid_spec=grid_spec,
        out_shape=[jax.ShapeDtypeStruct((p_max, D_PART), I32)] * N_PART,
        compiler_params=pltpu.CompilerParams(
            dimension_semantics=("arbitrary",), vmem_limit_bytes=VMEM_LIMIT),
        name="moe_ffn",
    )(tile_expert, n_valid, tile_rows, next_expert, *xs_parts, b_gu, b_down, w_gu, w_down)


def _combine_kernel(x1_ref, gate_ref, g_ref, *refs):
    yk_refs, o_ref = refs[:N_PART], refs[-1]
    acc = x1_ref[...]
    for k in range(TOP_K):
        y_k = _unpack_row(jnp.concatenate([r[k] for r in yk_refs], axis=1))
        acc = acc + gate_ref[:, k:k + 1] * y_k
    o_ref[...] = _rms(acc, g_ref[...])


def _combine(x1, gates, g, yk_parts):
    t, d = x1.shape
    tq = COMBINE_TILE
    row = lambda i: (i, 0)
    return pl.pallas_call(
        _combine_kernel,
        grid=(t // tq,),
        in_specs=[
            pl.BlockSpec((tq, d), row),
            pl.BlockSpec((tq, TOP_K), row),
            pl.BlockSpec((1, d), lambda i: (0, 0)),
        ] + [pl.BlockSpec((TOP_K, tq, D_PART), lambda i: (0, i, 0))] * N_PART,
        out_specs=pl.BlockSpec((tq, d), row),
        out_shape=jax.ShapeDtypeStruct((t, d), F32),
        compiler_params=pltpu.CompilerParams(
            dimension_semantics=("arbitrary",), vmem_limit_bytes=VMEM_LIMIT),
        name="combine_final",
    )(x1, gates, g, *yk_parts)


def kernel(x, g_mix, w_in, w_pool, pool_scale, w_out, g_ffn, w_router, b_router,
           w_gu, b_gu, w_down, b_down, g_final):
    batch, seq, d = x.shape
    t = batch * seq
    x2 = x.reshape(t, d)

    y_pool, qkv = _inproj(x2, g_mix[0].reshape(1, d), w_in[0], w_pool[0],
                          pool_scale[0].reshape(1, D_POOL), seq)
    y_attn = _attention(qkv, batch, seq)

    wr_t = w_router[0].T
    br_t = jnp.broadcast_to(b_router[0].reshape(N_EXPERTS, 1), (N_EXPERTS, LANES))
    x1, h2, meta, cnt = _outproj(x2, y_pool, y_attn, w_out[0], g_ffn[0].reshape(1, d), wr_t, br_t)

    eidx = meta[META_EIDX:META_EIDX + TOP_K].astype(I32)
    rank = meta[META_RANK:META_RANK + TOP_K].astype(I32)
    gates = meta[META_GATE:META_GATE + TOP_K].T
    counts = cnt[:, 0].astype(I32)
    padded = ((counts + MOE_TILE - 1) // MOE_TILE) * MOE_TILE
    ends = jnp.cumsum(padded)
    offsets = ends - padded
    e_ids = jnp.arange(N_EXPERTS, dtype=I32)
    hit = eidx[None] == e_ids[:, None, None]
    pos = rank + jnp.sum(jnp.where(hit, offsets[:, None, None], 0), axis=0)
    p_max = t * TOP_K + N_EXPERTS * MOE_TILE
    n_tiles = p_max // MOE_TILE
    tile_start = jnp.arange(n_tiles, dtype=I32) * MOE_TILE
    tile_expert = jnp.minimum(
        jnp.sum((tile_start[:, None] >= ends[None, :]).astype(I32), axis=1), N_EXPERTS - 1)
    mine = tile_expert[:, None] == e_ids[None, :]
    group_end = jnp.sum(jnp.where(mine, (offsets + counts)[None, :], 0), axis=1)
    tile_rows = jnp.clip(group_end - tile_start, 0, MOE_TILE)
    n_valid = (ends[-1] // MOE_TILE).reshape(1).astype(I32)
    later = (e_ids[None, :] > e_ids[:, None]) & (counts[None, :] > 0)
    next_nonempty = jnp.min(jnp.where(later, e_ids[None, :], N_EXPERTS), axis=1)
    next_nonempty = jnp.where(next_nonempty < N_EXPERTS, next_nonempty, -1)
    next_expert = jnp.sum(jnp.where(mine, next_nonempty[None, :], 0), axis=1)

    xs = _sc_dispatch(h2, [pos[k:k + 1] for k in range(TOP_K)], p_max)
    ys = _moe(tile_expert, n_valid, tile_rows, next_expert, xs, w_gu[0],
              b_gu[0].reshape(N_EXPERTS, 1, -1), w_down[0], b_down[0].reshape(N_EXPERTS, 1, -1))
    yk = _sc_unpermute(ys, pos.reshape(1, TOP_K * t))
    yk = [q.reshape(TOP_K, t, D_PART) for q in yk]
    out = _combine(x1, gates, g_final.reshape(1, d), yk)
    return out.reshape(batch, seq, d)
```

```python
import functools

import jax
import jax.numpy as jnp
from jax import lax
from jax.experimental import pallas as pl
from jax.experimental.pallas import tpu as pltpu
from jax.experimental.pallas import tpu_sc as plsc

F32 = jnp.float32
BF16 = jnp.bfloat16
I32 = jnp.int32

D_MODEL = 1024
D_POOL = 512
D_ATTN = 512
POOL_WINDOWS = (2, 4, 8, 16)
POOL_GROUP = 128
HEAD_DIM = 64
N_EXPERTS = 32
TOP_K = 4
SWIGLU_LIMIT = 7.0
SWIGLU_ALPHA = 1.702
EPS = 1e-5
NEG_INF = -1e30
LOG2_E = 1.4426950408889634

LANES = 128
QB = 128
DIL_MID = 4
DIL_FAR = 16
ROW_TILE = 512
MOE_TILE = 512
COMBINE_TILE = 256
SC_WINDOW = 128
D_PACKED = D_MODEL // 2
N_PART = 2
D_PART = D_PACKED // N_PART


def _pack_row(x):
    hi = lax.bitcast_convert_type(x[:, :D_PACKED].astype(BF16).astype(F32), I32)
    lo = lax.bitcast_convert_type(x[:, D_PACKED:].astype(BF16).astype(F32), I32)
    return hi | lax.shift_right_logical(lo, 16)


def _unpack_row(w):
    hi = lax.bitcast_convert_type(w & jnp.int32(-65536), F32)
    lo = lax.bitcast_convert_type(lax.shift_left(w, 16), F32)
    return jnp.concatenate([hi, lo], axis=1)
VMEM_LIMIT = 56 * 1024 * 1024


def _rms(x, g):
    ms = jnp.mean(x * x, axis=-1, keepdims=True)
    return x * lax.rsqrt(ms + EPS) * g


POOL_PAD = 16


def _inproj_kernel(x_ref, g_ref, w_ref, wp_ref, sc_ref, yp_ref, qkv_ref, w_bf, hist, pad_ref,
                   *, tiles_per_seq):
    i = pl.program_id(0)
    tm = x_ref.shape[0]

    @pl.when(i == 0)
    def _():
        w_bf[...] = w_ref[...].astype(BF16)
        pad_ref[0:POOL_PAD, :] = jnp.zeros((POOL_PAD, POOL_GROUP), F32)

    @pl.when(i % tiles_per_seq == 0)
    def _():
        hist[...] = jnp.zeros_like(hist)

    h = _rms(x_ref[...], g_ref[...]).astype(BF16)
    proj = jnp.dot(h, w_bf[...], preferred_element_type=F32)
    q = proj[:, D_POOL:D_POOL + D_ATTN] * (HEAD_DIM ** -0.5 * LOG2_E)
    qkv_ref[:, :D_ATTN] = q.astype(BF16)
    qkv_ref[:, D_ATTN:] = proj[:, D_POOL + D_ATTN:].astype(BF16)

    pos = (i % tiles_per_seq) * tm + lax.broadcasted_iota(I32, (tm, 1), 0)
    for g, w in enumerate(POOL_WINDOWS):
        lo, hi = g * POOL_GROUP, (g + 1) * POOL_GROUP
        e = proj[:, lo:hi]
        acc = jnp.concatenate([hist[g], e], axis=0)
        hist[g] = e[tm - POOL_PAD:, :]
        span = 1
        while span < w:
            pad_ref[POOL_PAD:, :] = acc
            acc = acc + pad_ref[pl.ds(POOL_PAD - span, POOL_PAD + tm), :]
            span *= 2
        count = jnp.minimum(pos + 1, w).astype(F32)
        pooled = acc[POOL_PAD:, :] / count - e
        y = jnp.dot(pooled.astype(BF16), wp_ref[g].astype(BF16), preferred_element_type=F32)
        yp_ref[:, lo:hi] = (y * sc_ref[:, lo:hi]).astype(BF16)


def _inproj(x2, g, w, w_pool, pool_scale, seq):
    t = x2.shape[0]
    d_in = w.shape[1]
    n_grp = len(POOL_WINDOWS)
    return pl.pallas_call(
        functools.partial(_inproj_kernel, tiles_per_seq=seq // ROW_TILE),
        grid=(t // ROW_TILE,),
        in_specs=[
            pl.BlockSpec((ROW_TILE, D_MODEL), lambda i: (i, 0)),
            pl.BlockSpec((1, D_MODEL), lambda i: (0, 0)),
            pl.BlockSpec((D_MODEL, d_in), lambda i: (0, 0)),
            pl.BlockSpec((n_grp, POOL_GROUP, POOL_GROUP), lambda i: (0, 0, 0)),
            pl.BlockSpec((1, D_POOL), lambda i: (0, 0)),
        ],
        out_specs=[
            pl.BlockSpec((ROW_TILE, D_POOL), lambda i: (i, 0)),
            pl.BlockSpec((ROW_TILE, 3 * D_ATTN), lambda i: (i, 0)),
        ],
        out_shape=[
            jax.ShapeDtypeStruct((t, D_POOL), BF16),
            jax.ShapeDtypeStruct((t, 3 * D_ATTN), BF16),
        ],
        scratch_shapes=[pltpu.VMEM((D_MODEL, d_in), BF16),
                        pltpu.VMEM((n_grp, POOL_PAD, POOL_GROUP), F32),
                        pltpu.VMEM((2 * POOL_PAD + ROW_TILE, POOL_GROUP), F32)],
        compiler_params=pltpu.CompilerParams(
            dimension_semantics=("arbitrary",), vmem_limit_bytes=VMEM_LIMIT),
        name="inproj_pool",
    )(x2, g, w, w_pool, pool_scale)


def _pair_block(q, k, v_ext, mask2, lane_h0):
    zero = jnp.zeros_like(q)
    q2 = jnp.concatenate([jnp.where(lane_h0, q, zero), jnp.where(lane_h0, zero, q)], axis=0)
    s = lax.dot_general(q2, k, (((1,), (1,)), ((), ())), preferred_element_type=F32)
    s = jnp.where(mask2, s, NEG_INF)
    m = jnp.max(s, axis=-1, keepdims=True)
    p = jnp.exp2(s - m).astype(BF16)
    ol = jnp.dot(p, v_ext, preferred_element_type=F32)
    o = jnp.where(lane_h0, ol[:QB, :LANES], ol[QB:, :LANES])
    l = jnp.where(lane_h0, ol[:QB, LANES:], ol[QB:, LANES:])
    mb = jnp.where(lane_h0, m[:QB], m[QB:])
    return o, mb, l


def _merge(o_a, m_a, l_a, o_b, m_b, l_b):
    m = jnp.maximum(m_a, m_b)
    ea = jnp.exp2(m_a - m)
    eb = jnp.exp2(m_b - m)
    return o_a * ea + o_b * eb, m, l_a * ea + l_b * eb


def _both_heads(mask):
    return jnp.concatenate([mask, mask], axis=0)


def _attn_kernel(q_ref, k_ref, v_ref, o_ref, qp, kp, vpx, vnx, o23, m23, l23, o2, m2, l2,
                 on, mn):
    s = q_ref.shape[0]
    n_chunk = DIL_FAR
    rows = s // n_chunk
    sub = rows // DIL_MID
    grp = n_chunk * n_chunk
    lane_h0 = lax.broadcasted_iota(I32, (1, LANES), 1) < HEAD_DIM

    @pl.when((pl.program_id(0) == 0) & (pl.program_id(1) == 0))
    def _():
        ones = jnp.ones((s, LANES), BF16)
        vnx[:, LANES:] = ones
        vpx[:, LANES:] = ones

    vnx[:, :LANES] = v_ref[...]

    pi = lax.broadcasted_iota(I32, (grp, grp), 0)
    pj = lax.broadcasted_iota(I32, (grp, grp), 1)
    swap = (pj == (pi % n_chunk) * n_chunk + pi // n_chunk).astype(BF16)
    for g in range(s // grp):
        src = slice(g * grp, (g + 1) * grp)
        qk = jnp.concatenate([q_ref[src, :], k_ref[src, :]], axis=1)
        yqk = jnp.dot(swap, qk, preferred_element_type=F32).astype(BF16)
        yv = jnp.dot(swap, v_ref[src, :], preferred_element_type=F32).astype(BF16)
        for r in range(n_chunk):
            dst = slice(r * rows + g * n_chunk, r * rows + (g + 1) * n_chunk)
            part = slice(r * n_chunk, (r + 1) * n_chunk)
            qp[dst, :] = yqk[part, :LANES]
            kp[dst, :] = yqk[part, LANES:]
            vpx[dst, 0:LANES] = yv[part]

    qi = lax.broadcasted_iota(I32, (QB, QB), 0)
    kj = lax.broadcasted_iota(I32, (QB, QB), 1)
    causal = _both_heads(qi >= kj)

    for r in range(n_chunk):
        blk = slice(r * rows, (r + 1) * rows)
        o, m, l = _pair_block(qp[blk, :], kp[blk, :], vpx[blk, :], causal, lane_h0)
        o23[blk, :] = o
        m23[blk, :] = m
        l23[blk, :] = l

    def mid_index(n_key_sub):
        i_q = lax.broadcasted_iota(I32, (QB, DIL_MID * n_key_sub), 0)
        i_k = lax.broadcasted_iota(I32, (QB, DIL_MID * n_key_sub), 1)
        return i_q // sub, i_q % sub, i_k // n_key_sub, i_k % n_key_sub

    jq, aq, jk, ak = mid_index(sub)
    d0 = DIL_FAR * (aq - ak) + DIL_MID * (jq - jk)
    mask_mid0 = _both_heads(d0 >= 0)
    jq, aq, jk, ak = mid_index(2 * sub)
    d1 = DIL_FAR * (aq + sub - ak) + DIL_MID * (jq - jk)
    mask_mid = _both_heads((d1 >= 0) & (d1 <= DIL_MID * QB))

    def mid_tile(r4, a0, k0, nk, mask):
        def at(j, off, n):
            return pl.ds((DIL_MID * j + r4) * rows + off, n)

        q = jnp.concatenate([qp[at(j, a0, sub), :] for j in range(DIL_MID)], axis=0)
        k = jnp.concatenate([kp[at(j, k0, nk), :] for j in range(DIL_MID)], axis=0)
        v = jnp.concatenate([vpx[at(j, k0, nk), :] for j in range(DIL_MID)], axis=0)
        o, m, l = _pair_block(q, k, v, mask, lane_h0)
        for j in range(DIL_MID):
            dst = at(j, a0, sub)
            src = slice(j * sub, (j + 1) * sub)
            o2[dst, :] = o[src]
            m2[dst, :] = m[src]
            l2[dst, :] = l[src]

    for r4 in range(DIL_MID):
        mid_tile(r4, 0, 0, sub, mask_mid0)
        for a_blk in range(1, rows // sub):
            mid_tile(r4, a_blk * sub, (a_blk - 1) * sub, 2 * sub, mask_mid)

    for r in range(n_chunk):
        blk = slice(r * rows, (r + 1) * rows)
        oo, mm, ll = _merge(o23[blk, :], m23[blk, :], l23[blk, :],
                            o2[blk, :], m2[blk, :], l2[blk, :])
        o23[blk, :] = oo / ll
        m23[blk, :] = mm + jnp.log2(ll)

    for g in range(s // grp):
        slabs = [o23[r * rows + g * n_chunk:r * rows + (g + 1) * n_chunk, :].astype(BF16)
                 for r in range(n_chunk)]
        on[g * grp:(g + 1) * grp, :] = jnp.dot(swap, jnp.concatenate(slabs, axis=0),
                                               preferred_element_type=F32)
    for r in range(n_chunk):
        mn[pl.ds(r, rows, stride=n_chunk), :] = m23[r * rows:(r + 1) * rows, :]

    qi2 = lax.broadcasted_iota(I32, (QB, 2 * QB), 0)
    kj2 = lax.broadcasted_iota(I32, (QB, 2 * QB), 1)
    dn = qi2 + QB - kj2
    mask_near = _both_heads((dn >= 0) & (dn <= QB))

    def near_finish(dst, o, m, l):
        oo, _, ll = _merge(on[dst, :], mn[dst, :], 1.0, o, m, l)
        o_ref[dst, :] = (oo / ll).astype(o_ref.dtype)

    first = pl.ds(0, QB)
    o, m, l = _pair_block(q_ref[first, :], k_ref[first, :], vnx[first, :], causal, lane_h0)
    near_finish(first, o, m, l)
    for n in range(1, s // QB):
        keys = pl.ds((n - 1) * QB, 2 * QB)
        o, m, l = _pair_block(q_ref[pl.ds(n * QB, QB), :], k_ref[keys, :], vnx[keys, :],
                              mask_near, lane_h0)
        near_finish(pl.ds(n * QB, QB), o, m, l)


def _attention(qkv, batch, seq):
    n_pair = D_ATTN // LANES
    blk = (seq, LANES)
    f32_scr = pltpu.VMEM(blk, F32)
    bf_scr = pltpu.VMEM(blk, BF16)
    bfx_scr = pltpu.VMEM((seq, 2 * LANES), BF16)
    return pl.pallas_call(
        _attn_kernel,
        grid=(batch, n_pair),
        in_specs=[
            pl.BlockSpec(blk, lambda b, h: (b, h)),
            pl.BlockSpec(blk, lambda b, h: (b, n_pair + h)),
            pl.BlockSpec(blk, lambda b, h: (b, 2 * n_pair + h)),
        ],
        out_specs=pl.BlockSpec(blk, lambda b, h: (b, h)),
        out_shape=jax.ShapeDtypeStruct((batch * seq, D_ATTN), BF16),
        scratch_shapes=[bf_scr, bf_scr, bfx_scr, bfx_scr] + [f32_scr] * 8,
        compiler_params=pltpu.CompilerParams(
            dimension_semantics=("arbitrary", "arbitrary"), vmem_limit_bytes=VMEM_LIMIT),
        name="dilated_attn",
    )(qkv, qkv, qkv)


META_ROWS = 16
META_EIDX, META_GATE, META_RANK = 0, 4, 8


def _outproj_kernel(x_ref, yp_ref, ya_ref, wo_ref, g_ref, wrt_ref, brt_ref,
                    x1_ref, h2_ref, meta_ref, cnt_ref, wo_bf, before, carry):
    tm = x_ref.shape[0]

    @pl.when(pl.program_id(0) == 0)
    def _():
        wo_bf[...] = wo_ref[...].astype(BF16)
        carry[...] = jnp.zeros_like(carry)
        ti = lax.broadcasted_iota(I32, (tm, tm), 0)
        tj = lax.broadcasted_iota(I32, (tm, tm), 1)
        before[...] = (ti < tj).astype(BF16)

    x1 = (x_ref[...]
          + jnp.dot(yp_ref[...], wo_bf[:D_POOL, :], preferred_element_type=F32)
          + jnp.dot(ya_ref[...], wo_bf[D_POOL:, :], preferred_element_type=F32))
    x1_ref[...] = x1
    h2 = _rms(x1, g_ref[...])
    h2_ref[...] = _pack_row(h2)

    logits_t = lax.dot_general(wrt_ref[...].astype(BF16), h2.astype(BF16),
                               (((1,), (1,)), ((), ())), preferred_element_type=F32)
    logits_t = logits_t + brt_ref[:, 0:1]
    eid = lax.broadcasted_iota(I32, (N_EXPERTS, tm), 0)
    work = logits_t
    idxs, vals = [], []
    for _ in range(TOP_K):
        mx = jnp.max(work, axis=0, keepdims=True)
        idx = jnp.min(jnp.where(work == mx, eid, N_EXPERTS), axis=0, keepdims=True)
        idxs.append(idx)
        vals.append(mx)
        work = jnp.where(eid == idx, -jnp.inf, work)
    exps = [jnp.exp(v - vals[0]) for v in vals]
    den = exps[0] + exps[1] + exps[2] + exps[3]

    onehot = jnp.zeros((N_EXPERTS, tm), F32)
    for idx in idxs:
        onehot = onehot + (eid == idx).astype(F32)
    rank_e = carry[:, 0:1] + jnp.dot(onehot.astype(BF16), before[...],
                                     preferred_element_type=F32)
    carry[...] = carry[...] + jnp.sum(onehot, axis=1, keepdims=True)
    cnt_ref[...] = carry[...]

    mrow = lax.broadcasted_iota(I32, (META_ROWS, tm), 0)
    meta = jnp.zeros((META_ROWS, tm), F32)
    for k in range(TOP_K):
        rank_k = jnp.sum(jnp.where(eid == idxs[k], rank_e, 0.0), axis=0, keepdims=True)
        meta = jnp.where(mrow == META_EIDX + k, idxs[k].astype(F32), meta)
        meta = jnp.where(mrow == META_GATE + k, exps[k] / den, meta)
        meta = jnp.where(mrow == META_RANK + k, rank_k, meta)
    meta_ref[...] = meta


def _outproj(x2, y_pool, y_attn, w_out, g, w_router_t, b_router_t):
    t = x2.shape[0]
    row = lambda i: (i, 0)
    const = lambda i: (0, 0)
    return pl.pallas_call(
        _outproj_kernel,
        grid=(t // ROW_TILE,),
        in_specs=[
            pl.BlockSpec((ROW_TILE, D_MODEL), row),
            pl.BlockSpec((ROW_TILE, D_POOL), row),
            pl.BlockSpec((ROW_TILE, D_ATTN), row),
            pl.BlockSpec((D_MODEL, D_MODEL), const),
            pl.BlockSpec((1, D_MODEL), const),
            pl.BlockSpec((N_EXPERTS, D_MODEL), const),
            pl.BlockSpec((N_EXPERTS, LANES), const),
        ],
        out_specs=[
            pl.BlockSpec((ROW_TILE, D_MODEL), row),
            pl.BlockSpec((ROW_TILE, D_PACKED), row),
            pl.BlockSpec((META_ROWS, ROW_TILE), lambda i: (0, i)),
            pl.BlockSpec((N_EXPERTS, LANES), const),
        ],
        out_shape=[
            jax.ShapeDtypeStruct((t, D_MODEL), F32),
            jax.ShapeDtypeStruct((t, D_PACKED), I32),
            jax.ShapeDtypeStruct((META_ROWS, t), F32),
            jax.ShapeDtypeStruct((N_EXPERTS, LANES), F32),
        ],
        scratch_shapes=[pltpu.VMEM((D_MODEL, D_MODEL), BF16),
                        pltpu.VMEM((ROW_TILE, ROW_TILE), BF16),
                        pltpu.VMEM((N_EXPERTS, LANES), F32)],
        compiler_params=pltpu.CompilerParams(
            dimension_semantics=("arbitrary",), vmem_limit_bytes=VMEM_LIMIT),
        name="outproj_router",
    )(x2, y_pool, y_attn, w_out, g, w_router_t, b_router_t)


def _sc_mesh():
    return plsc.VectorSubcoreMesh(core_axis_name="core", subcore_axis_name="subcore")


def _sc_dispatch(h2, pos_rows, p_max):
    t = h2.shape[0]

    @functools.partial(
        pl.kernel, mesh=_sc_mesh(),
        out_type=[jax.ShapeDtypeStruct((p_max, D_PART), h2.dtype)] * N_PART)
    def run(h_hbm, *refs):
        pos_hbm, xs_hbm = refs[:TOP_K], refs[TOP_K:]
        for c in range(N_PART):
            def body(x_vmem, *idx_vmem, dst=xs_hbm[c]):
                for iv in idx_vmem:
                    pltpu.sync_copy(x_vmem, dst.at[iv.at[0]])

            pltpu.emit_pipeline(
                body,
                grid=(t // SC_WINDOW,),
                in_specs=[pl.BlockSpec((SC_WINDOW, D_PART), lambda i, c=c: (i, c))]
                + [pl.BlockSpec((1, SC_WINDOW), lambda i: (0, i))] * TOP_K,
                out_specs=[],
                core_axis_name=("core", "subcore"),
                dimension_semantics=(pltpu.PARALLEL,),
            )(h_hbm, *pos_hbm)

    return run(h2, *pos_rows)


def _sc_unpermute(y_quarters, idx_row):
    n = idx_row.shape[1]

    @functools.partial(
        pl.kernel, mesh=_sc_mesh(),
        out_type=[jax.ShapeDtypeStruct((n, D_PART), y_quarters[0].dtype)] * N_PART)
    def run(*refs):
        y_hbm, i_hbm, o_hbm = refs[:N_PART], refs[N_PART], refs[N_PART + 1:]
        for c in range(N_PART):
            def body(i_vmem, o_vmem, src=y_hbm[c]):
                pltpu.sync_copy(src.at[i_vmem.at[0]], o_vmem)

            pltpu.emit_pipeline(
                body,
                grid=(n // SC_WINDOW,),
                in_specs=[pl.BlockSpec((1, SC_WINDOW), lambda i: (0, i))],
                out_specs=[pl.BlockSpec((SC_WINDOW, D_PART), lambda i: (i, 0))],
                core_axis_name=("core", "subcore"),
                dimension_semantics=(pltpu.PARALLEL,),
            )(i_hbm, o_hbm[c])

    return run(*y_quarters, idx_row)


def _moe_kernel(te_ref, nv_ref, rows_ref, nxt_ref, *refs):
    xs_refs = refs[:N_PART]
    bgu_ref, bd_ref, wgu_hbm, wd_hbm = refs[N_PART:N_PART + 4]
    y_refs = refs[N_PART + 4:2 * N_PART + 4]
    wgu_f32, wd_f32, wgu_bf, wd_bf, sem = refs[2 * N_PART + 4:]
    i = pl.program_id(0)

    def weight_copies(e):
        return (pltpu.make_async_copy(wgu_hbm.at[e], wgu_f32, sem.at[0]),
                pltpu.make_async_copy(wd_hbm.at[e], wd_f32, sem.at[1]))

    @pl.when(i < nv_ref[0])
    def _():
        prev = te_ref[jnp.maximum(i - 1, 0)]
        new_expert = jnp.logical_or(i == 0, te_ref[i] != prev)

        @pl.when(i == 0)
        def _():
            for cp in weight_copies(te_ref[0]):
                cp.start()

        @pl.when(new_expert)
        def _():
            for cp in weight_copies(te_ref[i]):
                cp.wait()
            wgu_bf[...] = wgu_f32[...].astype(BF16)
            wd_bf[...] = wd_f32[...].astype(BF16)

            @pl.when(nxt_ref[i] >= 0)
            def _():
                for cp in weight_copies(nxt_ref[i]):
                    cp.start()

        d_e = wd_bf.shape[0]

        def ffn(n_rows):
            x = _unpack_row(jnp.concatenate([r[0:n_rows, :] for r in xs_refs], axis=1))
            rid = lax.broadcasted_iota(I32, (n_rows, 1), 0)
            x = jnp.where(rid < rows_ref[i], x, 0.0).astype(BF16)
            gu = jnp.dot(x, wgu_bf[...], preferred_element_type=F32) + bgu_ref[0]
            gate = jnp.minimum(gu[:, :d_e], SWIGLU_LIMIT)
            lin = jnp.clip(gu[:, d_e:], -SWIGLU_LIMIT, SWIGLU_LIMIT)
            act = gate * jax.nn.sigmoid(SWIGLU_ALPHA * gate) * (lin + 1.0)
            y = jnp.dot(act.astype(BF16), wd_bf[...], preferred_element_type=F32) + bd_ref[0]
            yp = _pack_row(y)
            for c, y_ref in enumerate(y_refs):
                y_ref[0:n_rows, :] = yp[:, c * D_PART:(c + 1) * D_PART]
                if n_rows < MOE_TILE:
                    y_ref[n_rows:, :] = jnp.zeros((MOE_TILE - n_rows, D_PART), I32)

        half = MOE_TILE // 2

        @pl.when(rows_ref[i] > half)
        def _():
            ffn(MOE_TILE)

        @pl.when(rows_ref[i] <= half)
        def _():
            ffn(half)

    @pl.when(i >= nv_ref[0])
    def _():
        for y_ref in y_refs:
            y_ref[...] = jnp.zeros_like(y_ref)


def _moe(tile_expert, n_valid, tile_rows, next_expert, xs_parts, w_gu, b_gu, w_down, b_down):
    p_max = xs_parts[0].shape[0]
    n_tiles = p_max // MOE_TILE
    d_e = w_down.shape[1]

    def row(i, te, nv, tr, nx):
        return (jnp.minimum(i, nv[0] - 1), 0)

    def expert(i, te, nv, tr, nx):
        return (te[jnp.minimum(i, nv[0] - 1)], 0, 0)

    grid_spec = pltpu.PrefetchScalarGridSpec(
        num_scalar_prefetch=4,
        grid=(n_tiles,),
        in_specs=[pl.BlockSpec((MOE_TILE, D_PART), row)] * N_PART + [
            pl.BlockSpec((1, 1, 2 * d_e), expert),
            pl.BlockSpec((1, 1, D_MODEL), expert),
            pl.BlockSpec(memory_space=pl.ANY),
            pl.BlockSpec(memory_space=pl.ANY),
        ],
        out_specs=[pl.BlockSpec((MOE_TILE, D_PART), lambda i, te, nv, tr, nx: (i, 0))] * N_PART,
        scratch_shapes=[pltpu.VMEM((D_MODEL, 2 * d_e), F32), pltpu.VMEM((d_e, D_MODEL), F32),
                        pltpu.VMEM((D_MODEL, 2 * d_e), BF16), pltpu.VMEM((d_e, D_MODEL), BF16),
                        pltpu.SemaphoreType.DMA((2,))],
    )
    return pl.pallas_call(
        _moe_kernel,
        grid_spec=grid_spec,
        out_shape=[jax.ShapeDtypeStruct((p_max, D_PART), I32)] * N_PART,
        compiler_params=pltpu.CompilerParams(
            dimension_semantics=("arbitrary",), vmem_limit_bytes=VMEM_LIMIT),
        name="moe_ffn",
    )(tile_expert, n_valid, tile_rows, next_expert, *xs_parts, b_gu, b_down, w_gu, w_down)


def _combine_kernel(x1_ref, gate_ref, g_ref, *refs):
    yk_refs, o_ref = refs[:N_PART], refs[-1]
    acc = x1_ref[...]
    for k in range(TOP_K):
        y_k = _unpack_row(jnp.concatenate([r[k] for r in yk_refs], axis=1))
        acc = acc + gate_ref[:, k:k + 1] * y_k
    o_ref[...] = _rms(acc, g_ref[...])


def _combine(x1, gates, g, yk_parts):
    t, d = x1.shape
    tq = COMBINE_TILE
    row = lambda i: (i, 0)
    return pl.pallas_call(
        _combine_kernel,
        grid=(t // tq,),
        in_specs=[
            pl.BlockSpec((tq, d), row),
            pl.BlockSpec((tq, TOP_K), row),
            pl.BlockSpec((1, d), lambda i: (0, 0)),
        ] + [pl.BlockSpec((TOP_K, tq, D_PART), lambda i: (0, i, 0))] * N_PART,
        out_specs=pl.BlockSpec((tq, d), row),
        out_shape=jax.ShapeDtypeStruct((t, d), F32),
        compiler_params=pltpu.CompilerParams(
            dimension_semantics=("arbitrary",), vmem_limit_bytes=VMEM_LIMIT),
        name="combine_final",
    )(x1, gates, g, *yk_parts)


def kernel(x, g_mix, w_in, w_pool, pool_scale, w_out, g_ffn, w_router, b_router,
           w_gu, b_gu, w_down, b_down, g_final):
    batch, seq, d = x.shape
    t = batch * seq
    x2 = x.reshape(t, d)

    y_pool, qkv = _inproj(x2, g_mix[0].reshape(1, d), w_in[0], w_pool[0],
                          pool_scale[0].reshape(1, D_POOL), seq)
    y_attn = _attention(qkv, batch, seq)

    wr_t = w_router[0].T
    br_t = jnp.broadcast_to(b_router[0].reshape(N_EXPERTS, 1), (N_EXPERTS, LANES))
    x1, h2, meta, cnt = _outproj(x2, y_pool, y_attn, w_out[0], g_ffn[0].reshape(1, d), wr_t, br_t)

    eidx = meta[META_EIDX:META_EIDX + TOP_K].astype(I32)
    rank = meta[META_RANK:META_RANK + TOP_K].astype(I32)
    gates = meta[META_GATE:META_GATE + TOP_K].T
    counts = cnt[:, 0].astype(I32)
    padded = ((counts + MOE_TILE - 1) // MOE_TILE) * MOE_TILE
    ends = jnp.cumsum(padded)
    offsets = ends - padded
    e_ids = jnp.arange(N_EXPERTS, dtype=I32)
    hit = eidx[None] == e_ids[:, None, None]
    pos = rank + jnp.sum(jnp.where(hit, offsets[:, None, None], 0), axis=0)
    p_max = t * TOP_K + N_EXPERTS * MOE_TILE
    n_tiles = p_max // MOE_TILE
    tile_start = jnp.arange(n_tiles, dtype=I32) * MOE_TILE
    tile_expert = jnp.minimum(
        jnp.sum((tile_start[:, None] >= ends[None, :]).astype(I32), axis=1), N_EXPERTS - 1)
    mine = tile_expert[:, None] == e_ids[None, :]
    group_end = jnp.sum(jnp.where(mine, (offsets + counts)[None, :], 0), axis=1)
    tile_rows = jnp.clip(group_end - tile_start, 0, MOE_TILE)
    n_valid = (ends[-1] // MOE_TILE).reshape(1).astype(I32)
    later = (e_ids[None, :] > e_ids[:, None]) & (counts[None, :] > 0)
    next_nonempty = jnp.min(jnp.where(later, e_ids[None, :], N_EXPERTS), axis=1)
    next_nonempty = jnp.where(next_nonempty < N_EXPERTS, next_nonempty, -1)
    next_expert = jnp.sum(jnp.where(mine, next_nonempty[None, :], 0), axis=1)

    xs = _sc_dispatch(h2, [pos[k:k + 1] for k in range(TOP_K)], p_max)
    ys = _moe(tile_expert, n_valid, tile_rows, next_expert, xs, w_gu[0],
              b_gu[0].reshape(N_EXPERTS, 1, -1), w_down[0], b_down[0].reshape(N_EXPERTS, 1, -1))
    yk = _sc_unpermute(ys, pos.reshape(1, TOP_K * t))
    yk = [q.reshape(TOP_K, t, D_PART) for q in yk]
    out = _combine(x1, gates, g_final.reshape(1, d), yk)
    return out.reshape(batch, seq, d)
```

```python
import functools

import jax
import jax.numpy as jnp
from jax import lax
from jax.experimental import pallas as pl
from jax.experimental.pallas import tpu as pltpu
from jax.experimental.pallas import tpu_sc as plsc

F32 = jnp.float32
BF16 = jnp.bfloat16
I32 = jnp.int32

D_MODEL = 1024
D_POOL = 512
D_ATTN = 512
POOL_WINDOWS = (2, 4, 8, 16)
POOL_GROUP = 128
HEAD_DIM = 64
N_EXPERTS = 32
TOP_K = 4
SWIGLU_LIMIT = 7.0
SWIGLU_ALPHA = 1.702
EPS = 1e-5
NEG_INF = -1e30
LOG2_E = 1.4426950408889634

LANES = 128
QB = 128
DIL_MID = 4
DIL_FAR = 16
ROW_TILE = 512
MOE_TILE = 512
MOE_CHUNKS = 4
SPARE_ROWS = 128
COMBINE_TILE = 256
SC_WINDOW = 128
D_PACKED = D_MODEL // 2
N_PART = 2
D_PART = D_PACKED // N_PART


def _pack_row(x):
    hi = lax.bitcast_convert_type(x[:, :D_PACKED].astype(BF16).astype(F32), I32)
    lo = lax.bitcast_convert_type(x[:, D_PACKED:].astype(BF16).astype(F32), I32)
    return hi | lax.shift_right_logical(lo, 16)


def _unpack_row(w):
    hi = lax.bitcast_convert_type(w & jnp.int32(-65536), F32)
    lo = lax.bitcast_convert_type(lax.shift_left(w, 16), F32)
    return jnp.concatenate([hi, lo], axis=1)
VMEM_LIMIT = 56 * 1024 * 1024


def _rms(x, g):
    ms = jnp.mean(x * x, axis=-1, keepdims=True)
    return x * lax.rsqrt(ms + EPS) * g


POOL_PAD = 16


def _inproj_kernel(x_ref, g_ref, w_ref, wp_ref, sc_ref, yp_ref, qkv_ref, w_bf, hist, pad_ref,
                   *, tiles_per_seq):
    i = pl.program_id(0)
    tm = x_ref.shape[0]

    @pl.when(i == 0)
    def _():
        w_bf[...] = w_ref[...].astype(BF16)
        pad_ref[0:POOL_PAD, :] = jnp.zeros((POOL_PAD, POOL_GROUP), F32)

    @pl.when(i % tiles_per_seq == 0)
    def _():
        hist[...] = jnp.zeros_like(hist)

    h = _rms(x_ref[...], g_ref[...]).astype(BF16)
    proj = jnp.dot(h, w_bf[...], preferred_element_type=F32)
    q = proj[:, D_POOL:D_POOL + D_ATTN] * (HEAD_DIM ** -0.5 * LOG2_E)
    qkv_ref[:, :D_ATTN] = q.astype(BF16)
    qkv_ref[:, D_ATTN:] = proj[:, D_POOL + D_ATTN:].astype(BF16)

    pos = (i % tiles_per_seq) * tm + lax.broadcasted_iota(I32, (tm, 1), 0)
    for g, w in enumerate(POOL_WINDOWS):
        lo, hi = g * POOL_GROUP, (g + 1) * POOL_GROUP
        e = proj[:, lo:hi]
        acc = jnp.concatenate([hist[g], e], axis=0)
        hist[g] = e[tm - POOL_PAD:, :]
        span = 1
        while span < w:
            pad_ref[POOL_PAD:, :] = acc
            acc = acc + pad_ref[pl.ds(POOL_PAD - span, POOL_PAD + tm), :]
            span *= 2
        count = jnp.minimum(pos + 1, w).astype(F32)
        pooled = acc[POOL_PAD:, :] / count - e
        y = jnp.dot(pooled.astype(BF16), wp_ref[g].astype(BF16), preferred_element_type=F32)
        yp_ref[:, lo:hi] = (y * sc_ref[:, lo:hi]).astype(BF16)


def _inproj(x2, g, w, w_pool, pool_scale, seq):
    t = x2.shape[0]
    d_in = w.shape[1]
    n_grp = len(POOL_WINDOWS)
    return pl.pallas_call(
        functools.partial(_inproj_kernel, tiles_per_seq=seq // ROW_TILE),
        grid=(t // ROW_TILE,),
        in_specs=[
            pl.BlockSpec((ROW_TILE, D_MODEL), lambda i: (i, 0)),
            pl.BlockSpec((1, D_MODEL), lambda i: (0, 0)),
            pl.BlockSpec((D_MODEL, d_in), lambda i: (0, 0)),
            pl.BlockSpec((n_grp, POOL_GROUP, POOL_GROUP), lambda i: (0, 0, 0)),
            pl.BlockSpec((1, D_POOL), lambda i: (0, 0)),
        ],
        out_specs=[
            pl.BlockSpec((ROW_TILE, D_POOL), lambda i: (i, 0)),
            pl.BlockSpec((ROW_TILE, 3 * D_ATTN), lambda i: (i, 0)),
        ],
        out_shape=[
            jax.ShapeDtypeStruct((t, D_POOL), BF16),
            jax.ShapeDtypeStruct((t, 3 * D_ATTN), BF16),
        ],
        scratch_shapes=[pltpu.VMEM((D_MODEL, d_in), BF16),
                        pltpu.VMEM((n_grp, POOL_PAD, POOL_GROUP), F32),
                        pltpu.VMEM((2 * POOL_PAD + ROW_TILE, POOL_GROUP), F32)],
        compiler_params=pltpu.CompilerParams(
            dimension_semantics=("arbitrary",), vmem_limit_bytes=VMEM_LIMIT),
        name="inproj_pool",
    )(x2, g, w, w_pool, pool_scale)


def _pair_block(q, k, v_ext, mask2, lane_h0):
    zero = jnp.zeros_like(q)
    q2 = jnp.concatenate([jnp.where(lane_h0, q, zero), jnp.where(lane_h0, zero, q)], axis=0)
    s = lax.dot_general(q2, k, (((1,), (1,)), ((), ())), preferred_element_type=F32)
    s = jnp.where(mask2, s, NEG_INF)
    m = jnp.max(s, axis=-1, keepdims=True)
    p = jnp.exp2(s - m).astype(BF16)
    ol = jnp.dot(p, v_ext, preferred_element_type=F32)
    o = jnp.where(lane_h0, ol[:QB, :LANES], ol[QB:, :LANES])
    l = jnp.where(lane_h0, ol[:QB, LANES:], ol[QB:, LANES:])
    mb = jnp.where(lane_h0, m[:QB], m[QB:])
    return o, mb, l


def _merge(o_a, m_a, l_a, o_b, m_b, l_b):
    m = jnp.maximum(m_a, m_b)
    ea = jnp.exp2(m_a - m)
    eb = jnp.exp2(m_b - m)
    return o_a * ea + o_b * eb, m, l_a * ea + l_b * eb


def _both_heads(mask):
    return jnp.concatenate([mask, mask], axis=0)


def _attn_kernel(q_ref, k_ref, v_ref, o_ref, qp, kp, vpx, vnx, o23, m23, l23, o2, m2, l2,
                 on, mn):
    s = q_ref.shape[0]
    n_chunk = DIL_FAR
    rows = s // n_chunk
    sub = rows // DIL_MID
    grp = n_chunk * n_chunk
    lane_h0 = lax.broadcasted_iota(I32, (1, LANES), 1) < HEAD_DIM

    @pl.when((pl.program_id(0) == 0) & (pl.program_id(1) == 0))
    def _():
        ones = jnp.ones((s, LANES), BF16)
        vnx[:, LANES:] = ones
        vpx[:, LANES:] = ones

    vnx[:, :LANES] = v_ref[...]

    pi = lax.broadcasted_iota(I32, (grp, grp), 0)
    pj = lax.broadcasted_iota(I32, (grp, grp), 1)
    swap = (pj == (pi % n_chunk) * n_chunk + pi // n_chunk).astype(BF16)
    for g in range(s // grp):
        src = slice(g * grp, (g + 1) * grp)
        qk = jnp.concatenate([q_ref[src, :], k_ref[src, :]], axis=1)
        yqk = jnp.dot(swap, qk, preferred_element_type=F32).astype(BF16)
        yv = jnp.dot(swap, v_ref[src, :], preferred_element_type=F32).astype(BF16)
        for r in range(n_chunk):
            dst = slice(r * rows + g * n_chunk, r * rows + (g + 1) * n_chunk)
            part = slice(r * n_chunk, (r + 1) * n_chunk)
            qp[dst, :] = yqk[part, :LANES]
            kp[dst, :] = yqk[part, LANES:]
            vpx[dst, 0:LANES] = yv[part]

    qi = lax.broadcasted_iota(I32, (QB, QB), 0)
    kj = lax.broadcasted_iota(I32, (QB, QB), 1)
    causal = _both_heads(qi >= kj)

    for r in range(n_chunk):
        blk = slice(r * rows, (r + 1) * rows)
        o, m, l = _pair_block(qp[blk, :], kp[blk, :], vpx[blk, :], causal, lane_h0)
        o23[blk, :] = o
        m23[blk, :] = m
        l23[blk, :] = l

    def mid_index(n_key_sub):
        i_q = lax.broadcasted_iota(I32, (QB, DIL_MID * n_key_sub), 0)
        i_k = lax.broadcasted_iota(I32, (QB, DIL_MID * n_key_sub), 1)
        return i_q // sub, i_q % sub, i_k // n_key_sub, i_k % n_key_sub

    jq, aq, jk, ak = mid_index(sub)
    d0 = DIL_FAR * (aq - ak) + DIL_MID * (jq - jk)
    mask_mid0 = _both_heads(d0 >= 0)
    jq, aq, jk, ak = mid_index(2 * sub)
    d1 = DIL_FAR * (aq + sub - ak) + DIL_MID * (jq - jk)
    mask_mid = _both_heads((d1 >= 0) & (d1 <= DIL_MID * QB))

    def mid_tile(r4, a0, k0, nk, mask):
        def at(j, off, n):
            return pl.ds((DIL_MID * j + r4) * rows + off, n)

        q = jnp.concatenate([qp[at(j, a0, sub), :] for j in range(DIL_MID)], axis=0)
        k = jnp.concatenate([kp[at(j, k0, nk), :] for j in range(DIL_MID)], axis=0)
        v = jnp.concatenate([vpx[at(j, k0, nk), :] for j in range(DIL_MID)], axis=0)
        o, m, l = _pair_block(q, k, v, mask, lane_h0)
        for j in range(DIL_MID):
            dst = at(j, a0, sub)
            src = slice(j * sub, (j + 1) * sub)
            o2[dst, :] = o[src]
            m2[dst, :] = m[src]
            l2[dst, :] = l[src]

    for r4 in range(DIL_MID):
        mid_tile(r4, 0, 0, sub, mask_mid0)
        for a_blk in range(1, rows // sub):
            mid_tile(r4, a_blk * sub, (a_blk - 1) * sub, 2 * sub, mask_mid)

    for r in range(n_chunk):
        blk = slice(r * rows, (r + 1) * rows)
        oo, mm, ll = _merge(o23[blk, :], m23[blk, :], l23[blk, :],
                            o2[blk, :], m2[blk, :], l2[blk, :])
        o23[blk, :] = oo / ll
        m23[blk, :] = mm + jnp.log2(ll)

    for g in range(s // grp):
        slabs = [o23[r * rows + g * n_chunk:r * rows + (g + 1) * n_chunk, :].astype(BF16)
                 for r in range(n_chunk)]
        on[g * grp:(g + 1) * grp, :] = jnp.dot(swap, jnp.concatenate(slabs, axis=0),
                                               preferred_element_type=F32)
    for r in range(n_chunk):
        mn[pl.ds(r, rows, stride=n_chunk), :] = m23[r * rows:(r + 1) * rows, :]

    qi2 = lax.broadcasted_iota(I32, (QB, 2 * QB), 0)
    kj2 = lax.broadcasted_iota(I32, (QB, 2 * QB), 1)
    dn = qi2 + QB - kj2
    mask_near = _both_heads((dn >= 0) & (dn <= QB))

    def near_finish(dst, o, m, l):
        oo, _, ll = _merge(on[dst, :], mn[dst, :], 1.0, o, m, l)
        o_ref[dst, :] = (oo / ll).astype(o_ref.dtype)

    first = pl.ds(0, QB)
    o, m, l = _pair_block(q_ref[first, :], k_ref[first, :], vnx[first, :], causal, lane_h0)
    near_finish(first, o, m, l)
    for n in range(1, s // QB):
        keys = pl.ds((n - 1) * QB, 2 * QB)
        o, m, l = _pair_block(q_ref[pl.ds(n * QB, QB), :], k_ref[keys, :], vnx[keys, :],
                              mask_near, lane_h0)
        near_finish(pl.ds(n * QB, QB), o, m, l)


def _attention(qkv, batch, seq):
    n_pair = D_ATTN // LANES
    blk = (seq, LANES)
    f32_scr = pltpu.VMEM(blk, F32)
    bf_scr = pltpu.VMEM(blk, BF16)
    bfx_scr = pltpu.VMEM((seq, 2 * LANES), BF16)
    return pl.pallas_call(
        _attn_kernel,
        grid=(batch, n_pair),
        in_specs=[
            pl.BlockSpec(blk, lambda b, h: (b, h)),
            pl.BlockSpec(blk, lambda b, h: (b, n_pair + h)),
            pl.BlockSpec(blk, lambda b, h: (b, 2 * n_pair + h)),
        ],
        out_specs=pl.BlockSpec(blk, lambda b, h: (b, h)),
        out_shape=jax.ShapeDtypeStruct((batch * seq, D_ATTN), BF16),
        scratch_shapes=[bf_scr, bf_scr, bfx_scr, bfx_scr] + [f32_scr] * 8,
        compiler_params=pltpu.CompilerParams(
            dimension_semantics=("arbitrary", "arbitrary"), vmem_limit_bytes=VMEM_LIMIT),
        name="dilated_attn",
    )(qkv, qkv, qkv)


META_ROWS = 16
META_EIDX, META_GATE, META_RANK = 0, 4, 8


def _outproj_kernel(x_ref, yp_ref, ya_ref, wo_ref, g_ref, wrt_ref, brt_ref,
                    x1_ref, h2_ref, meta_ref, cnt_ref, wo_bf, before, carry):
    tm = x_ref.shape[0]

    @pl.when(pl.program_id(0) == 0)
    def _():
        wo_bf[...] = wo_ref[...].astype(BF16)
        carry[...] = jnp.zeros_like(carry)
        ti = lax.broadcasted_iota(I32, (tm, tm), 0)
        tj = lax.broadcasted_iota(I32, (tm, tm), 1)
        before[...] = (ti < tj).astype(BF16)

    x1 = (x_ref[...]
          + jnp.dot(yp_ref[...], wo_bf[:D_POOL, :], preferred_element_type=F32)
          + jnp.dot(ya_ref[...], wo_bf[D_POOL:, :], preferred_element_type=F32))
    x1_ref[...] = x1
    h2 = _rms(x1, g_ref[...])
    h2_ref[...] = _pack_row(h2)

    logits_t = lax.dot_general(wrt_ref[...].astype(BF16), h2.astype(BF16),
                               (((1,), (1,)), ((), ())), preferred_element_type=F32)
    logits_t = logits_t + brt_ref[:, 0:1]
    eid = lax.broadcasted_iota(I32, (N_EXPERTS, tm), 0)
    work = logits_t
    idxs, vals = [], []
    for _ in range(TOP_K):
        mx = jnp.max(work, axis=0, keepdims=True)
        idx = jnp.min(jnp.where(work == mx, eid, N_EXPERTS), axis=0, keepdims=True)
        idxs.append(idx)
        vals.append(mx)
        work = jnp.where(eid == idx, -jnp.inf, work)
    exps = [jnp.exp(v - vals[0]) for v in vals]
    den = exps[0] + exps[1] + exps[2] + exps[3]

    onehot = jnp.zeros((N_EXPERTS, tm), F32)
    for idx in idxs:
        onehot = onehot + (eid == idx).astype(F32)
    rank_e = carry[:, 0:1] + jnp.dot(onehot.astype(BF16), before[...],
                                     preferred_element_type=F32)
    carry[...] = carry[...] + jnp.sum(onehot, axis=1, keepdims=True)
    cnt_ref[...] = carry[...]

    mrow = lax.broadcasted_iota(I32, (META_ROWS, tm), 0)
    meta = jnp.zeros((META_ROWS, tm), F32)
    for k in range(TOP_K):
        rank_k = jnp.sum(jnp.where(eid == idxs[k], rank_e, 0.0), axis=0, keepdims=True)
        meta = jnp.where(mrow == META_EIDX + k, idxs[k].astype(F32), meta)
        meta = jnp.where(mrow == META_GATE + k, exps[k] / den, meta)
        meta = jnp.where(mrow == META_RANK + k, rank_k, meta)
    meta_ref[...] = meta


def _outproj(x2, y_pool, y_attn, w_out, g, w_router_t, b_router_t):
    t = x2.shape[0]
    row = lambda i: (i, 0)
    const = lambda i: (0, 0)
    return pl.pallas_call(
        _outproj_kernel,
        grid=(t // ROW_TILE,),
        in_specs=[
            pl.BlockSpec((ROW_TILE, D_MODEL), row),
            pl.BlockSpec((ROW_TILE, D_POOL), row),
            pl.BlockSpec((ROW_TILE, D_ATTN), row),
            pl.BlockSpec((D_MODEL, D_MODEL), const),
            pl.BlockSpec((1, D_MODEL), const),
            pl.BlockSpec((N_EXPERTS, D_MODEL), const),
            pl.BlockSpec((N_EXPERTS, LANES), const),
        ],
        out_specs=[
            pl.BlockSpec((ROW_TILE, D_MODEL), row),
            pl.BlockSpec((ROW_TILE, D_PACKED), row),
            pl.BlockSpec((META_ROWS, ROW_TILE), lambda i: (0, i)),
            pl.BlockSpec((N_EXPERTS, LANES), const),
        ],
        out_shape=[
            jax.ShapeDtypeStruct((t, D_MODEL), F32),
            jax.ShapeDtypeStruct((t, D_PACKED), I32),
            jax.ShapeDtypeStruct((META_ROWS, t), F32),
            jax.ShapeDtypeStruct((N_EXPERTS, LANES), F32),
        ],
        scratch_shapes=[pltpu.VMEM((D_MODEL, D_MODEL), BF16),
                        pltpu.VMEM((ROW_TILE, ROW_TILE), BF16),
                        pltpu.VMEM((N_EXPERTS, LANES), F32)],
        compiler_params=pltpu.CompilerParams(
            dimension_semantics=("arbitrary",), vmem_limit_bytes=VMEM_LIMIT),
        name="outproj_router",
    )(x2, y_pool, y_attn, w_out, g, w_router_t, b_router_t)


def _sc_mesh():
    return plsc.VectorSubcoreMesh(core_axis_name="core", subcore_axis_name="subcore")


def _sc_dispatch(h2, pos_rows, p_max):
    t = h2.shape[0]

    @functools.partial(
        pl.kernel, mesh=_sc_mesh(),
        out_type=[jax.ShapeDtypeStruct((p_max, D_PART), h2.dtype)] * N_PART)
    def run(h_hbm, *refs):
        pos_hbm, xs_hbm = refs[:TOP_K], refs[TOP_K:]
        for c in range(N_PART):
            def body(x_vmem, *idx_vmem, dst=xs_hbm[c]):
                for iv in idx_vmem:
                    pltpu.sync_copy(x_vmem, dst.at[iv.at[0]])

            pltpu.emit_pipeline(
                body,
                grid=(t // SC_WINDOW,),
                in_specs=[pl.BlockSpec((SC_WINDOW, D_PART), lambda i, c=c: (i, c))]
                + [pl.BlockSpec((1, SC_WINDOW), lambda i: (0, i))] * TOP_K,
                out_specs=[],
                core_axis_name=("core", "subcore"),
                dimension_semantics=(pltpu.PARALLEL,),
            )(h_hbm, *pos_hbm)

    return run(h2, *pos_rows)


def _sc_scatter(jobs):
    n_jobs = len(jobs)
    outs = []
    for _, _, out in jobs:
        if not any(out is o for o in outs):
            outs.append(out)
    out_of = [next(i for i, o in enumerate(outs) if o is out) for _, _, out in jobs]

    @functools.partial(pl.kernel, mesh=_sc_mesh(), out_type=())
    def run(*refs):
        src_hbm, idx_hbm, out_hbm = (refs[:n_jobs], refs[n_jobs:2 * n_jobs], refs[2 * n_jobs:])
        for src, idx, o in zip(src_hbm, idx_hbm, out_of):
            out = out_hbm[o]
            def body(x_vmem, i_vmem, dst=out):
                pltpu.sync_copy(x_vmem, dst.at[i_vmem.at[0]])

            pltpu.emit_pipeline(
                body,
                grid=(src.shape[0] // SC_WINDOW,),
                in_specs=[pl.BlockSpec((SC_WINDOW, src.shape[1]), lambda i: (i, 0)),
                          pl.BlockSpec((1, SC_WINDOW), lambda i: (0, i))],
                out_specs=[],
                core_axis_name=("core", "subcore"),
                dimension_semantics=(pltpu.PARALLEL,),
            )(src, idx)

    run(*[j[0] for j in jobs], *[j[1] for j in jobs], *outs)


def _moe_kernel(te_ref, nv_ref, rows_ref, nxt_ref, *refs):
    xs_refs = refs[:N_PART]
    bgu_ref, bd_ref, wgu_hbm, wd_hbm = refs[N_PART:N_PART + 4]
    y_refs = refs[N_PART + 4:2 * N_PART + 4]
    wgu_f32, wd_f32, wgu_bf, wd_bf, sem = refs[2 * N_PART + 4:]
    i = pl.program_id(0)

    def weight_copies(e):
        return (pltpu.make_async_copy(wgu_hbm.at[e], wgu_f32, sem.at[0]),
                pltpu.make_async_copy(wd_hbm.at[e], wd_f32, sem.at[1]))

    @pl.when(i < nv_ref[0])
    def _():
        prev = te_ref[jnp.maximum(i - 1, 0)]
        new_expert = jnp.logical_or(i == 0, te_ref[i] != prev)

        @pl.when(i == 0)
        def _():
            for cp in weight_copies(te_ref[0]):
                cp.start()

        @pl.when(new_expert)
        def _():
            for cp in weight_copies(te_ref[i]):
                cp.wait()
            wgu_bf[...] = wgu_f32[...].astype(BF16)
            wd_bf[...] = wd_f32[...].astype(BF16)

            @pl.when(nxt_ref[i] >= 0)
            def _():
                for cp in weight_copies(nxt_ref[i]):
                    cp.start()

        d_e = wd_bf.shape[0]

        def ffn(n_rows):
            x = _unpack_row(jnp.concatenate([r[0:n_rows, :] for r in xs_refs], axis=1))
            rid = lax.broadcasted_iota(I32, (n_rows, 1), 0)
            x = jnp.where(rid < rows_ref[i], x, 0.0).astype(BF16)
            gu = jnp.dot(x, wgu_bf[...], preferred_element_type=F32) + bgu_ref[0]
            gate = jnp.minimum(gu[:, :d_e], SWIGLU_LIMIT)
            lin = jnp.clip(gu[:, d_e:], -SWIGLU_LIMIT, SWIGLU_LIMIT)
            act = gate * jax.nn.sigmoid(SWIGLU_ALPHA * gate) * (lin + 1.0)
            y = jnp.dot(act.astype(BF16), wd_bf[...], preferred_element_type=F32) + bd_ref[0]
            yp = _pack_row(y)
            for c, y_ref in enumerate(y_refs):
                y_ref[0:n_rows, :] = yp[:, c * D_PART:(c + 1) * D_PART]
                if n_rows < MOE_TILE:
                    y_ref[n_rows:, :] = jnp.zeros((MOE_TILE - n_rows, D_PART), I32)

        half = MOE_TILE // 2

        @pl.when(rows_ref[i] > half)
        def _():
            ffn(MOE_TILE)

        @pl.when(rows_ref[i] <= half)
        def _():
            ffn(half)

    @pl.when(i >= nv_ref[0])
    def _():
        for y_ref in y_refs:
            y_ref[...] = jnp.zeros_like(y_ref)


def _moe(tile_expert, n_valid, tile_rows, next_expert, xs_parts, tile_base,
         w_gu, b_gu, w_down, b_down):
    n_tiles = tile_expert.shape[0]
    p_max = n_tiles * MOE_TILE
    d_e = w_down.shape[1]

    def last_used(i, nv):
        return jnp.maximum(jnp.minimum(i, nv[0] - 1), 0)

    def row(i, te, nv, tr, nx):
        return (tile_base + last_used(i, nv), 0)

    def expert(i, te, nv, tr, nx):
        return (te[last_used(i, nv)], 0, 0)

    grid_spec = pltpu.PrefetchScalarGridSpec(
        num_scalar_prefetch=4,
        grid=(n_tiles,),
        in_specs=[pl.BlockSpec((MOE_TILE, D_PART), row)] * N_PART + [
            pl.BlockSpec((1, 1, 2 * d_e), expert),
            pl.BlockSpec((1, 1, D_MODEL), expert),
            pl.BlockSpec(memory_space=pl.ANY),
            pl.BlockSpec(memory_space=pl.ANY),
        ],
        out_specs=[pl.BlockSpec((MOE_TILE, D_PART), lambda i, te, nv, tr, nx: (i, 0))] * N_PART,
        scratch_shapes=[pltpu.VMEM((D_MODEL, 2 * d_e), F32), pltpu.VMEM((d_e, D_MODEL), F32),
                        pltpu.VMEM((D_MODEL, 2 * d_e), BF16), pltpu.VMEM((d_e, D_MODEL), BF16),
                        pltpu.SemaphoreType.DMA((2,))],
    )
    return pl.pallas_call(
        _moe_kernel,
        grid_spec=grid_spec,
        out_shape=[jax.ShapeDtypeStruct((p_max, D_PART), I32)] * N_PART,
        compiler_params=pltpu.CompilerParams(
            dimension_semantics=("arbitrary",), vmem_limit_bytes=VMEM_LIMIT),
        name="moe_ffn",
    )(tile_expert, n_valid, tile_rows, next_expert, *xs_parts, b_gu, b_down, w_gu, w_down)


def _combine_kernel(x1_ref, gate_ref, g_ref, *refs):
    yk_refs, o_ref = refs[:-1], refs[-1]
    acc = x1_ref[...]
    for k in range(TOP_K):
        parts = yk_refs[k * N_PART:(k + 1) * N_PART]
        y_k = _unpack_row(jnp.concatenate([r[...] for r in parts], axis=1))
        acc = acc + gate_ref[:, k:k + 1] * y_k
    o_ref[...] = _rms(acc, g_ref[...])


def _combine(x1, gates, g, yk_parts):
    t, d = x1.shape
    tq = COMBINE_TILE
    row = lambda i: (i, 0)
    slot_specs = [pl.BlockSpec((tq, D_PART), lambda i, k=k: (k * (t // tq) + i, 0))
                  for k in range(TOP_K) for _ in range(N_PART)]
    slot_args = [yk_parts[p] for _ in range(TOP_K) for p in range(N_PART)]
    return pl.pallas_call(
        _combine_kernel,
        grid=(t // tq,),
        in_specs=[
            pl.BlockSpec((tq, d), row),
            pl.BlockSpec((tq, TOP_K), row),
            pl.BlockSpec((1, d), lambda i: (0, 0)),
        ] + slot_specs,
        out_specs=pl.BlockSpec((tq, d), row),
        out_shape=jax.ShapeDtypeStruct((t, d), F32),
        compiler_params=pltpu.CompilerParams(
            dimension_semantics=("arbitrary",), vmem_limit_bytes=VMEM_LIMIT),
        name="combine_final",
    )(x1, gates, g, *slot_args)


def kernel(x, g_mix, w_in, w_pool, pool_scale, w_out, g_ffn, w_router, b_router,
           w_gu, b_gu, w_down, b_down, g_final):
    batch, seq, d = x.shape
    t = batch * seq
    x2 = x.reshape(t, d)

    y_pool, qkv = _inproj(x2, g_mix[0].reshape(1, d), w_in[0], w_pool[0],
                          pool_scale[0].reshape(1, D_POOL), seq)
    y_attn = _attention(qkv, batch, seq)

    wr_t = w_router[0].T
    br_t = jnp.broadcast_to(b_router[0].reshape(N_EXPERTS, 1), (N_EXPERTS, LANES))
    x1, h2, meta, cnt = _outproj(x2, y_pool, y_attn, w_out[0], g_ffn[0].reshape(1, d), wr_t, br_t)

    eidx = meta[META_EIDX:META_EIDX + TOP_K].astype(I32)
    rank = meta[META_RANK:META_RANK + TOP_K].astype(I32)
    gates = meta[META_GATE:META_GATE + TOP_K].T
    counts = cnt[:, 0].astype(I32)
    padded = ((counts + MOE_TILE - 1) // MOE_TILE) * MOE_TILE
    ends = jnp.cumsum(padded)
    offsets = ends - padded
    e_ids = jnp.arange(N_EXPERTS, dtype=I32)
    hit = eidx[None] == e_ids[:, None, None]
    pos = rank + jnp.sum(jnp.where(hit, offsets[:, None, None], 0), axis=0)
    p_max = t * TOP_K + N_EXPERTS * MOE_TILE
    n_tiles = p_max // MOE_TILE
    tile_start = jnp.arange(n_tiles, dtype=I32) * MOE_TILE
    tile_expert = jnp.minimum(
        jnp.sum((tile_start[:, None] >= ends[None, :]).astype(I32), axis=1), N_EXPERTS - 1)
    mine = tile_expert[:, None] == e_ids[None, :]
    group_end = jnp.sum(jnp.where(mine, (offsets + counts)[None, :], 0), axis=1)
    tile_rows = jnp.clip(group_end - tile_start, 0, MOE_TILE)
    n_valid = (ends[-1] // MOE_TILE).reshape(1).astype(I32)
    tiles_per_chunk = n_tiles // MOE_CHUNKS
    tile_ids = jnp.arange(n_tiles, dtype=I32)
    after_group = jnp.sum(jnp.where(mine, ends[None, :], 0), axis=1) // MOE_TILE
    expert_after = jnp.sum(jnp.where(after_group[:, None] == tile_ids[None, :],
                                     tile_expert[None, :], 0), axis=1)
    prefetch = ((after_group < n_valid[0])
                & (after_group // tiles_per_chunk == tile_ids // tiles_per_chunk))
    next_expert = jnp.where(prefetch, expert_after, -1)

    pos_rows = [pos[k:k + 1] for k in range(TOP_K)]
    xs = _sc_dispatch(h2, pos_rows, p_max)

    ids = jnp.broadcast_to(jnp.arange(TOP_K * t, dtype=I32)[:, None], (TOP_K * t, LANES))
    ids, xs = lax.optimization_barrier((ids, xs))
    id_map = jax.new_ref(lax.empty((p_max, LANES), I32))
    _sc_scatter([(ids[k * t:(k + 1) * t], pos_rows[k], id_map) for k in range(TOP_K)])
    in_use = (jnp.arange(MOE_TILE, dtype=I32)[None, :] < tile_rows[:, None]).reshape(p_max)
    spare = TOP_K * t + jnp.arange(p_max, dtype=I32) % SPARE_ROWS
    slot_of_row = jnp.where(in_use, id_map[...][:, 0], spare).reshape(1, p_max)

    yk = [jax.new_ref(lax.empty((TOP_K * t + SPARE_ROWS, D_PART), I32)) for _ in range(N_PART)]
    rows_per_chunk = tiles_per_chunk * MOE_TILE
    for c in range(MOE_CHUNKS):
        tiles = slice(c * tiles_per_chunk, (c + 1) * tiles_per_chunk)
        n_used = jnp.clip(n_valid - c * tiles_per_chunk, 0, tiles_per_chunk)
        ys = _moe(tile_expert[tiles], n_used, tile_rows[tiles], next_expert[tiles], xs,
                  c * tiles_per_chunk, w_gu[0], b_gu[0].reshape(N_EXPERTS, 1, -1),
                  w_down[0], b_down[0].reshape(N_EXPERTS, 1, -1))
        slots = slot_of_row[:, c * rows_per_chunk:(c + 1) * rows_per_chunk]
        _sc_scatter([(ys[part], slots, yk[part]) for part in range(N_PART)])

    out = _combine(x1, gates, g_final.reshape(1, d), [r[...] for r in yk])
    return out.reshape(batch, seq, d)
```

```python
import functools

import jax
import jax.numpy as jnp
from jax import lax
from jax.experimental import pallas as pl
from jax.experimental.pallas import tpu as pltpu
from jax.experimental.pallas import tpu_sc as plsc

F32 = jnp.float32
BF16 = jnp.bfloat16
I32 = jnp.int32

D_MODEL = 1024
D_POOL = 512
D_ATTN = 512
POOL_WINDOWS = (2, 4, 8, 16)
POOL_GROUP = 128
HEAD_DIM = 64
N_EXPERTS = 32
TOP_K = 4
SWIGLU_LIMIT = 7.0
SWIGLU_ALPHA = 1.702
EPS = 1e-5
NEG_INF = -1e30
LOG2_E = 1.4426950408889634

LANES = 128
QB = 128
DIL_MID = 4
DIL_FAR = 16
ROW_TILE = 1024
MOE_TILE = 512
COMBINE_TILE = 512
SC_WINDOW = 128
D_PACKED = D_MODEL // 2
N_PART = 2
D_PART = D_PACKED // N_PART
VMEM_LIMIT = 56 * 1024 * 1024


def _pack_row(x):
    hi = lax.bitcast_convert_type(x[:, :D_PACKED].astype(BF16).astype(F32), I32)
    lo = lax.bitcast_convert_type(x[:, D_PACKED:].astype(BF16).astype(F32), I32)
    return hi | lax.shift_right_logical(lo, 16)


def _unpack_row(w):
    hi = lax.bitcast_convert_type(w & jnp.int32(-65536), F32)
    lo = lax.bitcast_convert_type(lax.shift_left(w, 16), F32)
    return jnp.concatenate([hi, lo], axis=1)


def _rms(x, g):
    ms = jnp.mean(x * x, axis=-1, keepdims=True)
    return x * lax.rsqrt(ms + EPS) * g


POOL_PAD = 16


def _inproj_kernel(x_ref, g_ref, w_ref, wp_ref, sc_ref, yp_ref, qkv_ref, w_bf, hist, pad_ref,
                   *, tiles_per_seq):
    i = pl.program_id(0)
    tm = x_ref.shape[0]

    @pl.when(i == 0)
    def _():
        w_bf[...] = w_ref[...].astype(BF16)
        pad_ref[0:POOL_PAD, :] = jnp.zeros((POOL_PAD, POOL_GROUP), F32)

    @pl.when(i % tiles_per_seq == 0)
    def _():
        hist[...] = jnp.zeros_like(hist)

    h = _rms(x_ref[...], g_ref[...]).astype(BF16)
    proj = jnp.dot(h, w_bf[...], preferred_element_type=F32)
    q = proj[:, D_POOL:D_POOL + D_ATTN] * (HEAD_DIM ** -0.5 * LOG2_E)
    qkv_ref[:, :D_ATTN] = q.astype(BF16)
    qkv_ref[:, D_ATTN:] = proj[:, D_POOL + D_ATTN:].astype(BF16)

    pos = (i % tiles_per_seq) * tm + lax.broadcasted_iota(I32, (tm, 1), 0)
    for g, w in enumerate(POOL_WINDOWS):
        lo, hi = g * POOL_GROUP, (g + 1) * POOL_GROUP
        e = proj[:, lo:hi]
        acc = jnp.concatenate([hist[g], e], axis=0)
        hist[g] = e[tm - POOL_PAD:, :]
        span = 1
        while span < w:
            pad_ref[POOL_PAD:, :] = acc
            acc = acc + pad_ref[pl.ds(POOL_PAD - span, POOL_PAD + tm), :]
            span *= 2
        count = jnp.minimum(pos + 1, w).astype(F32)
        pooled = acc[POOL_PAD:, :] / count - e
        y = jnp.dot(pooled.astype(BF16), wp_ref[g].astype(BF16), preferred_element_type=F32)
        yp_ref[:, lo:hi] = (y * sc_ref[:, lo:hi]).astype(BF16)


def _inproj(x2, g, w, w_pool, pool_scale, seq):
    t = x2.shape[0]
    d_in = w.shape[1]
    n_grp = len(POOL_WINDOWS)
    return pl.pallas_call(
        functools.partial(_inproj_kernel, tiles_per_seq=seq // ROW_TILE),
        grid=(t // ROW_TILE,),
        in_specs=[
            pl.BlockSpec((ROW_TILE, D_MODEL), lambda i: (i, 0)),
            pl.BlockSpec((1, D_MODEL), lambda i: (0, 0)),
            pl.BlockSpec((D_MODEL, d_in), lambda i: (0, 0)),
            pl.BlockSpec((n_grp, POOL_GROUP, POOL_GROUP), lambda i: (0, 0, 0)),
            pl.BlockSpec((1, D_POOL), lambda i: (0, 0)),
        ],
        out_specs=[
            pl.BlockSpec((ROW_TILE, D_POOL), lambda i: (i, 0)),
            pl.BlockSpec((ROW_TILE, 3 * D_ATTN), lambda i: (i, 0)),
        ],
        out_shape=[
            jax.ShapeDtypeStruct((t, D_POOL), BF16),
            jax.ShapeDtypeStruct((t, 3 * D_ATTN), BF16),
        ],
        scratch_shapes=[pltpu.VMEM((D_MODEL, d_in), BF16),
                        pltpu.VMEM((n_grp, POOL_PAD, POOL_GROUP), F32),
                        pltpu.VMEM((2 * POOL_PAD + ROW_TILE, POOL_GROUP), F32)],
        compiler_params=pltpu.CompilerParams(
            dimension_semantics=("arbitrary",), vmem_limit_bytes=VMEM_LIMIT),
        name="inproj_pool",
    )(x2, g, w, w_pool, pool_scale)


def _pair_block(q, k, v_ext, mask2, lane_h0):
    zero = jnp.zeros_like(q)
    q2 = jnp.concatenate([jnp.where(lane_h0, q, zero), jnp.where(lane_h0, zero, q)], axis=0)
    s = lax.dot_general(q2, k, (((1,), (1,)), ((), ())), preferred_element_type=F32)
    s = jnp.where(mask2, s, NEG_INF)
    m = jnp.max(s, axis=-1, keepdims=True)
    p = jnp.exp2(s - m).astype(BF16)
    ol = jnp.dot(p, v_ext, preferred_element_type=F32)
    o = jnp.where(lane_h0, ol[:QB, :LANES], ol[QB:, :LANES])
    l = jnp.where(lane_h0, ol[:QB, LANES:], ol[QB:, LANES:])
    mb = jnp.where(lane_h0, m[:QB], m[QB:])
    return o, mb, l


def _merge(o_a, m_a, l_a, o_b, m_b, l_b):
    m = jnp.maximum(m_a, m_b)
    ea = jnp.exp2(m_a - m)
    eb = jnp.exp2(m_b - m)
    return o_a * ea + o_b * eb, m, l_a * ea + l_b * eb


def _both_heads(mask):
    return jnp.concatenate([mask, mask], axis=0)


def _attn_kernel(q_ref, k_ref, v_ref, o_ref, qp, kp, vpx, vnx, o23, m23, l23, o2, m2, l2,
                 on, mn):
    s = q_ref.shape[0]
    n_chunk = DIL_FAR
    rows = s // n_chunk
    sub = rows // DIL_MID
    grp = n_chunk * n_chunk
    lane_h0 = lax.broadcasted_iota(I32, (1, LANES), 1) < HEAD_DIM

    @pl.when((pl.program_id(0) == 0) & (pl.program_id(1) == 0))
    def _():
        ones = jnp.ones((s, LANES), BF16)
        vnx[:, LANES:] = ones
        vpx[:, LANES:] = ones

    vnx[:, :LANES] = v_ref[...]

    pi = lax.broadcasted_iota(I32, (grp, grp), 0)
    pj = lax.broadcasted_iota(I32, (grp, grp), 1)
    swap = (pj == (pi % n_chunk) * n_chunk + pi // n_chunk).astype(BF16)
    for g in range(s // grp):
        src = slice(g * grp, (g + 1) * grp)
        qk = jnp.concatenate([q_ref[src, :], k_ref[src, :]], axis=1)
        yqk = jnp.dot(swap, qk, preferred_element_type=F32).astype(BF16)
        yv = jnp.dot(swap, v_ref[src, :], preferred_element_type=F32).astype(BF16)
        for r in range(n_chunk):
            dst = slice(r * rows + g * n_chunk, r * rows + (g + 1) * n_chunk)
            part = slice(r * n_chunk, (r + 1) * n_chunk)
            qp[dst, :] = yqk[part, :LANES]
            kp[dst, :] = yqk[part, LANES:]
            vpx[dst, 0:LANES] = yv[part]

    qi = lax.broadcasted_iota(I32, (QB, QB), 0)
    kj = lax.broadcasted_iota(I32, (QB, QB), 1)
    causal = _both_heads(qi >= kj)

    for r in range(n_chunk):
        blk = slice(r * rows, (r + 1) * rows)
        o, m, l = _pair_block(qp[blk, :], kp[blk, :], vpx[blk, :], causal, lane_h0)
        o23[blk, :] = o
        m23[blk, :] = m
        l23[blk, :] = l

    def mid_index(n_key_sub):
        i_q = lax.broadcasted_iota(I32, (QB, DIL_MID * n_key_sub), 0)
        i_k = lax.broadcasted_iota(I32, (QB, DIL_MID * n_key_sub), 1)
        return i_q // sub, i_q % sub, i_k // n_key_sub, i_k % n_key_sub

    jq, aq, jk, ak = mid_index(sub)
    d0 = DIL_FAR * (aq - ak) + DIL_MID * (jq - jk)
    mask_mid0 = _both_heads(d0 >= 0)
    jq, aq, jk, ak = mid_index(2 * sub)
    d1 = DIL_FAR * (aq + sub - ak) + DIL_MID * (jq - jk)
    mask_mid = _both_heads((d1 >= 0) & (d1 <= DIL_MID * QB))

    def mid_tile(r4, a0, k0, nk, mask):
        def at(j, off, n):
            return pl.ds((DIL_MID * j + r4) * rows + off, n)

        q = jnp.concatenate([qp[at(j, a0, sub), :] for j in range(DIL_MID)], axis=0)
        k = jnp.concatenate([kp[at(j, k0, nk), :] for j in range(DIL_MID)], axis=0)
        v = jnp.concatenate([vpx[at(j, k0, nk), :] for j in range(DIL_MID)], axis=0)
        o, m, l = _pair_block(q, k, v, mask, lane_h0)
        for j in range(DIL_MID):
            dst = at(j, a0, sub)
            src = slice(j * sub, (j + 1) * sub)
            o2[dst, :] = o[src]
            m2[dst, :] = m[src]
            l2[dst, :] = l[src]

    for r4 in range(DIL_MID):
        mid_tile(r4, 0, 0, sub, mask_mid0)
        for a_blk in range(1, rows // sub):
            mid_tile(r4, a_blk * sub, (a_blk - 1) * sub, 2 * sub, mask_mid)

    for r in range(n_chunk):
        blk = slice(r * rows, (r + 1) * rows)
        oo, mm, ll = _merge(o23[blk, :], m23[blk, :], l23[blk, :],
                            o2[blk, :], m2[blk, :], l2[blk, :])
        o23[blk, :] = oo / ll
        m23[blk, :] = mm + jnp.log2(ll)

    for g in range(s // grp):
        slabs = [o23[r * rows + g * n_chunk:r * rows + (g + 1) * n_chunk, :].astype(BF16)
                 for r in range(n_chunk)]
        on[g * grp:(g + 1) * grp, :] = jnp.dot(swap, jnp.concatenate(slabs, axis=0),
                                               preferred_element_type=F32)
    for r in range(n_chunk):
        mn[pl.ds(r, rows, stride=n_chunk), :] = m23[r * rows:(r + 1) * rows, :]

    qi2 = lax.broadcasted_iota(I32, (QB, 2 * QB), 0)
    kj2 = lax.broadcasted_iota(I32, (QB, 2 * QB), 1)
    dn = qi2 + QB - kj2
    mask_near = _both_heads((dn >= 0) & (dn <= QB))

    def near_finish(dst, o, m, l):
        oo, _, ll = _merge(on[dst, :], mn[dst, :], 1.0, o, m, l)
        o_ref[dst, :] = (oo / ll).astype(o_ref.dtype)

    first = pl.ds(0, QB)
    o, m, l = _pair_block(q_ref[first, :], k_ref[first, :], vnx[first, :], causal, lane_h0)
    near_finish(first, o, m, l)
    for n in range(1, s // QB):
        keys = pl.ds((n - 1) * QB, 2 * QB)
        o, m, l = _pair_block(q_ref[pl.ds(n * QB, QB), :], k_ref[keys, :], vnx[keys, :],
                              mask_near, lane_h0)
        near_finish(pl.ds(n * QB, QB), o, m, l)


def _attention(qkv, batch, seq):
    n_pair = D_ATTN // LANES
    blk = (seq, LANES)
    f32_scr = pltpu.VMEM(blk, F32)
    bf_scr = pltpu.VMEM(blk, BF16)
    bfx_scr = pltpu.VMEM((seq, 2 * LANES), BF16)
    return pl.pallas_call(
        _attn_kernel,
        grid=(batch, n_pair),
        in_specs=[
            pl.BlockSpec(blk, lambda b, h: (b, h)),
            pl.BlockSpec(blk, lambda b, h: (b, n_pair + h)),
            pl.BlockSpec(blk, lambda b, h: (b, 2 * n_pair + h)),
        ],
        out_specs=pl.BlockSpec(blk, lambda b, h: (b, h)),
        out_shape=jax.ShapeDtypeStruct((batch * seq, D_ATTN), BF16),
        scratch_shapes=[bf_scr, bf_scr, bfx_scr, bfx_scr] + [f32_scr] * 8,
        compiler_params=pltpu.CompilerParams(
            dimension_semantics=("arbitrary", "arbitrary"), vmem_limit_bytes=VMEM_LIMIT),
        name="dilated_attn",
    )(qkv, qkv, qkv)


META_ROWS = 16
META_EIDX, META_GATE, META_RANK = 0, 4, 8


def _outproj_kernel(x_ref, yp_ref, ya_ref, wo_ref, g_ref, wrt_ref, brt_ref,
                    x1_ref, h2_ref, meta_ref, cnt_ref, wo_bf, before, carry):
    tm = x_ref.shape[0]

    @pl.when(pl.program_id(0) == 0)
    def _():
        wo_bf[...] = wo_ref[...].astype(BF16)
        carry[...] = jnp.zeros_like(carry)
        ti = lax.broadcasted_iota(I32, (tm, tm), 0)
        tj = lax.broadcasted_iota(I32, (tm, tm), 1)
        before[...] = (ti < tj).astype(BF16)

    x1 = (x_ref[...]
          + jnp.dot(yp_ref[...], wo_bf[:D_POOL, :], preferred_element_type=F32)
          + jnp.dot(ya_ref[...], wo_bf[D_POOL:, :], preferred_element_type=F32))
    x1_ref[...] = x1
    h2 = _rms(x1, g_ref[...])
    h2_ref[...] = _pack_row(h2)

    logits_t = lax.dot_general(wrt_ref[...].astype(BF16), h2.astype(BF16),
                               (((1,), (1,)), ((), ())), preferred_element_type=F32)
    logits_t = logits_t + brt_ref[:, 0:1]
    eid = lax.broadcasted_iota(I32, (N_EXPERTS, tm), 0)
    work = logits_t
    idxs, vals = [], []
    for _ in range(TOP_K):
        mx = jnp.max(work, axis=0, keepdims=True)
        idx = jnp.min(jnp.where(work == mx, eid, N_EXPERTS), axis=0, keepdims=True)
        idxs.append(idx)
        vals.append(mx)
        work = jnp.where(eid == idx, -jnp.inf, work)
    exps = [jnp.exp(v - vals[0]) for v in vals]
    den = exps[0] + exps[1] + exps[2] + exps[3]

    onehot = jnp.zeros((N_EXPERTS, tm), F32)
    for idx in idxs:
        onehot = onehot + (eid == idx).astype(F32)
    rank_e = carry[:, 0:1] + jnp.dot(onehot.astype(BF16), before[...],
                                     preferred_element_type=F32)
    carry[...] = carry[...] + jnp.sum(onehot, axis=1, keepdims=True)
    cnt_ref[...] = carry[...]

    mrow = lax.broadcasted_iota(I32, (META_ROWS, tm), 0)
    meta = jnp.zeros((META_ROWS, tm), F32)
    for k in range(TOP_K):
        rank_k = jnp.sum(jnp.where(eid == idxs[k], rank_e, 0.0), axis=0, keepdims=True)
        meta = jnp.where(mrow == META_EIDX + k, idxs[k].astype(F32), meta)
        meta = jnp.where(mrow == META_GATE + k, exps[k] / den, meta)
        meta = jnp.where(mrow == META_RANK + k, rank_k, meta)
    meta_ref[...] = meta


def _outproj(x2, y_pool, y_attn, w_out, g, w_router_t, b_router_t):
    t = x2.shape[0]
    row = lambda i: (i, 0)
    const = lambda i: (0, 0)
    return pl.pallas_call(
        _outproj_kernel,
        grid=(t // ROW_TILE,),
        in_specs=[
            pl.BlockSpec((ROW_TILE, D_MODEL), row),
            pl.BlockSpec((ROW_TILE, D_POOL), row),
            pl.BlockSpec((ROW_TILE, D_ATTN), row),
            pl.BlockSpec((D_MODEL, D_MODEL), const),
            pl.BlockSpec((1, D_MODEL), const),
            pl.BlockSpec((N_EXPERTS, D_MODEL), const),
            pl.BlockSpec((N_EXPERTS, LANES), const),
        ],
        out_specs=[
            pl.BlockSpec((ROW_TILE, D_MODEL), row),
            pl.BlockSpec((ROW_TILE, D_PACKED), row),
            pl.BlockSpec((META_ROWS, ROW_TILE), lambda i: (0, i)),
            pl.BlockSpec((N_EXPERTS, LANES), const),
        ],
        out_shape=[
            jax.ShapeDtypeStruct((t, D_MODEL), F32),
            jax.ShapeDtypeStruct((t, D_PACKED), I32),
            jax.ShapeDtypeStruct((META_ROWS, t), F32),
            jax.ShapeDtypeStruct((N_EXPERTS, LANES), F32),
        ],
        scratch_shapes=[pltpu.VMEM((D_MODEL, D_MODEL), BF16),
                        pltpu.VMEM((ROW_TILE, ROW_TILE), BF16),
                        pltpu.VMEM((N_EXPERTS, LANES), F32)],
        compiler_params=pltpu.CompilerParams(
            dimension_semantics=("arbitrary",), vmem_limit_bytes=VMEM_LIMIT),
        name="outproj_router",
    )(x2, y_pool, y_attn, w_out, g, w_router_t, b_router_t)


def _sc_mesh():
    return plsc.VectorSubcoreMesh(core_axis_name="core", subcore_axis_name="subcore")


def _sc_dispatch(h2, pos_rows, p_max):
    t = h2.shape[0]

    @functools.partial(
        pl.kernel, mesh=_sc_mesh(),
        out_type=[jax.ShapeDtypeStruct((p_max, D_PART), h2.dtype)] * N_PART)
    def run(h_hbm, *refs):
        pos_hbm, xs_hbm = refs[:TOP_K], refs[TOP_K:]
        for c in range(N_PART):
            def body(x_vmem, *idx_vmem, dst=xs_hbm[c]):
                for iv in idx_vmem:
                    pltpu.sync_copy(x_vmem, dst.at[iv.at[0]])

            pltpu.emit_pipeline(
                body,
                grid=(t // SC_WINDOW,),
                in_specs=[pl.BlockSpec((SC_WINDOW, D_PART), lambda i, c=c: (i, c))]
                + [pl.BlockSpec((1, SC_WINDOW), lambda i: (0, i))] * TOP_K,
                out_specs=[],
                core_axis_name=("core", "subcore"),
                dimension_semantics=(pltpu.PARALLEL,),
            )(h_hbm, *pos_hbm)

    return run(h2, *pos_rows)


def _sc_unpermute(y_parts, idx_row):
    n = idx_row.shape[1]

    @functools.partial(
        pl.kernel, mesh=_sc_mesh(),
        out_type=[jax.ShapeDtypeStruct((n, D_PART), y_parts[0].dtype)] * N_PART)
    def run(*refs):
        y_hbm, i_hbm, o_hbm = refs[:N_PART], refs[N_PART], refs[N_PART + 1:]
        for c in range(N_PART):
            def body(i_vmem, o_vmem, src=y_hbm[c]):
                pltpu.sync_copy(src.at[i_vmem.at[0]], o_vmem)

            pltpu.emit_pipeline(
                body,
                grid=(n // SC_WINDOW,),
                in_specs=[pl.BlockSpec((1, SC_WINDOW), lambda i: (0, i))],
                out_specs=[pl.BlockSpec((SC_WINDOW, D_PART), lambda i: (i, 0))],
                core_axis_name=("core", "subcore"),
                dimension_semantics=(pltpu.PARALLEL,),
            )(i_hbm, o_hbm[c])

    return run(*y_parts, idx_row)


def _moe_kernel(te_ref, nv_ref, rows_ref, nxt_ref, *refs):
    xs_refs = refs[:N_PART]
    bgu_ref, bd_ref, wgu_hbm, wd_hbm = refs[N_PART:N_PART + 4]
    y_refs = refs[N_PART + 4:2 * N_PART + 4]
    wgu_f32, wd_f32, wgu_bf, wd_bf, sem = refs[2 * N_PART + 4:]
    i = pl.program_id(0)

    def weight_copies(e):
        return (pltpu.make_async_copy(wgu_hbm.at[e], wgu_f32, sem.at[0]),
                pltpu.make_async_copy(wd_hbm.at[e], wd_f32, sem.at[1]))

    @pl.when(i < nv_ref[0])
    def _():
        prev = te_ref[jnp.maximum(i - 1, 0)]
        new_expert = jnp.logical_or(i == 0, te_ref[i] != prev)

        @pl.when(i == 0)
        def _():
            for cp in weight_copies(te_ref[0]):
                cp.start()

        @pl.when(new_expert)
        def _():
            for cp in weight_copies(te_ref[i]):
                cp.wait()
            wgu_bf[...] = wgu_f32[...].astype(BF16)
            wd_bf[...] = wd_f32[...].astype(BF16)

            @pl.when(nxt_ref[i] >= 0)
            def _():
                for cp in weight_copies(nxt_ref[i]):
                    cp.start()

        d_e = wd_bf.shape[0]

        def ffn(n_rows):
            x = _unpack_row(jnp.concatenate([r[0:n_rows, :] for r in xs_refs], axis=1))
            rid = lax.broadcasted_iota(I32, (n_rows, 1), 0)
            x = jnp.where(rid < rows_ref[i], x, 0.0).astype(BF16)
            gu = jnp.dot(x, wgu_bf[...], preferred_element_type=F32) + bgu_ref[0]
            gate = jnp.minimum(gu[:, :d_e], SWIGLU_LIMIT)
            lin = jnp.clip(gu[:, d_e:], -SWIGLU_LIMIT, SWIGLU_LIMIT)
            act = gate * jax.nn.sigmoid(SWIGLU_ALPHA * gate) * (lin + 1.0)
            y = jnp.dot(act.astype(BF16), wd_bf[...], preferred_element_type=F32) + bd_ref[0]
            yp = _pack_row(y)
            for c, y_ref in enumerate(y_refs):
                y_ref[0:n_rows, :] = yp[:, c * D_PART:(c + 1) * D_PART]
                if n_rows < MOE_TILE:
                    y_ref[n_rows:, :] = jnp.zeros((MOE_TILE - n_rows, D_PART), I32)

        half = MOE_TILE // 2

        @pl.when(rows_ref[i] > half)
        def _():
            ffn(MOE_TILE)

        @pl.when(rows_ref[i] <= half)
        def _():
            ffn(half)

    @pl.when(i >= nv_ref[0])
    def _():
        for y_ref in y_refs:
            y_ref[...] = jnp.zeros_like(y_ref)


def _moe(tile_expert, n_valid, tile_rows, next_expert, xs_parts, w_gu, b_gu, w_down, b_down):
    p_max = xs_parts[0].shape[0]
    n_tiles = p_max // MOE_TILE
    d_e = w_down.shape[1]

    def row(i, te, nv, tr, nx):
        return (jnp.minimum(i, nv[0] - 1), 0)

    def expert(i, te, nv, tr, nx):
        return (te[jnp.minimum(i, nv[0] - 1)], 0, 0)

    grid_spec = pltpu.PrefetchScalarGridSpec(
        num_scalar_prefetch=4,
        grid=(n_tiles,),
        in_specs=[pl.BlockSpec((MOE_TILE, D_PART), row)] * N_PART + [
            pl.BlockSpec((1, 1, 2 * d_e), expert),
            pl.BlockSpec((1, 1, D_MODEL), expert),
            pl.BlockSpec(memory_space=pl.ANY),
            pl.BlockSpec(memory_space=pl.ANY),
        ],
        out_specs=[pl.BlockSpec((MOE_TILE, D_PART), lambda i, te, nv, tr, nx: (i, 0))] * N_PART,
        scratch_shapes=[pltpu.VMEM((D_MODEL, 2 * d_e), F32), pltpu.VMEM((d_e, D_MODEL), F32),
                        pltpu.VMEM((D_MODEL, 2 * d_e), BF16), pltpu.VMEM((d_e, D_MODEL), BF16),
                        pltpu.SemaphoreType.DMA((2,))],
    )
    return pl.pallas_call(
        _moe_kernel,
        grid_spec=grid_spec,
        out_shape=[jax.ShapeDtypeStruct((p_max, D_PART), I32)] * N_PART,
        compiler_params=pltpu.CompilerParams(
            dimension_semantics=("arbitrary",), vmem_limit_bytes=VMEM_LIMIT),
        name="moe_ffn",
    )(tile_expert, n_valid, tile_rows, next_expert, *xs_parts, b_gu, b_down, w_gu, w_down)


def _combine_kernel(x1_ref, gate_ref, g_ref, *refs):
    yk_refs, o_ref = refs[:N_PART], refs[-1]
    acc = x1_ref[...]
    for k in range(TOP_K):
        y_k = _unpack_row(jnp.concatenate([r[k] for r in yk_refs], axis=1))
        acc = acc + gate_ref[:, k:k + 1] * y_k
    o_ref[...] = _rms(acc, g_ref[...])


def _combine(x1, gates, g, yk_parts):
    t, d = x1.shape
    tq = COMBINE_TILE
    row = lambda i: (i, 0)
    return pl.pallas_call(
        _combine_kernel,
        grid=(t // tq,),
        in_specs=[
            pl.BlockSpec((tq, d), row),
            pl.BlockSpec((tq, TOP_K), row),
            pl.BlockSpec((1, d), lambda i: (0, 0)),
        ] + [pl.BlockSpec((TOP_K, tq, D_PART), lambda i: (0, i, 0))] * N_PART,
        out_specs=pl.BlockSpec((tq, d), row),
        out_shape=jax.ShapeDtypeStruct((t, d), F32),
        compiler_params=pltpu.CompilerParams(
            dimension_semantics=("arbitrary",), vmem_limit_bytes=VMEM_LIMIT),
        name="combine_final",
    )(x1, gates, g, *yk_parts)


def kernel(x, g_mix, w_in, w_pool, pool_scale, w_out, g_ffn, w_router, b_router,
           w_gu, b_gu, w_down, b_down, g_final):
    batch, seq, d = x.shape
    t = batch * seq
    x2 = x.reshape(t, d)

    y_pool, qkv = _inproj(x2, g_mix[0].reshape(1, d), w_in[0], w_pool[0],
                          pool_scale[0].reshape(1, D_POOL), seq)
    y_attn = _attention(qkv, batch, seq)

    wr_t = w_router[0].T
    br_t = jnp.broadcast_to(b_router[0].reshape(N_EXPERTS, 1), (N_EXPERTS, LANES))
    x1, h2, meta, cnt = _outproj(x2, y_pool, y_attn, w_out[0], g_ffn[0].reshape(1, d), wr_t, br_t)

    eidx = meta[META_EIDX:META_EIDX + TOP_K].astype(I32)
    rank = meta[META_RANK:META_RANK + TOP_K].astype(I32)
    gates = meta[META_GATE:META_GATE + TOP_K].T
    counts = cnt[:, 0].astype(I32)
    padded = ((counts + MOE_TILE - 1) // MOE_TILE) * MOE_TILE
    ends = jnp.cumsum(padded)
    offsets = ends - padded
    e_ids = jnp.arange(N_EXPERTS, dtype=I32)
    hit = eidx[None] == e_ids[:, None, None]
    pos = rank + jnp.sum(jnp.where(hit, offsets[:, None, None], 0), axis=0)
    p_max = t * TOP_K + N_EXPERTS * MOE_TILE
    n_tiles = p_max // MOE_TILE
    tile_start = jnp.arange(n_tiles, dtype=I32) * MOE_TILE
    tile_expert = jnp.minimum(
        jnp.sum((tile_start[:, None] >= ends[None, :]).astype(I32), axis=1), N_EXPERTS - 1)
    mine = tile_expert[:, None] == e_ids[None, :]
    group_end = jnp.sum(jnp.where(mine, (offsets + counts)[None, :], 0), axis=1)
    tile_rows = jnp.clip(group_end - tile_start, 0, MOE_TILE)
    n_valid = (ends[-1] // MOE_TILE).reshape(1).astype(I32)
    later = (e_ids[None, :] > e_ids[:, None]) & (counts[None, :] > 0)
    next_nonempty = jnp.min(jnp.where(later, e_ids[None, :], N_EXPERTS), axis=1)
    next_nonempty = jnp.where(next_nonempty < N_EXPERTS, next_nonempty, -1)
    next_expert = jnp.sum(jnp.where(mine, next_nonempty[None, :], 0), axis=1)

    xs = _sc_dispatch(h2, [pos[k:k + 1] for k in range(TOP_K)], p_max)
    ys = _moe(tile_expert, n_valid, tile_rows, next_expert, xs, w_gu[0],
              b_gu[0].reshape(N_EXPERTS, 1, -1), w_down[0], b_down[0].reshape(N_EXPERTS, 1, -1))
    yk = _sc_unpermute(ys, pos.reshape(1, TOP_K * t))
    yk = [q.reshape(TOP_K, t, D_PART) for q in yk]
    out = _combine(x1, gates, g_final.reshape(1, d), yk)
    return out.reshape(batch, seq, d)
```

```python
import functools

import jax
import jax.numpy as jnp
from jax import lax
from jax.experimental import pallas as pl
from jax.experimental.pallas import tpu as pltpu
from jax.experimental.pallas import tpu_sc as plsc

F32 = jnp.float32
BF16 = jnp.bfloat16
I32 = jnp.int32

D_MODEL = 1024
D_POOL = 512
D_ATTN = 512
POOL_WINDOWS = (2, 4, 8, 16)
POOL_GROUP = 128
HEAD_DIM = 64
N_EXPERTS = 32
TOP_K = 4
SWIGLU_LIMIT = 7.0
SWIGLU_ALPHA = 1.702
EPS = 1e-5
NEG_INF = -1e30
LOG2_E = 1.4426950408889634

LANES = 128
QB = 128
DIL_MID = 4
DIL_FAR = 16
ROW_TILE = 1024
MOE_TILE = 512
COMBINE_TILE = 1024
SC_WINDOW = 128
D_PACKED = D_MODEL // 2
N_PART = 2
D_PART = D_PACKED // N_PART
VMEM_LIMIT = 56 * 1024 * 1024


def _pack_row(x):
    hi = lax.bitcast_convert_type(x[:, :D_PACKED].astype(BF16).astype(F32), I32)
    lo = lax.bitcast_convert_type(x[:, D_PACKED:].astype(BF16).astype(F32), I32)
    return hi | lax.shift_right_logical(lo, 16)


def _unpack_row(w):
    hi = lax.bitcast_convert_type(w & jnp.int32(-65536), F32)
    lo = lax.bitcast_convert_type(lax.shift_left(w, 16), F32)
    return jnp.concatenate([hi, lo], axis=1)


def _rms(x, g):
    ms = jnp.mean(x * x, axis=-1, keepdims=True)
    return x * lax.rsqrt(ms + EPS) * g


POOL_PAD = 16


def _inproj_kernel(x_ref, g_ref, w_ref, wp_ref, sc_ref, yp_ref, qkv_ref, w_bf, hist, pad_ref,
                   *, tiles_per_seq):
    i = pl.program_id(0)
    tm = x_ref.shape[0]

    @pl.when(i == 0)
    def _():
        w_bf[...] = w_ref[...].astype(BF16)
        pad_ref[0:POOL_PAD, :] = jnp.zeros((POOL_PAD, POOL_GROUP), F32)

    @pl.when(i % tiles_per_seq == 0)
    def _():
        hist[...] = jnp.zeros_like(hist)

    h = _rms(x_ref[...], g_ref[...]).astype(BF16)
    proj = jnp.dot(h, w_bf[...], preferred_element_type=F32)
    q = proj[:, D_POOL:D_POOL + D_ATTN] * (HEAD_DIM ** -0.5 * LOG2_E)
    qkv_ref[:, :D_ATTN] = q.astype(BF16)
    qkv_ref[:, D_ATTN:] = proj[:, D_POOL + D_ATTN:].astype(BF16)

    pos = (i % tiles_per_seq) * tm + lax.broadcasted_iota(I32, (tm, 1), 0)
    for g, w in enumerate(POOL_WINDOWS):
        lo, hi = g * POOL_GROUP, (g + 1) * POOL_GROUP
        e = proj[:, lo:hi]
        acc = jnp.concatenate([hist[g], e], axis=0)
        hist[g] = e[tm - POOL_PAD:, :]
        span = 1
        while span < w:
            pad_ref[POOL_PAD:, :] = acc
            acc = acc + pad_ref[pl.ds(POOL_PAD - span, POOL_PAD + tm), :]
            span *= 2
        count = jnp.minimum(pos + 1, w).astype(F32)
        pooled = acc[POOL_PAD:, :] / count - e
        y = jnp.dot(pooled.astype(BF16), wp_ref[g].astype(BF16), preferred_element_type=F32)
        yp_ref[:, lo:hi] = (y * sc_ref[:, lo:hi]).astype(BF16)


def _inproj(x2, g, w, w_pool, pool_scale, seq):
    t = x2.shape[0]
    d_in = w.shape[1]
    n_grp = len(POOL_WINDOWS)
    return pl.pallas_call(
        functools.partial(_inproj_kernel, tiles_per_seq=seq // ROW_TILE),
        grid=(t // ROW_TILE,),
        in_specs=[
            pl.BlockSpec((ROW_TILE, D_MODEL), lambda i: (i, 0)),
            pl.BlockSpec((1, D_MODEL), lambda i: (0, 0)),
            pl.BlockSpec((D_MODEL, d_in), lambda i: (0, 0)),
            pl.BlockSpec((n_grp, POOL_GROUP, POOL_GROUP), lambda i: (0, 0, 0)),
            pl.BlockSpec((1, D_POOL), lambda i: (0, 0)),
        ],
        out_specs=[
            pl.BlockSpec((ROW_TILE, D_POOL), lambda i: (i, 0)),
            pl.BlockSpec((ROW_TILE, 3 * D_ATTN), lambda i: (i, 0)),
        ],
        out_shape=[
            jax.ShapeDtypeStruct((t, D_POOL), BF16),
            jax.ShapeDtypeStruct((t, 3 * D_ATTN), BF16),
        ],
        scratch_shapes=[pltpu.VMEM((D_MODEL, d_in), BF16),
                        pltpu.VMEM((n_grp, POOL_PAD, POOL_GROUP), F32),
                        pltpu.VMEM((2 * POOL_PAD + ROW_TILE, POOL_GROUP), F32)],
        compiler_params=pltpu.CompilerParams(
            dimension_semantics=("arbitrary",), vmem_limit_bytes=VMEM_LIMIT),
        name="inproj_pool",
    )(x2, g, w, w_pool, pool_scale)


def _pair_block(q, k, v_ext, mask2, lane_h0):
    zero = jnp.zeros_like(q)
    q2 = jnp.concatenate([jnp.where(lane_h0, q, zero), jnp.where(lane_h0, zero, q)], axis=0)
    s = lax.dot_general(q2, k, (((1,), (1,)), ((), ())), preferred_element_type=F32)
    s = jnp.where(mask2, s, NEG_INF)
    m = jnp.max(s, axis=-1, keepdims=True)
    p = jnp.exp2(s - m).astype(BF16)
    ol = jnp.dot(p, v_ext, preferred_element_type=F32)
    o = jnp.where(lane_h0, ol[:QB, :LANES], ol[QB:, :LANES])
    l = jnp.where(lane_h0, ol[:QB, LANES:], ol[QB:, LANES:])
    mb = jnp.where(lane_h0, m[:QB], m[QB:])
    return o, mb, l


def _merge(o_a, m_a, l_a, o_b, m_b, l_b):
    m = jnp.maximum(m_a, m_b)
    ea = jnp.exp2(m_a - m)
    eb = jnp.exp2(m_b - m)
    return o_a * ea + o_b * eb, m, l_a * ea + l_b * eb


def _both_heads(mask):
    return jnp.concatenate([mask, mask], axis=0)


def _attn_kernel(q_ref, k_ref, v_ref, o_ref, qp, kp, vpx, vnx, o23, m23, l23, o2, m2, l2,
                 on, mn):
    s = q_ref.shape[0]
    n_chunk = DIL_FAR
    rows = s // n_chunk
    sub = rows // DIL_MID
    grp = n_chunk * n_chunk
    lane_h0 = lax.broadcasted_iota(I32, (1, LANES), 1) < HEAD_DIM

    @pl.when((pl.program_id(0) == 0) & (pl.program_id(1) == 0))
    def _():
        ones = jnp.ones((s, LANES), BF16)
        vnx[:, LANES:] = ones
        vpx[:, LANES:] = ones

    vnx[:, :LANES] = v_ref[...]

    pi = lax.broadcasted_iota(I32, (grp, grp), 0)
    pj = lax.broadcasted_iota(I32, (grp, grp), 1)
    swap = (pj == (pi % n_chunk) * n_chunk + pi // n_chunk).astype(BF16)
    for g in range(s // grp):
        src = slice(g * grp, (g + 1) * grp)
        qk = jnp.concatenate([q_ref[src, :], k_ref[src, :]], axis=1)
        yqk = jnp.dot(swap, qk, preferred_element_type=F32).astype(BF16)
        yv = jnp.dot(swap, v_ref[src, :], preferred_element_type=F32).astype(BF16)
        for r in range(n_chunk):
            dst = slice(r * rows + g * n_chunk, r * rows + (g + 1) * n_chunk)
            part = slice(r * n_chunk, (r + 1) * n_chunk)
            qp[dst, :] = yqk[part, :LANES]
            kp[dst, :] = yqk[part, LANES:]
            vpx[dst, 0:LANES] = yv[part]

    qi = lax.broadcasted_iota(I32, (QB, QB), 0)
    kj = lax.broadcasted_iota(I32, (QB, QB), 1)
    causal = _both_heads(qi >= kj)

    for r in range(n_chunk):
        blk = slice(r * rows, (r + 1) * rows)
        o, m, l = _pair_block(qp[blk, :], kp[blk, :], vpx[blk, :], causal, lane_h0)
        o23[blk, :] = o
        m23[blk, :] = m
        l23[blk, :] = l

    def mid_index(n_key_sub):
        i_q = lax.broadcasted_iota(I32, (QB, DIL_MID * n_key_sub), 0)
        i_k = lax.broadcasted_iota(I32, (QB, DIL_MID * n_key_sub), 1)
        return i_q // sub, i_q % sub, i_k // n_key_sub, i_k % n_key_sub

    jq, aq, jk, ak = mid_index(sub)
    d0 = DIL_FAR * (aq - ak) + DIL_MID * (jq - jk)
    mask_mid0 = _both_heads(d0 >= 0)
    jq, aq, jk, ak = mid_index(2 * sub)
    d1 = DIL_FAR * (aq + sub - ak) + DIL_MID * (jq - jk)
    mask_mid = _both_heads((d1 >= 0) & (d1 <= DIL_MID * QB))

    def mid_tile(r4, a0, k0, nk, mask):
        def at(j, off, n):
            return pl.ds((DIL_MID * j + r4) * rows + off, n)

        q = jnp.concatenate([qp[at(j, a0, sub), :] for j in range(DIL_MID)], axis=0)
        k = jnp.concatenate([kp[at(j, k0, nk), :] for j in range(DIL_MID)], axis=0)
        v = jnp.concatenate([vpx[at(j, k0, nk), :] for j in range(DIL_MID)], axis=0)
        o, m, l = _pair_block(q, k, v, mask, lane_h0)
        for j in range(DIL_MID):
            dst = at(j, a0, sub)
            src = slice(j * sub, (j + 1) * sub)
            o2[dst, :] = o[src]
            m2[dst, :] = m[src]
            l2[dst, :] = l[src]

    for r4 in range(DIL_MID):
        mid_tile(r4, 0, 0, sub, mask_mid0)
        for a_blk in range(1, rows // sub):
            mid_tile(r4, a_blk * sub, (a_blk - 1) * sub, 2 * sub, mask_mid)

    for r in range(n_chunk):
        blk = slice(r * rows, (r + 1) * rows)
        oo, mm, ll = _merge(o23[blk, :], m23[blk, :], l23[blk, :],
                            o2[blk, :], m2[blk, :], l2[blk, :])
        o23[blk, :] = oo / ll
        m23[blk, :] = mm + jnp.log2(ll)

    for g in range(s // grp):
        slabs = [o23[r * rows + g * n_chunk:r * rows + (g + 1) * n_chunk, :].astype(BF16)
                 for r in range(n_chunk)]
        on[g * grp:(g + 1) * grp, :] = jnp.dot(swap, jnp.concatenate(slabs, axis=0),
                                               preferred_element_type=F32)
    for r in range(n_chunk):
        mn[pl.ds(r, rows, stride=n_chunk), :] = m23[r * rows:(r + 1) * rows, :]

    qi2 = lax.broadcasted_iota(I32, (QB, 2 * QB), 0)
    kj2 = lax.broadcasted_iota(I32, (QB, 2 * QB), 1)
    dn = qi2 + QB - kj2
    mask_near = _both_heads((dn >= 0) & (dn <= QB))

    def near_finish(dst, o, m, l):
        oo, _, ll = _merge(on[dst, :], mn[dst, :], 1.0, o, m, l)
        o_ref[dst, :] = (oo / ll).astype(o_ref.dtype)

    first = pl.ds(0, QB)
    o, m, l = _pair_block(q_ref[first, :], k_ref[first, :], vnx[first, :], causal, lane_h0)
    near_finish(first, o, m, l)
    for n in range(1, s // QB):
        keys = pl.ds((n - 1) * QB, 2 * QB)
        o, m, l = _pair_block(q_ref[pl.ds(n * QB, QB), :], k_ref[keys, :], vnx[keys, :],
                              mask_near, lane_h0)
        near_finish(pl.ds(n * QB, QB), o, m, l)


def _attention(qkv, batch, seq):
    n_pair = D_ATTN // LANES
    blk = (seq, LANES)
    f32_scr = pltpu.VMEM(blk, F32)
    bf_scr = pltpu.VMEM(blk, BF16)
    bfx_scr = pltpu.VMEM((seq, 2 * LANES), BF16)
    return pl.pallas_call(
        _attn_kernel,
        grid=(batch, n_pair),
        in_specs=[
            pl.BlockSpec(blk, lambda b, h: (b, h)),
            pl.BlockSpec(blk, lambda b, h: (b, n_pair + h)),
            pl.BlockSpec(blk, lambda b, h: (b, 2 * n_pair + h)),
        ],
        out_specs=pl.BlockSpec(blk, lambda b, h: (b, h)),
        out_shape=jax.ShapeDtypeStruct((batch * seq, D_ATTN), BF16),
        scratch_shapes=[bf_scr, bf_scr, bfx_scr, bfx_scr] + [f32_scr] * 8,
        compiler_params=pltpu.CompilerParams(
            dimension_semantics=("arbitrary", "arbitrary"), vmem_limit_bytes=VMEM_LIMIT),
        name="dilated_attn",
    )(qkv, qkv, qkv)


META_ROWS = 16
META_EIDX, META_GATE, META_RANK = 0, 4, 8


def _outproj_kernel(x_ref, yp_ref, ya_ref, wo_ref, g_ref, wrt_ref, brt_ref,
                    x1_ref, h2_ref, meta_ref, cnt_ref, wo_bf, before, carry):
    tm = x_ref.shape[0]

    @pl.when(pl.program_id(0) == 0)
    def _():
        wo_bf[...] = wo_ref[...].astype(BF16)
        carry[...] = jnp.zeros_like(carry)
        ti = lax.broadcasted_iota(I32, (tm, tm), 0)
        tj = lax.broadcasted_iota(I32, (tm, tm), 1)
        before[...] = (ti < tj).astype(BF16)

    x1 = (x_ref[...]
          + jnp.dot(yp_ref[...], wo_bf[:D_POOL, :], preferred_element_type=F32)
          + jnp.dot(ya_ref[...], wo_bf[D_POOL:, :], preferred_element_type=F32))
    x1_ref[...] = x1
    h2 = _rms(x1, g_ref[...])
    h2_ref[...] = _pack_row(h2)

    logits_t = lax.dot_general(wrt_ref[...].astype(BF16), h2.astype(BF16),
                               (((1,), (1,)), ((), ())), preferred_element_type=F32)
    logits_t = logits_t + brt_ref[:, 0:1]
    eid = lax.broadcasted_iota(I32, (N_EXPERTS, tm), 0)
    work = logits_t
    idxs, vals = [], []
    for _ in range(TOP_K):
        mx = jnp.max(work, axis=0, keepdims=True)
        idx = jnp.min(jnp.where(work == mx, eid, N_EXPERTS), axis=0, keepdims=True)
        idxs.append(idx)
        vals.append(mx)
        work = jnp.where(eid == idx, -jnp.inf, work)
    exps = [jnp.exp(v - vals[0]) for v in vals]
    den = exps[0] + exps[1] + exps[2] + exps[3]

    onehot = jnp.zeros((N_EXPERTS, tm), F32)
    for idx in idxs:
        onehot = onehot + (eid == idx).astype(F32)
    rank_e = carry[:, 0:1] + jnp.dot(onehot.astype(BF16), before[...],
                                     preferred_element_type=F32)
    carry[...] = carry[...] + jnp.sum(onehot, axis=1, keepdims=True)
    cnt_ref[...] = carry[...]

    mrow = lax.broadcasted_iota(I32, (META_ROWS, tm), 0)
    meta = jnp.zeros((META_ROWS, tm), F32)
    for k in range(TOP_K):
        rank_k = jnp.sum(jnp.where(eid == idxs[k], rank_e, 0.0), axis=0, keepdims=True)
        meta = jnp.where(mrow == META_EIDX + k, idxs[k].astype(F32), meta)
        meta = jnp.where(mrow == META_GATE + k, exps[k] / den, meta)
        meta = jnp.where(mrow == META_RANK + k, rank_k, meta)
    meta_ref[...] = meta


def _outproj(x2, y_pool, y_attn, w_out, g, w_router_t, b_router_t):
    t = x2.shape[0]
    row = lambda i: (i, 0)
    const = lambda i: (0, 0)
    return pl.pallas_call(
        _outproj_kernel,
        grid=(t // ROW_TILE,),
        in_specs=[
            pl.BlockSpec((ROW_TILE, D_MODEL), row),
            pl.BlockSpec((ROW_TILE, D_POOL), row),
            pl.BlockSpec((ROW_TILE, D_ATTN), row),
            pl.BlockSpec((D_MODEL, D_MODEL), const),
            pl.BlockSpec((1, D_MODEL), const),
            pl.BlockSpec((N_EXPERTS, D_MODEL), const),
            pl.BlockSpec((N_EXPERTS, LANES), const),
        ],
        out_specs=[
            pl.BlockSpec((ROW_TILE, D_MODEL), row),
            pl.BlockSpec((ROW_TILE, D_PACKED), row),
            pl.BlockSpec((META_ROWS, ROW_TILE), lambda i: (0, i)),
            pl.BlockSpec((N_EXPERTS, LANES), const),
        ],
        out_shape=[
            jax.ShapeDtypeStruct((t, D_MODEL), F32),
            jax.ShapeDtypeStruct((t, D_PACKED), I32),
            jax.ShapeDtypeStruct((META_ROWS, t), F32),
            jax.ShapeDtypeStruct((N_EXPERTS, LANES), F32),
        ],
        scratch_shapes=[pltpu.VMEM((D_MODEL, D_MODEL), BF16),
                        pltpu.VMEM((ROW_TILE, ROW_TILE), BF16),
                        pltpu.VMEM((N_EXPERTS, LANES), F32)],
        compiler_params=pltpu.CompilerParams(
            dimension_semantics=("arbitrary",), vmem_limit_bytes=VMEM_LIMIT),
        name="outproj_router",
    )(x2, y_pool, y_attn, w_out, g, w_router_t, b_router_t)


def _sc_mesh():
    return plsc.VectorSubcoreMesh(core_axis_name="core", subcore_axis_name="subcore")


def _sc_dispatch(h2, pos_rows, p_max):
    t = h2.shape[0]

    @functools.partial(
        pl.kernel, mesh=_sc_mesh(),
        out_type=[jax.ShapeDtypeStruct((p_max, D_PART), h2.dtype)] * N_PART)
    def run(h_hbm, *refs):
        pos_hbm, xs_hbm = refs[:TOP_K], refs[TOP_K:]
        for c in range(N_PART):
            def body(x_vmem, *idx_vmem, dst=xs_hbm[c]):
                for iv in idx_vmem:
                    pltpu.sync_copy(x_vmem, dst.at[iv.at[0]])

            pltpu.emit_pipeline(
                body,
                grid=(t // SC_WINDOW,),
                in_specs=[pl.BlockSpec((SC_WINDOW, D_PART), lambda i, c=c: (i, c))]
                + [pl.BlockSpec((1, SC_WINDOW), lambda i: (0, i))] * TOP_K,
                out_specs=[],
                core_axis_name=("core", "subcore"),
                dimension_semantics=(pltpu.PARALLEL,),
            )(h_hbm, *pos_hbm)

    return run(h2, *pos_rows)


def _sc_unpermute(y_parts, idx_row):
    n = idx_row.shape[1]

    @functools.partial(
        pl.kernel, mesh=_sc_mesh(),
        out_type=[jax.ShapeDtypeStruct((n, D_PART), y_parts[0].dtype)] * N_PART)
    def run(*refs):
        y_hbm, i_hbm, o_hbm = refs[:N_PART], refs[N_PART], refs[N_PART + 1:]
        for c in range(N_PART):
            def body(i_vmem, o_vmem, src=y_hbm[c]):
                pltpu.sync_copy(src.at[i_vmem.at[0]], o_vmem)

            pltpu.emit_pipeline(
                body,
                grid=(n // SC_WINDOW,),
                in_specs=[pl.BlockSpec((1, SC_WINDOW), lambda i: (0, i))],
                out_specs=[pl.BlockSpec((SC_WINDOW, D_PART), lambda i: (i, 0))],
                core_axis_name=("core", "subcore"),
                dimension_semantics=(pltpu.PARALLEL,),
            )(i_hbm, o_hbm[c])

    return run(*y_parts, idx_row)


def _moe_kernel(te_ref, nv_ref, rows_ref, nxt_ref, *refs):
    xs_refs = refs[:N_PART]
    bgu_ref, bd_ref, wgu_hbm, wd_hbm = refs[N_PART:N_PART + 4]
    y_refs = refs[N_PART + 4:2 * N_PART + 4]
    wgu_f32, wd_f32, wgu_bf, wd_bf, sem = refs[2 * N_PART + 4:]
    i = pl.program_id(0)

    def weight_copies(e):
        return (pltpu.make_async_copy(wgu_hbm.at[e], wgu_f32, sem.at[0]),
                pltpu.make_async_copy(wd_hbm.at[e], wd_f32, sem.at[1]))

    @pl.when(i < nv_ref[0])
    def _():
        prev = te_ref[jnp.maximum(i - 1, 0)]
        new_expert = jnp.logical_or(i == 0, te_ref[i] != prev)

        @pl.when(i == 0)
        def _():
            for cp in weight_copies(te_ref[0]):
                cp.start()

        @pl.when(new_expert)
        def _():
            for cp in weight_copies(te_ref[i]):
                cp.wait()
            wgu_bf[...] = wgu_f32[...].astype(BF16)
            wd_bf[...] = wd_f32[...].astype(BF16)

            @pl.when(nxt_ref[i] >= 0)
            def _():
                for cp in weight_copies(nxt_ref[i]):
                    cp.start()

        d_e = wd_bf.shape[0]

        def ffn(n_rows):
            x = _unpack_row(jnp.concatenate([r[0:n_rows, :] for r in xs_refs], axis=1))
            rid = lax.broadcasted_iota(I32, (n_rows, 1), 0)
            x = jnp.where(rid < rows_ref[i], x, 0.0).astype(BF16)
            gu = jnp.dot(x, wgu_bf[...], preferred_element_type=F32) + bgu_ref[0]
            gate = jnp.minimum(gu[:, :d_e], SWIGLU_LIMIT)
            lin = jnp.clip(gu[:, d_e:], -SWIGLU_LIMIT, SWIGLU_LIMIT)
            act = gate * jax.nn.sigmoid(SWIGLU_ALPHA * gate) * (lin + 1.0)
            y = jnp.dot(act.astype(BF16), wd_bf[...], preferred_element_type=F32) + bd_ref[0]
            yp = _pack_row(y)
            for c, y_ref in enumerate(y_refs):
                y_ref[0:n_rows, :] = yp[:, c * D_PART:(c + 1) * D_PART]
                if n_rows < MOE_TILE:
                    y_ref[n_rows:, :] = jnp.zeros((MOE_TILE - n_rows, D_PART), I32)

        quarter, half = MOE_TILE // 4, MOE_TILE // 2

        @pl.when(rows_ref[i] > half)
        def _():
            ffn(MOE_TILE)

        @pl.when((rows_ref[i] > quarter) & (rows_ref[i] <= half))
        def _():
            ffn(half)

        @pl.when(rows_ref[i] <= quarter)
        def _():
            ffn(quarter)

    @pl.when(i >= nv_ref[0])
    def _():
        for y_ref in y_refs:
            y_ref[...] = jnp.zeros_like(y_ref)


def _moe(tile_expert, n_valid, tile_rows, next_expert, xs_parts, w_gu, b_gu, w_down, b_down):
    p_max = xs_parts[0].shape[0]
    n_tiles = p_max // MOE_TILE
    d_e = w_down.shape[1]

    def row(i, te, nv, tr, nx):
        return (jnp.minimum(i, nv[0] - 1), 0)

    def expert(i, te, nv, tr, nx):
        return (te[jnp.minimum(i, nv[0] - 1)], 0, 0)

    grid_spec = pltpu.PrefetchScalarGridSpec(
        num_scalar_prefetch=4,
        grid=(n_tiles,),
        in_specs=[pl.BlockSpec((MOE_TILE, D_PART), row)] * N_PART + [
            pl.BlockSpec((1, 1, 2 * d_e), expert),
            pl.BlockSpec((1, 1, D_MODEL), expert),
            pl.BlockSpec(memory_space=pl.ANY),
            pl.BlockSpec(memory_space=pl.ANY),
        ],
        out_specs=[pl.BlockSpec((MOE_TILE, D_PART), lambda i, te, nv, tr, nx: (i, 0))] * N_PART,
        scratch_shapes=[pltpu.VMEM((D_MODEL, 2 * d_e), F32), pltpu.VMEM((d_e, D_MODEL), F32),
                        pltpu.VMEM((D_MODEL, 2 * d_e), BF16), pltpu.VMEM((d_e, D_MODEL), BF16),
                        pltpu.SemaphoreType.DMA((2,))],
    )
    return pl.pallas_call(
        _moe_kernel,
        grid_spec=grid_spec,
        out_shape=[jax.ShapeDtypeStruct((p_max, D_PART), I32)] * N_PART,
        compiler_params=pltpu.CompilerParams(
            dimension_semantics=("arbitrary",), vmem_limit_bytes=VMEM_LIMIT),
        name="moe_ffn",
    )(tile_expert, n_valid, tile_rows, next_expert, *xs_parts, b_gu, b_down, w_gu, w_down)


def _combine_kernel(x1_ref, gate_ref, g_ref, *refs):
    yk_refs, o_ref = refs[:N_PART], refs[-1]
    acc = x1_ref[...]
    for k in range(TOP_K):
        y_k = _unpack_row(jnp.concatenate([r[k] for r in yk_refs], axis=1))
        acc = acc + gate_ref[:, k:k + 1] * y_k
    o_ref[...] = _rms(acc, g_ref[...])


def _combine(x1, gates, g, yk_parts):
    t, d = x1.shape
    tq = COMBINE_TILE
    row = lambda i: (i, 0)
    return pl.pallas_call(
        _combine_kernel,
        grid=(t // tq,),
        in_specs=[
            pl.BlockSpec((tq, d), row),
            pl.BlockSpec((tq, TOP_K), row),
            pl.BlockSpec((1, d), lambda i: (0, 0)),
        ] + [pl.BlockSpec((TOP_K, tq, D_PART), lambda i: (0, i, 0))] * N_PART,
        out_specs=pl.BlockSpec((tq, d), row),
        out_shape=jax.ShapeDtypeStruct((t, d), F32),
        compiler_params=pltpu.CompilerParams(
            dimension_semantics=("arbitrary",), vmem_limit_bytes=VMEM_LIMIT),
        name="combine_final",
    )(x1, gates, g, *yk_parts)


def kernel(x, g_mix, w_in, w_pool, pool_scale, w_out, g_ffn, w_router, b_router,
           w_gu, b_gu, w_down, b_down, g_final):
    batch, seq, d = x.shape
    t = batch * seq
    x2 = x.reshape(t, d)

    y_pool, qkv = _inproj(x2, g_mix[0].reshape(1, d), w_in[0], w_pool[0],
                          pool_scale[0].reshape(1, D_POOL), seq)
    y_attn = _attention(qkv, batch, seq)

    wr_t = w_router[0].T
    br_t = jnp.broadcast_to(b_router[0].reshape(N_EXPERTS, 1), (N_EXPERTS, LANES))
    x1, h2, meta, cnt = _outproj(x2, y_pool, y_attn, w_out[0], g_ffn[0].reshape(1, d), wr_t, br_t)

    eidx = meta[META_EIDX:META_EIDX + TOP_K].astype(I32)
    rank = meta[META_RANK:META_RANK + TOP_K].astype(I32)
    gates = meta[META_GATE:META_GATE + TOP_K].T
    counts = cnt[:, 0].astype(I32)
    padded = ((counts + MOE_TILE - 1) // MOE_TILE) * MOE_TILE
    ends = jnp.cumsum(padded)
    offsets = ends - padded
    e_ids = jnp.arange(N_EXPERTS, dtype=I32)
    hit = eidx[None] == e_ids[:, None, None]
    pos = rank + jnp.sum(jnp.where(hit, offsets[:, None, None], 0), axis=0)
    p_max = t * TOP_K + N_EXPERTS * MOE_TILE
    n_tiles = p_max // MOE_TILE
    tile_start = jnp.arange(n_tiles, dtype=I32) * MOE_TILE
    tile_expert = jnp.minimum(
        jnp.sum((tile_start[:, None] >= ends[None, :]).astype(I32), axis=1), N_EXPERTS - 1)
    mine = tile_expert[:, None] == e_ids[None, :]
    group_end = jnp.sum(jnp.where(mine, (offsets + counts)[None, :], 0), axis=1)
    tile_rows = jnp.clip(group_end - tile_start, 0, MOE_TILE)
    n_valid = (ends[-1] // MOE_TILE).reshape(1).astype(I32)
    later = (e_ids[None, :] > e_ids[:, None]) & (counts[None, :] > 0)
    next_nonempty = jnp.min(jnp.where(later, e_ids[None, :], N_EXPERTS), axis=1)
    next_nonempty = jnp.where(next_nonempty < N_EXPERTS, next_nonempty, -1)
    next_expert = jnp.sum(jnp.where(mine, next_nonempty[None, :], 0), axis=1)

    xs = _sc_dispatch(h2, [pos[k:k + 1] for k in range(TOP_K)], p_max)
    ys = _moe(tile_expert, n_valid, tile_rows, next_expert, xs, w_gu[0],
              b_gu[0].reshape(N_EXPERTS, 1, -1), w_down[0], b_down[0].reshape(N_EXPERTS, 1, -1))
    yk = _sc_unpermute(ys, pos.reshape(1, TOP_K * t))
    yk = [q.reshape(TOP_K, t, D_PART) for q in yk]
    out = _combine(x1, gates, g_final.reshape(1, d), yk)
    return out.reshape(batch, seq, d)
```

```python
import functools

import jax
import jax.numpy as jnp
from jax import lax
from jax.experimental import pallas as pl
from jax.experimental.pallas import tpu as pltpu
from jax.experimental.pallas import tpu_sc as plsc

F32 = jnp.float32
BF16 = jnp.bfloat16
I32 = jnp.int32

D_MODEL = 1024
D_POOL = 512
D_ATTN = 512
POOL_WINDOWS = (2, 4, 8, 16)
POOL_GROUP = 128
HEAD_DIM = 64
N_EXPERTS = 32
TOP_K = 4
SWIGLU_LIMIT = 7.0
SWIGLU_ALPHA = 1.702
EPS = 1e-5
NEG_INF = -1e30
LOG2_E = 1.4426950408889634

LANES = 128
QB = 128
DIL_MID = 4
DIL_FAR = 16
ROW_TILE = 1024
MOE_TILE = 1024
MOE_PATHS = (128, 256, 384, 512, 768, 1024)
COMBINE_TILE = 1024
SC_WINDOW = 128
D_PACKED = D_MODEL // 2
N_PART = 2
D_PART = D_PACKED // N_PART
VMEM_LIMIT = 56 * 1024 * 1024


def _pack_row(x):
    hi = lax.bitcast_convert_type(x[:, :D_PACKED].astype(BF16).astype(F32), I32)
    lo = lax.bitcast_convert_type(x[:, D_PACKED:].astype(BF16).astype(F32), I32)
    return hi | lax.shift_right_logical(lo, 16)


def _unpack_row(w):
    hi = lax.bitcast_convert_type(w & jnp.int32(-65536), F32)
    lo = lax.bitcast_convert_type(lax.shift_left(w, 16), F32)
    return jnp.concatenate([hi, lo], axis=1)


def _rms(x, g):
    ms = jnp.mean(x * x, axis=-1, keepdims=True)
    return x * lax.rsqrt(ms + EPS) * g


POOL_PAD = 16


def _inproj_kernel(x_ref, g_ref, w_ref, wp_ref, sc_ref, yp_ref, qkv_ref, w_bf, hist, pad_ref,
                   *, tiles_per_seq):
    i = pl.program_id(0)
    tm = x_ref.shape[0]

    @pl.when(i == 0)
    def _():
        w_bf[...] = w_ref[...].astype(BF16)
        pad_ref[0:POOL_PAD, :] = jnp.zeros((POOL_PAD, POOL_GROUP), F32)

    @pl.when(i % tiles_per_seq == 0)
    def _():
        hist[...] = jnp.zeros_like(hist)

    h = _rms(x_ref[...], g_ref[...]).astype(BF16)
    proj = jnp.dot(h, w_bf[...], preferred_element_type=F32)
    q = proj[:, D_POOL:D_POOL + D_ATTN] * (HEAD_DIM ** -0.5 * LOG2_E)
    qkv_ref[:, :D_ATTN] = q.astype(BF16)
    qkv_ref[:, D_ATTN:] = proj[:, D_POOL + D_ATTN:].astype(BF16)

    pos = (i % tiles_per_seq) * tm + lax.broadcasted_iota(I32, (tm, 1), 0)
    for g, w in enumerate(POOL_WINDOWS):
        lo, hi = g * POOL_GROUP, (g + 1) * POOL_GROUP
        e = proj[:, lo:hi]
        acc = jnp.concatenate([hist[g], e], axis=0)
        hist[g] = e[tm - POOL_PAD:, :]
        span = 1
        while span < w:
            pad_ref[POOL_PAD:, :] = acc
            acc = acc + pad_ref[pl.ds(POOL_PAD - span, POOL_PAD + tm), :]
            span *= 2
        count = jnp.minimum(pos + 1, w).astype(F32)
        pooled = acc[POOL_PAD:, :] / count - e
        y = jnp.dot(pooled.astype(BF16), wp_ref[g].astype(BF16), preferred_element_type=F32)
        yp_ref[:, lo:hi] = (y * sc_ref[:, lo:hi]).astype(BF16)


def _inproj(x2, g, w, w_pool, pool_scale, seq):
    t = x2.shape[0]
    d_in = w.shape[1]
    n_grp = len(POOL_WINDOWS)
    return pl.pallas_call(
        functools.partial(_inproj_kernel, tiles_per_seq=seq // ROW_TILE),
        grid=(t // ROW_TILE,),
        in_specs=[
            pl.BlockSpec((ROW_TILE, D_MODEL), lambda i: (i, 0)),
            pl.BlockSpec((1, D_MODEL), lambda i: (0, 0)),
            pl.BlockSpec((D_MODEL, d_in), lambda i: (0, 0)),
            pl.BlockSpec((n_grp, POOL_GROUP, POOL_GROUP), lambda i: (0, 0, 0)),
            pl.BlockSpec((1, D_POOL), lambda i: (0, 0)),
        ],
        out_specs=[
            pl.BlockSpec((ROW_TILE, D_POOL), lambda i: (i, 0)),
            pl.BlockSpec((ROW_TILE, 3 * D_ATTN), lambda i: (i, 0)),
        ],
        out_shape=[
            jax.ShapeDtypeStruct((t, D_POOL), BF16),
            jax.ShapeDtypeStruct((t, 3 * D_ATTN), BF16),
        ],
        scratch_shapes=[pltpu.VMEM((D_MODEL, d_in), BF16),
                        pltpu.VMEM((n_grp, POOL_PAD, POOL_GROUP), F32),
                        pltpu.VMEM((2 * POOL_PAD + ROW_TILE, POOL_GROUP), F32)],
        compiler_params=pltpu.CompilerParams(
            dimension_semantics=("arbitrary",), vmem_limit_bytes=VMEM_LIMIT),
        name="inproj_pool",
    )(x2, g, w, w_pool, pool_scale)


def _pair_block(q, k, v_ext, mask2, lane_h0):
    zero = jnp.zeros_like(q)
    q2 = jnp.concatenate([jnp.where(lane_h0, q, zero), jnp.where(lane_h0, zero, q)], axis=0)
    s = lax.dot_general(q2, k, (((1,), (1,)), ((), ())), preferred_element_type=F32)
    s = jnp.where(mask2, s, NEG_INF)
    m = jnp.max(s, axis=-1, keepdims=True)
    p = jnp.exp2(s - m).astype(BF16)
    ol = jnp.dot(p, v_ext, preferred_element_type=F32)
    o = jnp.where(lane_h0, ol[:QB, :LANES], ol[QB:, :LANES])
    l = jnp.where(lane_h0, ol[:QB, LANES:], ol[QB:, LANES:])
    mb = jnp.where(lane_h0, m[:QB], m[QB:])
    return o, mb, l


def _merge(o_a, m_a, l_a, o_b, m_b, l_b):
    m = jnp.maximum(m_a, m_b)
    ea = jnp.exp2(m_a - m)
    eb = jnp.exp2(m_b - m)
    return o_a * ea + o_b * eb, m, l_a * ea + l_b * eb


def _both_heads(mask):
    return jnp.concatenate([mask, mask], axis=0)


def _attn_kernel(q_ref, k_ref, v_ref, o_ref, qp, kp, vpx, vnx, o23, m23, l23, o2, m2, l2,
                 on, mn):
    s = q_ref.shape[0]
    n_chunk = DIL_FAR
    rows = s // n_chunk
    sub = rows // DIL_MID
    grp = n_chunk * n_chunk
    lane_h0 = lax.broadcasted_iota(I32, (1, LANES), 1) < HEAD_DIM

    @pl.when((pl.program_id(0) == 0) & (pl.program_id(1) == 0))
    def _():
        ones = jnp.ones((s, LANES), BF16)
        vnx[:, LANES:] = ones
        vpx[:, LANES:] = ones

    vnx[:, :LANES] = v_ref[...]

    pi = lax.broadcasted_iota(I32, (grp, grp), 0)
    pj = lax.broadcasted_iota(I32, (grp, grp), 1)
    swap = (pj == (pi % n_chunk) * n_chunk + pi // n_chunk).astype(BF16)
    for g in range(s // grp):
        src = slice(g * grp, (g + 1) * grp)
        qk = jnp.concatenate([q_ref[src, :], k_ref[src, :]], axis=1)
        yqk = jnp.dot(swap, qk, preferred_element_type=F32).astype(BF16)
        yv = jnp.dot(swap, v_ref[src, :], preferred_element_type=F32).astype(BF16)
        for r in range(n_chunk):
            dst = slice(r * rows + g * n_chunk, r * rows + (g + 1) * n_chunk)
            part = slice(r * n_chunk, (r + 1) * n_chunk)
            qp[dst, :] = yqk[part, :LANES]
            kp[dst, :] = yqk[part, LANES:]
            vpx[dst, 0:LANES] = yv[part]

    qi = lax.broadcasted_iota(I32, (QB, QB), 0)
    kj = lax.broadcasted_iota(I32, (QB, QB), 1)
    causal = _both_heads(qi >= kj)

    for r in range(n_chunk):
        blk = slice(r * rows, (r + 1) * rows)
        o, m, l = _pair_block(qp[blk, :], kp[blk, :], vpx[blk, :], causal, lane_h0)
        o23[blk, :] = o
        m23[blk, :] = m
        l23[blk, :] = l

    def mid_index(n_key_sub):
        i_q = lax.broadcasted_iota(I32, (QB, DIL_MID * n_key_sub), 0)
        i_k = lax.broadcasted_iota(I32, (QB, DIL_MID * n_key_sub), 1)
        return i_q // sub, i_q % sub, i_k // n_key_sub, i_k % n_key_sub

    jq, aq, jk, ak = mid_index(sub)
    d0 = DIL_FAR * (aq - ak) + DIL_MID * (jq - jk)
    mask_mid0 = _both_heads(d0 >= 0)
    jq, aq, jk, ak = mid_index(2 * sub)
    d1 = DIL_FAR * (aq + sub - ak) + DIL_MID * (jq - jk)
    mask_mid = _both_heads((d1 >= 0) & (d1 <= DIL_MID * QB))

    def mid_tile(r4, a0, k0, nk, mask):
        def at(j, off, n):
            return pl.ds((DIL_MID * j + r4) * rows + off, n)

        q = jnp.concatenate([qp[at(j, a0, sub), :] for j in range(DIL_MID)], axis=0)
        k = jnp.concatenate([kp[at(j, k0, nk), :] for j in range(DIL_MID)], axis=0)
        v = jnp.concatenate([vpx[at(j, k0, nk), :] for j in range(DIL_MID)], axis=0)
        o, m, l = _pair_block(q, k, v, mask, lane_h0)
        for j in range(DIL_MID):
            dst = at(j, a0, sub)
            src = slice(j * sub, (j + 1) * sub)
            o2[dst, :] = o[src]
            m2[dst, :] = m[src]
            l2[dst, :] = l[src]

    for r4 in range(DIL_MID):
        mid_tile(r4, 0, 0, sub, mask_mid0)
        for a_blk in range(1, rows // sub):
            mid_tile(r4, a_blk * sub, (a_blk - 1) * sub, 2 * sub, mask_mid)

    for r in range(n_chunk):
        blk = slice(r * rows, (r + 1) * rows)
        oo, mm, ll = _merge(o23[blk, :], m23[blk, :], l23[blk, :],
                            o2[blk, :], m2[blk, :], l2[blk, :])
        o23[blk, :] = oo / ll
        m23[blk, :] = mm + jnp.log2(ll)

    for g in range(s // grp):
        slabs = [o23[r * rows + g * n_chunk:r * rows + (g + 1) * n_chunk, :].astype(BF16)
                 for r in range(n_chunk)]
        on[g * grp:(g + 1) * grp, :] = jnp.dot(swap, jnp.concatenate(slabs, axis=0),
                                               preferred_element_type=F32)
    for r in range(n_chunk):
        mn[pl.ds(r, rows, stride=n_chunk), :] = m23[r * rows:(r + 1) * rows, :]

    qi2 = lax.broadcasted_iota(I32, (QB, 2 * QB), 0)
    kj2 = lax.broadcasted_iota(I32, (QB, 2 * QB), 1)
    dn = qi2 + QB - kj2
    mask_near = _both_heads((dn >= 0) & (dn <= QB))

    def near_finish(dst, o, m, l):
        oo, _, ll = _merge(on[dst, :], mn[dst, :], 1.0, o, m, l)
        o_ref[dst, :] = (oo / ll).astype(o_ref.dtype)

    first = pl.ds(0, QB)
    o, m, l = _pair_block(q_ref[first, :], k_ref[first, :], vnx[first, :], causal, lane_h0)
    near_finish(first, o, m, l)
    for n in range(1, s // QB):
        keys = pl.ds((n - 1) * QB, 2 * QB)
        o, m, l = _pair_block(q_ref[pl.ds(n * QB, QB), :], k_ref[keys, :], vnx[keys, :],
                              mask_near, lane_h0)
        near_finish(pl.ds(n * QB, QB), o, m, l)


def _attention(qkv, batch, seq):
    n_pair = D_ATTN // LANES
    blk = (seq, LANES)
    f32_scr = pltpu.VMEM(blk, F32)
    bf_scr = pltpu.VMEM(blk, BF16)
    bfx_scr = pltpu.VMEM((seq, 2 * LANES), BF16)
    return pl.pallas_call(
        _attn_kernel,
        grid=(batch, n_pair),
        in_specs=[
            pl.BlockSpec(blk, lambda b, h: (b, h)),
            pl.BlockSpec(blk, lambda b, h: (b, n_pair + h)),
            pl.BlockSpec(blk, lambda b, h: (b, 2 * n_pair + h)),
        ],
        out_specs=pl.BlockSpec(blk, lambda b, h: (b, h)),
        out_shape=jax.ShapeDtypeStruct((batch * seq, D_ATTN), BF16),
        scratch_shapes=[bf_scr, bf_scr, bfx_scr, bfx_scr] + [f32_scr] * 8,
        compiler_params=pltpu.CompilerParams(
            dimension_semantics=("arbitrary", "arbitrary"), vmem_limit_bytes=VMEM_LIMIT),
        name="dilated_attn",
    )(qkv, qkv, qkv)


META_ROWS = 16
META_EIDX, META_GATE, META_RANK = 0, 4, 8


def _outproj_kernel(x_ref, yp_ref, ya_ref, wo_ref, g_ref, wrt_ref, brt_ref,
                    x1_ref, h2_ref, meta_ref, cnt_ref, wo_bf, before, carry):
    tm = x_ref.shape[0]

    @pl.when(pl.program_id(0) == 0)
    def _():
        wo_bf[...] = wo_ref[...].astype(BF16)
        carry[...] = jnp.zeros_like(carry)
        ti = lax.broadcasted_iota(I32, (tm, tm), 0)
        tj = lax.broadcasted_iota(I32, (tm, tm), 1)
        before[...] = (ti < tj).astype(BF16)

    x1 = (x_ref[...]
          + jnp.dot(yp_ref[...], wo_bf[:D_POOL, :], preferred_element_type=F32)
          + jnp.dot(ya_ref[...], wo_bf[D_POOL:, :], preferred_element_type=F32))
    x1_ref[...] = x1
    h2 = _rms(x1, g_ref[...])
    h2_ref[...] = _pack_row(h2)

    logits_t = lax.dot_general(wrt_ref[...].astype(BF16), h2.astype(BF16),
                               (((1,), (1,)), ((), ())), preferred_element_type=F32)
    logits_t = logits_t + brt_ref[:, 0:1]
    eid = lax.broadcasted_iota(I32, (N_EXPERTS, tm), 0)
    work = logits_t
    idxs, vals = [], []
    for _ in range(TOP_K):
        mx = jnp.max(work, axis=0, keepdims=True)
        idx = jnp.min(jnp.where(work == mx, eid, N_EXPERTS), axis=0, keepdims=True)
        idxs.append(idx)
        vals.append(mx)
        work = jnp.where(eid == idx, -jnp.inf, work)
    exps = [jnp.exp(v - vals[0]) for v in vals]
    den = exps[0] + exps[1] + exps[2] + exps[3]

    onehot = jnp.zeros((N_EXPERTS, tm), F32)
    for idx in idxs:
        onehot = onehot + (eid == idx).astype(F32)
    rank_e = carry[:, 0:1] + jnp.dot(onehot.astype(BF16), before[...],
                                     preferred_element_type=F32)
    carry[...] = carry[...] + jnp.sum(onehot, axis=1, keepdims=True)
    cnt_ref[...] = carry[...]

    mrow = lax.broadcasted_iota(I32, (META_ROWS, tm), 0)
    meta = jnp.zeros((META_ROWS, tm), F32)
    for k in range(TOP_K):
        rank_k = jnp.sum(jnp.where(eid == idxs[k], rank_e, 0.0), axis=0, keepdims=True)
        meta = jnp.where(mrow == META_EIDX + k, idxs[k].astype(F32), meta)
        meta = jnp.where(mrow == META_GATE + k, exps[k] / den, meta)
        meta = jnp.where(mrow == META_RANK + k, rank_k, meta)
    meta_ref[...] = meta


def _outproj(x2, y_pool, y_attn, w_out, g, w_router_t, b_router_t):
    t = x2.shape[0]
    row = lambda i: (i, 0)
    const = lambda i: (0, 0)
    return pl.pallas_call(
        _outproj_kernel,
        grid=(t // ROW_TILE,),
        in_specs=[
            pl.BlockSpec((ROW_TILE, D_MODEL), row),
            pl.BlockSpec((ROW_TILE, D_POOL), row),
            pl.BlockSpec((ROW_TILE, D_ATTN), row),
            pl.BlockSpec((D_MODEL, D_MODEL), const),
            pl.BlockSpec((1, D_MODEL), const),
            pl.BlockSpec((N_EXPERTS, D_MODEL), const),
            pl.BlockSpec((N_EXPERTS, LANES), const),
        ],
        out_specs=[
            pl.BlockSpec((ROW_TILE, D_MODEL), row),
            pl.BlockSpec((ROW_TILE, D_PACKED), row),
            pl.BlockSpec((META_ROWS, ROW_TILE), lambda i: (0, i)),
            pl.BlockSpec((N_EXPERTS, LANES), const),
        ],
        out_shape=[
            jax.ShapeDtypeStruct((t, D_MODEL), F32),
            jax.ShapeDtypeStruct((t, D_PACKED), I32),
            jax.ShapeDtypeStruct((META_ROWS, t), F32),
            jax.ShapeDtypeStruct((N_EXPERTS, LANES), F32),
        ],
        scratch_shapes=[pltpu.VMEM((D_MODEL, D_MODEL), BF16),
                        pltpu.VMEM((ROW_TILE, ROW_TILE), BF16),
                        pltpu.VMEM((N_EXPERTS, LANES), F32)],
        compiler_params=pltpu.CompilerParams(
            dimension_semantics=("arbitrary",), vmem_limit_bytes=VMEM_LIMIT),
        name="outproj_router",
    )(x2, y_pool, y_attn, w_out, g, w_router_t, b_router_t)


def _sc_mesh():
    return plsc.VectorSubcoreMesh(core_axis_name="core", subcore_axis_name="subcore")


def _sc_dispatch(h2, pos_rows, p_max):
    t = h2.shape[0]

    @functools.partial(
        pl.kernel, mesh=_sc_mesh(),
        out_type=[jax.ShapeDtypeStruct((p_max, D_PART), h2.dtype)] * N_PART)
    def run(h_hbm, *refs):
        pos_hbm, xs_hbm = refs[:TOP_K], refs[TOP_K:]
        for c in range(N_PART):
            def body(x_vmem, *idx_vmem, dst=xs_hbm[c]):
                for iv in idx_vmem:
                    pltpu.sync_copy(x_vmem, dst.at[iv.at[0]])

            pltpu.emit_pipeline(
                body,
                grid=(t // SC_WINDOW,),
                in_specs=[pl.BlockSpec((SC_WINDOW, D_PART), lambda i, c=c: (i, c))]
                + [pl.BlockSpec((1, SC_WINDOW), lambda i: (0, i))] * TOP_K,
                out_specs=[],
                core_axis_name=("core", "subcore"),
                dimension_semantics=(pltpu.PARALLEL,),
            )(h_hbm, *pos_hbm)

    return run(h2, *pos_rows)


def _sc_unpermute(y_parts, idx_row):
    n = idx_row.shape[1]

    @functools.partial(
        pl.kernel, mesh=_sc_mesh(),
        out_type=[jax.ShapeDtypeStruct((n, D_PART), y_parts[0].dtype)] * N_PART)
    def run(*refs):
        y_hbm, i_hbm, o_hbm = refs[:N_PART], refs[N_PART], refs[N_PART + 1:]
        for c in range(N_PART):
            def body(i_vmem, o_vmem, src=y_hbm[c]):
                pltpu.sync_copy(src.at[i_vmem.at[0]], o_vmem)

            pltpu.emit_pipeline(
                body,
                grid=(n // SC_WINDOW,),
                in_specs=[pl.BlockSpec((1, SC_WINDOW), lambda i: (0, i))],
                out_specs=[pl.BlockSpec((SC_WINDOW, D_PART), lambda i: (i, 0))],
                core_axis_name=("core", "subcore"),
                dimension_semantics=(pltpu.PARALLEL,),
            )(i_hbm, o_hbm[c])

    return run(*y_parts, idx_row)


def _moe_kernel(te_ref, nv_ref, rows_ref, nxt_ref, *refs):
    xs_refs = refs[:N_PART]
    bgu_ref, bd_ref, wgu_hbm, wd_hbm = refs[N_PART:N_PART + 4]
    y_refs = refs[N_PART + 4:2 * N_PART + 4]
    wgu_f32, wd_f32, wgu_bf, wd_bf, sem = refs[2 * N_PART + 4:]
    i = pl.program_id(0)

    def weight_copies(e):
        return (pltpu.make_async_copy(wgu_hbm.at[e], wgu_f32, sem.at[0]),
                pltpu.make_async_copy(wd_hbm.at[e], wd_f32, sem.at[1]))

    @pl.when(i < nv_ref[0])
    def _():
        prev = te_ref[jnp.maximum(i - 1, 0)]
        new_expert = jnp.logical_or(i == 0, te_ref[i] != prev)

        @pl.when(i == 0)
        def _():
            for cp in weight_copies(te_ref[0]):
                cp.start()

        @pl.when(new_expert)
        def _():
            for cp in weight_copies(te_ref[i]):
                cp.wait()
            wgu_bf[...] = wgu_f32[...].astype(BF16)
            wd_bf[...] = wd_f32[...].astype(BF16)

            @pl.when(nxt_ref[i] >= 0)
            def _():
                for cp in weight_copies(nxt_ref[i]):
                    cp.start()

        d_e = wd_bf.shape[0]

        def ffn(n_rows):
            x = _unpack_row(jnp.concatenate([r[0:n_rows, :] for r in xs_refs], axis=1))
            rid = lax.broadcasted_iota(I32, (n_rows, 1), 0)
            x = jnp.where(rid < rows_ref[i], x, 0.0).astype(BF16)
            gu = jnp.dot(x, wgu_bf[...], preferred_element_type=F32) + bgu_ref[0]
            gate = jnp.minimum(gu[:, :d_e], SWIGLU_LIMIT)
            lin = jnp.clip(gu[:, d_e:], -SWIGLU_LIMIT, SWIGLU_LIMIT)
            act = gate * jax.nn.sigmoid(SWIGLU_ALPHA * gate) * (lin + 1.0)
            y = jnp.dot(act.astype(BF16), wd_bf[...], preferred_element_type=F32) + bd_ref[0]
            yp = _pack_row(y)
            for c, y_ref in enumerate(y_refs):
                y_ref[0:n_rows, :] = yp[:, c * D_PART:(c + 1) * D_PART]
                if n_rows < MOE_TILE:
                    y_ref[n_rows:, :] = jnp.zeros((MOE_TILE - n_rows, D_PART), I32)

        lower = 0
        for n_rows in MOE_PATHS:
            @pl.when((rows_ref[i] > lower) & (rows_ref[i] <= n_rows))
            def _(n_rows=n_rows):
                ffn(n_rows)
            lower = n_rows

    @pl.when(i >= nv_ref[0])
    def _():
        for y_ref in y_refs:
            y_ref[...] = jnp.zeros_like(y_ref)


def _moe(tile_expert, n_valid, tile_rows, next_expert, xs_parts, w_gu, b_gu, w_down, b_down):
    p_max = xs_parts[0].shape[0]
    n_tiles = p_max // MOE_TILE
    d_e = w_down.shape[1]

    def row(i, te, nv, tr, nx):
        return (jnp.minimum(i, nv[0] - 1), 0)

    def expert(i, te, nv, tr, nx):
        return (te[jnp.minimum(i, nv[0] - 1)], 0, 0)

    grid_spec = pltpu.PrefetchScalarGridSpec(
        num_scalar_prefetch=4,
        grid=(n_tiles,),
        in_specs=[pl.BlockSpec((MOE_TILE, D_PART), row)] * N_PART + [
            pl.BlockSpec((1, 1, 2 * d_e), expert),
            pl.BlockSpec((1, 1, D_MODEL), expert),
            pl.BlockSpec(memory_space=pl.ANY),
            pl.BlockSpec(memory_space=pl.ANY),
        ],
        out_specs=[pl.BlockSpec((MOE_TILE, D_PART), lambda i, te, nv, tr, nx: (i, 0))] * N_PART,
        scratch_shapes=[pltpu.VMEM((D_MODEL, 2 * d_e), F32), pltpu.VMEM((d_e, D_MODEL), F32),
                        pltpu.VMEM((D_MODEL, 2 * d_e), BF16), pltpu.VMEM((d_e, D_MODEL), BF16),
                        pltpu.SemaphoreType.DMA((2,))],
    )
    return pl.pallas_call(
        _moe_kernel,
        grid_spec=grid_spec,
        out_shape=[jax.ShapeDtypeStruct((p_max, D_PART), I32)] * N_PART,
        compiler_params=pltpu.CompilerParams(
            dimension_semantics=("arbitrary",), vmem_limit_bytes=VMEM_LIMIT),
        name="moe_ffn",
    )(tile_expert, n_valid, tile_rows, next_expert, *xs_parts, b_gu, b_down, w_gu, w_down)


def _combine_kernel(x1_ref, gate_ref, g_ref, *refs):
    yk_refs, o_ref = refs[:N_PART], refs[-1]
    acc = x1_ref[...]
    for k in range(TOP_K):
        y_k = _unpack_row(jnp.concatenate([r[k] for r in yk_refs], axis=1))
        acc = acc + gate_ref[:, k:k + 1] * y_k
    o_ref[...] = _rms(acc, g_ref[...])


def _combine(x1, gates, g, yk_parts):
    t, d = x1.shape
    tq = COMBINE_TILE
    row = lambda i: (i, 0)
    return pl.pallas_call(
        _combine_kernel,
        grid=(t // tq,),
        in_specs=[
            pl.BlockSpec((tq, d), row),
            pl.BlockSpec((tq, TOP_K), row),
            pl.BlockSpec((1, d), lambda i: (0, 0)),
        ] + [pl.BlockSpec((TOP_K, tq, D_PART), lambda i: (0, i, 0))] * N_PART,
        out_specs=pl.BlockSpec((tq, d), row),
        out_shape=jax.ShapeDtypeStruct((t, d), F32),
        compiler_params=pltpu.CompilerParams(
            dimension_semantics=("arbitrary",), vmem_limit_bytes=VMEM_LIMIT),
        name="combine_final",
    )(x1, gates, g, *yk_parts)


def kernel(x, g_mix, w_in, w_pool, pool_scale, w_out, g_ffn, w_router, b_router,
           w_gu, b_gu, w_down, b_down, g_final):
    batch, seq, d = x.shape
    t = batch * seq
    x2 = x.reshape(t, d)

    y_pool, qkv = _inproj(x2, g_mix[0].reshape(1, d), w_in[0], w_pool[0],
                          pool_scale[0].reshape(1, D_POOL), seq)
    y_attn = _attention(qkv, batch, seq)

    wr_t = w_router[0].T
    br_t = jnp.broadcast_to(b_router[0].reshape(N_EXPERTS, 1), (N_EXPERTS, LANES))
    x1, h2, meta, cnt = _outproj(x2, y_pool, y_attn, w_out[0], g_ffn[0].reshape(1, d), wr_t, br_t)

    eidx = meta[META_EIDX:META_EIDX + TOP_K].astype(I32)
    rank = meta[META_RANK:META_RANK + TOP_K].astype(I32)
    gates = meta[META_GATE:META_GATE + TOP_K].T
    counts = cnt[:, 0].astype(I32)
    padded = ((counts + MOE_TILE - 1) // MOE_TILE) * MOE_TILE
    ends = jnp.cumsum(padded)
    offsets = ends - padded
    e_ids = jnp.arange(N_EXPERTS, dtype=I32)
    hit = eidx[None] == e_ids[:, None, None]
    pos = rank + jnp.sum(jnp.where(hit, offsets[:, None, None], 0), axis=0)
    p_max = t * TOP_K + N_EXPERTS * MOE_TILE
    n_tiles = p_max // MOE_TILE
    tile_start = jnp.arange(n_tiles, dtype=I32) * MOE_TILE
    tile_expert = jnp.minimum(
        jnp.sum((tile_start[:, None] >= ends[None, :]).astype(I32), axis=1), N_EXPERTS - 1)
    mine = tile_expert[:, None] == e_ids[None, :]
    group_end = jnp.sum(jnp.where(mine, (offsets + counts)[None, :], 0), axis=1)
    tile_rows = jnp.clip(group_end - tile_start, 0, MOE_TILE)
    n_valid = (ends[-1] // MOE_TILE).reshape(1).astype(I32)
    later = (e_ids[None, :] > e_ids[:, None]) & (counts[None, :] > 0)
    next_nonempty = jnp.min(jnp.where(later, e_ids[None, :], N_EXPERTS), axis=1)
    next_nonempty = jnp.where(next_nonempty < N_EXPERTS, next_nonempty, -1)
    next_expert = jnp.sum(jnp.where(mine, next_nonempty[None, :], 0), axis=1)

    xs = _sc_dispatch(h2, [pos[k:k + 1] for k in range(TOP_K)], p_max)
    ys = _moe(tile_expert, n_valid, tile_rows, next_expert, xs, w_gu[0],
              b_gu[0].reshape(N_EXPERTS, 1, -1), w_down[0], b_down[0].reshape(N_EXPERTS, 1, -1))
    yk = _sc_unpermute(ys, pos.reshape(1, TOP_K * t))
    yk = [q.reshape(TOP_K, t, D_PART) for q in yk]
    out = _combine(x1, gates, g_final.reshape(1, d), yk)
    return out.reshape(batch, seq, d)
```

```python
import functools

import jax
import jax.numpy as jnp
from jax import lax
from jax.experimental import pallas as pl
from jax.experimental.pallas import tpu as pltpu
from jax.experimental.pallas import tpu_sc as plsc

F32 = jnp.float32
BF16 = jnp.bfloat16
I32 = jnp.int32

D_MODEL = 1024
D_POOL = 512
D_ATTN = 512
POOL_WINDOWS = (2, 4, 8, 16)
POOL_GROUP = 128
HEAD_DIM = 64
N_EXPERTS = 32
TOP_K = 4
SWIGLU_LIMIT = 7.0
SWIGLU_ALPHA = 1.702
EPS = 1e-5
NEG_INF = -1e30
LOG2_E = 1.4426950408889634

LANES = 128
QB = 128
DIL_MID = 4
DIL_FAR = 16
ROW_TILE = 1024
MOE_TILE = 1024
MOE_PATHS = (128, 256, 384, 512, 640, 768, 896, 1024)
COMBINE_TILE = 1024
SC_WINDOW = 128
D_PACKED = D_MODEL // 2
N_PART = 2
D_PART = D_PACKED // N_PART
VMEM_LIMIT = 56 * 1024 * 1024


def _pack_row(x):
    hi = lax.bitcast_convert_type(x[:, :D_PACKED].astype(BF16).astype(F32), I32)
    lo = lax.bitcast_convert_type(x[:, D_PACKED:].astype(BF16).astype(F32), I32)
    return hi | lax.shift_right_logical(lo, 16)


def _unpack_row(w):
    hi = lax.bitcast_convert_type(w & jnp.int32(-65536), F32)
    lo = lax.bitcast_convert_type(lax.shift_left(w, 16), F32)
    return jnp.concatenate([hi, lo], axis=1)


def _rms(x, g):
    ms = jnp.mean(x * x, axis=-1, keepdims=True)
    return x * lax.rsqrt(ms + EPS) * g


POOL_PAD = 16


def _inproj_kernel(x_ref, g_ref, w_ref, wp_ref, sc_ref, yp_ref, qkv_ref, w_bf, hist, pad_ref,
                   *, tiles_per_seq):
    i = pl.program_id(0)
    tm = x_ref.shape[0]

    @pl.when(i == 0)
    def _():
        w_bf[...] = w_ref[...].astype(BF16)
        pad_ref[0:POOL_PAD, :] = jnp.zeros((POOL_PAD, POOL_GROUP), F32)

    @pl.when(i % tiles_per_seq == 0)
    def _():
        hist[...] = jnp.zeros_like(hist)

    h = _rms(x_ref[...], g_ref[...]).astype(BF16)
    proj = jnp.dot(h, w_bf[...], preferred_element_type=F32)
    q = proj[:, D_POOL:D_POOL + D_ATTN] * (HEAD_DIM ** -0.5 * LOG2_E)
    qkv_ref[:, :D_ATTN] = q.astype(BF16)
    qkv_ref[:, D_ATTN:] = proj[:, D_POOL + D_ATTN:].astype(BF16)

    pos = (i % tiles_per_seq) * tm + lax.broadcasted_iota(I32, (tm, 1), 0)
    for g, w in enumerate(POOL_WINDOWS):
        lo, hi = g * POOL_GROUP, (g + 1) * POOL_GROUP
        e = proj[:, lo:hi]
        acc = jnp.concatenate([hist[g], e], axis=0)
        hist[g] = e[tm - POOL_PAD:, :]
        span = 1
        while span < w:
            pad_ref[POOL_PAD:, :] = acc
            acc = acc + pad_ref[pl.ds(POOL_PAD - span, POOL_PAD + tm), :]
            span *= 2
        count = jnp.minimum(pos + 1, w).astype(F32)
        pooled = acc[POOL_PAD:, :] / count - e
        y = jnp.dot(pooled.astype(BF16), wp_ref[g].astype(BF16), preferred_element_type=F32)
        yp_ref[:, lo:hi] = (y * sc_ref[:, lo:hi]).astype(BF16)


def _inproj(x2, g, w, w_pool, pool_scale, seq):
    t = x2.shape[0]
    d_in = w.shape[1]
    n_grp = len(POOL_WINDOWS)
    return pl.pallas_call(
        functools.partial(_inproj_kernel, tiles_per_seq=seq // ROW_TILE),
        grid=(t // ROW_TILE,),
        in_specs=[
            pl.BlockSpec((ROW_TILE, D_MODEL), lambda i: (i, 0)),
            pl.BlockSpec((1, D_MODEL), lambda i: (0, 0)),
            pl.BlockSpec((D_MODEL, d_in), lambda i: (0, 0)),
            pl.BlockSpec((n_grp, POOL_GROUP, POOL_GROUP), lambda i: (0, 0, 0)),
            pl.BlockSpec((1, D_POOL), lambda i: (0, 0)),
        ],
        out_specs=[
            pl.BlockSpec((ROW_TILE, D_POOL), lambda i: (i, 0)),
            pl.BlockSpec((ROW_TILE, 3 * D_ATTN), lambda i: (i, 0)),
        ],
        out_shape=[
            jax.ShapeDtypeStruct((t, D_POOL), BF16),
            jax.ShapeDtypeStruct((t, 3 * D_ATTN), BF16),
        ],
        scratch_shapes=[pltpu.VMEM((D_MODEL, d_in), BF16),
                        pltpu.VMEM((n_grp, POOL_PAD, POOL_GROUP), F32),
                        pltpu.VMEM((2 * POOL_PAD + ROW_TILE, POOL_GROUP), F32)],
        compiler_params=pltpu.CompilerParams(
            dimension_semantics=("arbitrary",), vmem_limit_bytes=VMEM_LIMIT),
        name="inproj_pool",
    )(x2, g, w, w_pool, pool_scale)


def _pair_block(q, k, v_ext, mask2, lane_h0):
    zero = jnp.zeros_like(q)
    q2 = jnp.concatenate([jnp.where(lane_h0, q, zero), jnp.where(lane_h0, zero, q)], axis=0)
    s = lax.dot_general(q2, k, (((1,), (1,)), ((), ())), preferred_element_type=F32)
    s = jnp.where(mask2, s, NEG_INF)
    m = jnp.max(s, axis=-1, keepdims=True)
    p = jnp.exp2(s - m).astype(BF16)
    ol = jnp.dot(p, v_ext, preferred_element_type=F32)
    o = jnp.where(lane_h0, ol[:QB, :LANES], ol[QB:, :LANES])
    l = jnp.where(lane_h0, ol[:QB, LANES:], ol[QB:, LANES:])
    mb = jnp.where(lane_h0, m[:QB], m[QB:])
    return o, mb, l


def _merge(o_a, m_a, l_a, o_b, m_b, l_b):
    m = jnp.maximum(m_a, m_b)
    ea = jnp.exp2(m_a - m)
    eb = jnp.exp2(m_b - m)
    return o_a * ea + o_b * eb, m, l_a * ea + l_b * eb


def _both_heads(mask):
    return jnp.concatenate([mask, mask], axis=0)


def _attn_kernel(q_ref, k_ref, v_ref, o_ref, qp, kp, vpx, vnx, o23, m23, l23, o2, m2, l2,
                 on, mn):
    s = q_ref.shape[0]
    n_chunk = DIL_FAR
    rows = s // n_chunk
    sub = rows // DIL_MID
    grp = n_chunk * n_chunk
    lane_h0 = lax.broadcasted_iota(I32, (1, LANES), 1) < HEAD_DIM

    @pl.when((pl.program_id(0) == 0) & (pl.program_id(1) == 0))
    def _():
        ones = jnp.ones((s, LANES), BF16)
        vnx[:, LANES:] = ones
        vpx[:, LANES:] = ones

    vnx[:, :LANES] = v_ref[...]

    pi = lax.broadcasted_iota(I32, (grp, grp), 0)
    pj = lax.broadcasted_iota(I32, (grp, grp), 1)
    swap = (pj == (pi % n_chunk) * n_chunk + pi // n_chunk).astype(BF16)
    for g in range(s // grp):
        src = slice(g * grp, (g + 1) * grp)
        qk = jnp.concatenate([q_ref[src, :], k_ref[src, :]], axis=1)
        yqk = jnp.dot(swap, qk, preferred_element_type=F32).astype(BF16)
        yv = jnp.dot(swap, v_ref[src, :], preferred_element_type=F32).astype(BF16)
        for r in range(n_chunk):
            dst = slice(r * rows + g * n_chunk, r * rows + (g + 1) * n_chunk)
            part = slice(r * n_chunk, (r + 1) * n_chunk)
            qp[dst, :] = yqk[part, :LANES]
            kp[dst, :] = yqk[part, LANES:]
            vpx[dst, 0:LANES] = yv[part]

    qi = lax.broadcasted_iota(I32, (QB, QB), 0)
    kj = lax.broadcasted_iota(I32, (QB, QB), 1)
    causal = _both_heads(qi >= kj)

    for r in range(n_chunk):
        blk = slice(r * rows, (r + 1) * rows)
        o, m, l = _pair_block(qp[blk, :], kp[blk, :], vpx[blk, :], causal, lane_h0)
        o23[blk, :] = o
        m23[blk, :] = m
        l23[blk, :] = l

    def mid_index(n_key_sub):
        i_q = lax.broadcasted_iota(I32, (QB, DIL_MID * n_key_sub), 0)
        i_k = lax.broadcasted_iota(I32, (QB, DIL_MID * n_key_sub), 1)
        return i_q // sub, i_q % sub, i_k // n_key_sub, i_k % n_key_sub

    jq, aq, jk, ak = mid_index(sub)
    d0 = DIL_FAR * (aq - ak) + DIL_MID * (jq - jk)
    mask_mid0 = _both_heads(d0 >= 0)
    jq, aq, jk, ak = mid_index(2 * sub)
    d1 = DIL_FAR * (aq + sub - ak) + DIL_MID * (jq - jk)
    mask_mid = _both_heads((d1 >= 0) & (d1 <= DIL_MID * QB))

    def mid_tile(r4, a0, k0, nk, mask):
        def at(j, off, n):
            return pl.ds((DIL_MID * j + r4) * rows + off, n)

        q = jnp.concatenate([qp[at(j, a0, sub), :] for j in range(DIL_MID)], axis=0)
        k = jnp.concatenate([kp[at(j, k0, nk), :] for j in range(DIL_MID)], axis=0)
        v = jnp.concatenate([vpx[at(j, k0, nk), :] for j in range(DIL_MID)], axis=0)
        o, m, l = _pair_block(q, k, v, mask, lane_h0)
        for j in range(DIL_MID):
            dst = at(j, a0, sub)
            src = slice(j * sub, (j + 1) * sub)
            o2[dst, :] = o[src]
            m2[dst, :] = m[src]
            l2[dst, :] = l[src]

    for r4 in range(DIL_MID):
        mid_tile(r4, 0, 0, sub, mask_mid0)
        for a_blk in range(1, rows // sub):
            mid_tile(r4, a_blk * sub, (a_blk - 1) * sub, 2 * sub, mask_mid)

    for r in range(n_chunk):
        blk = slice(r * rows, (r + 1) * rows)
        oo, mm, ll = _merge(o23[blk, :], m23[blk, :], l23[blk, :],
                            o2[blk, :], m2[blk, :], l2[blk, :])
        o23[blk, :] = oo / ll
        m23[blk, :] = mm + jnp.log2(ll)

    for g in range(s // grp):
        slabs = [o23[r * rows + g * n_chunk:r * rows + (g + 1) * n_chunk, :].astype(BF16)
                 for r in range(n_chunk)]
        on[g * grp:(g + 1) * grp, :] = jnp.dot(swap, jnp.concatenate(slabs, axis=0),
                                               preferred_element_type=F32)
    for r in range(n_chunk):
        mn[pl.ds(r, rows, stride=n_chunk), :] = m23[r * rows:(r + 1) * rows, :]

    qi2 = lax.broadcasted_iota(I32, (QB, 2 * QB), 0)
    kj2 = lax.broadcasted_iota(I32, (QB, 2 * QB), 1)
    dn = qi2 + QB - kj2
    mask_near = _both_heads((dn >= 0) & (dn <= QB))

    def near_finish(dst, o, m, l):
        oo, _, ll = _merge(on[dst, :], mn[dst, :], 1.0, o, m, l)
        o_ref[dst, :] = (oo / ll).astype(o_ref.dtype)

    first = pl.ds(0, QB)
    o, m, l = _pair_block(q_ref[first, :], k_ref[first, :], vnx[first, :], causal, lane_h0)
    near_finish(first, o, m, l)
    for n in range(1, s // QB):
        keys = pl.ds((n - 1) * QB, 2 * QB)
        o, m, l = _pair_block(q_ref[pl.ds(n * QB, QB), :], k_ref[keys, :], vnx[keys, :],
                              mask_near, lane_h0)
        near_finish(pl.ds(n * QB, QB), o, m, l)


def _attention(qkv, batch, seq):
    n_pair = D_ATTN // LANES
    blk = (seq, LANES)
    f32_scr = pltpu.VMEM(blk, F32)
    bf_scr = pltpu.VMEM(blk, BF16)
    bfx_scr = pltpu.VMEM((seq, 2 * LANES), BF16)
    return pl.pallas_call(
        _attn_kernel,
        grid=(batch, n_pair),
        in_specs=[
            pl.BlockSpec(blk, lambda b, h: (b, h)),
            pl.BlockSpec(blk, lambda b, h: (b, n_pair + h)),
            pl.BlockSpec(blk, lambda b, h: (b, 2 * n_pair + h)),
        ],
        out_specs=pl.BlockSpec(blk, lambda b, h: (b, h)),
        out_shape=jax.ShapeDtypeStruct((batch * seq, D_ATTN), BF16),
        scratch_shapes=[bf_scr, bf_scr, bfx_scr, bfx_scr] + [f32_scr] * 8,
        compiler_params=pltpu.CompilerParams(
            dimension_semantics=("arbitrary", "arbitrary"), vmem_limit_bytes=VMEM_LIMIT),
        name="dilated_attn",
    )(qkv, qkv, qkv)


META_ROWS = 16
META_EIDX, META_GATE, META_RANK = 0, 4, 8


def _outproj_kernel(x_ref, yp_ref, ya_ref, wo_ref, g_ref, wrt_ref, brt_ref,
                    x1_ref, h2_ref, meta_ref, cnt_ref, wo_bf, before, carry):
    tm = x_ref.shape[0]

    @pl.when(pl.program_id(0) == 0)
    def _():
        wo_bf[...] = wo_ref[...].astype(BF16)
        carry[...] = jnp.zeros_like(carry)
        ti = lax.broadcasted_iota(I32, (tm, tm), 0)
        tj = lax.broadcasted_iota(I32, (tm, tm), 1)
        before[...] = (ti < tj).astype(BF16)

    x1 = (x_ref[...]
          + jnp.dot(yp_ref[...], wo_bf[:D_POOL, :], preferred_element_type=F32)
          + jnp.dot(ya_ref[...], wo_bf[D_POOL:, :], preferred_element_type=F32))
    x1_ref[...] = x1
    h2 = _rms(x1, g_ref[...])
    h2_ref[...] = _pack_row(h2)

    logits_t = lax.dot_general(wrt_ref[...].astype(BF16), h2.astype(BF16),
                               (((1,), (1,)), ((), ())), preferred_element_type=F32)
    logits_t = logits_t + brt_ref[:, 0:1]
    eid = lax.broadcasted_iota(I32, (N_EXPERTS, tm), 0)
    work = logits_t
    idxs, vals = [], []
    for _ in range(TOP_K):
        mx = jnp.max(work, axis=0, keepdims=True)
        idx = jnp.min(jnp.where(work == mx, eid, N_EXPERTS), axis=0, keepdims=True)
        idxs.append(idx)
        vals.append(mx)
        work = jnp.where(eid == idx, -jnp.inf, work)
    exps = [jnp.exp(v - vals[0]) for v in vals]
    den = exps[0] + exps[1] + exps[2] + exps[3]

    onehot = jnp.zeros((N_EXPERTS, tm), F32)
    for idx in idxs:
        onehot = onehot + (eid == idx).astype(F32)
    rank_e = carry[:, 0:1] + jnp.dot(onehot.astype(BF16), before[...],
                                     preferred_element_type=F32)
    carry[...] = carry[...] + jnp.sum(onehot, axis=1, keepdims=True)
    cnt_ref[...] = carry[...]

    mrow = lax.broadcasted_iota(I32, (META_ROWS, tm), 0)
    meta = jnp.zeros((META_ROWS, tm), F32)
    for k in range(TOP_K):
        rank_k = jnp.sum(jnp.where(eid == idxs[k], rank_e, 0.0), axis=0, keepdims=True)
        meta = jnp.where(mrow == META_EIDX + k, idxs[k].astype(F32), meta)
        meta = jnp.where(mrow == META_GATE + k, exps[k] / den, meta)
        meta = jnp.where(mrow == META_RANK + k, rank_k, meta)
    meta_ref[...] = meta


def _outproj(x2, y_pool, y_attn, w_out, g, w_router_t, b_router_t):
    t = x2.shape[0]
    row = lambda i: (i, 0)
    const = lambda i: (0, 0)
    return pl.pallas_call(
        _outproj_kernel,
        grid=(t // ROW_TILE,),
        in_specs=[
            pl.BlockSpec((ROW_TILE, D_MODEL), row),
            pl.BlockSpec((ROW_TILE, D_POOL), row),
            pl.BlockSpec((ROW_TILE, D_ATTN), row),
            pl.BlockSpec((D_MODEL, D_MODEL), const),
            pl.BlockSpec((1, D_MODEL), const),
            pl.BlockSpec((N_EXPERTS, D_MODEL), const),
            pl.BlockSpec((N_EXPERTS, LANES), const),
        ],
        out_specs=[
            pl.BlockSpec((ROW_TILE, D_MODEL), row),
            pl.BlockSpec((ROW_TILE, D_PACKED), row),
            pl.BlockSpec((META_ROWS, ROW_TILE), lambda i: (0, i)),
            pl.BlockSpec((N_EXPERTS, LANES), const),
        ],
        out_shape=[
            jax.ShapeDtypeStruct((t, D_MODEL), F32),
            jax.ShapeDtypeStruct((t, D_PACKED), I32),
            jax.ShapeDtypeStruct((META_ROWS, t), F32),
            jax.ShapeDtypeStruct((N_EXPERTS, LANES), F32),
        ],
        scratch_shapes=[pltpu.VMEM((D_MODEL, D_MODEL), BF16),
                        pltpu.VMEM((ROW_TILE, ROW_TILE), BF16),
                        pltpu.VMEM((N_EXPERTS, LANES), F32)],
        compiler_params=pltpu.CompilerParams(
            dimension_semantics=("arbitrary",), vmem_limit_bytes=VMEM_LIMIT),
        name="outproj_router",
    )(x2, y_pool, y_attn, w_out, g, w_router_t, b_router_t)


def _sc_mesh():
    return plsc.VectorSubcoreMesh(core_axis_name="core", subcore_axis_name="subcore")


def _sc_dispatch(h2, pos_rows, p_max):
    t = h2.shape[0]

    @functools.partial(
        pl.kernel, mesh=_sc_mesh(),
        out_type=[jax.ShapeDtypeStruct((p_max, D_PART), h2.dtype)] * N_PART)
    def run(h_hbm, *refs):
        pos_hbm, xs_hbm = refs[:TOP_K], refs[TOP_K:]
        for c in range(N_PART):
            def body(x_vmem, *idx_vmem, dst=xs_hbm[c]):
                for iv in idx_vmem:
                    pltpu.sync_copy(x_vmem, dst.at[iv.at[0]])

            pltpu.emit_pipeline(
                body,
                grid=(t // SC_WINDOW,),
                in_specs=[pl.BlockSpec((SC_WINDOW, D_PART), lambda i, c=c: (i, c))]
                + [pl.BlockSpec((1, SC_WINDOW), lambda i: (0, i))] * TOP_K,
                out_specs=[],
                core_axis_name=("core", "subcore"),
                dimension_semantics=(pltpu.PARALLEL,),
            )(h_hbm, *pos_hbm)

    return run(h2, *pos_rows)


def _sc_unpermute(y_parts, idx_row):
    n = idx_row.shape[1]

    @functools.partial(
        pl.kernel, mesh=_sc_mesh(),
        out_type=[jax.ShapeDtypeStruct((n, D_PART), y_parts[0].dtype)] * N_PART)
    def run(*refs):
        y_hbm, i_hbm, o_hbm = refs[:N_PART], refs[N_PART], refs[N_PART + 1:]
        for c in range(N_PART):
            def body(i_vmem, o_vmem, src=y_hbm[c]):
                pltpu.sync_copy(src.at[i_vmem.at[0]], o_vmem)

            pltpu.emit_pipeline(
                body,
                grid=(n // SC_WINDOW,),
                in_specs=[pl.BlockSpec((1, SC_WINDOW), lambda i: (0, i))],
                out_specs=[pl.BlockSpec((SC_WINDOW, D_PART), lambda i: (i, 0))],
                core_axis_name=("core", "subcore"),
                dimension_semantics=(pltpu.PARALLEL,),
            )(i_hbm, o_hbm[c])

    return run(*y_parts, idx_row)


def _moe_kernel(te_ref, nv_ref, rows_ref, nxt_ref, *refs):
    xs_refs = refs[:N_PART]
    bgu_ref, bd_ref, wgu_hbm, wd_hbm = refs[N_PART:N_PART + 4]
    y_refs = refs[N_PART + 4:2 * N_PART + 4]
    wgu_f32, wd_f32, wgu_bf, wd_bf, sem = refs[2 * N_PART + 4:]
    i = pl.program_id(0)

    def weight_copies(e):
        return (pltpu.make_async_copy(wgu_hbm.at[e], wgu_f32, sem.at[0]),
                pltpu.make_async_copy(wd_hbm.at[e], wd_f32, sem.at[1]))

    @pl.when(i < nv_ref[0])
    def _():
        prev = te_ref[jnp.maximum(i - 1, 0)]
        new_expert = jnp.logical_or(i == 0, te_ref[i] != prev)

        @pl.when(i == 0)
        def _():
            for cp in weight_copies(te_ref[0]):
                cp.start()

        @pl.when(new_expert)
        def _():
            for cp in weight_copies(te_ref[i]):
                cp.wait()
            wgu_bf[...] = wgu_f32[...].astype(BF16)
            wd_bf[...] = wd_f32[...].astype(BF16)

            @pl.when(nxt_ref[i] >= 0)
            def _():
                for cp in weight_copies(nxt_ref[i]):
                    cp.start()

        d_e = wd_bf.shape[0]

        def ffn(n_rows):
            x = _unpack_row(jnp.concatenate([r[0:n_rows, :] for r in xs_refs], axis=1))
            rid = lax.broadcasted_iota(I32, (n_rows, 1), 0)
            x = jnp.where(rid < rows_ref[i], x, 0.0).astype(BF16)
            gu = jnp.dot(x, wgu_bf[...], preferred_element_type=F32) + bgu_ref[0]
            gate = jnp.minimum(gu[:, :d_e], SWIGLU_LIMIT)
            lin = jnp.clip(gu[:, d_e:], -SWIGLU_LIMIT, SWIGLU_LIMIT)
            act = gate * jax.nn.sigmoid(SWIGLU_ALPHA * gate) * (lin + 1.0)
            y = jnp.dot(act.astype(BF16), wd_bf[...], preferred_element_type=F32) + bd_ref[0]
            yp = _pack_row(y)
            for c, y_ref in enumerate(y_refs):
                y_ref[0:n_rows, :] = yp[:, c * D_PART:(c + 1) * D_PART]
                if n_rows < MOE_TILE:
                    y_ref[n_rows:, :] = jnp.zeros((MOE_TILE - n_rows, D_PART), I32)

        lower = 0
        for n_rows in MOE_PATHS:
            @pl.when((rows_ref[i] > lower) & (rows_ref[i] <= n_rows))
            def _(n_rows=n_rows):
                ffn(n_rows)
            lower = n_rows

    @pl.when(i >= nv_ref[0])
    def _():
        for y_ref in y_refs:
            y_ref[...] = jnp.zeros_like(y_ref)


def _moe(tile_expert, n_valid, tile_rows, next_expert, xs_parts, w_gu, b_gu, w_down, b_down):
    p_max = xs_parts[0].shape[0]
    n_tiles = p_max // MOE_TILE
    d_e = w_down.shape[1]

    def row(i, te, nv, tr, nx):
        return (jnp.minimum(i, nv[0] - 1), 0)

    def expert(i, te, nv, tr, nx):
        return (te[jnp.minimum(i, nv[0] - 1)], 0, 0)

    grid_spec = pltpu.PrefetchScalarGridSpec(
        num_scalar_prefetch=4,
        grid=(n_tiles,),
        in_specs=[pl.BlockSpec((MOE_TILE, D_PART), row)] * N_PART + [
            pl.BlockSpec((1, 1, 2 * d_e), expert),
            pl.BlockSpec((1, 1, D_MODEL), expert),
            pl.BlockSpec(memory_space=pl.ANY),
            pl.BlockSpec(memory_space=pl.ANY),
        ],
        out_specs=[pl.BlockSpec((MOE_TILE, D_PART), lambda i, te, nv, tr, nx: (i, 0))] * N_PART,
        scratch_shapes=[pltpu.VMEM((D_MODEL, 2 * d_e), F32), pltpu.VMEM((d_e, D_MODEL), F32),
                        pltpu.VMEM((D_MODEL, 2 * d_e), BF16), pltpu.VMEM((d_e, D_MODEL), BF16),
                        pltpu.SemaphoreType.DMA((2,))],
    )
    return pl.pallas_call(
        _moe_kernel,
        grid_spec=grid_spec,
        out_shape=[jax.ShapeDtypeStruct((p_max, D_PART), I32)] * N_PART,
        compiler_params=pltpu.CompilerParams(
            dimension_semantics=("arbitrary",), vmem_limit_bytes=VMEM_LIMIT),
        name="moe_ffn",
    )(tile_expert, n_valid, tile_rows, next_expert, *xs_parts, b_gu, b_down, w_gu, w_down)


def _combine_kernel(x1_ref, gate_ref, g_ref, *refs):
    yk_refs, o_ref = refs[:N_PART], refs[-1]
    acc = x1_ref[...]
    for k in range(TOP_K):
        y_k = _unpack_row(jnp.concatenate([r[k] for r in yk_refs], axis=1))
        acc = acc + gate_ref[:, k:k + 1] * y_k
    o_ref[...] = _rms(acc, g_ref[...])


def _combine(x1, gates, g, yk_parts):
    t, d = x1.shape
    tq = COMBINE_TILE
    row = lambda i: (i, 0)
    return pl.pallas_call(
        _combine_kernel,
        grid=(t // tq,),
        in_specs=[
            pl.BlockSpec((tq, d), row),
            pl.BlockSpec((tq, TOP_K), row),
            pl.BlockSpec((1, d), lambda i: (0, 0)),
        ] + [pl.BlockSpec((TOP_K, tq, D_PART), lambda i: (0, i, 0))] * N_PART,
        out_specs=pl.BlockSpec((tq, d), row),
        out_shape=jax.ShapeDtypeStruct((t, d), F32),
        compiler_params=pltpu.CompilerParams(
            dimension_semantics=("arbitrary",), vmem_limit_bytes=VMEM_LIMIT),
        name="combine_final",
    )(x1, gates, g, *yk_parts)


def kernel(x, g_mix, w_in, w_pool, pool_scale, w_out, g_ffn, w_router, b_router,
           w_gu, b_gu, w_down, b_down, g_final):
    batch, seq, d = x.shape
    t = batch * seq
    x2 = x.reshape(t, d)

    y_pool, qkv = _inproj(x2, g_mix[0].reshape(1, d), w_in[0], w_pool[0],
                          pool_scale[0].reshape(1, D_POOL), seq)
    y_attn = _attention(qkv, batch, seq)

    wr_t = w_router[0].T
    br_t = jnp.broadcast_to(b_router[0].reshape(N_EXPERTS, 1), (N_EXPERTS, LANES))
    x1, h2, meta, cnt = _outproj(x2, y_pool, y_attn, w_out[0], g_ffn[0].reshape(1, d), wr_t, br_t)

    eidx = meta[META_EIDX:META_EIDX + TOP_K].astype(I32)
    rank = meta[META_RANK:META_RANK + TOP_K].astype(I32)
    gates = meta[META_GATE:META_GATE + TOP_K].T
    counts = cnt[:, 0].astype(I32)
    padded = ((counts + MOE_TILE - 1) // MOE_TILE) * MOE_TILE
    ends = jnp.cumsum(padded)
    offsets = ends - padded
    e_ids = jnp.arange(N_EXPERTS, dtype=I32)
    hit = eidx[None] == e_ids[:, None, None]
    pos = rank + jnp.sum(jnp.where(hit, offsets[:, None, None], 0), axis=0)
    p_max = t * TOP_K + N_EXPERTS * MOE_TILE
    n_tiles = p_max // MOE_TILE
    tile_start = jnp.arange(n_tiles, dtype=I32) * MOE_TILE
    tile_expert = jnp.minimum(
        jnp.sum((tile_start[:, None] >= ends[None, :]).astype(I32), axis=1), N_EXPERTS - 1)
    mine = tile_expert[:, None] == e_ids[None, :]
    group_end = jnp.sum(jnp.where(mine, (offsets + counts)[None, :], 0), axis=1)
    tile_rows = jnp.clip(group_end - tile_start, 0, MOE_TILE)
    n_valid = (ends[-1] // MOE_TILE).reshape(1).astype(I32)
    later = (e_ids[None, :] > e_ids[:, None]) & (counts[None, :] > 0)
    next_nonempty = jnp.min(jnp.where(later, e_ids[None, :], N_EXPERTS), axis=1)
    next_nonempty = jnp.where(next_nonempty < N_EXPERTS, next_nonempty, -1)
    next_expert = jnp.sum(jnp.where(mine, next_nonempty[None, :], 0), axis=1)

    xs = _sc_dispatch(h2, [pos[k:k + 1] for k in range(TOP_K)], p_max)
    ys = _moe(tile_expert, n_valid, tile_rows, next_expert, xs, w_gu[0],
              b_gu[0].reshape(N_EXPERTS, 1, -1), w_down[0], b_down[0].reshape(N_EXPERTS, 1, -1))
    yk = _sc_unpermute(ys, pos.reshape(1, TOP_K * t))
    yk = [q.reshape(TOP_K, t, D_PART) for q in yk]
    out = _combine(x1, gates, g_final.reshape(1, d), yk)
    return out.reshape(batch, seq, d)
```

```python
import functools

import jax
import jax.numpy as jnp
from jax import lax
from jax.experimental import pallas as pl
from jax.experimental.pallas import tpu as pltpu
from jax.experimental.pallas import tpu_sc as plsc

F32 = jnp.float32
BF16 = jnp.bfloat16
I32 = jnp.int32

D_MODEL = 1024
D_POOL = 512
D_ATTN = 512
POOL_WINDOWS = (2, 4, 8, 16)
POOL_GROUP = 128
HEAD_DIM = 64
N_EXPERTS = 32
TOP_K = 4
SWIGLU_LIMIT = 7.0
SWIGLU_ALPHA = 1.702
EPS = 1e-5
NEG_INF = -1e30
LOG2_E = 1.4426950408889634

LANES = 128
QB = 128
DIL_MID = 4
DIL_FAR = 16
ROW_TILE = 1024
MOE_TILE = 1024
MOE_PATHS = (256, 512, 768, 1024)
COMBINE_TILE = 1024
SC_WINDOW = 128
D_PACKED = D_MODEL // 2
N_PART = 2
D_PART = D_PACKED // N_PART
VMEM_LIMIT = 56 * 1024 * 1024


def _pack_row(x):
    hi = lax.bitcast_convert_type(x[:, :D_PACKED].astype(BF16).astype(F32), I32)
    lo = lax.bitcast_convert_type(x[:, D_PACKED:].astype(BF16).astype(F32), I32)
    return hi | lax.shift_right_logical(lo, 16)


def _unpack_row(w):
    hi = lax.bitcast_convert_type(w & jnp.int32(-65536), F32)
    lo = lax.bitcast_convert_type(lax.shift_left(w, 16), F32)
    return jnp.concatenate([hi, lo], axis=1)


def _rms(x, g):
    ms = jnp.mean(x * x, axis=-1, keepdims=True)
    return x * lax.rsqrt(ms + EPS) * g


POOL_PAD = 16


def _inproj_kernel(x_ref, g_ref, w_ref, wp_ref, sc_ref, yp_ref, qkv_ref, w_bf, hist, pad_ref,
                   *, tiles_per_seq):
    i = pl.program_id(0)
    tm = x_ref.shape[0]

    @pl.when(i == 0)
    def _():
        w_bf[...] = w_ref[...].astype(BF16)
        pad_ref[0:POOL_PAD, :] = jnp.zeros((POOL_PAD, POOL_GROUP), F32)

    @pl.when(i % tiles_per_seq == 0)
    def _():
        hist[...] = jnp.zeros_like(hist)

    h = _rms(x_ref[...], g_ref[...]).astype(BF16)
    proj = jnp.dot(h, w_bf[...], preferred_element_type=F32)
    q = proj[:, D_POOL:D_POOL + D_ATTN] * (HEAD_DIM ** -0.5 * LOG2_E)
    qkv_ref[:, :D_ATTN] = q.astype(BF16)
    qkv_ref[:, D_ATTN:] = proj[:, D_POOL + D_ATTN:].astype(BF16)

    pos = (i % tiles_per_seq) * tm + lax.broadcasted_iota(I32, (tm, 1), 0)
    for g, w in enumerate(POOL_WINDOWS):
        lo, hi = g * POOL_GROUP, (g + 1) * POOL_GROUP
        e = proj[:, lo:hi]
        acc = jnp.concatenate([hist[g], e], axis=0)
        hist[g] = e[tm - POOL_PAD:, :]
        span = 1
        while span < w:
            pad_ref[POOL_PAD:, :] = acc
            acc = acc + pad_ref[pl.ds(POOL_PAD - span, POOL_PAD + tm), :]
            span *= 2
        count = jnp.minimum(pos + 1, w).astype(F32)
        pooled = acc[POOL_PAD:, :] / count - e
        y = jnp.dot(pooled.astype(BF16), wp_ref[g].astype(BF16), preferred_element_type=F32)
        yp_ref[:, lo:hi] = (y * sc_ref[:, lo:hi]).astype(BF16)


def _inproj(x2, g, w, w_pool, pool_scale, seq):
    t = x2.shape[0]
    d_in = w.shape[1]
    n_grp = len(POOL_WINDOWS)
    return pl.pallas_call(
        functools.partial(_inproj_kernel, tiles_per_seq=seq // ROW_TILE),
        grid=(t // ROW_TILE,),
        in_specs=[
            pl.BlockSpec((ROW_TILE, D_MODEL), lambda i: (i, 0)),
            pl.BlockSpec((1, D_MODEL), lambda i: (0, 0)),
            pl.BlockSpec((D_MODEL, d_in), lambda i: (0, 0)),
            pl.BlockSpec((n_grp, POOL_GROUP, POOL_GROUP), lambda i: (0, 0, 0)),
            pl.BlockSpec((1, D_POOL), lambda i: (0, 0)),
        ],
        out_specs=[
            pl.BlockSpec((ROW_TILE, D_POOL), lambda i: (i, 0)),
            pl.BlockSpec((ROW_TILE, 3 * D_ATTN), lambda i: (i, 0)),
        ],
        out_shape=[
            jax.ShapeDtypeStruct((t, D_POOL), BF16),
            jax.ShapeDtypeStruct((t, 3 * D_ATTN), BF16),
        ],
        scratch_shapes=[pltpu.VMEM((D_MODEL, d_in), BF16),
                        pltpu.VMEM((n_grp, POOL_PAD, POOL_GROUP), F32),
                        pltpu.VMEM((2 * POOL_PAD + ROW_TILE, POOL_GROUP), F32)],
        compiler_params=pltpu.CompilerParams(
            dimension_semantics=("arbitrary",), vmem_limit_bytes=VMEM_LIMIT),
        name="inproj_pool",
    )(x2, g, w, w_pool, pool_scale)


def _pair_block(q, k, v_ext, mask2, lane_h0):
    zero = jnp.zeros_like(q)
    q2 = jnp.concatenate([jnp.where(lane_h0, q, zero), jnp.where(lane_h0, zero, q)], axis=0)
    s = lax.dot_general(q2, k, (((1,), (1,)), ((), ())), preferred_element_type=F32)
    s = jnp.where(mask2, s, NEG_INF)
    m = jnp.max(s, axis=-1, keepdims=True)
    p = jnp.exp2(s - m).astype(BF16)
    ol = jnp.dot(p, v_ext, preferred_element_type=F32)
    o = jnp.where(lane_h0, ol[:QB, :LANES], ol[QB:, :LANES])
    l = jnp.where(lane_h0, ol[:QB, LANES:], ol[QB:, LANES:])
    mb = jnp.where(lane_h0, m[:QB], m[QB:])
    return o, mb, l


def _merge(o_a, m_a, l_a, o_b, m_b, l_b):
    m = jnp.maximum(m_a, m_b)
    ea = jnp.exp2(m_a - m)
    eb = jnp.exp2(m_b - m)
    return o_a * ea + o_b * eb, m, l_a * ea + l_b * eb


def _both_heads(mask):
    return jnp.concatenate([mask, mask], axis=0)


def _attn_kernel(q_ref, k_ref, v_ref, o_ref, qp, kp, vpx, vnx, o23, m23, l23, o2, m2, l2,
                 on, mn):
    s = q_ref.shape[0]
    n_chunk = DIL_FAR
    rows = s // n_chunk
    sub = rows // DIL_MID
    grp = n_chunk * n_chunk
    lane_h0 = lax.broadcasted_iota(I32, (1, LANES), 1) < HEAD_DIM

    @pl.when((pl.program_id(0) == 0) & (pl.program_id(1) == 0))
    def _():
        ones = jnp.ones((s, LANES), BF16)
        vnx[:, LANES:] = ones
        vpx[:, LANES:] = ones

    vnx[:, :LANES] = v_ref[...]

    pi = lax.broadcasted_iota(I32, (grp, grp), 0)
    pj = lax.broadcasted_iota(I32, (grp, grp), 1)
    swap = (pj == (pi % n_chunk) * n_chunk + pi // n_chunk).astype(BF16)
    for g in range(s // grp):
        src = slice(g * grp, (g + 1) * grp)
        qk = jnp.concatenate([q_ref[src, :], k_ref[src, :]], axis=1)
        yqk = jnp.dot(swap, qk, preferred_element_type=F32).astype(BF16)
        yv = jnp.dot(swap, v_ref[src, :], preferred_element_type=F32).astype(BF16)
        for r in range(n_chunk):
            dst = slice(r * rows + g * n_chunk, r * rows + (g + 1) * n_chunk)
            part = slice(r * n_chunk, (r + 1) * n_chunk)
            qp[dst, :] = yqk[part, :LANES]
            kp[dst, :] = yqk[part, LANES:]
            vpx[dst, 0:LANES] = yv[part]

    qi = lax.broadcasted_iota(I32, (QB, QB), 0)
    kj = lax.broadcasted_iota(I32, (QB, QB), 1)
    causal = _both_heads(qi >= kj)

    for r in range(n_chunk):
        blk = slice(r * rows, (r + 1) * rows)
        o, m, l = _pair_block(qp[blk, :], kp[blk, :], vpx[blk, :], causal, lane_h0)
        o23[blk, :] = o
        m23[blk, :] = m
        l23[blk, :] = l

    def mid_index(n_key_sub):
        i_q = lax.broadcasted_iota(I32, (QB, DIL_MID * n_key_sub), 0)
        i_k = lax.broadcasted_iota(I32, (QB, DIL_MID * n_key_sub), 1)
        return i_q // sub, i_q % sub, i_k // n_key_sub, i_k % n_key_sub

    jq, aq, jk, ak = mid_index(sub)
    d0 = DIL_FAR * (aq - ak) + DIL_MID * (jq - jk)
    mask_mid0 = _both_heads(d0 >= 0)
    jq, aq, jk, ak = mid_index(2 * sub)
    d1 = DIL_FAR * (aq + sub - ak) + DIL_MID * (jq - jk)
    mask_mid = _both_heads((d1 >= 0) & (d1 <= DIL_MID * QB))

    def mid_tile(r4, a0, k0, nk, mask):
        def at(j, off, n):
            return pl.ds((DIL_MID * j + r4) * rows + off, n)

        q = jnp.concatenate([qp[at(j, a0, sub), :] for j in range(DIL_MID)], axis=0)
        k = jnp.concatenate([kp[at(j, k0, nk), :] for j in range(DIL_MID)], axis=0)
        v = jnp.concatenate([vpx[at(j, k0, nk), :] for j in range(DIL_MID)], axis=0)
        o, m, l = _pair_block(q, k, v, mask, lane_h0)
        for j in range(DIL_MID):
            dst = at(j, a0, sub)
            src = slice(j * sub, (j + 1) * sub)
            o2[dst, :] = o[src]
            m2[dst, :] = m[src]
            l2[dst, :] = l[src]

    for r4 in range(DIL_MID):
        mid_tile(r4, 0, 0, sub, mask_mid0)
        for a_blk in range(1, rows // sub):
            mid_tile(r4, a_blk * sub, (a_blk - 1) * sub, 2 * sub, mask_mid)

    for r in range(n_chunk):
        blk = slice(r * rows, (r + 1) * rows)
        oo, mm, ll = _merge(o23[blk, :], m23[blk, :], l23[blk, :],
                            o2[blk, :], m2[blk, :], l2[blk, :])
        o23[blk, :] = oo / ll
        m23[blk, :] = mm + jnp.log2(ll)

    for g in range(s // grp):
        slabs = [o23[r * rows + g * n_chunk:r * rows + (g + 1) * n_chunk, :].astype(BF16)
                 for r in range(n_chunk)]
        on[g * grp:(g + 1) * grp, :] = jnp.dot(swap, jnp.concatenate(slabs, axis=0),
                                               preferred_element_type=F32)
    for r in range(n_chunk):
        mn[pl.ds(r, rows, stride=n_chunk), :] = m23[r * rows:(r + 1) * rows, :]

    qi2 = lax.broadcasted_iota(I32, (QB, 2 * QB), 0)
    kj2 = lax.broadcasted_iota(I32, (QB, 2 * QB), 1)
    dn = qi2 + QB - kj2
    mask_near = _both_heads((dn >= 0) & (dn <= QB))

    def near_finish(dst, o, m, l):
        oo, _, ll = _merge(on[dst, :], mn[dst, :], 1.0, o, m, l)
        o_ref[dst, :] = (oo / ll).astype(o_ref.dtype)

    first = pl.ds(0, QB)
    o, m, l = _pair_block(q_ref[first, :], k_ref[first, :], vnx[first, :], causal, lane_h0)
    near_finish(first, o, m, l)
    for n in range(1, s // QB):
        keys = pl.ds((n - 1) * QB, 2 * QB)
        o, m, l = _pair_block(q_ref[pl.ds(n * QB, QB), :], k_ref[keys, :], vnx[keys, :],
                              mask_near, lane_h0)
        near_finish(pl.ds(n * QB, QB), o, m, l)


def _attention(qkv, batch, seq):
    n_pair = D_ATTN // LANES
    blk = (seq, LANES)
    f32_scr = pltpu.VMEM(blk, F32)
    bf_scr = pltpu.VMEM(blk, BF16)
    bfx_scr = pltpu.VMEM((seq, 2 * LANES), BF16)
    return pl.pallas_call(
        _attn_kernel,
        grid=(batch, n_pair),
        in_specs=[
            pl.BlockSpec(blk, lambda b, h: (b, h)),
            pl.BlockSpec(blk, lambda b, h: (b, n_pair + h)),
            pl.BlockSpec(blk, lambda b, h: (b, 2 * n_pair + h)),
        ],
        out_specs=pl.BlockSpec(blk, lambda b, h: (b, h)),
        out_shape=jax.ShapeDtypeStruct((batch * seq, D_ATTN), BF16),
        scratch_shapes=[bf_scr, bf_scr, bfx_scr, bfx_scr] + [f32_scr] * 8,
        compiler_params=pltpu.CompilerParams(
            dimension_semantics=("arbitrary", "arbitrary"), vmem_limit_bytes=VMEM_LIMIT),
        name="dilated_attn",
    )(qkv, qkv, qkv)


META_ROWS = 16
META_EIDX, META_GATE, META_RANK = 0, 4, 8


def _outproj_kernel(x_ref, yp_ref, ya_ref, wo_ref, g_ref, wrt_ref, brt_ref,
                    x1_ref, h2_ref, meta_ref, cnt_ref, wo_bf, before, carry):
    tm = x_ref.shape[0]

    @pl.when(pl.program_id(0) == 0)
    def _():
        wo_bf[...] = wo_ref[...].astype(BF16)
        carry[...] = jnp.zeros_like(carry)
        ti = lax.broadcasted_iota(I32, (tm, tm), 0)
        tj = lax.broadcasted_iota(I32, (tm, tm), 1)
        before[...] = (ti < tj).astype(BF16)

    x1 = (x_ref[...]
          + jnp.dot(yp_ref[...], wo_bf[:D_POOL, :], preferred_element_type=F32)
          + jnp.dot(ya_ref[...], wo_bf[D_POOL:, :], preferred_element_type=F32))
    x1_ref[...] = x1
    h2 = _rms(x1, g_ref[...])
    h2_ref[...] = _pack_row(h2)

    logits_t = lax.dot_general(wrt_ref[...].astype(BF16), h2.astype(BF16),
                               (((1,), (1,)), ((), ())), preferred_element_type=F32)
    logits_t = logits_t + brt_ref[:, 0:1]
    eid = lax.broadcasted_iota(I32, (N_EXPERTS, tm), 0)
    work = logits_t
    idxs, vals = [], []
    for _ in range(TOP_K):
        mx = jnp.max(work, axis=0, keepdims=True)
        idx = jnp.min(jnp.where(work == mx, eid, N_EXPERTS), axis=0, keepdims=True)
        idxs.append(idx)
        vals.append(mx)
        work = jnp.where(eid == idx, -jnp.inf, work)
    exps = [jnp.exp(v - vals[0]) for v in vals]
    den = exps[0] + exps[1] + exps[2] + exps[3]

    onehot = jnp.zeros((N_EXPERTS, tm), F32)
    for idx in idxs:
        onehot = onehot + (eid == idx).astype(F32)
    rank_e = carry[:, 0:1] + jnp.dot(onehot.astype(BF16), before[...],
                                     preferred_element_type=F32)
    carry[...] = carry[...] + jnp.sum(onehot, axis=1, keepdims=True)
    cnt_ref[...] = carry[...]

    mrow = lax.broadcasted_iota(I32, (META_ROWS, tm), 0)
    meta = jnp.zeros((META_ROWS, tm), F32)
    for k in range(TOP_K):
        rank_k = jnp.sum(jnp.where(eid == idxs[k], rank_e, 0.0), axis=0, keepdims=True)
        meta = jnp.where(mrow == META_EIDX + k, idxs[k].astype(F32), meta)
        meta = jnp.where(mrow == META_GATE + k, exps[k] / den, meta)
        meta = jnp.where(mrow == META_RANK + k, rank_k, meta)
    meta_ref[...] = meta


def _outproj(x2, y_pool, y_attn, w_out, g, w_router_t, b_router_t):
    t = x2.shape[0]
    row = lambda i: (i, 0)
    const = lambda i: (0, 0)
    return pl.pallas_call(
        _outproj_kernel,
        grid=(t // ROW_TILE,),
        in_specs=[
            pl.BlockSpec((ROW_TILE, D_MODEL), row),
            pl.BlockSpec((ROW_TILE, D_POOL), row),
            pl.BlockSpec((ROW_TILE, D_ATTN), row),
            pl.BlockSpec((D_MODEL, D_MODEL), const),
            pl.BlockSpec((1, D_MODEL), const),
            pl.BlockSpec((N_EXPERTS, D_MODEL), const),
            pl.BlockSpec((N_EXPERTS, LANES), const),
        ],
        out_specs=[
            pl.BlockSpec((ROW_TILE, D_MODEL), row),
            pl.BlockSpec((ROW_TILE, D_PACKED), row),
            pl.BlockSpec((META_ROWS, ROW_TILE), lambda i: (0, i)),
            pl.BlockSpec((N_EXPERTS, LANES), const),
        ],
        out_shape=[
            jax.ShapeDtypeStruct((t, D_MODEL), F32),
            jax.ShapeDtypeStruct((t, D_PACKED), I32),
            jax.ShapeDtypeStruct((META_ROWS, t), F32),
            jax.ShapeDtypeStruct((N_EXPERTS, LANES), F32),
        ],
        scratch_shapes=[pltpu.VMEM((D_MODEL, D_MODEL), BF16),
                        pltpu.VMEM((ROW_TILE, ROW_TILE), BF16),
                        pltpu.VMEM((N_EXPERTS, LANES), F32)],
        compiler_params=pltpu.CompilerParams(
            dimension_semantics=("arbitrary",), vmem_limit_bytes=VMEM_LIMIT),
        name="outproj_router",
    )(x2, y_pool, y_attn, w_out, g, w_router_t, b_router_t)


def _sc_mesh():
    return plsc.VectorSubcoreMesh(core_axis_name="core", subcore_axis_name="subcore")


def _sc_dispatch(h2, pos_rows, p_max):
    t = h2.shape[0]

    @functools.partial(
        pl.kernel, mesh=_sc_mesh(),
        out_type=[jax.ShapeDtypeStruct((p_max, D_PART), h2.dtype)] * N_PART)
    def run(h_hbm, *refs):
        pos_hbm, xs_hbm = refs[:TOP_K], refs[TOP_K:]
        for c in range(N_PART):
            def body(x_vmem, *idx_vmem, dst=xs_hbm[c]):
                for iv in idx_vmem:
                    pltpu.sync_copy(x_vmem, dst.at[iv.at[0]])

            pltpu.emit_pipeline(
                body,
                grid=(t // SC_WINDOW,),
                in_specs=[pl.BlockSpec((SC_WINDOW, D_PART), lambda i, c=c: (i, c))]
                + [pl.BlockSpec((1, SC_WINDOW), lambda i: (0, i))] * TOP_K,
                out_specs=[],
                core_axis_name=("core", "subcore"),
                dimension_semantics=(pltpu.PARALLEL,),
            )(h_hbm, *pos_hbm)

    return run(h2, *pos_rows)


def _sc_unpermute(y_parts, idx_row):
    n = idx_row.shape[1]

    @functools.partial(
        pl.kernel, mesh=_sc_mesh(),
        out_type=[jax.ShapeDtypeStruct((n, D_PART), y_parts[0].dtype)] * N_PART)
    def run(*refs):
        y_hbm, i_hbm, o_hbm = refs[:N_PART], refs[N_PART], refs[N_PART + 1:]
        for c in range(N_PART):
            def body(i_vmem, o_vmem, src=y_hbm[c]):
                pltpu.sync_copy(src.at[i_vmem.at[0]], o_vmem)

            pltpu.emit_pipeline(
                body,
                grid=(n // SC_WINDOW,),
                in_specs=[pl.BlockSpec((1, SC_WINDOW), lambda i: (0, i))],
                out_specs=[pl.BlockSpec((SC_WINDOW, D_PART), lambda i: (i, 0))],
                core_axis_name=("core", "subcore"),
                dimension_semantics=(pltpu.PARALLEL,),
            )(i_hbm, o_hbm[c])

    return run(*y_parts, idx_row)


def _moe_kernel(te_ref, nv_ref, rows_ref, nxt_ref, *refs):
    xs_refs = refs[:N_PART]
    bgu_ref, bd_ref, wgu_hbm, wd_hbm = refs[N_PART:N_PART + 4]
    y_refs = refs[N_PART + 4:2 * N_PART + 4]
    wgu_f32, wd_f32, wgu_bf, wd_bf, sem = refs[2 * N_PART + 4:]
    i = pl.program_id(0)

    def weight_copies(e):
        return (pltpu.make_async_copy(wgu_hbm.at[e], wgu_f32, sem.at[0]),
                pltpu.make_async_copy(wd_hbm.at[e], wd_f32, sem.at[1]))

    @pl.when(i < nv_ref[0])
    def _():
        prev = te_ref[jnp.maximum(i - 1, 0)]
        new_expert = jnp.logical_or(i == 0, te_ref[i] != prev)

        @pl.when(i == 0)
        def _():
            for cp in weight_copies(te_ref[0]):
                cp.start()

        @pl.when(new_expert)
        def _():
            for cp in weight_copies(te_ref[i]):
                cp.wait()
            wgu_bf[...] = wgu_f32[...].astype(BF16)
            wd_bf[...] = wd_f32[...].astype(BF16)

            @pl.when(nxt_ref[i] >= 0)
            def _():
                for cp in weight_copies(nxt_ref[i]):
                    cp.start()

        d_e = wd_bf.shape[0]

        def ffn(n_rows):
            x = _unpack_row(jnp.concatenate([r[0:n_rows, :] for r in xs_refs], axis=1))
            rid = lax.broadcasted_iota(I32, (n_rows, 1), 0)
            x = jnp.where(rid < rows_ref[i], x, 0.0).astype(BF16)
            gu = jnp.dot(x, wgu_bf[...], preferred_element_type=F32) + bgu_ref[0]
            gate = jnp.minimum(gu[:, :d_e], SWIGLU_LIMIT)
            lin = jnp.clip(gu[:, d_e:], -SWIGLU_LIMIT, SWIGLU_LIMIT)
            act = gate * jax.nn.sigmoid(SWIGLU_ALPHA * gate) * (lin + 1.0)
            y = jnp.dot(act.astype(BF16), wd_bf[...], preferred_element_type=F32) + bd_ref[0]
            yp = _pack_row(y)
            for c, y_ref in enumerate(y_refs):
                y_ref[0:n_rows, :] = yp[:, c * D_PART:(c + 1) * D_PART]
                if n_rows < MOE_TILE:
                    y_ref[n_rows:, :] = jnp.zeros((MOE_TILE - n_rows, D_PART), I32)

        lower = 0
        for n_rows in MOE_PATHS:
            @pl.when((rows_ref[i] > lower) & (rows_ref[i] <= n_rows))
            def _(n_rows=n_rows):
                ffn(n_rows)
            lower = n_rows

    @pl.when(i >= nv_ref[0])
    def _():
        for y_ref in y_refs:
            y_ref[...] = jnp.zeros_like(y_ref)


def _moe(tile_expert, n_valid, tile_rows, next_expert, xs_parts, w_gu, b_gu, w_down, b_down):
    p_max = xs_parts[0].shape[0]
    n_tiles = p_max // MOE_TILE
    d_e = w_down.shape[1]

    def row(i, te, nv, tr, nx):
        return (jnp.minimum(i, nv[0] - 1), 0)

    def expert(i, te, nv, tr, nx):
        return (te[jnp.minimum(i, nv[0] - 1)], 0, 0)

    grid_spec = pltpu.PrefetchScalarGridSpec(
        num_scalar_prefetch=4,
        grid=(n_tiles,),
        in_specs=[pl.BlockSpec((MOE_TILE, D_PART), row)] * N_PART + [
            pl.BlockSpec((1, 1, 2 * d_e), expert),
            pl.BlockSpec((1, 1, D_MODEL), expert),
            pl.BlockSpec(memory_space=pl.ANY),
            pl.BlockSpec(memory_space=pl.ANY),
        ],
        out_specs=[pl.BlockSpec((MOE_TILE, D_PART), lambda i, te, nv, tr, nx: (i, 0))] * N_PART,
        scratch_shapes=[pltpu.VMEM((D_MODEL, 2 * d_e), F32), pltpu.VMEM((d_e, D_MODEL), F32),
                        pltpu.VMEM((D_MODEL, 2 * d_e), BF16), pltpu.VMEM((d_e, D_MODEL), BF16),
                        pltpu.SemaphoreType.DMA((2,))],
    )
    return pl.pallas_call(
        _moe_kernel,
        grid_spec=grid_spec,
        out_shape=[jax.ShapeDtypeStruct((p_max, D_PART), I32)] * N_PART,
        compiler_params=pltpu.CompilerParams(
            dimension_semantics=("arbitrary",), vmem_limit_bytes=VMEM_LIMIT),
        name="moe_ffn",
    )(tile_expert, n_valid, tile_rows, next_expert, *xs_parts, b_gu, b_down, w_gu, w_down)


def _combine_kernel(x1_ref, gate_ref, g_ref, *refs):
    yk_refs, o_ref = refs[:N_PART], refs[-1]
    acc = x1_ref[...]
    for k in range(TOP_K):
        y_k = _unpack_row(jnp.concatenate([r[k] for r in yk_refs], axis=1))
        acc = acc + gate_ref[:, k:k + 1] * y_k
    o_ref[...] = _rms(acc, g_ref[...])


def _combine(x1, gates, g, yk_parts):
    t, d = x1.shape
    tq = COMBINE_TILE
    row = lambda i: (i, 0)
    return pl.pallas_call(
        _combine_kernel,
        grid=(t // tq,),
        in_specs=[
            pl.BlockSpec((tq, d), row),
            pl.BlockSpec((tq, TOP_K), row),
            pl.BlockSpec((1, d), lambda i: (0, 0)),
        ] + [pl.BlockSpec((TOP_K, tq, D_PART), lambda i: (0, i, 0))] * N_PART,
        out_specs=pl.BlockSpec((tq, d), row),
        out_shape=jax.ShapeDtypeStruct((t, d), F32),
        compiler_params=pltpu.CompilerParams(
            dimension_semantics=("arbitrary",), vmem_limit_bytes=VMEM_LIMIT),
        name="combine_final",
    )(x1, gates, g, *yk_parts)


def kernel(x, g_mix, w_in, w_pool, pool_scale, w_out, g_ffn, w_router, b_router,
           w_gu, b_gu, w_down, b_down, g_final):
    batch, seq, d = x.shape
    t = batch * seq
    x2 = x.reshape(t, d)

    y_pool, qkv = _inproj(x2, g_mix[0].reshape(1, d), w_in[0], w_pool[0],
                          pool_scale[0].reshape(1, D_POOL), seq)
    y_attn = _attention(qkv, batch, seq)

    wr_t = w_router[0].T
    br_t = jnp.broadcast_to(b_router[0].reshape(N_EXPERTS, 1), (N_EXPERTS, LANES))
    x1, h2, meta, cnt = _outproj(x2, y_pool, y_attn, w_out[0], g_ffn[0].reshape(1, d), wr_t, br_t)

    eidx = meta[META_EIDX:META_EIDX + TOP_K].astype(I32)
    rank = meta[META_RANK:META_RANK + TOP_K].astype(I32)
    gates = meta[META_GATE:META_GATE + TOP_K].T
    counts = cnt[:, 0].astype(I32)
    padded = ((counts + MOE_TILE - 1) // MOE_TILE) * MOE_TILE
    ends = jnp.cumsum(padded)
    offsets = ends - padded
    e_ids = jnp.arange(N_EXPERTS, dtype=I32)
    hit = eidx[None] == e_ids[:, None, None]
    pos = rank + jnp.sum(jnp.where(hit, offsets[:, None, None], 0), axis=0)
    p_max = t * TOP_K + N_EXPERTS * MOE_TILE
    n_tiles = p_max // MOE_TILE
    tile_start = jnp.arange(n_tiles, dtype=I32) * MOE_TILE
    tile_expert = jnp.minimum(
        jnp.sum((tile_start[:, None] >= ends[None, :]).astype(I32), axis=1), N_EXPERTS - 1)
    mine = tile_expert[:, None] == e_ids[None, :]
    group_end = jnp.sum(jnp.where(mine, (offsets + counts)[None, :], 0), axis=1)
    tile_rows = jnp.clip(group_end - tile_start, 0, MOE_TILE)
    n_valid = (ends[-1] // MOE_TILE).reshape(1).astype(I32)
    later = (e_ids[None, :] > e_ids[:, None]) & (counts[None, :] > 0)
    next_nonempty = jnp.min(jnp.where(later, e_ids[None, :], N_EXPERTS), axis=1)
    next_nonempty = jnp.where(next_nonempty < N_EXPERTS, next_nonempty, -1)
    next_expert = jnp.sum(jnp.where(mine, next_nonempty[None, :], 0), axis=1)

    xs = _sc_dispatch(h2, [pos[k:k + 1] for k in range(TOP_K)], p_max)
    ys = _moe(tile_expert, n_valid, tile_rows, next_expert, xs, w_gu[0],
              b_gu[0].reshape(N_EXPERTS, 1, -1), w_down[0], b_down[0].reshape(N_EXPERTS, 1, -1))
    yk = _sc_unpermute(ys, pos.reshape(1, TOP_K * t))
    yk = [q.reshape(TOP_K, t, D_PART) for q in yk]
    out = _combine(x1, gates, g_final.reshape(1, d), yk)
    return out.reshape(batch, seq, d)
```

```python
import functools

import jax
import jax.numpy as jnp
from jax import lax
from jax.experimental import pallas as pl
from jax.experimental.pallas import tpu as pltpu
from jax.experimental.pallas import tpu_sc as plsc

F32 = jnp.float32
BF16 = jnp.bfloat16
I32 = jnp.int32

D_MODEL = 1024
D_POOL = 512
D_ATTN = 512
POOL_WINDOWS = (2, 4, 8, 16)
POOL_GROUP = 128
HEAD_DIM = 64
N_EXPERTS = 32
TOP_K = 4
SWIGLU_LIMIT = 7.0
SWIGLU_ALPHA = 1.702
EPS = 1e-5
NEG_INF = -1e30
LOG2_E = 1.4426950408889634

LANES = 128
QB = 128
DIL_MID = 4
DIL_FAR = 16
ROW_TILE = 1024
MOE_TILE = 1024
MOE_PATHS = (256, 512, 768, 1024)
COMBINE_TILE = 1024
SC_WINDOW = 128
D_PACKED = D_MODEL // 2
N_PART = 2
D_PART = D_PACKED // N_PART
VMEM_LIMIT = 56 * 1024 * 1024


def _pack_row(x):
    hi = lax.bitcast_convert_type(x[:, :D_PACKED].astype(BF16).astype(F32), I32)
    lo = lax.bitcast_convert_type(x[:, D_PACKED:].astype(BF16).astype(F32), I32)
    return hi | lax.shift_right_logical(lo, 16)


def _unpack_row(w):
    hi = lax.bitcast_convert_type(w & jnp.int32(-65536), F32)
    lo = lax.bitcast_convert_type(lax.shift_left(w, 16), F32)
    return jnp.concatenate([hi, lo], axis=1)


def _rms(x, g):
    ms = jnp.mean(x * x, axis=-1, keepdims=True)
    return x * lax.rsqrt(ms + EPS) * g


POOL_PAD = 16


def _inproj_kernel(x_ref, g_ref, w_ref, wp_ref, sc_ref, yp_ref, qkv_ref, w_bf, hist, pad_ref,
                   *, tiles_per_seq):
    i = pl.program_id(0)
    tm = x_ref.shape[0]

    @pl.when(i == 0)
    def _():
        w_bf[...] = w_ref[...].astype(BF16)
        pad_ref[0:POOL_PAD, :] = jnp.zeros((POOL_PAD, POOL_GROUP), F32)

    @pl.when(i % tiles_per_seq == 0)
    def _():
        hist[...] = jnp.zeros_like(hist)

    h = _rms(x_ref[...], g_ref[...]).astype(BF16)
    proj = jnp.dot(h, w_bf[...], preferred_element_type=F32)
    q = proj[:, D_POOL:D_POOL + D_ATTN] * (HEAD_DIM ** -0.5 * LOG2_E)
    qkv_ref[:, :D_ATTN] = q.astype(BF16)
    qkv_ref[:, D_ATTN:] = proj[:, D_POOL + D_ATTN:].astype(BF16)

    pos = (i % tiles_per_seq) * tm + lax.broadcasted_iota(I32, (tm, 1), 0)
    for g, w in enumerate(POOL_WINDOWS):
        lo, hi = g * POOL_GROUP, (g + 1) * POOL_GROUP
        e = proj[:, lo:hi]
        acc = jnp.concatenate([hist[g], e], axis=0)
        hist[g] = e[tm - POOL_PAD:, :]
        span = 1
        while span < w:
            pad_ref[POOL_PAD:, :] = acc
            acc = acc + pad_ref[pl.ds(POOL_PAD - span, POOL_PAD + tm), :]
            span *= 2
        count = jnp.minimum(pos + 1, w).astype(F32)
        pooled = acc[POOL_PAD:, :] / count - e
        y = jnp.dot(pooled.astype(BF16), wp_ref[g].astype(BF16), preferred_element_type=F32)
        yp_ref[:, lo:hi] = (y * sc_ref[:, lo:hi]).astype(BF16)


def _inproj(x2, g, w, w_pool, pool_scale, seq):
    t = x2.shape[0]
    d_in = w.shape[1]
    n_grp = len(POOL_WINDOWS)
    return pl.pallas_call(
        functools.partial(_inproj_kernel, tiles_per_seq=seq // ROW_TILE),
        grid=(t // ROW_TILE,),
        in_specs=[
            pl.BlockSpec((ROW_TILE, D_MODEL), lambda i: (i, 0)),
            pl.BlockSpec((1, D_MODEL), lambda i: (0, 0)),
            pl.BlockSpec((D_MODEL, d_in), lambda i: (0, 0)),
            pl.BlockSpec((n_grp, POOL_GROUP, POOL_GROUP), lambda i: (0, 0, 0)),
            pl.BlockSpec((1, D_POOL), lambda i: (0, 0)),
        ],
        out_specs=[
            pl.BlockSpec((ROW_TILE, D_POOL), lambda i: (i, 0)),
            pl.BlockSpec((ROW_TILE, 3 * D_ATTN), lambda i: (i, 0)),
        ],
        out_shape=[
            jax.ShapeDtypeStruct((t, D_POOL), BF16),
            jax.ShapeDtypeStruct((t, 3 * D_ATTN), BF16),
        ],
        scratch_shapes=[pltpu.VMEM((D_MODEL, d_in), BF16),
                        pltpu.VMEM((n_grp, POOL_PAD, POOL_GROUP), F32),
                        pltpu.VMEM((2 * POOL_PAD + ROW_TILE, POOL_GROUP), F32)],
        compiler_params=pltpu.CompilerParams(
            dimension_semantics=("arbitrary",), vmem_limit_bytes=VMEM_LIMIT),
        name="inproj_pool",
    )(x2, g, w, w_pool, pool_scale)


def _pair_block(q, k, v_ext, mask2, lane_h0):
    zero = jnp.zeros_like(q)
    q2 = jnp.concatenate([jnp.where(lane_h0, q, zero), jnp.where(lane_h0, zero, q)], axis=0)
    s = lax.dot_general(q2, k, (((1,), (1,)), ((), ())), preferred_element_type=F32)
    s = jnp.where(mask2, s, NEG_INF)
    m = jnp.max(s, axis=-1, keepdims=True)
    p = jnp.exp2(s - m).astype(BF16)
    ol = jnp.dot(p, v_ext, preferred_element_type=F32)
    o = jnp.where(lane_h0, ol[:QB, :LANES], ol[QB:, :LANES])
    l = jnp.where(lane_h0, ol[:QB, LANES:], ol[QB:, LANES:])
    mb = jnp.where(lane_h0, m[:QB], m[QB:])
    return o, mb, l


def _merge(o_a, m_a, l_a, o_b, m_b, l_b):
    m = jnp.maximum(m_a, m_b)
    ea = jnp.exp2(m_a - m)
    eb = jnp.exp2(m_b - m)
    return o_a * ea + o_b * eb, m, l_a * ea + l_b * eb


def _both_heads(mask):
    return jnp.concatenate([mask, mask], axis=0)


def _attn_kernel(q_ref, k_ref, v_ref, o_ref, qp, kp, vpx, vnx, o23, m23, l23, o2, m2, l2,
                 on, mn):
    s = q_ref.shape[0]
    n_chunk = DIL_FAR
    rows = s // n_chunk
    sub = rows // DIL_MID
    grp = n_chunk * n_chunk
    lane_h0 = lax.broadcasted_iota(I32, (1, LANES), 1) < HEAD_DIM

    @pl.when((pl.program_id(0) == 0) & (pl.program_id(1) == 0))
    def _():
        ones = jnp.ones((s, LANES), BF16)
        vnx[:, LANES:] = ones
        vpx[:, LANES:] = ones

    vnx[:, :LANES] = v_ref[...]

    pi = lax.broadcasted_iota(I32, (grp, grp), 0)
    pj = lax.broadcasted_iota(I32, (grp, grp), 1)
    swap = (pj == (pi % n_chunk) * n_chunk + pi // n_chunk).astype(BF16)
    for g in range(s // grp):
        src = slice(g * grp, (g + 1) * grp)
        qk = jnp.concatenate([q_ref[src, :], k_ref[src, :]], axis=1)
        yqk = jnp.dot(swap, qk, preferred_element_type=F32).astype(BF16)
        yv = jnp.dot(swap, v_ref[src, :], preferred_element_type=F32).astype(BF16)
        for r in range(n_chunk):
            dst = slice(r * rows + g * n_chunk, r * rows + (g + 1) * n_chunk)
            part = slice(r * n_chunk, (r + 1) * n_chunk)
            qp[dst, :] = yqk[part, :LANES]
            kp[dst, :] = yqk[part, LANES:]
            vpx[dst, 0:LANES] = yv[part]

    qi = lax.broadcasted_iota(I32, (QB, QB), 0)
    kj = lax.broadcasted_iota(I32, (QB, QB), 1)
    causal = _both_heads(qi >= kj)

    for r in range(n_chunk):
        blk = slice(r * rows, (r + 1) * rows)
        o, m, l = _pair_block(qp[blk, :], kp[blk, :], vpx[blk, :], causal, lane_h0)
        o23[blk, :] = o
        m23[blk, :] = m
        l23[blk, :] = l

    def mid_index(n_key_sub):
        i_q = lax.broadcasted_iota(I32, (QB, DIL_MID * n_key_sub), 0)
        i_k = lax.broadcasted_iota(I32, (QB, DIL_MID * n_key_sub), 1)
        return i_q // sub, i_q % sub, i_k // n_key_sub, i_k % n_key_sub

    jq, aq, jk, ak = mid_index(sub)
    d0 = DIL_FAR * (aq - ak) + DIL_MID * (jq - jk)
    mask_mid0 = _both_heads(d0 >= 0)
    jq, aq, jk, ak = mid_index(2 * sub)
    d1 = DIL_FAR * (aq + sub - ak) + DIL_MID * (jq - jk)
    mask_mid = _both_heads((d1 >= 0) & (d1 <= DIL_MID * QB))

    def mid_tile(r4, a0, k0, nk, mask):
        def at(j, off, n):
            return pl.ds((DIL_MID * j + r4) * rows + off, n)

        q = jnp.concatenate([qp[at(j, a0, sub), :] for j in range(DIL_MID)], axis=0)
        k = jnp.concatenate([kp[at(j, k0, nk), :] for j in range(DIL_MID)], axis=0)
        v = jnp.concatenate([vpx[at(j, k0, nk), :] for j in range(DIL_MID)], axis=0)
        o, m, l = _pair_block(q, k, v, mask, lane_h0)
        for j in range(DIL_MID):
            dst = at(j, a0, sub)
            src = slice(j * sub, (j + 1) * sub)
            o2[dst, :] = o[src]
            m2[dst, :] = m[src]
            l2[dst, :] = l[src]

    for r4 in range(DIL_MID):
        mid_tile(r4, 0, 0, sub, mask_mid0)
        for a_blk in range(1, rows // sub):
            mid_tile(r4, a_blk * sub, (a_blk - 1) * sub, 2 * sub, mask_mid)

    for r in range(n_chunk):
        blk = slice(r * rows, (r + 1) * rows)
        oo, mm, ll = _merge(o23[blk, :], m23[blk, :], l23[blk, :],
                            o2[blk, :], m2[blk, :], l2[blk, :])
        o23[blk, :] = oo / ll
        m23[blk, :] = mm + jnp.log2(ll)

    for g in range(s // grp):
        slabs = [o23[r * rows + g * n_chunk:r * rows + (g + 1) * n_chunk, :].astype(BF16)
                 for r in range(n_chunk)]
        on[g * grp:(g + 1) * grp, :] = jnp.dot(swap, jnp.concatenate(slabs, axis=0),
                                               preferred_element_type=F32)
    for r in range(n_chunk):
        mn[pl.ds(r, rows, stride=n_chunk), :] = m23[r * rows:(r + 1) * rows, :]

    qi2 = lax.broadcasted_iota(I32, (QB, 2 * QB), 0)
    kj2 = lax.broadcasted_iota(I32, (QB, 2 * QB), 1)
    dn = qi2 + QB - kj2
    mask_near = _both_heads((dn >= 0) & (dn <= QB))

    def near_finish(dst, o, m, l):
        oo, _, ll = _merge(on[dst, :], mn[dst, :], 1.0, o, m, l)
        o_ref[dst, :] = (oo / ll).astype(o_ref.dtype)

    first = pl.ds(0, QB)
    o, m, l = _pair_block(q_ref[first, :], k_ref[first, :], vnx[first, :], causal, lane_h0)
    near_finish(first, o, m, l)
    for n in range(1, s // QB):
        keys = pl.ds((n - 1) * QB, 2 * QB)
        o, m, l = _pair_block(q_ref[pl.ds(n * QB, QB), :], k_ref[keys, :], vnx[keys, :],
                              mask_near, lane_h0)
        near_finish(pl.ds(n * QB, QB), o, m, l)


def _attention(qkv, batch, seq):
    n_pair = D_ATTN // LANES
    blk = (seq, LANES)
    f32_scr = pltpu.VMEM(blk, F32)
    bf_scr = pltpu.VMEM(blk, BF16)
    bfx_scr = pltpu.VMEM((seq, 2 * LANES), BF16)
    return pl.pallas_call(
        _attn_kernel,
        grid=(batch, n_pair),
        in_specs=[
            pl.BlockSpec(blk, lambda b, h: (b, h)),
            pl.BlockSpec(blk, lambda b, h: (b, n_pair + h)),
            pl.BlockSpec(blk, lambda b, h: (b, 2 * n_pair + h)),
        ],
        out_specs=pl.BlockSpec(blk, lambda b, h: (b, h)),
        out_shape=jax.ShapeDtypeStruct((batch * seq, D_ATTN), BF16),
        scratch_shapes=[bf_scr, bf_scr, bfx_scr, bfx_scr] + [f32_scr] * 8,
        compiler_params=pltpu.CompilerParams(
            dimension_semantics=("arbitrary", "arbitrary"), vmem_limit_bytes=VMEM_LIMIT),
        name="dilated_attn",
    )(qkv, qkv, qkv)


META_ROWS = 16
META_EIDX, META_GATE, META_RANK = 0, 4, 8


def _outproj_kernel(x_ref, yp_ref, ya_ref, wo_ref, g_ref, wrt_ref, brt_ref,
                    x1_ref, h2_ref, meta_ref, cnt_ref, wo_bf, before, carry):
    tm = x_ref.shape[0]

    @pl.when(pl.program_id(0) == 0)
    def _():
        wo_bf[...] = wo_ref[...].astype(BF16)
        carry[...] = jnp.zeros_like(carry)
        ti = lax.broadcasted_iota(I32, (tm, tm), 0)
        tj = lax.broadcasted_iota(I32, (tm, tm), 1)
        before[...] = (ti < tj).astype(BF16)

    x1 = (x_ref[...]
          + jnp.dot(yp_ref[...], wo_bf[:D_POOL, :], preferred_element_type=F32)
          + jnp.dot(ya_ref[...], wo_bf[D_POOL:, :], preferred_element_type=F32))
    x1_ref[...] = x1
    h2 = _rms(x1, g_ref[...])
    h2_ref[...] = _pack_row(h2)

    logits_t = lax.dot_general(wrt_ref[...].astype(BF16), h2.astype(BF16),
                               (((1,), (1,)), ((), ())), preferred_element_type=F32)
    logits_t = logits_t + brt_ref[:, 0:1]
    eid = lax.broadcasted_iota(I32, (N_EXPERTS, tm), 0)
    work = logits_t
    idxs, vals = [], []
    for _ in range(TOP_K):
        mx = jnp.max(work, axis=0, keepdims=True)
        idx = jnp.min(jnp.where(work == mx, eid, N_EXPERTS), axis=0, keepdims=True)
        idxs.append(idx)
        vals.append(mx)
        work = jnp.where(eid == idx, -jnp.inf, work)
    exps = [jnp.exp(v - vals[0]) for v in vals]
    den = exps[0] + exps[1] + exps[2] + exps[3]

    onehot = jnp.zeros((N_EXPERTS, tm), F32)
    for idx in idxs:
        onehot = onehot + (eid == idx).astype(F32)
    rank_e = carry[:, 0:1] + jnp.dot(onehot.astype(BF16), before[...],
                                     preferred_element_type=F32)
    carry[...] = carry[...] + jnp.sum(onehot, axis=1, keepdims=True)
    cnt_ref[...] = carry[...]

    mrow = lax.broadcasted_iota(I32, (META_ROWS, tm), 0)
    meta = jnp.zeros((META_ROWS, tm), F32)
    for k in range(TOP_K):
        rank_k = jnp.sum(jnp.where(eid == idxs[k], rank_e, 0.0), axis=0, keepdims=True)
        meta = jnp.where(mrow == META_EIDX + k, idxs[k].astype(F32), meta)
        meta = jnp.where(mrow == META_GATE + k, exps[k] / den, meta)
        meta = jnp.where(mrow == META_RANK + k, rank_k, meta)
    meta_ref[...] = meta


def _outproj(x2, y_pool, y_attn, w_out, g, w_router_t, b_router_t):
    t = x2.shape[0]
    row = lambda i: (i, 0)
    const = lambda i: (0, 0)
    return pl.pallas_call(
        _outproj_kernel,
        grid=(t // ROW_TILE,),
        in_specs=[
            pl.BlockSpec((ROW_TILE, D_MODEL), row),
            pl.BlockSpec((ROW_TILE, D_POOL), row),
            pl.BlockSpec((ROW_TILE, D_ATTN), row),
            pl.BlockSpec((D_MODEL, D_MODEL), const),
            pl.BlockSpec((1, D_MODEL), const),
            pl.BlockSpec((N_EXPERTS, D_MODEL), const),
            pl.BlockSpec((N_EXPERTS, LANES), const),
        ],
        out_specs=[
            pl.BlockSpec((ROW_TILE, D_MODEL), row),
            pl.BlockSpec((ROW_TILE, D_PACKED), row),
            pl.BlockSpec((META_ROWS, ROW_TILE), lambda i: (0, i)),
            pl.BlockSpec((N_EXPERTS, LANES), const),
        ],
        out_shape=[
            jax.ShapeDtypeStruct((t, D_MODEL), F32),
            jax.ShapeDtypeStruct((t, D_PACKED), I32),
            jax.ShapeDtypeStruct((META_ROWS, t), F32),
            jax.ShapeDtypeStruct((N_EXPERTS, LANES), F32),
        ],
        scratch_shapes=[pltpu.VMEM((D_MODEL, D_MODEL), BF16),
                        pltpu.VMEM((ROW_TILE, ROW_TILE), BF16),
                        pltpu.VMEM((N_EXPERTS, LANES), F32)],
        compiler_params=pltpu.CompilerParams(
            dimension_semantics=("arbitrary",), vmem_limit_bytes=VMEM_LIMIT),
        name="outproj_router",
    )(x2, y_pool, y_attn, w_out, g, w_router_t, b_router_t)


def _sc_mesh():
    return plsc.VectorSubcoreMesh(core_axis_name="core", subcore_axis_name="subcore")


def _sc_dispatch(h2, pos_rows, p_max):
    t = h2.shape[0]

    @functools.partial(
        pl.kernel, mesh=_sc_mesh(),
        out_type=[jax.ShapeDtypeStruct((p_max, D_PART), h2.dtype)] * N_PART)
    def run(h_hbm, *refs):
        pos_hbm, xs_hbm = refs[:TOP_K], refs[TOP_K:]
        for c in range(N_PART):
            def body(x_vmem, *idx_vmem, dst=xs_hbm[c]):
                for iv in idx_vmem:
                    pltpu.sync_copy(x_vmem, dst.at[iv.at[0]])

            pltpu.emit_pipeline(
                body,
                grid=(t // SC_WINDOW,),
                in_specs=[pl.BlockSpec((SC_WINDOW, D_PART), lambda i, c=c: (i, c))]
                + [pl.BlockSpec((1, SC_WINDOW), lambda i: (0, i))] * TOP_K,
                out_specs=[],
                core_axis_name=("core", "subcore"),
                dimension_semantics=(pltpu.PARALLEL,),
            )(h_hbm, *pos_hbm)

    return run(h2, *pos_rows)


def _sc_unpermute(y_parts, idx_row):
    n = idx_row.shape[1]

    @functools.partial(
        pl.kernel, mesh=_sc_mesh(),
        out_type=[jax.ShapeDtypeStruct((n, D_PART), y_parts[0].dtype)] * N_PART)
    def run(*refs):
        y_hbm, i_hbm, o_hbm = refs[:N_PART], refs[N_PART], refs[N_PART + 1:]
        for c in range(N_PART):
            def body(i_vmem, o_vmem, src=y_hbm[c]):
                pltpu.sync_copy(src.at[i_vmem.at[0]], o_vmem)

            pltpu.emit_pipeline(
                body,
                grid=(n // SC_WINDOW,),
                in_specs=[pl.BlockSpec((1, SC_WINDOW), lambda i: (0, i))],
                out_specs=[pl.BlockSpec((SC_WINDOW, D_PART), lambda i: (i, 0))],
                core_axis_name=("core", "subcore"),
                dimension_semantics=(pltpu.PARALLEL,),
            )(i_hbm, o_hbm[c])

    return run(*y_parts, idx_row)


def _moe_kernel(te_ref, nv_ref, rows_ref, nxt_ref, *refs):
    xs_refs = refs[:N_PART]
    bgu_ref, bd_ref, wgu_hbm, wd_hbm = refs[N_PART:N_PART + 4]
    y_refs = refs[N_PART + 4:2 * N_PART + 4]
    wgu_f32, wd_f32, wgu_bf, wd_bf, sem = refs[2 * N_PART + 4:]
    i = pl.program_id(0)

    def weight_copies(e):
        return (pltpu.make_async_copy(wgu_hbm.at[e], wgu_f32, sem.at[0]),
                pltpu.make_async_copy(wd_hbm.at[e], wd_f32, sem.at[1]))

    @pl.when(i < nv_ref[0])
    def _():
        prev = te_ref[jnp.maximum(i - 1, 0)]
        new_expert = jnp.logical_or(i == 0, te_ref[i] != prev)

        @pl.when(i == 0)
        def _():
            for cp in weight_copies(te_ref[0]):
                cp.start()

        @pl.when(new_expert)
        def _():
            for cp in weight_copies(te_ref[i]):
                cp.wait()
            wgu_bf[...] = wgu_f32[...].astype(BF16)
            wd_bf[...] = wd_f32[...].astype(BF16)

            @pl.when(nxt_ref[i] >= 0)
            def _():
                for cp in weight_copies(nxt_ref[i]):
                    cp.start()

        d_e = wd_bf.shape[0]

        def ffn(n_rows):
            x = _unpack_row(jnp.concatenate([r[0:n_rows, :] for r in xs_refs], axis=1))
            rid = lax.broadcasted_iota(I32, (n_rows, 1), 0)
            x = jnp.where(rid < rows_ref[i], x, 0.0).astype(BF16)
            gu = jnp.dot(x, wgu_bf[...], preferred_element_type=F32) + bgu_ref[0]
            gate = jnp.minimum(gu[:, :d_e], SWIGLU_LIMIT)
            lin = jnp.clip(gu[:, d_e:], -SWIGLU_LIMIT, SWIGLU_LIMIT)
            act = gate * jax.nn.sigmoid(SWIGLU_ALPHA * gate) * (lin + 1.0)
            y = jnp.dot(act.astype(BF16), wd_bf[...], preferred_element_type=F32) + bd_ref[0]
            yp = _pack_row(y)
            for c, y_ref in enumerate(y_refs):
                y_ref[0:n_rows, :] = yp[:, c * D_PART:(c + 1) * D_PART]
                if n_rows < MOE_TILE:
                    y_ref[n_rows:, :] = jnp.zeros((MOE_TILE - n_rows, D_PART), I32)

        lower = 0
        for n_rows in MOE_PATHS:
            @pl.when((rows_ref[i] > lower) & (rows_ref[i] <= n_rows))
            def _(n_rows=n_rows):
                ffn(n_rows)
            lower = n_rows

    @pl.when(i >= nv_ref[0])
    def _():
        for y_ref in y_refs:
            y_ref[...] = jnp.zeros_like(y_ref)


def _moe(tile_expert, n_valid, tile_rows, next_expert, xs_parts, w_gu, b_gu, w_down, b_down):
    p_max = xs_parts[0].shape[0]
    n_tiles = p_max // MOE_TILE
    d_e = w_down.shape[1]

    def row(i, te, nv, tr, nx):
        return (jnp.minimum(i, nv[0] - 1), 0)

    def expert(i, te, nv, tr, nx):
        return (te[jnp.minimum(i, nv[0] - 1)], 0, 0)

    grid_spec = pltpu.PrefetchScalarGridSpec(
        num_scalar_prefetch=4,
        grid=(n_tiles,),
        in_specs=[pl.BlockSpec((MOE_TILE, D_PART), row)] * N_PART + [
            pl.BlockSpec((1, 1, 2 * d_e), expert),
            pl.BlockSpec((1, 1, D_MODEL), expert),
            pl.BlockSpec(memory_space=pl.ANY),
            pl.BlockSpec(memory_space=pl.ANY),
        ],
        out_specs=[pl.BlockSpec((MOE_TILE, D_PART), lambda i, te, nv, tr, nx: (i, 0))] * N_PART,
        scratch_shapes=[pltpu.VMEM((D_MODEL, 2 * d_e), F32), pltpu.VMEM((d_e, D_MODEL), F32),
                        pltpu.VMEM((D_MODEL, 2 * d_e), BF16), pltpu.VMEM((d_e, D_MODEL), BF16),
                        pltpu.SemaphoreType.DMA((2,))],
    )
    return pl.pallas_call(
        _moe_kernel,
        grid_spec=grid_spec,
        out_shape=[jax.ShapeDtypeStruct((p_max, D_PART), I32)] * N_PART,
        compiler_params=pltpu.CompilerParams(
            dimension_semantics=("arbitrary",), vmem_limit_bytes=VMEM_LIMIT),
        name="moe_ffn",
    )(tile_expert, n_valid, tile_rows, next_expert, *xs_parts, b_gu, b_down, w_gu, w_down)


def _combine_kernel(x1_ref, gate_ref, g_ref, *refs):
    yk_refs, o_ref = refs[:N_PART], refs[-1]
    acc = x1_ref[...]
    for k in range(TOP_K):
        y_k = _unpack_row(jnp.concatenate([r[k] for r in yk_refs], axis=1))
        acc = acc + gate_ref[:, k:k + 1] * y_k
    o_ref[...] = _rms(acc, g_ref[...])


def _combine(x1, gates, g, yk_parts):
    t, d = x1.shape
    tq = COMBINE_TILE
    row = lambda i: (i, 0)
    return pl.pallas_call(
        _combine_kernel,
        grid=(t // tq,),
        in_specs=[
            pl.BlockSpec((tq, d), row),
            pl.BlockSpec((tq, TOP_K), row),
            pl.BlockSpec((1, d), lambda i: (0, 0)),
        ] + [pl.BlockSpec((TOP_K, tq, D_PART), lambda i: (0, i, 0))] * N_PART,
        out_specs=pl.BlockSpec((tq, d), row),
        out_shape=jax.ShapeDtypeStruct((t, d), F32),
        compiler_params=pltpu.CompilerParams(
            dimension_semantics=("arbitrary",), vmem_limit_bytes=VMEM_LIMIT),
        name="combine_final",
    )(x1, gates, g, *yk_parts)


def kernel(x, g_mix, w_in, w_pool, pool_scale, w_out, g_ffn, w_router, b_router,
           w_gu, b_gu, w_down, b_down, g_final):
    batch, seq, d = x.shape
    t = batch * seq
    assert d == D_MODEL and seq == DIL_FAR * QB, (x.shape, "unsupported shape")
    assert seq % ROW_TILE == 0 and t % COMBINE_TILE == 0 and t % SC_WINDOW == 0
    assert w_in.shape[0] == 1, "one layer"
    x2 = x.reshape(t, d)

    y_pool, qkv = _inproj(x2, g_mix[0].reshape(1, d), w_in[0], w_pool[0],
                          pool_scale[0].reshape(1, D_POOL), seq)
    y_attn = _attention(qkv, batch, seq)

    wr_t = w_router[0].T
    br_t = jnp.broadcast_to(b_router[0].reshape(N_EXPERTS, 1), (N_EXPERTS, LANES))
    x1, h2, meta, cnt = _outproj(x2, y_pool, y_attn, w_out[0], g_ffn[0].reshape(1, d), wr_t, br_t)

    eidx = meta[META_EIDX:META_EIDX + TOP_K].astype(I32)
    rank = meta[META_RANK:META_RANK + TOP_K].astype(I32)
    gates = meta[META_GATE:META_GATE + TOP_K].T
    counts = cnt[:, 0].astype(I32)
    padded = ((counts + MOE_TILE - 1) // MOE_TILE) * MOE_TILE
    ends = jnp.cumsum(padded)
    offsets = ends - padded
    e_ids = jnp.arange(N_EXPERTS, dtype=I32)
    hit = eidx[None] == e_ids[:, None, None]
    pos = rank + jnp.sum(jnp.where(hit, offsets[:, None, None], 0), axis=0)
    p_max = t * TOP_K + N_EXPERTS * MOE_TILE
    n_tiles = p_max // MOE_TILE
    tile_start = jnp.arange(n_tiles, dtype=I32) * MOE_TILE
    tile_expert = jnp.minimum(
        jnp.sum((tile_start[:, None] >= ends[None, :]).astype(I32), axis=1), N_EXPERTS - 1)
    mine = tile_expert[:, None] == e_ids[None, :]
    group_end = jnp.sum(jnp.where(mine, (offsets + counts)[None, :], 0), axis=1)
    tile_rows = jnp.clip(group_end - tile_start, 0, MOE_TILE)
    n_valid = (ends[-1] // MOE_TILE).reshape(1).astype(I32)
    later = (e_ids[None, :] > e_ids[:, None]) & (counts[None, :] > 0)
    next_nonempty = jnp.min(jnp.where(later, e_ids[None, :], N_EXPERTS), axis=1)
    next_nonempty = jnp.where(next_nonempty < N_EXPERTS, next_nonempty, -1)
    next_expert = jnp.sum(jnp.where(mine, next_nonempty[None, :], 0), axis=1)

    xs = _sc_dispatch(h2, [pos[k:k + 1] for k in range(TOP_K)], p_max)
    ys = _moe(tile_expert, n_valid, tile_rows, next_expert, xs, w_gu[0],
              b_gu[0].reshape(N_EXPERTS, 1, -1), w_down[0], b_down[0].reshape(N_EXPERTS, 1, -1))
    yk = _sc_unpermute(ys, pos.reshape(1, TOP_K * t))
    yk = [q.reshape(TOP_K, t, D_PART) for q in yk]
    out = _combine(x1, gates, g_final.reshape(1, d), yk)
    return out.reshape(batch, seq, d)
```

```python
import functools

import jax
import jax.numpy as jnp
from jax import lax
from jax.experimental import pallas as pl
from jax.experimental.pallas import tpu as pltpu
from jax.experimental.pallas import tpu_sc as plsc

F32 = jnp.float32
BF16 = jnp.bfloat16
I32 = jnp.int32

D_MODEL = 1024
D_POOL = 512
D_ATTN = 512
POOL_WINDOWS = (2, 4, 8, 16)
POOL_GROUP = 128
HEAD_DIM = 64
N_EXPERTS = 32
TOP_K = 4
SWIGLU_LIMIT = 7.0
SWIGLU_ALPHA = 1.702
EPS = 1e-5
NEG_INF = -1e30
LOG2_E = 1.4426950408889634

LANES = 128
QB = 128
DIL_MID = 4
DIL_FAR = 16
ROW_TILE = 1024
MOE_TILE = 1024
MOE_PATHS = (256, 512, 768, 1024)
COMBINE_TILE = 1024
SC_WINDOW = 128
D_PACKED = D_MODEL // 2
N_PART = 2
D_PART = D_PACKED // N_PART
VMEM_LIMIT = 56 * 1024 * 1024


def _pack_row(x):
    hi = lax.bitcast_convert_type(x[:, :D_PACKED].astype(BF16).astype(F32), I32)
    lo = lax.bitcast_convert_type(x[:, D_PACKED:].astype(BF16).astype(F32), I32)
    return hi | lax.shift_right_logical(lo, 16)


def _unpack_row(w):
    hi = lax.bitcast_convert_type(w & jnp.int32(-65536), F32)
    lo = lax.bitcast_convert_type(lax.shift_left(w, 16), F32)
    return jnp.concatenate([hi, lo], axis=1)


def _rms(x, g):
    ms = jnp.mean(x * x, axis=-1, keepdims=True)
    return x * lax.rsqrt(ms + EPS) * g


POOL_PAD = 16


def _inproj_kernel(x_ref, g_ref, w_ref, wp_ref, sc_ref, yp_ref, qkv_ref, w_bf, hist, pad_ref,
                   *, tiles_per_seq):
    i = pl.program_id(0)
    tm = x_ref.shape[0]

    @pl.when(i == 0)
    def _():
        w_bf[...] = w_ref[...].astype(BF16)
        pad_ref[0:POOL_PAD, :] = jnp.zeros((POOL_PAD, POOL_GROUP), F32)

    @pl.when(i % tiles_per_seq == 0)
    def _():
        hist[...] = jnp.zeros_like(hist)

    h = _rms(x_ref[...], g_ref[...]).astype(BF16)
    proj = jnp.dot(h, w_bf[...], preferred_element_type=F32)
    q = proj[:, D_POOL:D_POOL + D_ATTN] * (HEAD_DIM ** -0.5 * LOG2_E)
    qkv_ref[:, :D_ATTN] = q.astype(BF16)
    qkv_ref[:, D_ATTN:] = proj[:, D_POOL + D_ATTN:].astype(BF16)

    pos = (i % tiles_per_seq) * tm + lax.broadcasted_iota(I32, (tm, 1), 0)
    for g, w in enumerate(POOL_WINDOWS):
        lo, hi = g * POOL_GROUP, (g + 1) * POOL_GROUP
        e = proj[:, lo:hi]
        acc = jnp.concatenate([hist[g], e], axis=0)
        hist[g] = e[tm - POOL_PAD:, :]
        span = 1
        while span < w:
            pad_ref[POOL_PAD:, :] = acc
            acc = acc + pad_ref[pl.ds(POOL_PAD - span, POOL_PAD + tm), :]
            span *= 2
        count = jnp.minimum(pos + 1, w).astype(F32)
        pooled = acc[POOL_PAD:, :] / count - e
        y = jnp.dot(pooled.astype(BF16), wp_ref[g].astype(BF16), preferred_element_type=F32)
        yp_ref[:, lo:hi] = (y * sc_ref[:, lo:hi]).astype(BF16)


def _inproj(x2, g, w, w_pool, pool_scale, seq):
    t = x2.shape[0]
    d_in = w.shape[1]
    n_grp = len(POOL_WINDOWS)
    return pl.pallas_call(
        functools.partial(_inproj_kernel, tiles_per_seq=seq // ROW_TILE),
        grid=(t // ROW_TILE,),
        in_specs=[
            pl.BlockSpec((ROW_TILE, D_MODEL), lambda i: (i, 0)),
            pl.BlockSpec((1, D_MODEL), lambda i: (0, 0)),
            pl.BlockSpec((D_MODEL, d_in), lambda i: (0, 0)),
            pl.BlockSpec((n_grp, POOL_GROUP, POOL_GROUP), lambda i: (0, 0, 0)),
            pl.BlockSpec((1, D_POOL), lambda i: (0, 0)),
        ],
        out_specs=[
            pl.BlockSpec((ROW_TILE, D_POOL), lambda i: (i, 0)),
            pl.BlockSpec((ROW_TILE, 3 * D_ATTN), lambda i: (i, 0)),
        ],
        out_shape=[
            jax.ShapeDtypeStruct((t, D_POOL), BF16),
            jax.ShapeDtypeStruct((t, 3 * D_ATTN), BF16),
        ],
        scratch_shapes=[pltpu.VMEM((D_MODEL, d_in), BF16),
                        pltpu.VMEM((n_grp, POOL_PAD, POOL_GROUP), F32),
                        pltpu.VMEM((2 * POOL_PAD + ROW_TILE, POOL_GROUP), F32)],
        compiler_params=pltpu.CompilerParams(
            dimension_semantics=("arbitrary",), vmem_limit_bytes=VMEM_LIMIT),
        name="inproj_pool",
    )(x2, g, w, w_pool, pool_scale)


def _pair_block(q, k, v_ext, mask2, lane_h0):
    zero = jnp.zeros_like(q)
    q2 = jnp.concatenate([jnp.where(lane_h0, q, zero), jnp.where(lane_h0, zero, q)], axis=0)
    s = lax.dot_general(q2, k, (((1,), (1,)), ((), ())), preferred_element_type=F32)
    s = jnp.where(mask2, s, NEG_INF)
    m = jnp.max(s, axis=-1, keepdims=True)
    p = jnp.exp2(s - m).astype(BF16)
    ol = jnp.dot(p, v_ext, preferred_element_type=F32)
    o = jnp.where(lane_h0, ol[:QB, :LANES], ol[QB:, :LANES])
    l = jnp.where(lane_h0, ol[:QB, LANES:], ol[QB:, LANES:])
    mb = jnp.where(lane_h0, m[:QB], m[QB:])
    return o, mb, l


def _merge(o_a, m_a, l_a, o_b, m_b, l_b):
    m = jnp.maximum(m_a, m_b)
    ea = jnp.exp2(m_a - m)
    eb = jnp.exp2(m_b - m)
    return o_a * ea + o_b * eb, m, l_a * ea + l_b * eb


def _both_heads(mask):
    return jnp.concatenate([mask, mask], axis=0)


def _attn_kernel(q_ref, k_ref, v_ref, o_ref, qp, kp, vpx, vnx, o23, m23, l23, o2, m2, l2,
                 on, mn):
    s = q_ref.shape[0]
    n_chunk = DIL_FAR
    rows = s // n_chunk
    sub = rows // DIL_MID
    grp = n_chunk * n_chunk
    lane_h0 = lax.broadcasted_iota(I32, (1, LANES), 1) < HEAD_DIM

    @pl.when((pl.program_id(0) == 0) & (pl.program_id(1) == 0))
    def _():
        ones = jnp.ones((s, LANES), BF16)
        vnx[:, LANES:] = ones
        vpx[:, LANES:] = ones

    vnx[:, :LANES] = v_ref[...]

    pi = lax.broadcasted_iota(I32, (grp, grp), 0)
    pj = lax.broadcasted_iota(I32, (grp, grp), 1)
    swap = (pj == (pi % n_chunk) * n_chunk + pi // n_chunk).astype(BF16)
    for g in range(s // grp):
        src = slice(g * grp, (g + 1) * grp)
        qk = jnp.concatenate([q_ref[src, :], k_ref[src, :]], axis=1)
        yqk = jnp.dot(swap, qk, preferred_element_type=F32).astype(BF16)
        yv = jnp.dot(swap, v_ref[src, :], preferred_element_type=F32).astype(BF16)
        for r in range(n_chunk):
            dst = slice(r * rows + g * n_chunk, r * rows + (g + 1) * n_chunk)
            part = slice(r * n_chunk, (r + 1) * n_chunk)
            qp[dst, :] = yqk[part, :LANES]
            kp[dst, :] = yqk[part, LANES:]
            vpx[dst, 0:LANES] = yv[part]

    qi = lax.broadcasted_iota(I32, (QB, QB), 0)
    kj = lax.broadcasted_iota(I32, (QB, QB), 1)
    causal = _both_heads(qi >= kj)

    for r in range(n_chunk):
        blk = slice(r * rows, (r + 1) * rows)
        o, m, l = _pair_block(qp[blk, :], kp[blk, :], vpx[blk, :], causal, lane_h0)
        o23[blk, :] = o
        m23[blk, :] = m
        l23[blk, :] = l

    def mid_index(n_key_sub):
        i_q = lax.broadcasted_iota(I32, (QB, DIL_MID * n_key_sub), 0)
        i_k = lax.broadcasted_iota(I32, (QB, DIL_MID * n_key_sub), 1)
        return i_q // sub, i_q % sub, i_k // n_key_sub, i_k % n_key_sub

    jq, aq, jk, ak = mid_index(sub)
    d0 = DIL_FAR * (aq - ak) + DIL_MID * (jq - jk)
    mask_mid0 = _both_heads(d0 >= 0)
    jq, aq, jk, ak = mid_index(2 * sub)
    d1 = DIL_FAR * (aq + sub - ak) + DIL_MID * (jq - jk)
    mask_mid = _both_heads((d1 >= 0) & (d1 <= DIL_MID * QB))

    def mid_tile(r4, a0, k0, nk, mask):
        def at(j, off, n):
            return pl.ds((DIL_MID * j + r4) * rows + off, n)

        q = jnp.concatenate([qp[at(j, a0, sub), :] for j in range(DIL_MID)], axis=0)
        k = jnp.concatenate([kp[at(j, k0, nk), :] for j in range(DIL_MID)], axis=0)
        v = jnp.concatenate([vpx[at(j, k0, nk), :] for j in range(DIL_MID)], axis=0)
        o, m, l = _pair_block(q, k, v, mask, lane_h0)
        for j in range(DIL_MID):
            dst = at(j, a0, sub)
            src = slice(j * sub, (j + 1) * sub)
            o2[dst, :] = o[src]
            m2[dst, :] = m[src]
            l2[dst, :] = l[src]

    for r4 in range(DIL_MID):
        mid_tile(r4, 0, 0, sub, mask_mid0)
        for a_blk in range(1, rows // sub):
            mid_tile(r4, a_blk * sub, (a_blk - 1) * sub, 2 * sub, mask_mid)

    for r in range(n_chunk):
        blk = slice(r * rows, (r + 1) * rows)
        oo, mm, ll = _merge(o23[blk, :], m23[blk, :], l23[blk, :],
                            o2[blk, :], m2[blk, :], l2[blk, :])
        o23[blk, :] = oo / ll
        m23[blk, :] = mm + jnp.log2(ll)

    for g in range(s // grp):
        slabs = [o23[r * rows + g * n_chunk:r * rows + (g + 1) * n_chunk, :].astype(BF16)
                 for r in range(n_chunk)]
        on[g * grp:(g + 1) * grp, :] = jnp.dot(swap, jnp.concatenate(slabs, axis=0),
                                               preferred_element_type=F32)
    for r in range(n_chunk):
        mn[pl.ds(r, rows, stride=n_chunk), :] = m23[r * rows:(r + 1) * rows, :]

    qi2 = lax.broadcasted_iota(I32, (QB, 2 * QB), 0)
    kj2 = lax.broadcasted_iota(I32, (QB, 2 * QB), 1)
    dn = qi2 + QB - kj2
    mask_near = _both_heads((dn >= 0) & (dn <= QB))

    def near_finish(dst, o, m, l):
        oo, _, ll = _merge(on[dst, :], mn[dst, :], 1.0, o, m, l)
        o_ref[dst, :] = (oo / ll).astype(o_ref.dtype)

    first = pl.ds(0, QB)
    o, m, l = _pair_block(q_ref[first, :], k_ref[first, :], vnx[first, :], causal, lane_h0)
    near_finish(first, o, m, l)
    for n in range(1, s // QB):
        keys = pl.ds((n - 1) * QB, 2 * QB)
        o, m, l = _pair_block(q_ref[pl.ds(n * QB, QB), :], k_ref[keys, :], vnx[keys, :],
                              mask_near, lane_h0)
        near_finish(pl.ds(n * QB, QB), o, m, l)


def _attention(qkv, batch, seq):
    n_pair = D_ATTN // LANES
    blk = (seq, LANES)
    f32_scr = pltpu.VMEM(blk, F32)
    bf_scr = pltpu.VMEM(blk, BF16)
    bfx_scr = pltpu.VMEM((seq, 2 * LANES), BF16)
    return pl.pallas_call(
        _attn_kernel,
        grid=(batch, n_pair),
        in_specs=[
            pl.BlockSpec(blk, lambda b, h: (b, h)),
            pl.BlockSpec(blk, lambda b, h: (b, n_pair + h)),
            pl.BlockSpec(blk, lambda b, h: (b, 2 * n_pair + h)),
        ],
        out_specs=pl.BlockSpec(blk, lambda b, h: (b, h)),
        out_shape=jax.ShapeDtypeStruct((batch * seq, D_ATTN), BF16),
        scratch_shapes=[bf_scr, bf_scr, bfx_scr, bfx_scr] + [f32_scr] * 8,
        compiler_params=pltpu.CompilerParams(
            dimension_semantics=("arbitrary", "arbitrary"), vmem_limit_bytes=VMEM_LIMIT),
        name="dilated_attn",
    )(qkv, qkv, qkv)


META_ROWS = 16
META_EIDX, META_GATE, META_RANK = 0, 4, 8


def _outproj_kernel(x_ref, yp_ref, ya_ref, wo_ref, g_ref, wrt_ref, brt_ref,
                    x1_ref, h2_ref, meta_ref, cnt_ref, wo_bf, before, carry):
    tm = x_ref.shape[0]

    @pl.when(pl.program_id(0) == 0)
    def _():
        wo_bf[...] = wo_ref[...].astype(BF16)
        carry[...] = jnp.zeros_like(carry)
        ti = lax.broadcasted_iota(I32, (tm, tm), 0)
        tj = lax.broadcasted_iota(I32, (tm, tm), 1)
        before[...] = (ti < tj).astype(BF16)

    x1 = (x_ref[...]
          + jnp.dot(yp_ref[...], wo_bf[:D_POOL, :], preferred_element_type=F32)
          + jnp.dot(ya_ref[...], wo_bf[D_POOL:, :], preferred_element_type=F32))
    x1_ref[...] = x1
    h2 = _rms(x1, g_ref[...])
    h2_ref[...] = _pack_row(h2)

    logits_t = lax.dot_general(wrt_ref[...].astype(BF16), h2.astype(BF16),
                               (((1,), (1,)), ((), ())), preferred_element_type=F32)
    logits_t = logits_t + brt_ref[:, 0:1]
    eid = lax.broadcasted_iota(I32, (N_EXPERTS, tm), 0)
    work = logits_t
    idxs, vals = [], []
    for _ in range(TOP_K):
        mx = jnp.max(work, axis=0, keepdims=True)
        idx = jnp.min(jnp.where(work == mx, eid, N_EXPERTS), axis=0, keepdims=True)
        idxs.append(idx)
        vals.append(mx)
        work = jnp.where(eid == idx, -jnp.inf, work)
    exps = [jnp.exp(v - vals[0]) for v in vals]
    den = exps[0] + exps[1] + exps[2] + exps[3]

    onehot = jnp.zeros((N_EXPERTS, tm), F32)
    for idx in idxs:
        onehot = onehot + (eid == idx).astype(F32)
    rank_e = carry[:, 0:1] + jnp.dot(onehot.astype(BF16), before[...],
                                     preferred_element_type=F32)
    carry[...] = carry[...] + jnp.sum(onehot, axis=1, keepdims=True)
    cnt_ref[...] = carry[...]

    mrow = lax.broadcasted_iota(I32, (META_ROWS, tm), 0)
    meta = jnp.zeros((META_ROWS, tm), F32)
    for k in range(TOP_K):
        rank_k = jnp.sum(jnp.where(eid == idxs[k], rank_e, 0.0), axis=0, keepdims=True)
        meta = jnp.where(mrow == META_EIDX + k, idxs[k].astype(F32), meta)
        meta = jnp.where(mrow == META_GATE + k, exps[k] / den, meta)
        meta = jnp.where(mrow == META_RANK + k, rank_k, meta)
    meta_ref[...] = meta


def _outproj(x2, y_pool, y_attn, w_out, g, w_router_t, b_router_t):
    t = x2.shape[0]
    row = lambda i: (i, 0)
    const = lambda i: (0, 0)
    return pl.pallas_call(
        _outproj_kernel,
        grid=(t // ROW_TILE,),
        in_specs=[
            pl.BlockSpec((ROW_TILE, D_MODEL), row),
            pl.BlockSpec((ROW_TILE, D_POOL), row),
            pl.BlockSpec((ROW_TILE, D_ATTN), row),
            pl.BlockSpec((D_MODEL, D_MODEL), const),
            pl.BlockSpec((1, D_MODEL), const),
            pl.BlockSpec((N_EXPERTS, D_MODEL), const),
            pl.BlockSpec((N_EXPERTS, LANES), const),
        ],
        out_specs=[
            pl.BlockSpec((ROW_TILE, D_MODEL), row),
            pl.BlockSpec((ROW_TILE, D_PACKED), row),
            pl.BlockSpec((META_ROWS, ROW_TILE), lambda i: (0, i)),
            pl.BlockSpec((N_EXPERTS, LANES), const),
        ],
        out_shape=[
            jax.ShapeDtypeStruct((t, D_MODEL), F32),
            jax.ShapeDtypeStruct((t, D_PACKED), I32),
            jax.ShapeDtypeStruct((META_ROWS, t), F32),
            jax.ShapeDtypeStruct((N_EXPERTS, LANES), F32),
        ],
        scratch_shapes=[pltpu.VMEM((D_MODEL, D_MODEL), BF16),
                        pltpu.VMEM((ROW_TILE, ROW_TILE), BF16),
                        pltpu.VMEM((N_EXPERTS, LANES), F32)],
        compiler_params=pltpu.CompilerParams(
            dimension_semantics=("arbitrary",), vmem_limit_bytes=VMEM_LIMIT),
        name="outproj_router",
    )(x2, y_pool, y_attn, w_out, g, w_router_t, b_router_t)


def _sc_mesh():
    return plsc.VectorSubcoreMesh(core_axis_name="core", subcore_axis_name="subcore")


def _sc_dispatch(h2, pos_rows, p_max):
    t = h2.shape[0]

    @functools.partial(
        pl.kernel, mesh=_sc_mesh(),
        out_type=[jax.ShapeDtypeStruct((p_max, D_PART), h2.dtype)] * N_PART)
    def run(h_hbm, *refs):
        pos_hbm, xs_hbm = refs[:TOP_K], refs[TOP_K:]
        for c in range(N_PART):
            def body(x_vmem, *idx_vmem, dst=xs_hbm[c]):
                for iv in idx_vmem:
                    pltpu.sync_copy(x_vmem, dst.at[iv.at[0]])

            pltpu.emit_pipeline(
                body,
                grid=(t // SC_WINDOW,),
                in_specs=[pl.BlockSpec((SC_WINDOW, D_PART), lambda i, c=c: (i, c))]
                + [pl.BlockSpec((1, SC_WINDOW), lambda i: (0, i))] * TOP_K,
                out_specs=[],
                core_axis_name=("core", "subcore"),
                dimension_semantics=(pltpu.PARALLEL,),
            )(h_hbm, *pos_hbm)

    return run(h2, *pos_rows)


def _sc_unpermute(y_parts, idx_row):
    n = idx_row.shape[1]

    @functools.partial(
        pl.kernel, mesh=_sc_mesh(),
        out_type=[jax.ShapeDtypeStruct((n, D_PART), y_parts[0].dtype)] * N_PART)
    def run(*refs):
        y_hbm, i_hbm, o_hbm = refs[:N_PART], refs[N_PART], refs[N_PART + 1:]
        for c in range(N_PART):
            def body(i_vmem, o_vmem, src=y_hbm[c]):
                pltpu.sync_copy(src.at[i_vmem.at[0]], o_vmem)

            pltpu.emit_pipeline(
                body,
                grid=(n // SC_WINDOW,),
                in_specs=[pl.BlockSpec((1, SC_WINDOW), lambda i: (0, i))],
                out_specs=[pl.BlockSpec((SC_WINDOW, D_PART), lambda i: (i, 0))],
                core_axis_name=("core", "subcore"),
                dimension_semantics=(pltpu.PARALLEL,),
            )(i_hbm, o_hbm[c])

    return run(*y_parts, idx_row)


def _moe_kernel(te_ref, nv_ref, rows_ref, nxt_ref, *refs):
    xs_refs = refs[:N_PART]
    bgu_ref, bd_ref, wgu_hbm, wd_hbm = refs[N_PART:N_PART + 4]
    y_refs = refs[N_PART + 4:2 * N_PART + 4]
    wgu_f32, wd_f32, wgu_bf, wd_bf, sem = refs[2 * N_PART + 4:]
    i = pl.program_id(0)

    def weight_copies(e):
        return (pltpu.make_async_copy(wgu_hbm.at[e], wgu_f32, sem.at[0]),
                pltpu.make_async_copy(wd_hbm.at[e], wd_f32, sem.at[1]))

    @pl.when(i < nv_ref[0])
    def _():
        prev = te_ref[jnp.maximum(i - 1, 0)]
        new_expert = jnp.logical_or(i == 0, te_ref[i] != prev)

        @pl.when(i == 0)
        def _():
            for cp in weight_copies(te_ref[0]):
                cp.start()

        @pl.when(new_expert)
        def _():
            for cp in weight_copies(te_ref[i]):
                cp.wait()
            wgu_bf[...] = wgu_f32[...].astype(BF16)
            wd_bf[...] = wd_f32[...].astype(BF16)

            @pl.when(nxt_ref[i] >= 0)
            def _():
                for cp in weight_copies(nxt_ref[i]):
                    cp.start()

        d_e = wd_bf.shape[0]

        def ffn(n_rows):
            x = _unpack_row(jnp.concatenate([r[0:n_rows, :] for r in xs_refs], axis=1))
            rid = lax.broadcasted_iota(I32, (n_rows, 1), 0)
            x = jnp.where(rid < rows_ref[i], x, 0.0).astype(BF16)
            gu = jnp.dot(x, wgu_bf[...], preferred_element_type=F32) + bgu_ref[0]
            gate = jnp.minimum(gu[:, :d_e], SWIGLU_LIMIT)
            lin = jnp.clip(gu[:, d_e:], -SWIGLU_LIMIT, SWIGLU_LIMIT)
            act = gate * jax.nn.sigmoid(SWIGLU_ALPHA * gate) * (lin + 1.0)
            y = jnp.dot(act.astype(BF16), wd_bf[...], preferred_element_type=F32) + bd_ref[0]
            yp = _pack_row(y)
            for c, y_ref in enumerate(y_refs):
                y_ref[0:n_rows, :] = yp[:, c * D_PART:(c + 1) * D_PART]
                if n_rows < MOE_TILE:
                    y_ref[n_rows:, :] = jnp.zeros((MOE_TILE - n_rows, D_PART), I32)

        lower = 0
        for n_rows in MOE_PATHS:
            @pl.when((rows_ref[i] > lower) & (rows_ref[i] <= n_rows))
            def _(n_rows=n_rows):
                ffn(n_rows)
            lower = n_rows


def _moe(tile_expert, n_valid, tile_rows, next_expert, xs_parts, w_gu, b_gu, w_down, b_down):
    p_max = xs_parts[0].shape[0]
    n_tiles = p_max // MOE_TILE
    d_e = w_down.shape[1]

    def row(i, te, nv, tr, nx):
        return (jnp.minimum(i, nv[0] - 1), 0)

    def expert(i, te, nv, tr, nx):
        return (te[jnp.minimum(i, nv[0] - 1)], 0, 0)

    grid_spec = pltpu.PrefetchScalarGridSpec(
        num_scalar_prefetch=4,
        grid=(n_tiles,),
        in_specs=[pl.BlockSpec((MOE_TILE, D_PART), row)] * N_PART + [
            pl.BlockSpec((1, 1, 2 * d_e), expert),
            pl.BlockSpec((1, 1, D_MODEL), expert),
            pl.BlockSpec(memory_space=pl.ANY),
            pl.BlockSpec(memory_space=pl.ANY),
        ],
        out_specs=[pl.BlockSpec((MOE_TILE, D_PART), row)] * N_PART,
        scratch_shapes=[pltpu.VMEM((D_MODEL, 2 * d_e), F32), pltpu.VMEM((d_e, D_MODEL), F32),
                        pltpu.VMEM((D_MODEL, 2 * d_e), BF16), pltpu.VMEM((d_e, D_MODEL), BF16),
                        pltpu.SemaphoreType.DMA((2,))],
    )
    return pl.pallas_call(
        _moe_kernel,
        grid_spec=grid_spec,
        out_shape=[jax.ShapeDtypeStruct((p_max, D_PART), I32)] * N_PART,
        compiler_params=pltpu.CompilerParams(
            dimension_semantics=("arbitrary",), vmem_limit_bytes=VMEM_LIMIT),
        name="moe_ffn",
    )(tile_expert, n_valid, tile_rows, next_expert, *xs_parts, b_gu, b_down, w_gu, w_down)


def _combine_kernel(x1_ref, gate_ref, g_ref, *refs):
    yk_refs, o_ref = refs[:N_PART], refs[-1]
    acc = x1_ref[...]
    for k in range(TOP_K):
        y_k = _unpack_row(jnp.concatenate([r[k] for r in yk_refs], axis=1))
        acc = acc + gate_ref[:, k:k + 1] * y_k
    o_ref[...] = _rms(acc, g_ref[...])


def _combine(x1, gates, g, yk_parts):
    t, d = x1.shape
    tq = COMBINE_TILE
    row = lambda i: (i, 0)
    return pl.pallas_call(
        _combine_kernel,
        grid=(t // tq,),
        in_specs=[
            pl.BlockSpec((tq, d), row),
            pl.BlockSpec((tq, TOP_K), row),
            pl.BlockSpec((1, d), lambda i: (0, 0)),
        ] + [pl.BlockSpec((TOP_K, tq, D_PART), lambda i: (0, i, 0))] * N_PART,
        out_specs=pl.BlockSpec((tq, d), row),
        out_shape=jax.ShapeDtypeStruct((t, d), F32),
        compiler_params=pltpu.CompilerParams(
            dimension_semantics=("arbitrary",), vmem_limit_bytes=VMEM_LIMIT),
        name="combine_final",
    )(x1, gates, g, *yk_parts)


def kernel(x, g_mix, w_in, w_pool, pool_scale, w_out, g_ffn, w_router, b_router,
           w_gu, b_gu, w_down, b_down, g_final):
    batch, seq, d = x.shape
    t = batch * seq
    assert d == D_MODEL and seq == DIL_FAR * QB, (x.shape, "unsupported shape")
    assert seq % ROW_TILE == 0 and t % COMBINE_TILE == 0 and t % SC_WINDOW == 0
    assert w_in.shape[0] == 1, "one layer"
    x2 = x.reshape(t, d)

    y_pool, qkv = _inproj(x2, g_mix[0].reshape(1, d), w_in[0], w_pool[0],
                          pool_scale[0].reshape(1, D_POOL), seq)
    y_attn = _attention(qkv, batch, seq)

    wr_t = w_router[0].T
    br_t = jnp.broadcast_to(b_router[0].reshape(N_EXPERTS, 1), (N_EXPERTS, LANES))
    x1, h2, meta, cnt = _outproj(x2, y_pool, y_attn, w_out[0], g_ffn[0].reshape(1, d), wr_t, br_t)

    eidx = meta[META_EIDX:META_EIDX + TOP_K].astype(I32)
    rank = meta[META_RANK:META_RANK + TOP_K].astype(I32)
    gates = meta[META_GATE:META_GATE + TOP_K].T
    counts = cnt[:, 0].astype(I32)
    padded = ((counts + MOE_TILE - 1) // MOE_TILE) * MOE_TILE
    ends = jnp.cumsum(padded)
    offsets = ends - padded
    e_ids = jnp.arange(N_EXPERTS, dtype=I32)
    hit = eidx[None] == e_ids[:, None, None]
    pos = rank + jnp.sum(jnp.where(hit, offsets[:, None, None], 0), axis=0)
    p_max = t * TOP_K + N_EXPERTS * MOE_TILE
    n_tiles = p_max // MOE_TILE
    tile_start = jnp.arange(n_tiles, dtype=I32) * MOE_TILE
    tile_expert = jnp.minimum(
        jnp.sum((tile_start[:, None] >= ends[None, :]).astype(I32), axis=1), N_EXPERTS - 1)
    mine = tile_expert[:, None] == e_ids[None, :]
    group_end = jnp.sum(jnp.where(mine, (offsets + counts)[None, :], 0), axis=1)
    tile_rows = jnp.clip(group_end - tile_start, 0, MOE_TILE)
    n_valid = (ends[-1] // MOE_TILE).reshape(1).astype(I32)
    later = (e_ids[None, :] > e_ids[:, None]) & (counts[None, :] > 0)
    next_nonempty = jnp.min(jnp.where(later, e_ids[None, :], N_EXPERTS), axis=1)
    next_nonempty = jnp.where(next_nonempty < N_EXPERTS, next_nonempty, -1)
    next_expert = jnp.sum(jnp.where(mine, next_nonempty[None, :], 0), axis=1)

    xs = _sc_dispatch(h2, [pos[k:k + 1] for k in range(TOP_K)], p_max)
    ys = _moe(tile_expert, n_valid, tile_rows, next_expert, xs, w_gu[0],
              b_gu[0].reshape(N_EXPERTS, 1, -1), w_down[0], b_down[0].reshape(N_EXPERTS, 1, -1))
    yk = _sc_unpermute(ys, pos.reshape(1, TOP_K * t))
    yk = [q.reshape(TOP_K, t, D_PART) for q in yk]
    out = _combine(x1, gates, g_final.reshape(1, d), yk)
    return out.reshape(batch, seq, d)
```

```python
import functools

import jax
import jax.numpy as jnp
from jax import lax
from jax.experimental import pallas as pl
from jax.experimental.pallas import tpu as pltpu
from jax.experimental.pallas import tpu_sc as plsc

F32 = jnp.float32
BF16 = jnp.bfloat16
I32 = jnp.int32

D_MODEL = 1024
D_POOL = 512
D_ATTN = 512
POOL_WINDOWS = (2, 4, 8, 16)
POOL_GROUP = 128
HEAD_DIM = 64
N_EXPERTS = 32
TOP_K = 4
SWIGLU_LIMIT = 7.0
SWIGLU_ALPHA = 1.702
EPS = 1e-5
NEG_INF = -1e30
LOG2_E = 1.4426950408889634

LANES = 128
QB = 128
DIL_MID = 4
DIL_FAR = 16
ROW_TILE = 1024
X_RING = 3
MOE_TILE = 1024
MOE_PATHS = (256, 512, 768, 1024)
COMBINE_TILE = 1024
SC_WINDOW = 128
D_PACKED = D_MODEL // 2
N_PART = 2
D_PART = D_PACKED // N_PART
VMEM_LIMIT = 56 * 1024 * 1024


def _pack_row(x):
    hi = lax.bitcast_convert_type(x[:, :D_PACKED].astype(BF16).astype(F32), I32)
    lo = lax.bitcast_convert_type(x[:, D_PACKED:].astype(BF16).astype(F32), I32)
    return hi | lax.shift_right_logical(lo, 16)


def _unpack_row(w):
    hi = lax.bitcast_convert_type(w & jnp.int32(-65536), F32)
    lo = lax.bitcast_convert_type(lax.shift_left(w, 16), F32)
    return jnp.concatenate([hi, lo], axis=1)


def _rms(x, g):
    ms = jnp.mean(x * x, axis=-1, keepdims=True)
    return x * lax.rsqrt(ms + EPS) * g


POOL_PAD = 16


def _inproj_kernel(x_ref, g_ref, w_ref, wp_ref, sc_ref, yp_ref, qkv_ref, w_bf, hist, pad_ref,
                   *, tiles_per_seq):
    i = pl.program_id(0)
    tm = x_ref.shape[0]

    @pl.when(i == 0)
    def _():
        w_bf[...] = w_ref[...].astype(BF16)
        pad_ref[0:POOL_PAD, :] = jnp.zeros((POOL_PAD, POOL_GROUP), F32)

    @pl.when(i % tiles_per_seq == 0)
    def _():
        hist[...] = jnp.zeros_like(hist)

    h = _rms(x_ref[...], g_ref[...]).astype(BF16)
    proj = jnp.dot(h, w_bf[...], preferred_element_type=F32)
    q = proj[:, D_POOL:D_POOL + D_ATTN] * (HEAD_DIM ** -0.5 * LOG2_E)
    qkv_ref[:, :D_ATTN] = q.astype(BF16)
    qkv_ref[:, D_ATTN:] = proj[:, D_POOL + D_ATTN:].astype(BF16)

    pos = (i % tiles_per_seq) * tm + lax.broadcasted_iota(I32, (tm, 1), 0)
    for g, w in enumerate(POOL_WINDOWS):
        lo, hi = g * POOL_GROUP, (g + 1) * POOL_GROUP
        e = proj[:, lo:hi]
        acc = jnp.concatenate([hist[g], e], axis=0)
        hist[g] = e[tm - POOL_PAD:, :]
        span = 1
        while span < w:
            pad_ref[POOL_PAD:, :] = acc
            acc = acc + pad_ref[pl.ds(POOL_PAD - span, POOL_PAD + tm), :]
            span *= 2
        count = jnp.minimum(pos + 1, w).astype(F32)
        pooled = acc[POOL_PAD:, :] / count - e
        y = jnp.dot(pooled.astype(BF16), wp_ref[g].astype(BF16), preferred_element_type=F32)
        yp_ref[:, lo:hi] = (y * sc_ref[:, lo:hi]).astype(BF16)


def _inproj(x2, g, w, w_pool, pool_scale, seq):
    t = x2.shape[0]
    d_in = w.shape[1]
    n_grp = len(POOL_WINDOWS)
    return pl.pallas_call(
        functools.partial(_inproj_kernel, tiles_per_seq=seq // ROW_TILE),
        grid=(t // ROW_TILE,),
        in_specs=[
            pl.BlockSpec((ROW_TILE, D_MODEL), lambda i: (i, 0)),
            pl.BlockSpec((1, D_MODEL), lambda i: (0, 0)),
            pl.BlockSpec((D_MODEL, d_in), lambda i: (0, 0)),
            pl.BlockSpec((n_grp, POOL_GROUP, POOL_GROUP), lambda i: (0, 0, 0)),
            pl.BlockSpec((1, D_POOL), lambda i: (0, 0)),
        ],
        out_specs=[
            pl.BlockSpec((ROW_TILE, D_POOL), lambda i: (i, 0)),
            pl.BlockSpec((ROW_TILE, 3 * D_ATTN), lambda i: (i, 0)),
        ],
        out_shape=[
            jax.ShapeDtypeStruct((t, D_POOL), BF16),
            jax.ShapeDtypeStruct((t, 3 * D_ATTN), BF16),
        ],
        scratch_shapes=[pltpu.VMEM((D_MODEL, d_in), BF16),
                        pltpu.VMEM((n_grp, POOL_PAD, POOL_GROUP), F32),
                        pltpu.VMEM((2 * POOL_PAD + ROW_TILE, POOL_GROUP), F32)],
        compiler_params=pltpu.CompilerParams(
            dimension_semantics=("arbitrary",), vmem_limit_bytes=VMEM_LIMIT),
        name="inproj_pool",
    )(x2, g, w, w_pool, pool_scale)


def _pair_block(q, k, v_ext, mask2, lane_h0):
    zero = jnp.zeros_like(q)
    q2 = jnp.concatenate([jnp.where(lane_h0, q, zero), jnp.where(lane_h0, zero, q)], axis=0)
    s = lax.dot_general(q2, k, (((1,), (1,)), ((), ())), preferred_element_type=F32)
    s = jnp.where(mask2, s, NEG_INF)
    m = jnp.max(s, axis=-1, keepdims=True)
    p = jnp.exp2(s - m).astype(BF16)
    ol = jnp.dot(p, v_ext, preferred_element_type=F32)
    o = jnp.where(lane_h0, ol[:QB, :LANES], ol[QB:, :LANES])
    l = jnp.where(lane_h0, ol[:QB, LANES:], ol[QB:, LANES:])
    mb = jnp.where(lane_h0, m[:QB], m[QB:])
    return o, mb, l


def _merge(o_a, m_a, l_a, o_b, m_b, l_b):
    m = jnp.maximum(m_a, m_b)
    ea = jnp.exp2(m_a - m)
    eb = jnp.exp2(m_b - m)
    return o_a * ea + o_b * eb, m, l_a * ea + l_b * eb


def _both_heads(mask):
    return jnp.concatenate([mask, mask], axis=0)


def _attn_kernel(q_ref, k_ref, v_ref, o_ref, qp, kp, vpx, vnx, o23, m23, l23, o2, m2, l2,
                 on, mn):
    s = q_ref.shape[0]
    n_chunk = DIL_FAR
    rows = s // n_chunk
    sub = rows // DIL_MID
    grp = n_chunk * n_chunk
    lane_h0 = lax.broadcasted_iota(I32, (1, LANES), 1) < HEAD_DIM

    @pl.when((pl.program_id(0) == 0) & (pl.program_id(1) == 0))
    def _():
        ones = jnp.ones((s, LANES), BF16)
        vnx[:, LANES:] = ones
        vpx[:, LANES:] = ones

    vnx[:, :LANES] = v_ref[...]

    pi = lax.broadcasted_iota(I32, (grp, grp), 0)
    pj = lax.broadcasted_iota(I32, (grp, grp), 1)
    swap = (pj == (pi % n_chunk) * n_chunk + pi // n_chunk).astype(BF16)
    for g in range(s // grp):
        src = slice(g * grp, (g + 1) * grp)
        qk = jnp.concatenate([q_ref[src, :], k_ref[src, :]], axis=1)
        yqk = jnp.dot(swap, qk, preferred_element_type=F32).astype(BF16)
        yv = jnp.dot(swap, v_ref[src, :], preferred_element_type=F32).astype(BF16)
        for r in range(n_chunk):
            dst = slice(r * rows + g * n_chunk, r * rows + (g + 1) * n_chunk)
            part = slice(r * n_chunk, (r + 1) * n_chunk)
            qp[dst, :] = yqk[part, :LANES]
            kp[dst, :] = yqk[part, LANES:]
            vpx[dst, 0:LANES] = yv[part]

    qi = lax.broadcasted_iota(I32, (QB, QB), 0)
    kj = lax.broadcasted_iota(I32, (QB, QB), 1)
    causal = _both_heads(qi >= kj)

    for r in range(n_chunk):
        blk = slice(r * rows, (r + 1) * rows)
        o, m, l = _pair_block(qp[blk, :], kp[blk, :], vpx[blk, :], causal, lane_h0)
        o23[blk, :] = o
        m23[blk, :] = m
        l23[blk, :] = l

    def mid_index(n_key_sub):
        i_q = lax.broadcasted_iota(I32, (QB, DIL_MID * n_key_sub), 0)
        i_k = lax.broadcasted_iota(I32, (QB, DIL_MID * n_key_sub), 1)
        return i_q // sub, i_q % sub, i_k // n_key_sub, i_k % n_key_sub

    jq, aq, jk, ak = mid_index(sub)
    d0 = DIL_FAR * (aq - ak) + DIL_MID * (jq - jk)
    mask_mid0 = _both_heads(d0 >= 0)
    jq, aq, jk, ak = mid_index(2 * sub)
    d1 = DIL_FAR * (aq + sub - ak) + DIL_MID * (jq - jk)
    mask_mid = _both_heads((d1 >= 0) & (d1 <= DIL_MID * QB))

    def mid_tile(r4, a0, k0, nk, mask):
        def at(j, off, n):
            return pl.ds((DIL_MID * j + r4) * rows + off, n)

        q = jnp.concatenate([qp[at(j, a0, sub), :] for j in range(DIL_MID)], axis=0)
        k = jnp.concatenate([kp[at(j, k0, nk), :] for j in range(DIL_MID)], axis=0)
        v = jnp.concatenate([vpx[at(j, k0, nk), :] for j in range(DIL_MID)], axis=0)
        o, m, l = _pair_block(q, k, v, mask, lane_h0)
        for j in range(DIL_MID):
            dst = at(j, a0, sub)
            src = slice(j * sub, (j + 1) * sub)
            o2[dst, :] = o[src]
            m2[dst, :] = m[src]
            l2[dst, :] = l[src]

    for r4 in range(DIL_MID):
        mid_tile(r4, 0, 0, sub, mask_mid0)
        for a_blk in range(1, rows // sub):
            mid_tile(r4, a_blk * sub, (a_blk - 1) * sub, 2 * sub, mask_mid)

    for r in range(n_chunk):
        blk = slice(r * rows, (r + 1) * rows)
        oo, mm, ll = _merge(o23[blk, :], m23[blk, :], l23[blk, :],
                            o2[blk, :], m2[blk, :], l2[blk, :])
        o23[blk, :] = oo / ll
        m23[blk, :] = mm + jnp.log2(ll)

    for g in range(s // grp):
        slabs = [o23[r * rows + g * n_chunk:r * rows + (g + 1) * n_chunk, :].astype(BF16)
                 for r in range(n_chunk)]
        on[g * grp:(g + 1) * grp, :] = jnp.dot(swap, jnp.concatenate(slabs, axis=0),
                                               preferred_element_type=F32)
    for r in range(n_chunk):
        mn[pl.ds(r, rows, stride=n_chunk), :] = m23[r * rows:(r + 1) * rows, :]

    qi2 = lax.broadcasted_iota(I32, (QB, 2 * QB), 0)
    kj2 = lax.broadcasted_iota(I32, (QB, 2 * QB), 1)
    dn = qi2 + QB - kj2
    mask_near = _both_heads((dn >= 0) & (dn <= QB))

    def near_finish(dst, o, m, l):
        oo, _, ll = _merge(on[dst, :], mn[dst, :], 1.0, o, m, l)
        o_ref[dst, :] = (oo / ll).astype(o_ref.dtype)

    first = pl.ds(0, QB)
    o, m, l = _pair_block(q_ref[first, :], k_ref[first, :], vnx[first, :], causal, lane_h0)
    near_finish(first, o, m, l)
    for n in range(1, s // QB):
        keys = pl.ds((n - 1) * QB, 2 * QB)
        o, m, l = _pair_block(q_ref[pl.ds(n * QB, QB), :], k_ref[keys, :], vnx[keys, :],
                              mask_near, lane_h0)
        near_finish(pl.ds(n * QB, QB), o, m, l)


def _attention(qkv, batch, seq):
    n_pair = D_ATTN // LANES
    blk = (seq, LANES)
    f32_scr = pltpu.VMEM(blk, F32)
    bf_scr = pltpu.VMEM(blk, BF16)
    bfx_scr = pltpu.VMEM((seq, 2 * LANES), BF16)
    return pl.pallas_call(
        _attn_kernel,
        grid=(batch, n_pair),
        in_specs=[
            pl.BlockSpec(blk, lambda b, h: (b, h)),
            pl.BlockSpec(blk, lambda b, h: (b, n_pair + h)),
            pl.BlockSpec(blk, lambda b, h: (b, 2 * n_pair + h)),
        ],
        out_specs=pl.BlockSpec(blk, lambda b, h: (b, h)),
        out_shape=jax.ShapeDtypeStruct((batch * seq, D_ATTN), BF16),
        scratch_shapes=[bf_scr, bf_scr, bfx_scr, bfx_scr] + [f32_scr] * 8,
        compiler_params=pltpu.CompilerParams(
            dimension_semantics=("arbitrary", "arbitrary"), vmem_limit_bytes=VMEM_LIMIT),
        name="dilated_attn",
    )(qkv, qkv, qkv)


META_ROWS = 16
META_EIDX, META_GATE, META_RANK = 0, 4, 8


def _outproj_kernel(yp_ref, ya_ref, wo_ref, g_ref, wrt_ref, brt_ref, x_hbm,
                    x1_ref, h2_ref, meta_ref, cnt_ref, wo_bf, before, carry, x_ring, x_sem):
    tm = yp_ref.shape[0]
    i = pl.program_id(0)
    n = pl.num_programs(0)

    def x_copy(step):
        slot = step % X_RING
        return pltpu.make_async_copy(x_hbm.at[pl.ds(step * tm, tm), :], x_ring.at[slot],
                                     x_sem.at[slot])

    @pl.when(i == 0)
    def _():
        for step in range(X_RING - 1):
            x_copy(step).start()
        wo_bf[...] = wo_ref[...].astype(BF16)
        carry[...] = jnp.zeros_like(carry)
        ti = lax.broadcasted_iota(I32, (tm, tm), 0)
        tj = lax.broadcasted_iota(I32, (tm, tm), 1)
        before[...] = (ti < tj).astype(BF16)

    @pl.when(i + X_RING - 1 < n)
    def _():
        x_copy(i + X_RING - 1).start()

    x_copy(i).wait()
    x1 = (x_ring[i % X_RING]
          + jnp.dot(yp_ref[...], wo_bf[:D_POOL, :], preferred_element_type=F32)
          + jnp.dot(ya_ref[...], wo_bf[D_POOL:, :], preferred_element_type=F32))
    x1_ref[...] = x1
    h2 = _rms(x1, g_ref[...])
    h2_ref[...] = _pack_row(h2)

    logits_t = lax.dot_general(wrt_ref[...].astype(BF16), h2.astype(BF16),
                               (((1,), (1,)), ((), ())), preferred_element_type=F32)
    logits_t = logits_t + brt_ref[:, 0:1]
    eid = lax.broadcasted_iota(I32, (N_EXPERTS, tm), 0)
    work = logits_t
    idxs, vals = [], []
    for _ in range(TOP_K):
        mx = jnp.max(work, axis=0, keepdims=True)
        idx = jnp.min(jnp.where(work == mx, eid, N_EXPERTS), axis=0, keepdims=True)
        idxs.append(idx)
        vals.append(mx)
        work = jnp.where(eid == idx, -jnp.inf, work)
    exps = [jnp.exp(v - vals[0]) for v in vals]
    den = exps[0] + exps[1] + exps[2] + exps[3]

    onehot = jnp.zeros((N_EXPERTS, tm), F32)
    for idx in idxs:
        onehot = onehot + (eid == idx).astype(F32)
    rank_e = carry[:, 0:1] + jnp.dot(onehot.astype(BF16), before[...],
                                     preferred_element_type=F32)
    carry[...] = carry[...] + jnp.sum(onehot, axis=1, keepdims=True)
    cnt_ref[...] = carry[...]

    mrow = lax.broadcasted_iota(I32, (META_ROWS, tm), 0)
    meta = jnp.zeros((META_ROWS, tm), F32)
    for k in range(TOP_K):
        rank_k = jnp.sum(jnp.where(eid == idxs[k], rank_e, 0.0), axis=0, keepdims=True)
        meta = jnp.where(mrow == META_EIDX + k, idxs[k].astype(F32), meta)
        meta = jnp.where(mrow == META_GATE + k, exps[k] / den, meta)
        meta = jnp.where(mrow == META_RANK + k, rank_k, meta)
    meta_ref[...] = meta


def _outproj(x2, y_pool, y_attn, w_out, g, w_router_t, b_router_t):
    t = x2.shape[0]
    row = lambda i: (i, 0)
    const = lambda i: (0, 0)
    return pl.pallas_call(
        _outproj_kernel,
        grid=(t // ROW_TILE,),
        in_specs=[
            pl.BlockSpec((ROW_TILE, D_POOL), row),
            pl.BlockSpec((ROW_TILE, D_ATTN), row),
            pl.BlockSpec((D_MODEL, D_MODEL), const),
            pl.BlockSpec((1, D_MODEL), const),
            pl.BlockSpec((N_EXPERTS, D_MODEL), const),
            pl.BlockSpec((N_EXPERTS, LANES), const),
            pl.BlockSpec(memory_space=pl.ANY),
        ],
        out_specs=[
            pl.BlockSpec((ROW_TILE, D_MODEL), row),
            pl.BlockSpec((ROW_TILE, D_PACKED), row),
            pl.BlockSpec((META_ROWS, ROW_TILE), lambda i: (0, i)),
            pl.BlockSpec((N_EXPERTS, LANES), const),
        ],
        out_shape=[
            jax.ShapeDtypeStruct((t, D_MODEL), F32),
            jax.ShapeDtypeStruct((t, D_PACKED), I32),
            jax.ShapeDtypeStruct((META_ROWS, t), F32),
            jax.ShapeDtypeStruct((N_EXPERTS, LANES), F32),
        ],
        scratch_shapes=[pltpu.VMEM((D_MODEL, D_MODEL), BF16),
                        pltpu.VMEM((ROW_TILE, ROW_TILE), BF16),
                        pltpu.VMEM((N_EXPERTS, LANES), F32),
                        pltpu.VMEM((X_RING, ROW_TILE, D_MODEL), F32),
                        pltpu.SemaphoreType.DMA((X_RING,))],
        compiler_params=pltpu.CompilerParams(
            dimension_semantics=("arbitrary",), vmem_limit_bytes=VMEM_LIMIT),
        name="outproj_router",
    )(y_pool, y_attn, w_out, g, w_router_t, b_router_t, x2)


def _sc_mesh():
    return plsc.VectorSubcoreMesh(core_axis_name="core", subcore_axis_name="subcore")


def _sc_dispatch(h2, pos_rows, p_max):
    t = h2.shape[0]

    @functools.partial(
        pl.kernel, mesh=_sc_mesh(),
        out_type=[jax.ShapeDtypeStruct((p_max, D_PART), h2.dtype)] * N_PART)
    def run(h_hbm, *refs):
        pos_hbm, xs_hbm = refs[:TOP_K], refs[TOP_K:]
        for c in range(N_PART):
            def body(x_vmem, *idx_vmem, dst=xs_hbm[c]):
                for iv in idx_vmem:
                    pltpu.sync_copy(x_vmem, dst.at[iv.at[0]])

            pltpu.emit_pipeline(
                body,
                grid=(t // SC_WINDOW,),
                in_specs=[pl.BlockSpec((SC_WINDOW, D_PART), lambda i, c=c: (i, c))]
                + [pl.BlockSpec((1, SC_WINDOW), lambda i: (0, i))] * TOP_K,
                out_specs=[],
                core_axis_name=("core", "subcore"),
                dimension_semantics=(pltpu.PARALLEL,),
            )(h_hbm, *pos_hbm)

    return run(h2, *pos_rows)


def _sc_unpermute(y_parts, idx_row):
    n = idx_row.shape[1]

    @functools.partial(
        pl.kernel, mesh=_sc_mesh(),
        out_type=[jax.ShapeDtypeStruct((n, D_PART), y_parts[0].dtype)] * N_PART)
    def run(*refs):
        y_hbm, i_hbm, o_hbm = refs[:N_PART], refs[N_PART], refs[N_PART + 1:]
        for c in range(N_PART):
            def body(i_vmem, o_vmem, src=y_hbm[c]):
                pltpu.sync_copy(src.at[i_vmem.at[0]], o_vmem)

            pltpu.emit_pipeline(
                body,
                grid=(n // SC_WINDOW,),
                in_specs=[pl.BlockSpec((1, SC_WINDOW), lambda i: (0, i))],
                out_specs=[pl.BlockSpec((SC_WINDOW, D_PART), lambda i: (i, 0))],
                core_axis_name=("core", "subcore"),
                dimension_semantics=(pltpu.PARALLEL,),
            )(i_hbm, o_hbm[c])

    return run(*y_parts, idx_row)


def _moe_kernel(te_ref, nv_ref, rows_ref, nxt_ref, *refs):
    xs_refs = refs[:N_PART]
    bgu_ref, bd_ref, wgu_hbm, wd_hbm = refs[N_PART:N_PART + 4]
    y_refs = refs[N_PART + 4:2 * N_PART + 4]
    wgu_f32, wd_f32, wgu_bf, wd_bf, sem = refs[2 * N_PART + 4:]
    i = pl.program_id(0)

    def weight_copies(e):
        return (pltpu.make_async_copy(wgu_hbm.at[e], wgu_f32, sem.at[0]),
                pltpu.make_async_copy(wd_hbm.at[e], wd_f32, sem.at[1]))

    @pl.when(i < nv_ref[0])
    def _():
        prev = te_ref[jnp.maximum(i - 1, 0)]
        new_expert = jnp.logical_or(i == 0, te_ref[i] != prev)

        @pl.when(i == 0)
        def _():
            for cp in weight_copies(te_ref[0]):
                cp.start()

        @pl.when(new_expert)
        def _():
            for cp in weight_copies(te_ref[i]):
                cp.wait()
            wgu_bf[...] = wgu_f32[...].astype(BF16)
            wd_bf[...] = wd_f32[...].astype(BF16)

            @pl.when(nxt_ref[i] >= 0)
            def _():
                for cp in weight_copies(nxt_ref[i]):
                    cp.start()

        d_e = wd_bf.shape[0]

        def ffn(n_rows):
            x = _unpack_row(jnp.concatenate([r[0:n_rows, :] for r in xs_refs], axis=1))
            rid = lax.broadcasted_iota(I32, (n_rows, 1), 0)
            x = jnp.where(rid < rows_ref[i], x, 0.0).astype(BF16)
            gu = jnp.dot(x, wgu_bf[...], preferred_element_type=F32) + bgu_ref[0]
            gate = jnp.minimum(gu[:, :d_e], SWIGLU_LIMIT)
            lin = jnp.clip(gu[:, d_e:], -SWIGLU_LIMIT, SWIGLU_LIMIT)
            act = gate * jax.nn.sigmoid(SWIGLU_ALPHA * gate) * (lin + 1.0)
            y = jnp.dot(act.astype(BF16), wd_bf[...], preferred_element_type=F32) + bd_ref[0]
            yp = _pack_row(y)
            for c, y_ref in enumerate(y_refs):
                y_ref[0:n_rows, :] = yp[:, c * D_PART:(c + 1) * D_PART]
                if n_rows < MOE_TILE:
                    y_ref[n_rows:, :] = jnp.zeros((MOE_TILE - n_rows, D_PART), I32)

        lower = 0
        for n_rows in MOE_PATHS:
            @pl.when((rows_ref[i] > lower) & (rows_ref[i] <= n_rows))
            def _(n_rows=n_rows):
                ffn(n_rows)
            lower = n_rows


def _moe(tile_expert, n_valid, tile_rows, next_expert, xs_parts, w_gu, b_gu, w_down, b_down):
    p_max = xs_parts[0].shape[0]
    n_tiles = p_max // MOE_TILE
    d_e = w_down.shape[1]

    def row(i, te, nv, tr, nx):
        return (jnp.minimum(i, nv[0] - 1), 0)

    def expert(i, te, nv, tr, nx):
        return (te[jnp.minimum(i, nv[0] - 1)], 0, 0)

    grid_spec = pltpu.PrefetchScalarGridSpec(
        num_scalar_prefetch=4,
        grid=(n_tiles,),
        in_specs=[pl.BlockSpec((MOE_TILE, D_PART), row)] * N_PART + [
            pl.BlockSpec((1, 1, 2 * d_e), expert),
            pl.BlockSpec((1, 1, D_MODEL), expert),
            pl.BlockSpec(memory_space=pl.ANY),
            pl.BlockSpec(memory_space=pl.ANY),
        ],
        out_specs=[pl.BlockSpec((MOE_TILE, D_PART), row)] * N_PART,
        scratch_shapes=[pltpu.VMEM((D_MODEL, 2 * d_e), F32), pltpu.VMEM((d_e, D_MODEL), F32),
                        pltpu.VMEM((D_MODEL, 2 * d_e), BF16), pltpu.VMEM((d_e, D_MODEL), BF16),
                        pltpu.SemaphoreType.DMA((2,))],
    )
    return pl.pallas_call(
        _moe_kernel,
        grid_spec=grid_spec,
        out_shape=[jax.ShapeDtypeStruct((p_max, D_PART), I32)] * N_PART,
        compiler_params=pltpu.CompilerParams(
            dimension_semantics=("arbitrary",), vmem_limit_bytes=VMEM_LIMIT),
        name="moe_ffn",
    )(tile_expert, n_valid, tile_rows, next_expert, *xs_parts, b_gu, b_down, w_gu, w_down)


def _combine_kernel(x1_ref, gate_ref, g_ref, *refs):
    yk_refs, o_ref = refs[:N_PART], refs[-1]
    acc = x1_ref[...]
    for k in range(TOP_K):
        y_k = _unpack_row(jnp.concatenate([r[k] for r in yk_refs], axis=1))
        acc = acc + gate_ref[:, k:k + 1] * y_k
    o_ref[...] = _rms(acc, g_ref[...])


def _combine(x1, gates, g, yk_parts):
    t, d = x1.shape
    tq = COMBINE_TILE
    row = lambda i: (i, 0)
    return pl.pallas_call(
        _combine_kernel,
        grid=(t // tq,),
        in_specs=[
            pl.BlockSpec((tq, d), row),
            pl.BlockSpec((tq, TOP_K), row),
            pl.BlockSpec((1, d), lambda i: (0, 0)),
        ] + [pl.BlockSpec((TOP_K, tq, D_PART), lambda i: (0, i, 0))] * N_PART,
        out_specs=pl.BlockSpec((tq, d), row),
        out_shape=jax.ShapeDtypeStruct((t, d), F32),
        compiler_params=pltpu.CompilerParams(
            dimension_semantics=("arbitrary",), vmem_limit_bytes=VMEM_LIMIT),
        name="combine_final",
    )(x1, gates, g, *yk_parts)


def kernel(x, g_mix, w_in, w_pool, pool_scale, w_out, g_ffn, w_router, b_router,
           w_gu, b_gu, w_down, b_down, g_final):
    batch, seq, d = x.shape
    t = batch * seq
    assert d == D_MODEL and seq == DIL_FAR * QB, (x.shape, "unsupported shape")
    assert seq % ROW_TILE == 0 and t % COMBINE_TILE == 0 and t % SC_WINDOW == 0
    assert w_in.shape[0] == 1, "one layer"
    x2 = x.reshape(t, d)

    y_pool, qkv = _inproj(x2, g_mix[0].reshape(1, d), w_in[0], w_pool[0],
                          pool_scale[0].reshape(1, D_POOL), seq)
    y_attn = _attention(qkv, batch, seq)

    wr_t = w_router[0].T
    br_t = jnp.broadcast_to(b_router[0].reshape(N_EXPERTS, 1), (N_EXPERTS, LANES))
    x1, h2, meta, cnt = _outproj(x2, y_pool, y_attn, w_out[0], g_ffn[0].reshape(1, d), wr_t, br_t)

    eidx = meta[META_EIDX:META_EIDX + TOP_K].astype(I32)
    rank = meta[META_RANK:META_RANK + TOP_K].astype(I32)
    gates = meta[META_GATE:META_GATE + TOP_K].T
    counts = cnt[:, 0].astype(I32)
    padded = ((counts + MOE_TILE - 1) // MOE_TILE) * MOE_TILE
    ends = jnp.cumsum(padded)
    offsets = ends - padded
    e_ids = jnp.arange(N_EXPERTS, dtype=I32)
    hit = eidx[None] == e_ids[:, None, None]
    pos = rank + jnp.sum(jnp.where(hit, offsets[:, None, None], 0), axis=0)
    p_max = t * TOP_K + N_EXPERTS * MOE_TILE
    n_tiles = p_max // MOE_TILE
    tile_start = jnp.arange(n_tiles, dtype=I32) * MOE_TILE
    tile_expert = jnp.minimum(
        jnp.sum((tile_start[:, None] >= ends[None, :]).astype(I32), axis=1), N_EXPERTS - 1)
    mine = tile_expert[:, None] == e_ids[None, :]
    group_end = jnp.sum(jnp.where(mine, (offsets + counts)[None, :], 0), axis=1)
    tile_rows = jnp.clip(group_end - tile_start, 0, MOE_TILE)
    n_valid = (ends[-1] // MOE_TILE).reshape(1).astype(I32)
    later = (e_ids[None, :] > e_ids[:, None]) & (counts[None, :] > 0)
    next_nonempty = jnp.min(jnp.where(later, e_ids[None, :], N_EXPERTS), axis=1)
    next_nonempty = jnp.where(next_nonempty < N_EXPERTS, next_nonempty, -1)
    next_expert = jnp.sum(jnp.where(mine, next_nonempty[None, :], 0), axis=1)

    xs = _sc_dispatch(h2, [pos[k:k + 1] for k in range(TOP_K)], p_max)
    ys = _moe(tile_expert, n_valid, tile_rows, next_expert, xs, w_gu[0],
              b_gu[0].reshape(N_EXPERTS, 1, -1), w_down[0], b_down[0].reshape(N_EXPERTS, 1, -1))
    yk = _sc_unpermute(ys, pos.reshape(1, TOP_K * t))
    yk = [q.reshape(TOP_K, t, D_PART) for q in yk]
    out = _combine(x1, gates, g_final.reshape(1, d), yk)
    return out.reshape(batch, seq, d)
```

```python
import functools

import jax
import jax.numpy as jnp
from jax import lax
from jax.experimental import pallas as pl
from jax.experimental.pallas import tpu as pltpu
from jax.experimental.pallas import tpu_sc as plsc

F32 = jnp.float32
BF16 = jnp.bfloat16
I32 = jnp.int32

D_MODEL = 1024
D_POOL = 512
D_ATTN = 512
POOL_WINDOWS = (2, 4, 8, 16)
POOL_GROUP = 128
HEAD_DIM = 64
N_EXPERTS = 32
TOP_K = 4
SWIGLU_LIMIT = 7.0
SWIGLU_ALPHA = 1.702
EPS = 1e-5
NEG_INF = -1e30
LOG2_E = 1.4426950408889634

LANES = 128
QB = 128
DIL_MID = 4
DIL_FAR = 16
ROW_TILE = 1024
MOE_TILE = 1024
MOE_PATHS = (256, 512, 768, 1024)
COMBINE_TILE = 1024
SC_WINDOW = 128
D_PACKED = D_MODEL // 2
N_PART = 2
D_PART = D_PACKED // N_PART
VMEM_LIMIT = 56 * 1024 * 1024


def _pack_row(x):
    hi = lax.bitcast_convert_type(x[:, :D_PACKED].astype(BF16).astype(F32), I32)
    lo = lax.bitcast_convert_type(x[:, D_PACKED:].astype(BF16).astype(F32), I32)
    return hi | lax.shift_right_logical(lo, 16)


def _unpack_row(w):
    hi = lax.bitcast_convert_type(w & jnp.int32(-65536), F32)
    lo = lax.bitcast_convert_type(lax.shift_left(w, 16), F32)
    return jnp.concatenate([hi, lo], axis=1)


def _rms(x, g):
    ms = jnp.mean(x * x, axis=-1, keepdims=True)
    return x * lax.rsqrt(ms + EPS) * g


POOL_PAD = 16


def _inproj_kernel(x_ref, g_ref, w_ref, wp_ref, sc_ref, yp_ref, qkv_ref, w_bf, hist, pad_ref,
                   *, tiles_per_seq):
    i = pl.program_id(0)
    tm = x_ref.shape[0]

    @pl.when(i == 0)
    def _():
        w_bf[...] = w_ref[...].astype(BF16)
        pad_ref[0:POOL_PAD, :] = jnp.zeros((POOL_PAD, POOL_GROUP), F32)

    @pl.when(i % tiles_per_seq == 0)
    def _():
        hist[...] = jnp.zeros_like(hist)

    h = _rms(x_ref[...], g_ref[...]).astype(BF16)
    proj = jnp.dot(h, w_bf[...], preferred_element_type=F32)
    q = proj[:, D_POOL:D_POOL + D_ATTN] * (HEAD_DIM ** -0.5 * LOG2_E)
    qkv_ref[:, :D_ATTN] = q.astype(BF16)
    qkv_ref[:, D_ATTN:] = proj[:, D_POOL + D_ATTN:].astype(BF16)

    pos = (i % tiles_per_seq) * tm + lax.broadcasted_iota(I32, (tm, 1), 0)
    for g, w in enumerate(POOL_WINDOWS):
        lo, hi = g * POOL_GROUP, (g + 1) * POOL_GROUP
        e = proj[:, lo:hi]
        acc = jnp.concatenate([hist[g], e], axis=0)
        hist[g] = e[tm - POOL_PAD:, :]
        span = 1
        while span < w:
            pad_ref[POOL_PAD:, :] = acc
            acc = acc + pad_ref[pl.ds(POOL_PAD - span, POOL_PAD + tm), :]
            span *= 2
        count = jnp.minimum(pos + 1, w).astype(F32)
        pooled = acc[POOL_PAD:, :] / count - e
        y = jnp.dot(pooled.astype(BF16), wp_ref[g].astype(BF16), preferred_element_type=F32)
        yp_ref[:, lo:hi] = (y * sc_ref[:, lo:hi]).astype(BF16)


def _inproj(x2, g, w, w_pool, pool_scale, seq):
    t = x2.shape[0]
    d_in = w.shape[1]
    n_grp = len(POOL_WINDOWS)
    return pl.pallas_call(
        functools.partial(_inproj_kernel, tiles_per_seq=seq // ROW_TILE),
        grid=(t // ROW_TILE,),
        in_specs=[
            pl.BlockSpec((ROW_TILE, D_MODEL), lambda i: (i, 0)),
            pl.BlockSpec((1, D_MODEL), lambda i: (0, 0)),
            pl.BlockSpec((D_MODEL, d_in), lambda i: (0, 0)),
            pl.BlockSpec((n_grp, POOL_GROUP, POOL_GROUP), lambda i: (0, 0, 0)),
            pl.BlockSpec((1, D_POOL), lambda i: (0, 0)),
        ],
        out_specs=[
            pl.BlockSpec((ROW_TILE, D_POOL), lambda i: (i, 0)),
            pl.BlockSpec((ROW_TILE, 3 * D_ATTN), lambda i: (i, 0)),
        ],
        out_shape=[
            jax.ShapeDtypeStruct((t, D_POOL), BF16),
            jax.ShapeDtypeStruct((t, 3 * D_ATTN), BF16),
        ],
        scratch_shapes=[pltpu.VMEM((D_MODEL, d_in), BF16),
                        pltpu.VMEM((n_grp, POOL_PAD, POOL_GROUP), F32),
                        pltpu.VMEM((2 * POOL_PAD + ROW_TILE, POOL_GROUP), F32)],
        compiler_params=pltpu.CompilerParams(
            dimension_semantics=("arbitrary",), vmem_limit_bytes=VMEM_LIMIT),
        name="inproj_pool",
    )(x2, g, w, w_pool, pool_scale)


def _pair_block(q, k, v_ext, mask2, lane_h0):
    zero = jnp.zeros_like(q)
    q2 = jnp.concatenate([jnp.where(lane_h0, q, zero), jnp.where(lane_h0, zero, q)], axis=0)
    s = lax.dot_general(q2, k, (((1,), (1,)), ((), ())), preferred_element_type=F32)
    s = jnp.where(mask2, s, NEG_INF)
    m = jnp.max(s, axis=-1, keepdims=True)
    p = jnp.exp2(s - m).astype(BF16)
    ol = jnp.dot(p, v_ext, preferred_element_type=F32)
    o = jnp.where(lane_h0, ol[:QB, :LANES], ol[QB:, :LANES])
    l = jnp.where(lane_h0, ol[:QB, LANES:], ol[QB:, LANES:])
    mb = jnp.where(lane_h0, m[:QB], m[QB:])
    return o, mb, l


def _merge(o_a, m_a, l_a, o_b, m_b, l_b):
    m = jnp.maximum(m_a, m_b)
    ea = jnp.exp2(m_a - m)
    eb = jnp.exp2(m_b - m)
    return o_a * ea + o_b * eb, m, l_a * ea + l_b * eb


def _both_heads(mask):
    return jnp.concatenate([mask, mask], axis=0)


def _attn_kernel(q_ref, k_ref, v_ref, o_ref, qp, kp, vpx, vnx, o23, m23, l23, o2, m2, l2,
                 on, mn):
    s = q_ref.shape[0]
    n_chunk = DIL_FAR
    rows = s // n_chunk
    sub = rows // DIL_MID
    grp = n_chunk * n_chunk
    lane_h0 = lax.broadcasted_iota(I32, (1, LANES), 1) < HEAD_DIM

    @pl.when((pl.program_id(0) == 0) & (pl.program_id(1) == 0))
    def _():
        ones = jnp.ones((s, LANES), BF16)
        vnx[:, LANES:] = ones
        vpx[:, LANES:] = ones

    vnx[:, :LANES] = v_ref[...]

    pi = lax.broadcasted_iota(I32, (grp, grp), 0)
    pj = lax.broadcasted_iota(I32, (grp, grp), 1)
    swap = (pj == (pi % n_chunk) * n_chunk + pi // n_chunk).astype(BF16)
    n_grp = s // grp
    wide = jnp.concatenate([ref[g * grp:(g + 1) * grp, :]
                            for ref in (q_ref, k_ref, v_ref) for g in range(n_grp)], axis=1)
    moved = jnp.dot(swap, wide, preferred_element_type=F32).astype(BF16)
    for which, dst_ref in enumerate((qp, kp, vpx)):
        for g in range(n_grp):
            col = (which * n_grp + g) * LANES
            for r in range(n_chunk):
                dst = slice(r * rows + g * n_chunk, r * rows + (g + 1) * n_chunk)
                dst_ref[dst, 0:LANES] = moved[r * n_chunk:(r + 1) * n_chunk, col:col + LANES]

    qi = lax.broadcasted_iota(I32, (QB, QB), 0)
    kj = lax.broadcasted_iota(I32, (QB, QB), 1)
    causal = _both_heads(qi >= kj)

    def mid_index(n_key_sub):
        i_q = lax.broadcasted_iota(I32, (QB, DIL_MID * n_key_sub), 0)
        i_k = lax.broadcasted_iota(I32, (QB, DIL_MID * n_key_sub), 1)
        return i_q // sub, i_q % sub, i_k // n_key_sub, i_k % n_key_sub

    jq, aq, jk, ak = mid_index(sub)
    d0 = DIL_FAR * (aq - ak) + DIL_MID * (jq - jk)
    mask_mid0 = _both_heads(d0 >= 0)
    jq, aq, jk, ak = mid_index(2 * sub)
    d1 = DIL_FAR * (aq + sub - ak) + DIL_MID * (jq - jk)
    mask_mid = _both_heads((d1 >= 0) & (d1 <= DIL_MID * QB))

    def mid_tile(r4, a0, k0, nk, mask):
        def at(j, off, n):
            return pl.ds((DIL_MID * j + r4) * rows + off, n)

        q = jnp.concatenate([qp[at(j, a0, sub), :] for j in range(DIL_MID)], axis=0)
        k = jnp.concatenate([kp[at(j, k0, nk), :] for j in range(DIL_MID)], axis=0)
        v = jnp.concatenate([vpx[at(j, k0, nk), :] for j in range(DIL_MID)], axis=0)
        o, m, l = _pair_block(q, k, v, mask, lane_h0)
        for j in range(DIL_MID):
            dst = at(j, a0, sub)
            src = slice(j * sub, (j + 1) * sub)
            o2[dst, :] = o[src]
            m2[dst, :] = m[src]
            l2[dst, :] = l[src]

    for r in range(n_chunk):
        blk = slice(r * rows, (r + 1) * rows)
        o, m, l = _pair_block(qp[blk, :], kp[blk, :], vpx[blk, :], causal, lane_h0)
        o23[blk, :] = o
        m23[blk, :] = m
        l23[blk, :] = l

    for r4 in range(DIL_MID):
        mid_tile(r4, 0, 0, sub, mask_mid0)
        for a_blk in range(1, rows // sub):
            mid_tile(r4, a_blk * sub, (a_blk - 1) * sub, 2 * sub, mask_mid)

    for r in range(n_chunk):
        blk = slice(r * rows, (r + 1) * rows)
        oo, mm, ll = _merge(o23[blk, :], m23[blk, :], l23[blk, :],
                            o2[blk, :], m2[blk, :], l2[blk, :])
        o23[blk, :] = oo / ll
        m23[blk, :] = mm + jnp.log2(ll)

    back = jnp.concatenate(
        [jnp.concatenate([o23[r * rows + g * n_chunk:r * rows + (g + 1) * n_chunk, :].astype(BF16)
                          for r in range(n_chunk)], axis=0) for g in range(n_grp)], axis=1)
    restored = jnp.dot(swap, back, preferred_element_type=F32)
    for g in range(n_grp):
        on[g * grp:(g + 1) * grp, :] = restored[:, g * LANES:(g + 1) * LANES]
    for r in range(n_chunk):
        mn[pl.ds(r, rows, stride=n_chunk), :] = m23[r * rows:(r + 1) * rows, :]

    qi2 = lax.broadcasted_iota(I32, (QB, 2 * QB), 0)
    kj2 = lax.broadcasted_iota(I32, (QB, 2 * QB), 1)
    dn = qi2 + QB - kj2
    mask_near = _both_heads((dn >= 0) & (dn <= QB))

    def near_finish(dst, o, m, l):
        oo, _, ll = _merge(on[dst, :], mn[dst, :], 1.0, o, m, l)
        o_ref[dst, :] = (oo / ll).astype(o_ref.dtype)

    first = pl.ds(0, QB)
    o, m, l = _pair_block(q_ref[first, :], k_ref[first, :], vnx[first, :], causal, lane_h0)
    near_finish(first, o, m, l)
    for n in range(1, s // QB):
        keys = pl.ds((n - 1) * QB, 2 * QB)
        o, m, l = _pair_block(q_ref[pl.ds(n * QB, QB), :], k_ref[keys, :], vnx[keys, :],
                              mask_near, lane_h0)
        near_finish(pl.ds(n * QB, QB), o, m, l)


def _attention(qkv, batch, seq):
    n_pair = D_ATTN // LANES
    blk = (seq, LANES)
    f32_scr = pltpu.VMEM(blk, F32)
    bf_scr = pltpu.VMEM(blk, BF16)
    bfx_scr = pltpu.VMEM((seq, 2 * LANES), BF16)
    return pl.pallas_call(
        _attn_kernel,
        grid=(batch, n_pair),
        in_specs=[
            pl.BlockSpec(blk, lambda b, h: (b, h)),
            pl.BlockSpec(blk, lambda b, h: (b, n_pair + h)),
            pl.BlockSpec(blk, lambda b, h: (b, 2 * n_pair + h)),
        ],
        out_specs=pl.BlockSpec(blk, lambda b, h: (b, h)),
        out_shape=jax.ShapeDtypeStruct((batch * seq, D_ATTN), BF16),
        scratch_shapes=[bf_scr, bf_scr, bfx_scr, bfx_scr] + [f32_scr] * 8,
        compiler_params=pltpu.CompilerParams(
            dimension_semantics=("arbitrary", "arbitrary"), vmem_limit_bytes=VMEM_LIMIT),
        name="dilated_attn",
    )(qkv, qkv, qkv)


META_ROWS = 16
META_EIDX, META_GATE, META_RANK = 0, 4, 8


def _outproj_kernel(x_ref, yp_ref, ya_ref, wo_ref, g_ref, wrt_ref, brt_ref,
                    x1_ref, h2_ref, meta_ref, cnt_ref, wo_bf, before, carry):
    tm = x_ref.shape[0]

    @pl.when(pl.program_id(0) == 0)
    def _():
        wo_bf[...] = wo_ref[...].astype(BF16)
        carry[...] = jnp.zeros_like(carry)
        ti = lax.broadcasted_iota(I32, (tm, tm), 0)
        tj = lax.broadcasted_iota(I32, (tm, tm), 1)
        before[...] = (ti < tj).astype(BF16)

    x1 = (x_ref[...]
          + jnp.dot(yp_ref[...], wo_bf[:D_POOL, :], preferred_element_type=F32)
          + jnp.dot(ya_ref[...], wo_bf[D_POOL:, :], preferred_element_type=F32))
    x1_ref[...] = x1
    h2 = _rms(x1, g_ref[...])
    h2_ref[...] = _pack_row(h2)

    logits_t = lax.dot_general(wrt_ref[...].astype(BF16), h2.astype(BF16),
                               (((1,), (1,)), ((), ())), preferred_element_type=F32)
    logits_t = logits_t + brt_ref[:, 0:1]
    eid = lax.broadcasted_iota(I32, (N_EXPERTS, tm), 0)
    work = logits_t
    idxs, vals = [], []
    for _ in range(TOP_K):
        mx = jnp.max(work, axis=0, keepdims=True)
        idx = jnp.min(jnp.where(work == mx, eid, N_EXPERTS), axis=0, keepdims=True)
        idxs.append(idx)
        vals.append(mx)
        work = jnp.where(eid == idx, -jnp.inf, work)
    exps = [jnp.exp(v - vals[0]) for v in vals]
    den = exps[0] + exps[1] + exps[2] + exps[3]

    onehot = jnp.zeros((N_EXPERTS, tm), F32)
    for idx in idxs:
        onehot = onehot + (eid == idx).astype(F32)
    rank_e = carry[:, 0:1] + jnp.dot(onehot.astype(BF16), before[...],
                                     preferred_element_type=F32)
    carry[...] = carry[...] + jnp.sum(onehot, axis=1, keepdims=True)
    cnt_ref[...] = carry[...]

    mrow = lax.broadcasted_iota(I32, (META_ROWS, tm), 0)
    meta = jnp.zeros((META_ROWS, tm), F32)
    for k in range(TOP_K):
        rank_k = jnp.sum(jnp.where(eid == idxs[k], rank_e, 0.0), axis=0, keepdims=True)
        meta = jnp.where(mrow == META_EIDX + k, idxs[k].astype(F32), meta)
        meta = jnp.where(mrow == META_GATE + k, exps[k] / den, meta)
        meta = jnp.where(mrow == META_RANK + k, rank_k, meta)
    meta_ref[...] = meta


def _outproj(x2, y_pool, y_attn, w_out, g, w_router_t, b_router_t):
    t = x2.shape[0]
    row = lambda i: (i, 0)
    const = lambda i: (0, 0)
    return pl.pallas_call(
        _outproj_kernel,
        grid=(t // ROW_TILE,),
        in_specs=[
            pl.BlockSpec((ROW_TILE, D_MODEL), row),
            pl.BlockSpec((ROW_TILE, D_POOL), row),
            pl.BlockSpec((ROW_TILE, D_ATTN), row),
            pl.BlockSpec((D_MODEL, D_MODEL), const),
            pl.BlockSpec((1, D_MODEL), const),
            pl.BlockSpec((N_EXPERTS, D_MODEL), const),
            pl.BlockSpec((N_EXPERTS, LANES), const),
        ],
        out_specs=[
            pl.BlockSpec((ROW_TILE, D_MODEL), row),
            pl.BlockSpec((ROW_TILE, D_PACKED), row),
            pl.BlockSpec((META_ROWS, ROW_TILE), lambda i: (0, i)),
            pl.BlockSpec((N_EXPERTS, LANES), const),
        ],
        out_shape=[
            jax.ShapeDtypeStruct((t, D_MODEL), F32),
            jax.ShapeDtypeStruct((t, D_PACKED), I32),
            jax.ShapeDtypeStruct((META_ROWS, t), F32),
            jax.ShapeDtypeStruct((N_EXPERTS, LANES), F32),
        ],
        scratch_shapes=[pltpu.VMEM((D_MODEL, D_MODEL), BF16),
                        pltpu.VMEM((ROW_TILE, ROW_TILE), BF16),
                        pltpu.VMEM((N_EXPERTS, LANES), F32)],
        compiler_params=pltpu.CompilerParams(
            dimension_semantics=("arbitrary",), vmem_limit_bytes=VMEM_LIMIT),
        name="outproj_router",
    )(x2, y_pool, y_attn, w_out, g, w_router_t, b_router_t)


def _sc_mesh():
    return plsc.VectorSubcoreMesh(core_axis_name="core", subcore_axis_name="subcore")


def _sc_dispatch(h2, pos_rows, p_max):
    t = h2.shape[0]

    @functools.partial(
        pl.kernel, mesh=_sc_mesh(),
        out_type=[jax.ShapeDtypeStruct((p_max, D_PART), h2.dtype)] * N_PART)
    def run(h_hbm, *refs):
        pos_hbm, xs_hbm = refs[:TOP_K], refs[TOP_K:]
        for c in range(N_PART):
            def body(x_vmem, *idx_vmem, dst=xs_hbm[c]):
                for iv in idx_vmem:
                    pltpu.sync_copy(x_vmem, dst.at[iv.at[0]])

            pltpu.emit_pipeline(
                body,
                grid=(t // SC_WINDOW,),
                in_specs=[pl.BlockSpec((SC_WINDOW, D_PART), lambda i, c=c: (i, c))]
                + [pl.BlockSpec((1, SC_WINDOW), lambda i: (0, i))] * TOP_K,
                out_specs=[],
                core_axis_name=("core", "subcore"),
                dimension_semantics=(pltpu.PARALLEL,),
            )(h_hbm, *pos_hbm)

    return run(h2, *pos_rows)


def _sc_unpermute(y_parts, idx_row):
    n = idx_row.shape[1]

    @functools.partial(
        pl.kernel, mesh=_sc_mesh(),
        out_type=[jax.ShapeDtypeStruct((n, D_PART), y_parts[0].dtype)] * N_PART)
    def run(*refs):
        y_hbm, i_hbm, o_hbm = refs[:N_PART], refs[N_PART], refs[N_PART + 1:]
        for c in range(N_PART):
            def body(i_vmem, o_vmem, src=y_hbm[c]):
                pltpu.sync_copy(src.at[i_vmem.at[0]], o_vmem)

            pltpu.emit_pipeline(
                body,
                grid=(n // SC_WINDOW,),
                in_specs=[pl.BlockSpec((1, SC_WINDOW), lambda i: (0, i))],
                out_specs=[pl.BlockSpec((SC_WINDOW, D_PART), lambda i: (i, 0))],
                core_axis_name=("core", "subcore"),
                dimension_semantics=(pltpu.PARALLEL,),
            )(i_hbm, o_hbm[c])

    return run(*y_parts, idx_row)


def _moe_kernel(te_ref, nv_ref, rows_ref, nxt_ref, *refs):
    xs_refs = refs[:N_PART]
    bgu_ref, bd_ref, wgu_hbm, wd_hbm = refs[N_PART:N_PART + 4]
    y_refs = refs[N_PART + 4:2 * N_PART + 4]
    wgu_f32, wd_f32, wgu_bf, wd_bf, sem = refs[2 * N_PART + 4:]
    i = pl.program_id(0)

    def weight_copies(e):
        return (pltpu.make_async_copy(wgu_hbm.at[e], wgu_f32, sem.at[0]),
                pltpu.make_async_copy(wd_hbm.at[e], wd_f32, sem.at[1]))

    @pl.when(i < nv_ref[0])
    def _():
        prev = te_ref[jnp.maximum(i - 1, 0)]
        new_expert = jnp.logical_or(i == 0, te_ref[i] != prev)

        @pl.when(i == 0)
        def _():
            for cp in weight_copies(te_ref[0]):
                cp.start()

        @pl.when(new_expert)
        def _():
            for cp in weight_copies(te_ref[i]):
                cp.wait()
            wgu_bf[...] = wgu_f32[...].astype(BF16)
            wd_bf[...] = wd_f32[...].astype(BF16)

            @pl.when(nxt_ref[i] >= 0)
            def _():
                for cp in weight_copies(nxt_ref[i]):
                    cp.start()

        d_e = wd_bf.shape[0]

        def ffn(n_rows):
            x = _unpack_row(jnp.concatenate([r[0:n_rows, :] for r in xs_refs], axis=1))
            rid = lax.broadcasted_iota(I32, (n_rows, 1), 0)
            x = jnp.where(rid < rows_ref[i], x, 0.0).astype(BF16)
            gu = jnp.dot(x, wgu_bf[...], preferred_element_type=F32) + bgu_ref[0]
            gate = jnp.minimum(gu[:, :d_e], SWIGLU_LIMIT)
            lin = jnp.clip(gu[:, d_e:], -SWIGLU_LIMIT, SWIGLU_LIMIT)
            act = gate * jax.nn.sigmoid(SWIGLU_ALPHA * gate) * (lin + 1.0)
            y = jnp.dot(act.astype(BF16), wd_bf[...], preferred_element_type=F32) + bd_ref[0]
            yp = _pack_row(y)
            for c, y_ref in enumerate(y_refs):
                y_ref[0:n_rows, :] = yp[:, c * D_PART:(c + 1) * D_PART]
                if n_rows < MOE_TILE:
                    y_ref[n_rows:, :] = jnp.zeros((MOE_TILE - n_rows, D_PART), I32)

        lower = 0
        for n_rows in MOE_PATHS:
            @pl.when((rows_ref[i] > lower) & (rows_ref[i] <= n_rows))
            def _(n_rows=n_rows):
                ffn(n_rows)
            lower = n_rows


def _moe(tile_expert, n_valid, tile_rows, next_expert, xs_parts, w_gu, b_gu, w_down, b_down):
    p_max = xs_parts[0].shape[0]
    n_tiles = p_max // MOE_TILE
    d_e = w_down.shape[1]

    def row(i, te, nv, tr, nx):
        return (jnp.minimum(i, nv[0] - 1), 0)

    def expert(i, te, nv, tr, nx):
        return (te[jnp.minimum(i, nv[0] - 1)], 0, 0)

    grid_spec = pltpu.PrefetchScalarGridSpec(
        num_scalar_prefetch=4,
        grid=(n_tiles,),
        in_specs=[pl.BlockSpec((MOE_TILE, D_PART), row)] * N_PART + [
            pl.BlockSpec((1, 1, 2 * d_e), expert),
            pl.BlockSpec((1, 1, D_MODEL), expert),
            pl.BlockSpec(memory_space=pl.ANY),
            pl.BlockSpec(memory_space=pl.ANY),
        ],
        out_specs=[pl.BlockSpec((MOE_TILE, D_PART), row)] * N_PART,
        scratch_shapes=[pltpu.VMEM((D_MODEL, 2 * d_e), F32), pltpu.VMEM((d_e, D_MODEL), F32),
                        pltpu.VMEM((D_MODEL, 2 * d_e), BF16), pltpu.VMEM((d_e, D_MODEL), BF16),
                        pltpu.SemaphoreType.DMA((2,))],
    )
    return pl.pallas_call(
        _moe_kernel,
        grid_spec=grid_spec,
        out_shape=[jax.ShapeDtypeStruct((p_max, D_PART), I32)] * N_PART,
        compiler_params=pltpu.CompilerParams(
            dimension_semantics=("arbitrary",), vmem_limit_bytes=VMEM_LIMIT),
        name="moe_ffn",
    )(tile_expert, n_valid, tile_rows, next_expert, *xs_parts, b_gu, b_down, w_gu, w_down)


def _combine_kernel(x1_ref, gate_ref, g_ref, *refs):
    yk_refs, o_ref = refs[:N_PART], refs[-1]
    acc = x1_ref[...]
    for k in range(TOP_K):
        y_k = _unpack_row(jnp.concatenate([r[k] for r in yk_refs], axis=1))
        acc = acc + gate_ref[:, k:k + 1] * y_k
    o_ref[...] = _rms(acc, g_ref[...])


def _combine(x1, gates, g, yk_parts):
    t, d = x1.shape
    tq = COMBINE_TILE
    row = lambda i: (i, 0)
    return pl.pallas_call(
        _combine_kernel,
        grid=(t // tq,),
        in_specs=[
            pl.BlockSpec((tq, d), row),
            pl.BlockSpec((tq, TOP_K), row),
            pl.BlockSpec((1, d), lambda i: (0, 0)),
        ] + [pl.BlockSpec((TOP_K, tq, D_PART), lambda i: (0, i, 0))] * N_PART,
        out_specs=pl.BlockSpec((tq, d), row),
        out_shape=jax.ShapeDtypeStruct((t, d), F32),
        compiler_params=pltpu.CompilerParams(
            dimension_semantics=("arbitrary",), vmem_limit_bytes=VMEM_LIMIT),
        name="combine_final",
    )(x1, gates, g, *yk_parts)


def kernel(x, g_mix, w_in, w_pool, pool_scale, w_out, g_ffn, w_router, b_router,
           w_gu, b_gu, w_down, b_down, g_final):
    batch, seq, d = x.shape
    t = batch * seq
    assert d == D_MODEL and seq == DIL_FAR * QB, (x.shape, "unsupported shape")
    assert seq % ROW_TILE == 0 and t % COMBINE_TILE == 0 and t % SC_WINDOW == 0
    assert w_in.shape[0] == 1, "one layer"
    x2 = x.reshape(t, d)

    y_pool, qkv = _inproj(x2, g_mix[0].reshape(1, d), w_in[0], w_pool[0],
                          pool_scale[0].reshape(1, D_POOL), seq)
    y_attn = _attention(qkv, batch, seq)

    wr_t = w_router[0].T
    br_t = jnp.broadcast_to(b_router[0].reshape(N_EXPERTS, 1), (N_EXPERTS, LANES))
    x1, h2, meta, cnt = _outproj(x2, y_pool, y_attn, w_out[0], g_ffn[0].reshape(1, d), wr_t, br_t)

    eidx = meta[META_EIDX:META_EIDX + TOP_K].astype(I32)
    rank = meta[META_RANK:META_RANK + TOP_K].astype(I32)
    gates = meta[META_GATE:META_GATE + TOP_K].T
    counts = cnt[:, 0].astype(I32)
    padded = ((counts + MOE_TILE - 1) // MOE_TILE) * MOE_TILE
    ends = jnp.cumsum(padded)
    offsets = ends - padded
    e_ids = jnp.arange(N_EXPERTS, dtype=I32)
    hit = eidx[None] == e_ids[:, None, None]
    pos = rank + jnp.sum(jnp.where(hit, offsets[:, None, None], 0), axis=0)
    p_max = t * TOP_K + N_EXPERTS * MOE_TILE
    n_tiles = p_max // MOE_TILE
    tile_start = jnp.arange(n_tiles, dtype=I32) * MOE_TILE
    tile_expert = jnp.minimum(
        jnp.sum((tile_start[:, None] >= ends[None, :]).astype(I32), axis=1), N_EXPERTS - 1)
    mine = tile_expert[:, None] == e_ids[None, :]
    group_end = jnp.sum(jnp.where(mine, (offsets + counts)[None, :], 0), axis=1)
    tile_rows = jnp.clip(group_end - tile_start, 0, MOE_TILE)
    n_valid = (ends[-1] // MOE_TILE).reshape(1).astype(I32)
    later = (e_ids[None, :] > e_ids[:, None]) & (counts[None, :] > 0)
    next_nonempty = jnp.min(jnp.where(later, e_ids[None, :], N_EXPERTS), axis=1)
    next_nonempty = jnp.where(next_nonempty < N_EXPERTS, next_nonempty, -1)
    next_expert = jnp.sum(jnp.where(mine, next_nonempty[None, :], 0), axis=1)

    xs = _sc_dispatch(h2, [pos[k:k + 1] for k in range(TOP_K)], p_max)
    ys = _moe(tile_expert, n_valid, tile_rows, next_expert, xs, w_gu[0],
              b_gu[0].reshape(N_EXPERTS, 1, -1), w_down[0], b_down[0].reshape(N_EXPERTS, 1, -1))
    yk = _sc_unpermute(ys, pos.reshape(1, TOP_K * t))
    yk = [q.reshape(TOP_K, t, D_PART) for q in yk]
    out = _combine(x1, gates, g_final.reshape(1, d), yk)
    return out.reshape(batch, seq, d)
```

```python
import functools

import jax
import jax.numpy as jnp
from jax import lax
from jax.experimental import pallas as pl
from jax.experimental.pallas import tpu as pltpu
from jax.experimental.pallas import tpu_sc as plsc

F32 = jnp.float32
BF16 = jnp.bfloat16
I32 = jnp.int32

D_MODEL = 1024
D_POOL = 512
D_ATTN = 512
POOL_WINDOWS = (2, 4, 8, 16)
POOL_GROUP = 128
HEAD_DIM = 64
N_EXPERTS = 32
TOP_K = 4
SWIGLU_LIMIT = 7.0
SWIGLU_ALPHA = 1.702
EPS = 1e-5
NEG_INF = -1e30
LOG2_E = 1.4426950408889634

LANES = 128
QB = 128
DIL_MID = 4
DIL_FAR = 16
FAR_BATCH = 2
ROW_TILE = 1024
MOE_TILE = 1024
MOE_PATHS = (256, 512, 768, 1024)
COMBINE_TILE = 1024
SC_WINDOW = 128
D_PACKED = D_MODEL // 2
N_PART = 2
D_PART = D_PACKED // N_PART
VMEM_LIMIT = 56 * 1024 * 1024


def _pack_row(x):
    hi = lax.bitcast_convert_type(x[:, :D_PACKED].astype(BF16).astype(F32), I32)
    lo = lax.bitcast_convert_type(x[:, D_PACKED:].astype(BF16).astype(F32), I32)
    return hi | lax.shift_right_logical(lo, 16)


def _unpack_row(w):
    hi = lax.bitcast_convert_type(w & jnp.int32(-65536), F32)
    lo = lax.bitcast_convert_type(lax.shift_left(w, 16), F32)
    return jnp.concatenate([hi, lo], axis=1)


def _rms(x, g):
    ms = jnp.mean(x * x, axis=-1, keepdims=True)
    return x * lax.rsqrt(ms + EPS) * g


POOL_PAD = 16


def _inproj_kernel(x_ref, g_ref, w_ref, wp_ref, sc_ref, yp_ref, qkv_ref, w_bf, hist, pad_ref,
                   *, tiles_per_seq):
    i = pl.program_id(0)
    tm = x_ref.shape[0]

    @pl.when(i == 0)
    def _():
        w_bf[...] = w_ref[...].astype(BF16)
        pad_ref[0:POOL_PAD, :] = jnp.zeros((POOL_PAD, POOL_GROUP), F32)

    @pl.when(i % tiles_per_seq == 0)
    def _():
        hist[...] = jnp.zeros_like(hist)

    h = _rms(x_ref[...], g_ref[...]).astype(BF16)
    proj = jnp.dot(h, w_bf[...], preferred_element_type=F32)
    q = proj[:, D_POOL:D_POOL + D_ATTN] * (HEAD_DIM ** -0.5 * LOG2_E)
    qkv_ref[:, :D_ATTN] = q.astype(BF16)
    qkv_ref[:, D_ATTN:] = proj[:, D_POOL + D_ATTN:].astype(BF16)

    pos = (i % tiles_per_seq) * tm + lax.broadcasted_iota(I32, (tm, 1), 0)
    for g, w in enumerate(POOL_WINDOWS):
        lo, hi = g * POOL_GROUP, (g + 1) * POOL_GROUP
        e = proj[:, lo:hi]
        acc = jnp.concatenate([hist[g], e], axis=0)
        hist[g] = e[tm - POOL_PAD:, :]
        span = 1
        while span < w:
            pad_ref[POOL_PAD:, :] = acc
            acc = acc + pad_ref[pl.ds(POOL_PAD - span, POOL_PAD + tm), :]
            span *= 2
        count = jnp.minimum(pos + 1, w).astype(F32)
        pooled = acc[POOL_PAD:, :] / count - e
        y = jnp.dot(pooled.astype(BF16), wp_ref[g].astype(BF16), preferred_element_type=F32)
        yp_ref[:, lo:hi] = (y * sc_ref[:, lo:hi]).astype(BF16)


def _inproj(x2, g, w, w_pool, pool_scale, seq):
    t = x2.shape[0]
    d_in = w.shape[1]
    n_grp = len(POOL_WINDOWS)
    return pl.pallas_call(
        functools.partial(_inproj_kernel, tiles_per_seq=seq // ROW_TILE),
        grid=(t // ROW_TILE,),
        in_specs=[
            pl.BlockSpec((ROW_TILE, D_MODEL), lambda i: (i, 0)),
            pl.BlockSpec((1, D_MODEL), lambda i: (0, 0)),
            pl.BlockSpec((D_MODEL, d_in), lambda i: (0, 0)),
            pl.BlockSpec((n_grp, POOL_GROUP, POOL_GROUP), lambda i: (0, 0, 0)),
            pl.BlockSpec((1, D_POOL), lambda i: (0, 0)),
        ],
        out_specs=[
            pl.BlockSpec((ROW_TILE, D_POOL), lambda i: (i, 0)),
            pl.BlockSpec((ROW_TILE, 3 * D_ATTN), lambda i: (i, 0)),
        ],
        out_shape=[
            jax.ShapeDtypeStruct((t, D_POOL), BF16),
            jax.ShapeDtypeStruct((t, 3 * D_ATTN), BF16),
        ],
        scratch_shapes=[pltpu.VMEM((D_MODEL, d_in), BF16),
                        pltpu.VMEM((n_grp, POOL_PAD, POOL_GROUP), F32),
                        pltpu.VMEM((2 * POOL_PAD + ROW_TILE, POOL_GROUP), F32)],
        compiler_params=pltpu.CompilerParams(
            dimension_semantics=("arbitrary",), vmem_limit_bytes=VMEM_LIMIT),
        name="inproj_pool",
    )(x2, g, w, w_pool, pool_scale)


def _pair_block(q, k, v_ext, mask2, lane_h0):
    nq = q.shape[0]
    zero = jnp.zeros_like(q)
    q2 = jnp.concatenate([jnp.where(lane_h0, q, zero), jnp.where(lane_h0, zero, q)], axis=0)
    s = lax.dot_general(q2, k, (((1,), (1,)), ((), ())), preferred_element_type=F32)
    s = jnp.where(mask2, s, NEG_INF)
    m = jnp.max(s, axis=-1, keepdims=True)
    p = jnp.exp2(s - m).astype(BF16)
    ol = jnp.dot(p, v_ext, preferred_element_type=F32)
    o = jnp.where(lane_h0, ol[:nq, :LANES], ol[nq:, :LANES])
    l = jnp.where(lane_h0, ol[:nq, LANES:], ol[nq:, LANES:])
    mb = jnp.where(lane_h0, m[:nq], m[nq:])
    return o, mb, l


def _merge(o_a, m_a, l_a, o_b, m_b, l_b):
    m = jnp.maximum(m_a, m_b)
    ea = jnp.exp2(m_a - m)
    eb = jnp.exp2(m_b - m)
    return o_a * ea + o_b * eb, m, l_a * ea + l_b * eb


def _both_heads(mask):
    return jnp.concatenate([mask, mask], axis=0)


def _attn_kernel(q_ref, k_ref, v_ref, o_ref, qp, kp, vpx, vnx, o23, m23, l23, o2, m2, l2,
                 on, mn):
    s = q_ref.shape[0]
    n_chunk = DIL_FAR
    rows = s // n_chunk
    sub = rows // DIL_MID
    grp = n_chunk * n_chunk
    lane_h0 = lax.broadcasted_iota(I32, (1, LANES), 1) < HEAD_DIM

    @pl.when((pl.program_id(0) == 0) & (pl.program_id(1) == 0))
    def _():
        ones = jnp.ones((s, LANES), BF16)
        vnx[:, LANES:] = ones
        vpx[:, LANES:] = ones

    vnx[:, :LANES] = v_ref[...]

    pi = lax.broadcasted_iota(I32, (grp, grp), 0)
    pj = lax.broadcasted_iota(I32, (grp, grp), 1)
    swap = (pj == (pi % n_chunk) * n_chunk + pi // n_chunk).astype(BF16)
    n_grp = s // grp
    wide = jnp.concatenate([ref[g * grp:(g + 1) * grp, :]
                            for ref in (q_ref, k_ref, v_ref) for g in range(n_grp)], axis=1)
    moved = jnp.dot(swap, wide, preferred_element_type=F32).astype(BF16)
    for which, dst_ref in enumerate((qp, kp, vpx)):
        for g in range(n_grp):
            col = (which * n_grp + g) * LANES
            for r in range(n_chunk):
                dst = slice(r * rows + g * n_chunk, r * rows + (g + 1) * n_chunk)
                dst_ref[dst, 0:LANES] = moved[r * n_chunk:(r + 1) * n_chunk, col:col + LANES]

    qi = lax.broadcasted_iota(I32, (QB, QB), 0)
    kj = lax.broadcasted_iota(I32, (QB, QB), 1)
    causal = _both_heads(qi >= kj)

    def mid_index(n_key_sub):
        i_q = lax.broadcasted_iota(I32, (QB, DIL_MID * n_key_sub), 0)
        i_k = lax.broadcasted_iota(I32, (QB, DIL_MID * n_key_sub), 1)
        return i_q // sub, i_q % sub, i_k // n_key_sub, i_k % n_key_sub

    jq, aq, jk, ak = mid_index(sub)
    d0 = DIL_FAR * (aq - ak) + DIL_MID * (jq - jk)
    mask_mid0 = _both_heads(d0 >= 0)
    jq, aq, jk, ak = mid_index(2 * sub)
    d1 = DIL_FAR * (aq + sub - ak) + DIL_MID * (jq - jk)
    mask_mid = _both_heads((d1 >= 0) & (d1 <= DIL_MID * QB))

    def mid_tile(r4, a0, k0, nk, mask):
        def at(j, off, n):
            return pl.ds((DIL_MID * j + r4) * rows + off, n)

        q = jnp.concatenate([qp[at(j, a0, sub), :] for j in range(DIL_MID)], axis=0)
        k = jnp.concatenate([kp[at(j, k0, nk), :] for j in range(DIL_MID)], axis=0)
        v = jnp.concatenate([vpx[at(j, k0, nk), :] for j in range(DIL_MID)], axis=0)
        o, m, l = _pair_block(q, k, v, mask, lane_h0)
        for j in range(DIL_MID):
            dst = at(j, a0, sub)
            src = slice(j * sub, (j + 1) * sub)
            o2[dst, :] = o[src]
            m2[dst, :] = m[src]
            l2[dst, :] = l[src]

    fq = lax.broadcasted_iota(I32, (FAR_BATCH * rows, FAR_BATCH * rows), 0)
    fk = lax.broadcasted_iota(I32, (FAR_BATCH * rows, FAR_BATCH * rows), 1)
    far_mask = _both_heads((fq // rows == fk // rows) & (fq >= fk))
    for r in range(0, n_chunk, FAR_BATCH):
        blk = slice(r * rows, (r + FAR_BATCH) * rows)
        o, m, l = _pair_block(qp[blk, :], kp[blk, :], vpx[blk, :], far_mask, lane_h0)
        o23[blk, :] = o
        m23[blk, :] = m
        l23[blk, :] = l

    for r4 in range(DIL_MID):
        mid_tile(r4, 0, 0, sub, mask_mid0)
        for a_blk in range(1, rows // sub):
            mid_tile(r4, a_blk * sub, (a_blk - 1) * sub, 2 * sub, mask_mid)

    for r in range(n_chunk):
        blk = slice(r * rows, (r + 1) * rows)
        oo, mm, ll = _merge(o23[blk, :], m23[blk, :], l23[blk, :],
                            o2[blk, :], m2[blk, :], l2[blk, :])
        o23[blk, :] = oo / ll
        m23[blk, :] = mm + jnp.log2(ll)

    back = jnp.concatenate(
        [jnp.concatenate([o23[r * rows + g * n_chunk:r * rows + (g + 1) * n_chunk, :].astype(BF16)
                          for r in range(n_chunk)], axis=0) for g in range(n_grp)], axis=1)
    restored = jnp.dot(swap, back, preferred_element_type=F32)
    for g in range(n_grp):
        on[g * grp:(g + 1) * grp, :] = restored[:, g * LANES:(g + 1) * LANES]
    for r in range(n_chunk):
        mn[pl.ds(r, rows, stride=n_chunk), :] = m23[r * rows:(r + 1) * rows, :]

    qi2 = lax.broadcasted_iota(I32, (QB, 2 * QB), 0)
    kj2 = lax.broadcasted_iota(I32, (QB, 2 * QB), 1)
    dn = qi2 + QB - kj2
    mask_near = _both_heads((dn >= 0) & (dn <= QB))

    def near_finish(dst, o, m, l):
        oo, _, ll = _merge(on[dst, :], mn[dst, :], 1.0, o, m, l)
        o_ref[dst, :] = (oo / ll).astype(o_ref.dtype)

    first = pl.ds(0, QB)
    o, m, l = _pair_block(q_ref[first, :], k_ref[first, :], vnx[first, :], causal, lane_h0)
    near_finish(first, o, m, l)
    for n in range(1, s // QB):
        keys = pl.ds((n - 1) * QB, 2 * QB)
        o, m, l = _pair_block(q_ref[pl.ds(n * QB, QB), :], k_ref[keys, :], vnx[keys, :],
                              mask_near, lane_h0)
        near_finish(pl.ds(n * QB, QB), o, m, l)


def _attention(qkv, batch, seq):
    n_pair = D_ATTN // LANES
    blk = (seq, LANES)
    f32_scr = pltpu.VMEM(blk, F32)
    bf_scr = pltpu.VMEM(blk, BF16)
    bfx_scr = pltpu.VMEM((seq, 2 * LANES), BF16)
    return pl.pallas_call(
        _attn_kernel,
        grid=(batch, n_pair),
        in_specs=[
            pl.BlockSpec(blk, lambda b, h: (b, h)),
            pl.BlockSpec(blk, lambda b, h: (b, n_pair + h)),
            pl.BlockSpec(blk, lambda b, h: (b, 2 * n_pair + h)),
        ],
        out_specs=pl.BlockSpec(blk, lambda b, h: (b, h)),
        out_shape=jax.ShapeDtypeStruct((batch * seq, D_ATTN), BF16),
        scratch_shapes=[bf_scr, bf_scr, bfx_scr, bfx_scr] + [f32_scr] * 8,
        compiler_params=pltpu.CompilerParams(
            dimension_semantics=("arbitrary", "arbitrary"), vmem_limit_bytes=VMEM_LIMIT),
        name="dilated_attn",
    )(qkv, qkv, qkv)


META_ROWS = 16
META_EIDX, META_GATE, META_RANK = 0, 4, 8


def _outproj_kernel(x_ref, yp_ref, ya_ref, wo_ref, g_ref, wrt_ref, brt_ref,
                    x1_ref, h2_ref, meta_ref, cnt_ref, wo_bf, before, carry):
    tm = x_ref.shape[0]

    @pl.when(pl.program_id(0) == 0)
    def _():
        wo_bf[...] = wo_ref[...].astype(BF16)
        carry[...] = jnp.zeros_like(carry)
        ti = lax.broadcasted_iota(I32, (tm, tm), 0)
        tj = lax.broadcasted_iota(I32, (tm, tm), 1)
        before[...] = (ti < tj).astype(BF16)

    x1 = (x_ref[...]
          + jnp.dot(yp_ref[...], wo_bf[:D_POOL, :], preferred_element_type=F32)
          + jnp.dot(ya_ref[...], wo_bf[D_POOL:, :], preferred_element_type=F32))
    x1_ref[...] = x1
    h2 = _rms(x1, g_ref[...])
    h2_ref[...] = _pack_row(h2)

    logits_t = lax.dot_general(wrt_ref[...].astype(BF16), h2.astype(BF16),
                               (((1,), (1,)), ((), ())), preferred_element_type=F32)
    logits_t = logits_t + brt_ref[:, 0:1]
    eid = lax.broadcasted_iota(I32, (N_EXPERTS, tm), 0)
    work = logits_t
    idxs, vals = [], []
    for _ in range(TOP_K):
        mx = jnp.max(work, axis=0, keepdims=True)
        idx = jnp.min(jnp.where(work == mx, eid, N_EXPERTS), axis=0, keepdims=True)
        idxs.append(idx)
        vals.append(mx)
        work = jnp.where(eid == idx, -jnp.inf, work)
    exps = [jnp.exp(v - vals[0]) for v in vals]
    den = exps[0] + exps[1] + exps[2] + exps[3]

    onehot = jnp.zeros((N_EXPERTS, tm), F32)
    for idx in idxs:
        onehot = onehot + (eid == idx).astype(F32)
    rank_e = carry[:, 0:1] + jnp.dot(onehot.astype(BF16), before[...],
                                     preferred_element_type=F32)
    carry[...] = carry[...] + jnp.sum(onehot, axis=1, keepdims=True)
    cnt_ref[...] = carry[...]

    mrow = lax.broadcasted_iota(I32, (META_ROWS, tm), 0)
    meta = jnp.zeros((META_ROWS, tm), F32)
    for k in range(TOP_K):
        rank_k = jnp.sum(jnp.where(eid == idxs[k], rank_e, 0.0), axis=0, keepdims=True)
        meta = jnp.where(mrow == META_EIDX + k, idxs[k].astype(F32), meta)
        meta = jnp.where(mrow == META_GATE + k, exps[k] / den, meta)
        meta = jnp.where(mrow == META_RANK + k, rank_k, meta)
    meta_ref[...] = meta


def _outproj(x2, y_pool, y_attn, w_out, g, w_router_t, b_router_t):
    t = x2.shape[0]
    row = lambda i: (i, 0)
    const = lambda i: (0, 0)
    return pl.pallas_call(
        _outproj_kernel,
        grid=(t // ROW_TILE,),
        in_specs=[
            pl.BlockSpec((ROW_TILE, D_MODEL), row),
            pl.BlockSpec((ROW_TILE, D_POOL), row),
            pl.BlockSpec((ROW_TILE, D_ATTN), row),
            pl.BlockSpec((D_MODEL, D_MODEL), const),
            pl.BlockSpec((1, D_MODEL), const),
            pl.BlockSpec((N_EXPERTS, D_MODEL), const),
            pl.BlockSpec((N_EXPERTS, LANES), const),
        ],
        out_specs=[
            pl.BlockSpec((ROW_TILE, D_MODEL), row),
            pl.BlockSpec((ROW_TILE, D_PACKED), row),
            pl.BlockSpec((META_ROWS, ROW_TILE), lambda i: (0, i)),
            pl.BlockSpec((N_EXPERTS, LANES), const),
        ],
        out_shape=[
            jax.ShapeDtypeStruct((t, D_MODEL), F32),
            jax.ShapeDtypeStruct((t, D_PACKED), I32),
            jax.ShapeDtypeStruct((META_ROWS, t), F32),
            jax.ShapeDtypeStruct((N_EXPERTS, LANES), F32),
        ],
        scratch_shapes=[pltpu.VMEM((D_MODEL, D_MODEL), BF16),
                        pltpu.VMEM((ROW_TILE, ROW_TILE), BF16),
                        pltpu.VMEM((N_EXPERTS, LANES), F32)],
        compiler_params=pltpu.CompilerParams(
            dimension_semantics=("arbitrary",), vmem_limit_bytes=VMEM_LIMIT),
        name="outproj_router",
    )(x2, y_pool, y_attn, w_out, g, w_router_t, b_router_t)


def _sc_mesh():
    return plsc.VectorSubcoreMesh(core_axis_name="core", subcore_axis_name="subcore")


def _sc_dispatch(h2, pos_rows, p_max):
    t = h2.shape[0]

    @functools.partial(
        pl.kernel, mesh=_sc_mesh(),
        out_type=[jax.ShapeDtypeStruct((p_max, D_PART), h2.dtype)] * N_PART)
    def run(h_hbm, *refs):
        pos_hbm, xs_hbm = refs[:TOP_K], refs[TOP_K:]
        for c in range(N_PART):
            def body(x_vmem, *idx_vmem, dst=xs_hbm[c]):
                for iv in idx_vmem:
                    pltpu.sync_copy(x_vmem, dst.at[iv.at[0]])

            pltpu.emit_pipeline(
                body,
                grid=(t // SC_WINDOW,),
                in_specs=[pl.BlockSpec((SC_WINDOW, D_PART), lambda i, c=c: (i, c))]
                + [pl.BlockSpec((1, SC_WINDOW), lambda i: (0, i))] * TOP_K,
                out_specs=[],
                core_axis_name=("core", "subcore"),
                dimension_semantics=(pltpu.PARALLEL,),
            )(h_hbm, *pos_hbm)

    return run(h2, *pos_rows)


def _sc_unpermute(y_parts, idx_row):
    n = idx_row.shape[1]

    @functools.partial(
        pl.kernel, mesh=_sc_mesh(),
        out_type=[jax.ShapeDtypeStruct((n, D_PART), y_parts[0].dtype)] * N_PART)
    def run(*refs):
        y_hbm, i_hbm, o_hbm = refs[:N_PART], refs[N_PART], refs[N_PART + 1:]
        for c in range(N_PART):
            def body(i_vmem, o_vmem, src=y_hbm[c]):
                pltpu.sync_copy(src.at[i_vmem.at[0]], o_vmem)

            pltpu.emit_pipeline(
                body,
                grid=(n // SC_WINDOW,),
                in_specs=[pl.BlockSpec((1, SC_WINDOW), lambda i: (0, i))],
                out_specs=[pl.BlockSpec((SC_WINDOW, D_PART), lambda i: (i, 0))],
                core_axis_name=("core", "subcore"),
                dimension_semantics=(pltpu.PARALLEL,),
            )(i_hbm, o_hbm[c])

    return run(*y_parts, idx_row)


def _moe_kernel(te_ref, nv_ref, rows_ref, nxt_ref, *refs):
    xs_refs = refs[:N_PART]
    bgu_ref, bd_ref, wgu_hbm, wd_hbm = refs[N_PART:N_PART + 4]
    y_refs = refs[N_PART + 4:2 * N_PART + 4]
    wgu_f32, wd_f32, wgu_bf, wd_bf, sem = refs[2 * N_PART + 4:]
    i = pl.program_id(0)

    def weight_copies(e):
        return (pltpu.make_async_copy(wgu_hbm.at[e], wgu_f32, sem.at[0]),
                pltpu.make_async_copy(wd_hbm.at[e], wd_f32, sem.at[1]))

    @pl.when(i < nv_ref[0])
    def _():
        prev = te_ref[jnp.maximum(i - 1, 0)]
        new_expert = jnp.logical_or(i == 0, te_ref[i] != prev)

        @pl.when(i == 0)
        def _():
            for cp in weight_copies(te_ref[0]):
                cp.start()

        @pl.when(new_expert)
        def _():
            for cp in weight_copies(te_ref[i]):
                cp.wait()
            wgu_bf[...] = wgu_f32[...].astype(BF16)
            wd_bf[...] = wd_f32[...].astype(BF16)

            @pl.when(nxt_ref[i] >= 0)
            def _():
                for cp in weight_copies(nxt_ref[i]):
                    cp.start()

        d_e = wd_bf.shape[0]

        def ffn(n_rows):
            x = _unpack_row(jnp.concatenate([r[0:n_rows, :] for r in xs_refs], axis=1))
            rid = lax.broadcasted_iota(I32, (n_rows, 1), 0)
            x = jnp.where(rid < rows_ref[i], x, 0.0).astype(BF16)
            gu = jnp.dot(x, wgu_bf[...], preferred_element_type=F32) + bgu_ref[0]
            gate = jnp.minimum(gu[:, :d_e], SWIGLU_LIMIT)
            lin = jnp.clip(gu[:, d_e:], -SWIGLU_LIMIT, SWIGLU_LIMIT)
            act = gate * jax.nn.sigmoid(SWIGLU_ALPHA * gate) * (lin + 1.0)
            y = jnp.dot(act.astype(BF16), wd_bf[...], preferred_element_type=F32) + bd_ref[0]
            yp = _pack_row(y)
            for c, y_ref in enumerate(y_refs):
                y_ref[0:n_rows, :] = yp[:, c * D_PART:(c + 1) * D_PART]
                if n_rows < MOE_TILE:
                    y_ref[n_rows:, :] = jnp.zeros((MOE_TILE - n_rows, D_PART), I32)

        lower = 0
        for n_rows in MOE_PATHS:
            @pl.when((rows_ref[i] > lower) & (rows_ref[i] <= n_rows))
            def _(n_rows=n_rows):
                ffn(n_rows)
            lower = n_rows


def _moe(tile_expert, n_valid, tile_rows, next_expert, xs_parts, w_gu, b_gu, w_down, b_down):
    p_max = xs_parts[0].shape[0]
    n_tiles = p_max // MOE_TILE
    d_e = w_down.shape[1]

    def row(i, te, nv, tr, nx):
        return (jnp.minimum(i, nv[0] - 1), 0)

    def expert(i, te, nv, tr, nx):
        return (te[jnp.minimum(i, nv[0] - 1)], 0, 0)

    grid_spec = pltpu.PrefetchScalarGridSpec(
        num_scalar_prefetch=4,
        grid=(n_tiles,),
        in_specs=[pl.BlockSpec((MOE_TILE, D_PART), row)] * N_PART + [
            pl.BlockSpec((1, 1, 2 * d_e), expert),
            pl.BlockSpec((1, 1, D_MODEL), expert),
            pl.BlockSpec(memory_space=pl.ANY),
            pl.BlockSpec(memory_space=pl.ANY),
        ],
        out_specs=[pl.BlockSpec((MOE_TILE, D_PART), row)] * N_PART,
        scratch_shapes=[pltpu.VMEM((D_MODEL, 2 * d_e), F32), pltpu.VMEM((d_e, D_MODEL), F32),
                        pltpu.VMEM((D_MODEL, 2 * d_e), BF16), pltpu.VMEM((d_e, D_MODEL), BF16),
                        pltpu.SemaphoreType.DMA((2,))],
    )
    return pl.pallas_call(
        _moe_kernel,
        grid_spec=grid_spec,
        out_shape=[jax.ShapeDtypeStruct((p_max, D_PART), I32)] * N_PART,
        compiler_params=pltpu.CompilerParams(
            dimension_semantics=("arbitrary",), vmem_limit_bytes=VMEM_LIMIT),
        name="moe_ffn",
    )(tile_expert, n_valid, tile_rows, next_expert, *xs_parts, b_gu, b_down, w_gu, w_down)


def _combine_kernel(x1_ref, gate_ref, g_ref, *refs):
    yk_refs, o_ref = refs[:N_PART], refs[-1]
    acc = x1_ref[...]
    for k in range(TOP_K):
        y_k = _unpack_row(jnp.concatenate([r[k] for r in yk_refs], axis=1))
        acc = acc + gate_ref[:, k:k + 1] * y_k
    o_ref[...] = _rms(acc, g_ref[...])


def _combine(x1, gates, g, yk_parts):
    t, d = x1.shape
    tq = COMBINE_TILE
    row = lambda i: (i, 0)
    return pl.pallas_call(
        _combine_kernel,
        grid=(t // tq,),
        in_specs=[
            pl.BlockSpec((tq, d), row),
            pl.BlockSpec((tq, TOP_K), row),
            pl.BlockSpec((1, d), lambda i: (0, 0)),
        ] + [pl.BlockSpec((TOP_K, tq, D_PART), lambda i: (0, i, 0))] * N_PART,
        out_specs=pl.BlockSpec((tq, d), row),
        out_shape=jax.ShapeDtypeStruct((t, d), F32),
        compiler_params=pltpu.CompilerParams(
            dimension_semantics=("arbitrary",), vmem_limit_bytes=VMEM_LIMIT),
        name="combine_final",
    )(x1, gates, g, *yk_parts)


def kernel(x, g_mix, w_in, w_pool, pool_scale, w_out, g_ffn, w_router, b_router,
           w_gu, b_gu, w_down, b_down, g_final):
    batch, seq, d = x.shape
    t = batch * seq
    assert d == D_MODEL and seq == DIL_FAR * QB, (x.shape, "unsupported shape")
    assert seq % ROW_TILE == 0 and t % COMBINE_TILE == 0 and t % SC_WINDOW == 0
    assert w_in.shape[0] == 1, "one layer"
    x2 = x.reshape(t, d)

    y_pool, qkv = _inproj(x2, g_mix[0].reshape(1, d), w_in[0], w_pool[0],
                          pool_scale[0].reshape(1, D_POOL), seq)
    y_attn = _attention(qkv, batch, seq)

    wr_t = w_router[0].T
    br_t = jnp.broadcast_to(b_router[0].reshape(N_EXPERTS, 1), (N_EXPERTS, LANES))
    x1, h2, meta, cnt = _outproj(x2, y_pool, y_attn, w_out[0], g_ffn[0].reshape(1, d), wr_t, br_t)

    eidx = meta[META_EIDX:META_EIDX + TOP_K].astype(I32)
    rank = meta[META_RANK:META_RANK + TOP_K].astype(I32)
    gates = meta[META_GATE:META_GATE + TOP_K].T
    counts = cnt[:, 0].astype(I32)
    padded = ((counts + MOE_TILE - 1) // MOE_TILE) * MOE_TILE
    ends = jnp.cumsum(padded)
    offsets = ends - padded
    e_ids = jnp.arange(N_EXPERTS, dtype=I32)
    hit = eidx[None] == e_ids[:, None, None]
    pos = rank + jnp.sum(jnp.where(hit, offsets[:, None, None], 0), axis=0)
    p_max = t * TOP_K + N_EXPERTS * MOE_TILE
    n_tiles = p_max // MOE_TILE
    tile_start = jnp.arange(n_tiles, dtype=I32) * MOE_TILE
    tile_expert = jnp.minimum(
        jnp.sum((tile_start[:, None] >= ends[None, :]).astype(I32), axis=1), N_EXPERTS - 1)
    mine = tile_expert[:, None] == e_ids[None, :]
    group_end = jnp.sum(jnp.where(mine, (offsets + counts)[None, :], 0), axis=1)
    tile_rows = jnp.clip(group_end - tile_start, 0, MOE_TILE)
    n_valid = (ends[-1] // MOE_TILE).reshape(1).astype(I32)
    later = (e_ids[None, :] > e_ids[:, None]) & (counts[None, :] > 0)
    next_nonempty = jnp.min(jnp.where(later, e_ids[None, :], N_EXPERTS), axis=1)
    next_nonempty = jnp.where(next_nonempty < N_EXPERTS, next_nonempty, -1)
    next_expert = jnp.sum(jnp.where(mine, next_nonempty[None, :], 0), axis=1)

    xs = _sc_dispatch(h2, [pos[k:k + 1] for k in range(TOP_K)], p_max)
    ys = _moe(tile_expert, n_valid, tile_rows, next_expert, xs, w_gu[0],
              b_gu[0].reshape(N_EXPERTS, 1, -1), w_down[0], b_down[0].reshape(N_EXPERTS, 1, -1))
    yk = _sc_unpermute(ys, pos.reshape(1, TOP_K * t))
    yk = [q.reshape(TOP_K, t, D_PART) for q in yk]
    out = _combine(x1, gates, g_final.reshape(1, d), yk)
    return out.reshape(batch, seq, d)
```

```python
import functools

import jax
import jax.numpy as jnp
from jax import lax
from jax.experimental import pallas as pl
from jax.experimental.pallas import tpu as pltpu
from jax.experimental.pallas import tpu_sc as plsc

F32 = jnp.float32
BF16 = jnp.bfloat16
I32 = jnp.int32

D_MODEL = 1024
D_POOL = 512
D_ATTN = 512
POOL_WINDOWS = (2, 4, 8, 16)
POOL_GROUP = 128
HEAD_DIM = 64
N_EXPERTS = 32
TOP_K = 4
SWIGLU_LIMIT = 7.0
SWIGLU_ALPHA = 1.702
EPS = 1e-5
NEG_INF = -1e30
LOG2_E = 1.4426950408889634

LANES = 128
QB = 128
DIL_MID = 4
DIL_FAR = 16
FAR_BATCH = 2
ROW_TILE = 1024
MOE_TILE = 1024
MOE_PIECE = 256
COMBINE_TILE = 1024
SC_WINDOW = 128
D_PACKED = D_MODEL // 2
N_PART = 2
D_PART = D_PACKED // N_PART
VMEM_LIMIT = 56 * 1024 * 1024


def _pack_row(x):
    hi = lax.bitcast_convert_type(x[:, :D_PACKED].astype(BF16).astype(F32), I32)
    lo = lax.bitcast_convert_type(x[:, D_PACKED:].astype(BF16).astype(F32), I32)
    return hi | lax.shift_right_logical(lo, 16)


def _unpack_row(w):
    hi = lax.bitcast_convert_type(w & jnp.int32(-65536), F32)
    lo = lax.bitcast_convert_type(lax.shift_left(w, 16), F32)
    return jnp.concatenate([hi, lo], axis=1)


def _rms(x, g):
    ms = jnp.mean(x * x, axis=-1, keepdims=True)
    return x * lax.rsqrt(ms + EPS) * g


POOL_PAD = 16


def _inproj_kernel(x_ref, g_ref, w_ref, wp_ref, sc_ref, yp_ref, qkv_ref, w_bf, hist, pad_ref,
                   *, tiles_per_seq):
    i = pl.program_id(0)
    tm = x_ref.shape[0]

    @pl.when(i == 0)
    def _():
        w_bf[...] = w_ref[...].astype(BF16)
        pad_ref[0:POOL_PAD, :] = jnp.zeros((POOL_PAD, POOL_GROUP), F32)

    @pl.when(i % tiles_per_seq == 0)
    def _():
        hist[...] = jnp.zeros_like(hist)

    h = _rms(x_ref[...], g_ref[...]).astype(BF16)
    proj = jnp.dot(h, w_bf[...], preferred_element_type=F32)
    q = proj[:, D_POOL:D_POOL + D_ATTN] * (HEAD_DIM ** -0.5 * LOG2_E)
    qkv_ref[:, :D_ATTN] = q.astype(BF16)
    qkv_ref[:, D_ATTN:] = proj[:, D_POOL + D_ATTN:].astype(BF16)

    pos = (i % tiles_per_seq) * tm + lax.broadcasted_iota(I32, (tm, 1), 0)
    for g, w in enumerate(POOL_WINDOWS):
        lo, hi = g * POOL_GROUP, (g + 1) * POOL_GROUP
        e = proj[:, lo:hi]
        acc = jnp.concatenate([hist[g], e], axis=0)
        hist[g] = e[tm - POOL_PAD:, :]
        span = 1
        while span < w:
            pad_ref[POOL_PAD:, :] = acc
            acc = acc + pad_ref[pl.ds(POOL_PAD - span, POOL_PAD + tm), :]
            span *= 2
        count = jnp.minimum(pos + 1, w).astype(F32)
        pooled = acc[POOL_PAD:, :] / count - e
        y = jnp.dot(pooled.astype(BF16), wp_ref[g].astype(BF16), preferred_element_type=F32)
        yp_ref[:, lo:hi] = (y * sc_ref[:, lo:hi]).astype(BF16)


def _inproj(x2, g, w, w_pool, pool_scale, seq):
    t = x2.shape[0]
    d_in = w.shape[1]
    n_grp = len(POOL_WINDOWS)
    return pl.pallas_call(
        functools.partial(_inproj_kernel, tiles_per_seq=seq // ROW_TILE),
        grid=(t // ROW_TILE,),
        in_specs=[
            pl.BlockSpec((ROW_TILE, D_MODEL), lambda i: (i, 0)),
            pl.BlockSpec((1, D_MODEL), lambda i: (0, 0)),
            pl.BlockSpec((D_MODEL, d_in), lambda i: (0, 0)),
            pl.BlockSpec((n_grp, POOL_GROUP, POOL_GROUP), lambda i: (0, 0, 0)),
            pl.BlockSpec((1, D_POOL), lambda i: (0, 0)),
        ],
        out_specs=[
            pl.BlockSpec((ROW_TILE, D_POOL), lambda i: (i, 0)),
            pl.BlockSpec((ROW_TILE, 3 * D_ATTN), lambda i: (i, 0)),
        ],
        out_shape=[
            jax.ShapeDtypeStruct((t, D_POOL), BF16),
            jax.ShapeDtypeStruct((t, 3 * D_ATTN), BF16),
        ],
        scratch_shapes=[pltpu.VMEM((D_MODEL, d_in), BF16),
                        pltpu.VMEM((n_grp, POOL_PAD, POOL_GROUP), F32),
                        pltpu.VMEM((2 * POOL_PAD + ROW_TILE, POOL_GROUP), F32)],
        compiler_params=pltpu.CompilerParams(
            dimension_semantics=("arbitrary",), vmem_limit_bytes=VMEM_LIMIT),
        name="inproj_pool",
    )(x2, g, w, w_pool, pool_scale)


def _pair_block(q, k, v_ext, mask2, lane_h0):
    nq = q.shape[0]
    zero = jnp.zeros_like(q)
    q2 = jnp.concatenate([jnp.where(lane_h0, q, zero), jnp.where(lane_h0, zero, q)], axis=0)
    s = lax.dot_general(q2, k, (((1,), (1,)), ((), ())), preferred_element_type=F32)
    s = jnp.where(mask2, s, NEG_INF)
    m = jnp.max(s, axis=-1, keepdims=True)
    p = jnp.exp2(s - m).astype(BF16)
    ol = jnp.dot(p, v_ext, preferred_element_type=F32)
    o = jnp.where(lane_h0, ol[:nq, :LANES], ol[nq:, :LANES])
    l = jnp.where(lane_h0, ol[:nq, LANES:], ol[nq:, LANES:])
    mb = jnp.where(lane_h0, m[:nq], m[nq:])
    return o, mb, l


def _merge(o_a, m_a, l_a, o_b, m_b, l_b):
    m = jnp.maximum(m_a, m_b)
    ea = jnp.exp2(m_a - m)
    eb = jnp.exp2(m_b - m)
    return o_a * ea + o_b * eb, m, l_a * ea + l_b * eb


def _both_heads(mask):
    return jnp.concatenate([mask, mask], axis=0)


def _attn_kernel(q_ref, k_ref, v_ref, o_ref, qp, kp, vpx, vnx, o23, m23, l23, o2, m2, l2,
                 on, mn):
    s = q_ref.shape[0]
    n_chunk = DIL_FAR
    rows = s // n_chunk
    sub = rows // DIL_MID
    grp = n_chunk * n_chunk
    lane_h0 = lax.broadcasted_iota(I32, (1, LANES), 1) < HEAD_DIM

    @pl.when((pl.program_id(0) == 0) & (pl.program_id(1) == 0))
    def _():
        ones = jnp.ones((s, LANES), BF16)
        vnx[:, LANES:] = ones
        vpx[:, LANES:] = ones

    vnx[:, :LANES] = v_ref[...]

    pi = lax.broadcasted_iota(I32, (grp, grp), 0)
    pj = lax.broadcasted_iota(I32, (grp, grp), 1)
    swap = (pj == (pi % n_chunk) * n_chunk + pi // n_chunk).astype(BF16)
    n_grp = s // grp
    wide = jnp.concatenate([ref[g * grp:(g + 1) * grp, :]
                            for ref in (q_ref, k_ref, v_ref) for g in range(n_grp)], axis=1)
    moved = jnp.dot(swap, wide, preferred_element_type=F32).astype(BF16)
    for which, dst_ref in enumerate((qp, kp, vpx)):
        for g in range(n_grp):
            col = (which * n_grp + g) * LANES
            for r in range(n_chunk):
                dst = slice(r * rows + g * n_chunk, r * rows + (g + 1) * n_chunk)
                dst_ref[dst, 0:LANES] = moved[r * n_chunk:(r + 1) * n_chunk, col:col + LANES]

    qi = lax.broadcasted_iota(I32, (QB, QB), 0)
    kj = lax.broadcasted_iota(I32, (QB, QB), 1)
    causal = _both_heads(qi >= kj)

    def mid_index(n_key_sub):
        i_q = lax.broadcasted_iota(I32, (QB, DIL_MID * n_key_sub), 0)
        i_k = lax.broadcasted_iota(I32, (QB, DIL_MID * n_key_sub), 1)
        return i_q // sub, i_q % sub, i_k // n_key_sub, i_k % n_key_sub

    jq, aq, jk, ak = mid_index(sub)
    d0 = DIL_FAR * (aq - ak) + DIL_MID * (jq - jk)
    mask_mid0 = _both_heads(d0 >= 0)
    jq, aq, jk, ak = mid_index(2 * sub)
    d1 = DIL_FAR * (aq + sub - ak) + DIL_MID * (jq - jk)
    mask_mid = _both_heads((d1 >= 0) & (d1 <= DIL_MID * QB))

    def mid_tile(r4, a0, k0, nk, mask):
        def at(j, off, n):
            return pl.ds((DIL_MID * j + r4) * rows + off, n)

        q = jnp.concatenate([qp[at(j, a0, sub), :] for j in range(DIL_MID)], axis=0)
        k = jnp.concatenate([kp[at(j, k0, nk), :] for j in range(DIL_MID)], axis=0)
        v = jnp.concatenate([vpx[at(j, k0, nk), :] for j in range(DIL_MID)], axis=0)
        o, m, l = _pair_block(q, k, v, mask, lane_h0)
        for j in range(DIL_MID):
            dst = at(j, a0, sub)
            src = slice(j * sub, (j + 1) * sub)
            o2[dst, :] = o[src]
            m2[dst, :] = m[src]
            l2[dst, :] = l[src]

    fq = lax.broadcasted_iota(I32, (FAR_BATCH * rows, FAR_BATCH * rows), 0)
    fk = lax.broadcasted_iota(I32, (FAR_BATCH * rows, FAR_BATCH * rows), 1)
    far_mask = _both_heads((fq // rows == fk // rows) & (fq >= fk))
    for r in range(0, n_chunk, FAR_BATCH):
        blk = slice(r * rows, (r + FAR_BATCH) * rows)
        o, m, l = _pair_block(qp[blk, :], kp[blk, :], vpx[blk, :], far_mask, lane_h0)
        o23[blk, :] = o
        m23[blk, :] = m
        l23[blk, :] = l

    for r4 in range(DIL_MID):
        mid_tile(r4, 0, 0, sub, mask_mid0)
        for a_blk in range(1, rows // sub):
            mid_tile(r4, a_blk * sub, (a_blk - 1) * sub, 2 * sub, mask_mid)

    for r in range(n_chunk):
        blk = slice(r * rows, (r + 1) * rows)
        oo, mm, ll = _merge(o23[blk, :], m23[blk, :], l23[blk, :],
                            o2[blk, :], m2[blk, :], l2[blk, :])
        o23[blk, :] = oo / ll
        m23[blk, :] = mm + jnp.log2(ll)

    back = jnp.concatenate(
        [jnp.concatenate([o23[r * rows + g * n_chunk:r * rows + (g + 1) * n_chunk, :].astype(BF16)
                          for r in range(n_chunk)], axis=0) for g in range(n_grp)], axis=1)
    restored = jnp.dot(swap, back, preferred_element_type=F32)
    for g in range(n_grp):
        on[g * grp:(g + 1) * grp, :] = restored[:, g * LANES:(g + 1) * LANES]
    for r in range(n_chunk):
        mn[pl.ds(r, rows, stride=n_chunk), :] = m23[r * rows:(r + 1) * rows, :]

    qi2 = lax.broadcasted_iota(I32, (QB, 2 * QB), 0)
    kj2 = lax.broadcasted_iota(I32, (QB, 2 * QB), 1)
    dn = qi2 + QB - kj2
    mask_near = _both_heads((dn >= 0) & (dn <= QB))

    def near_finish(dst, o, m, l):
        oo, _, ll = _merge(on[dst, :], mn[dst, :], 1.0, o, m, l)
        o_ref[dst, :] = (oo / ll).astype(o_ref.dtype)

    first = pl.ds(0, QB)
    o, m, l = _pair_block(q_ref[first, :], k_ref[first, :], vnx[first, :], causal, lane_h0)
    near_finish(first, o, m, l)
    for n in range(1, s // QB):
        keys = pl.ds((n - 1) * QB, 2 * QB)
        o, m, l = _pair_block(q_ref[pl.ds(n * QB, QB), :], k_ref[keys, :], vnx[keys, :],
                              mask_near, lane_h0)
        near_finish(pl.ds(n * QB, QB), o, m, l)


def _attention(qkv, batch, seq):
    n_pair = D_ATTN // LANES
    blk = (seq, LANES)
    f32_scr = pltpu.VMEM(blk, F32)
    bf_scr = pltpu.VMEM(blk, BF16)
    bfx_scr = pltpu.VMEM((seq, 2 * LANES), BF16)
    return pl.pallas_call(
        _attn_kernel,
        grid=(batch, n_pair),
        in_specs=[
            pl.BlockSpec(blk, lambda b, h: (b, h)),
            pl.BlockSpec(blk, lambda b, h: (b, n_pair + h)),
            pl.BlockSpec(blk, lambda b, h: (b, 2 * n_pair + h)),
        ],
        out_specs=pl.BlockSpec(blk, lambda b, h: (b, h)),
        out_shape=jax.ShapeDtypeStruct((batch * seq, D_ATTN), BF16),
        scratch_shapes=[bf_scr, bf_scr, bfx_scr, bfx_scr] + [f32_scr] * 8,
        compiler_params=pltpu.CompilerParams(
            dimension_semantics=("arbitrary", "arbitrary"), vmem_limit_bytes=VMEM_LIMIT),
        name="dilated_attn",
    )(qkv, qkv, qkv)


META_ROWS = 16
META_EIDX, META_GATE, META_RANK = 0, 4, 8


def _outproj_kernel(x_ref, yp_ref, ya_ref, wo_ref, g_ref, wrt_ref, brt_ref,
                    x1_ref, h2_ref, meta_ref, cnt_ref, wo_bf, before, carry):
    tm = x_ref.shape[0]

    @pl.when(pl.program_id(0) == 0)
    def _():
        wo_bf[...] = wo_ref[...].astype(BF16)
        carry[...] = jnp.zeros_like(carry)
        ti = lax.broadcasted_iota(I32, (tm, tm), 0)
        tj = lax.broadcasted_iota(I32, (tm, tm), 1)
        before[...] = (ti < tj).astype(BF16)

    x1 = (x_ref[...]
          + jnp.dot(yp_ref[...], wo_bf[:D_POOL, :], preferred_element_type=F32)
          + jnp.dot(ya_ref[...], wo_bf[D_POOL:, :], preferred_element_type=F32))
    x1_ref[...] = x1
    h2 = _rms(x1, g_ref[...])
    h2_ref[...] = _pack_row(h2)

    logits_t = lax.dot_general(wrt_ref[...].astype(BF16), h2.astype(BF16),
                               (((1,), (1,)), ((), ())), preferred_element_type=F32)
    logits_t = logits_t + brt_ref[:, 0:1]
    eid = lax.broadcasted_iota(I32, (N_EXPERTS, tm), 0)
    work = logits_t
    idxs, vals = [], []
    for _ in range(TOP_K):
        mx = jnp.max(work, axis=0, keepdims=True)
        idx = jnp.min(jnp.where(work == mx, eid, N_EXPERTS), axis=0, keepdims=True)
        idxs.append(idx)
        vals.append(mx)
        work = jnp.where(eid == idx, -jnp.inf, work)
    exps = [jnp.exp(v - vals[0]) for v in vals]
    den = exps[0] + exps[1] + exps[2] + exps[3]

    onehot = jnp.zeros((N_EXPERTS, tm), F32)
    for idx in idxs:
        onehot = onehot + (eid == idx).astype(F32)
    rank_e = carry[:, 0:1] + jnp.dot(onehot.astype(BF16), before[...],
                                     preferred_element_type=F32)
    carry[...] = carry[...] + jnp.sum(onehot, axis=1, keepdims=True)
    cnt_ref[...] = carry[...]

    mrow = lax.broadcasted_iota(I32, (META_ROWS, tm), 0)
    meta = jnp.zeros((META_ROWS, tm), F32)
    for k in range(TOP_K):
        rank_k = jnp.sum(jnp.where(eid == idxs[k], rank_e, 0.0), axis=0, keepdims=True)
        meta = jnp.where(mrow == META_EIDX + k, idxs[k].astype(F32), meta)
        meta = jnp.where(mrow == META_GATE + k, exps[k] / den, meta)
        meta = jnp.where(mrow == META_RANK + k, rank_k, meta)
    meta_ref[...] = meta


def _outproj(x2, y_pool, y_attn, w_out, g, w_router_t, b_router_t):
    t = x2.shape[0]
    row = lambda i: (i, 0)
    const = lambda i: (0, 0)
    return pl.pallas_call(
        _outproj_kernel,
        grid=(t // ROW_TILE,),
        in_specs=[
            pl.BlockSpec((ROW_TILE, D_MODEL), row),
            pl.BlockSpec((ROW_TILE, D_POOL), row),
            pl.BlockSpec((ROW_TILE, D_ATTN), row),
            pl.BlockSpec((D_MODEL, D_MODEL), const),
            pl.BlockSpec((1, D_MODEL), const),
            pl.BlockSpec((N_EXPERTS, D_MODEL), const),
            pl.BlockSpec((N_EXPERTS, LANES), const),
        ],
        out_specs=[
            pl.BlockSpec((ROW_TILE, D_MODEL), row),
            pl.BlockSpec((ROW_TILE, D_PACKED), row),
            pl.BlockSpec((META_ROWS, ROW_TILE), lambda i: (0, i)),
            pl.BlockSpec((N_EXPERTS, LANES), const),
        ],
        out_shape=[
            jax.ShapeDtypeStruct((t, D_MODEL), F32),
            jax.ShapeDtypeStruct((t, D_PACKED), I32),
            jax.ShapeDtypeStruct((META_ROWS, t), F32),
            jax.ShapeDtypeStruct((N_EXPERTS, LANES), F32),
        ],
        scratch_shapes=[pltpu.VMEM((D_MODEL, D_MODEL), BF16),
                        pltpu.VMEM((ROW_TILE, ROW_TILE), BF16),
                        pltpu.VMEM((N_EXPERTS, LANES), F32)],
        compiler_params=pltpu.CompilerParams(
            dimension_semantics=("arbitrary",), vmem_limit_bytes=VMEM_LIMIT),
        name="outproj_router",
    )(x2, y_pool, y_attn, w_out, g, w_router_t, b_router_t)


def _sc_mesh():
    return plsc.VectorSubcoreMesh(core_axis_name="core", subcore_axis_name="subcore")


def _sc_dispatch(h2, pos_rows, p_max):
    t = h2.shape[0]

    @functools.partial(
        pl.kernel, mesh=_sc_mesh(),
        out_type=[jax.ShapeDtypeStruct((p_max, D_PART), h2.dtype)] * N_PART)
    def run(h_hbm, *refs):
        pos_hbm, xs_hbm = refs[:TOP_K], refs[TOP_K:]
        for c in range(N_PART):
            def body(x_vmem, *idx_vmem, dst=xs_hbm[c]):
                for iv in idx_vmem:
                    pltpu.sync_copy(x_vmem, dst.at[iv.at[0]])

            pltpu.emit_pipeline(
                body,
                grid=(t // SC_WINDOW,),
                in_specs=[pl.BlockSpec((SC_WINDOW, D_PART), lambda i, c=c: (i, c))]
                + [pl.BlockSpec((1, SC_WINDOW), lambda i: (0, i))] * TOP_K,
                out_specs=[],
                core_axis_name=("core", "subcore"),
                dimension_semantics=(pltpu.PARALLEL,),
            )(h_hbm, *pos_hbm)

    return run(h2, *pos_rows)


def _sc_unpermute(y_parts, idx_row):
    n = idx_row.shape[1]

    @functools.partial(
        pl.kernel, mesh=_sc_mesh(),
        out_type=[jax.ShapeDtypeStruct((n, D_PART), y_parts[0].dtype)] * N_PART)
    def run(*refs):
        y_hbm, i_hbm, o_hbm = refs[:N_PART], refs[N_PART], refs[N_PART + 1:]
        for c in range(N_PART):
            def body(i_vmem, o_vmem, src=y_hbm[c]):
                pltpu.sync_copy(src.at[i_vmem.at[0]], o_vmem)

            pltpu.emit_pipeline(
                body,
                grid=(n // SC_WINDOW,),
                in_specs=[pl.BlockSpec((1, SC_WINDOW), lambda i: (0, i))],
                out_specs=[pl.BlockSpec((SC_WINDOW, D_PART), lambda i: (i, 0))],
                core_axis_name=("core", "subcore"),
                dimension_semantics=(pltpu.PARALLEL,),
            )(i_hbm, o_hbm[c])

    return run(*y_parts, idx_row)


def _moe_kernel(te_ref, nv_ref, rows_ref, nxt_ref, *refs):
    xs_refs = refs[:N_PART]
    bgu_ref, bd_ref, wgu_hbm, wd_hbm = refs[N_PART:N_PART + 4]
    y_refs = refs[N_PART + 4:2 * N_PART + 4]
    wgu_f32, wd_f32, wgu_bf, wd_bf, sem = refs[2 * N_PART + 4:]
    i = pl.program_id(0)

    def weight_copies(e):
        return (pltpu.make_async_copy(wgu_hbm.at[e], wgu_f32, sem.at[0]),
                pltpu.make_async_copy(wd_hbm.at[e], wd_f32, sem.at[1]))

    @pl.when(i < nv_ref[0])
    def _():
        prev = te_ref[jnp.maximum(i - 1, 0)]
        new_expert = jnp.logical_or(i == 0, te_ref[i] != prev)

        @pl.when(i == 0)
        def _():
            for cp in weight_copies(te_ref[0]):
                cp.start()

        @pl.when(new_expert)
        def _():
            for cp in weight_copies(te_ref[i]):
                cp.wait()
            wgu_bf[...] = wgu_f32[...].astype(BF16)
            wd_bf[...] = wd_f32[...].astype(BF16)

            @pl.when(nxt_ref[i] >= 0)
            def _():
                for cp in weight_copies(nxt_ref[i]):
                    cp.start()

        d_e = wd_bf.shape[0]

        def ffn(row0, n_rows):
            rows = pl.ds(row0, n_rows)
            x = _unpack_row(jnp.concatenate([r[rows, :] for r in xs_refs], axis=1))
            rid = row0 + lax.broadcasted_iota(I32, (n_rows, 1), 0)
            x = jnp.where(rid < rows_ref[i], x, 0.0).astype(BF16)
            gu = jnp.dot(x, wgu_bf[...], preferred_element_type=F32) + bgu_ref[0]
            gate = jnp.minimum(gu[:, :d_e], SWIGLU_LIMIT)
            lin = jnp.clip(gu[:, d_e:], -SWIGLU_LIMIT, SWIGLU_LIMIT)
            act = gate * jax.nn.sigmoid(SWIGLU_ALPHA * gate) * (lin + 1.0)
            y = jnp.dot(act.astype(BF16), wd_bf[...], preferred_element_type=F32) + bd_ref[0]
            yp = _pack_row(y)
            for c, y_ref in enumerate(y_refs):
                y_ref[rows, :] = yp[:, c * D_PART:(c + 1) * D_PART]

        full = rows_ref[i] > MOE_TILE - MOE_PIECE

        @pl.when(full)
        def _():
            ffn(0, MOE_TILE)

        @pl.when(jnp.logical_not(full))
        def _():
            for y_ref in y_refs:
                y_ref[...] = jnp.zeros_like(y_ref)

            def piece(j, carry):
                ffn(pl.multiple_of(j * MOE_PIECE, MOE_PIECE), MOE_PIECE)
                return carry

            lax.fori_loop(0, (rows_ref[i] + MOE_PIECE - 1) // MOE_PIECE, piece, 0)


def _moe(tile_expert, n_valid, tile_rows, next_expert, xs_parts, w_gu, b_gu, w_down, b_down):
    p_max = xs_parts[0].shape[0]
    n_tiles = p_max // MOE_TILE
    d_e = w_down.shape[1]

    def row(i, te, nv, tr, nx):
        return (jnp.minimum(i, nv[0] - 1), 0)

    def expert(i, te, nv, tr, nx):
        return (te[jnp.minimum(i, nv[0] - 1)], 0, 0)

    grid_spec = pltpu.PrefetchScalarGridSpec(
        num_scalar_prefetch=4,
        grid=(n_tiles,),
        in_specs=[pl.BlockSpec((MOE_TILE, D_PART), row)] * N_PART + [
            pl.BlockSpec((1, 1, 2 * d_e), expert),
            pl.BlockSpec((1, 1, D_MODEL), expert),
            pl.BlockSpec(memory_space=pl.ANY),
            pl.BlockSpec(memory_space=pl.ANY),
        ],
        out_specs=[pl.BlockSpec((MOE_TILE, D_PART), row)] * N_PART,
        scratch_shapes=[pltpu.VMEM((D_MODEL, 2 * d_e), F32), pltpu.VMEM((d_e, D_MODEL), F32),
                        pltpu.VMEM((D_MODEL, 2 * d_e), BF16), pltpu.VMEM((d_e, D_MODEL), BF16),
                        pltpu.SemaphoreType.DMA((2,))],
    )
    return pl.pallas_call(
        _moe_kernel,
        grid_spec=grid_spec,
        out_shape=[jax.ShapeDtypeStruct((p_max, D_PART), I32)] * N_PART,
        compiler_params=pltpu.CompilerParams(
            dimension_semantics=("arbitrary",), vmem_limit_bytes=VMEM_LIMIT),
        name="moe_ffn",
    )(tile_expert, n_valid, tile_rows, next_expert, *xs_parts, b_gu, b_down, w_gu, w_down)


def _combine_kernel(x1_ref, gate_ref, g_ref, *refs):
    yk_refs, o_ref = refs[:N_PART], refs[-1]
    acc = x1_ref[...]
    for k in range(TOP_K):
        y_k = _unpack_row(jnp.concatenate([r[k] for r in yk_refs], axis=1))
        acc = acc + gate_ref[:, k:k + 1] * y_k
    o_ref[...] = _rms(acc, g_ref[...])


def _combine(x1, gates, g, yk_parts):
    t, d = x1.shape
    tq = COMBINE_TILE
    row = lambda i: (i, 0)
    return pl.pallas_call(
        _combine_kernel,
        grid=(t // tq,),
        in_specs=[
            pl.BlockSpec((tq, d), row),
            pl.BlockSpec((tq, TOP_K), row),
            pl.BlockSpec((1, d), lambda i: (0, 0)),
        ] + [pl.BlockSpec((TOP_K, tq, D_PART), lambda i: (0, i, 0))] * N_PART,
        out_specs=pl.BlockSpec((tq, d), row),
        out_shape=jax.ShapeDtypeStruct((t, d), F32),
        compiler_params=pltpu.CompilerParams(
            dimension_semantics=("arbitrary",), vmem_limit_bytes=VMEM_LIMIT),
        name="combine_final",
    )(x1, gates, g, *yk_parts)


def kernel(x, g_mix, w_in, w_pool, pool_scale, w_out, g_ffn, w_router, b_router,
           w_gu, b_gu, w_down, b_down, g_final):
    batch, seq, d = x.shape
    t = batch * seq
    assert d == D_MODEL and seq == DIL_FAR * QB, (x.shape, "unsupported shape")
    assert seq % ROW_TILE == 0 and t % COMBINE_TILE == 0 and t % SC_WINDOW == 0
    assert w_in.shape[0] == 1, "one layer"
    x2 = x.reshape(t, d)

    y_pool, qkv = _inproj(x2, g_mix[0].reshape(1, d), w_in[0], w_pool[0],
                          pool_scale[0].reshape(1, D_POOL), seq)
    y_attn = _attention(qkv, batch, seq)

    wr_t = w_router[0].T
    br_t = jnp.broadcast_to(b_router[0].reshape(N_EXPERTS, 1), (N_EXPERTS, LANES))
    x1, h2, meta, cnt = _outproj(x2, y_pool, y_attn, w_out[0], g_ffn[0].reshape(1, d), wr_t, br_t)

    eidx = meta[META_EIDX:META_EIDX + TOP_K].astype(I32)
    rank = meta[META_RANK:META_RANK + TOP_K].astype(I32)
    gates = meta[META_GATE:META_GATE + TOP_K].T
    counts = cnt[:, 0].astype(I32)
    padded = ((counts + MOE_TILE - 1) // MOE_TILE) * MOE_TILE
    ends = jnp.cumsum(padded)
    offsets = ends - padded
    e_ids = jnp.arange(N_EXPERTS, dtype=I32)
    hit = eidx[None] == e_ids[:, None, None]
    pos = rank + jnp.sum(jnp.where(hit, offsets[:, None, None], 0), axis=0)
    p_max = t * TOP_K + N_EXPERTS * MOE_TILE
    n_tiles = p_max // MOE_TILE
    tile_start = jnp.arange(n_tiles, dtype=I32) * MOE_TILE
    tile_expert = jnp.minimum(
        jnp.sum((tile_start[:, None] >= ends[None, :]).astype(I32), axis=1), N_EXPERTS - 1)
    mine = tile_expert[:, None] == e_ids[None, :]
    group_end = jnp.sum(jnp.where(mine, (offsets + counts)[None, :], 0), axis=1)
    tile_rows = jnp.clip(group_end - tile_start, 0, MOE_TILE)
    n_valid = (ends[-1] // MOE_TILE).reshape(1).astype(I32)
    later = (e_ids[None, :] > e_ids[:, None]) & (counts[None, :] > 0)
    next_nonempty = jnp.min(jnp.where(later, e_ids[None, :], N_EXPERTS), axis=1)
    next_nonempty = jnp.where(next_nonempty < N_EXPERTS, next_nonempty, -1)
    next_expert = jnp.sum(jnp.where(mine, next_nonempty[None, :], 0), axis=1)

    xs = _sc_dispatch(h2, [pos[k:k + 1] for k in range(TOP_K)], p_max)
    ys = _moe(tile_expert, n_valid, tile_rows, next_expert, xs, w_gu[0],
              b_gu[0].reshape(N_EXPERTS, 1, -1), w_down[0], b_down[0].reshape(N_EXPERTS, 1, -1))
    yk = _sc_unpermute(ys, pos.reshape(1, TOP_K * t))
    yk = [q.reshape(TOP_K, t, D_PART) for q in yk]
    out = _combine(x1, gates, g_final.reshape(1, d), yk)
    return out.reshape(batch, seq, d)
```

```python
import functools

import jax
import jax.numpy as jnp
from jax import lax
from jax.experimental import pallas as pl
from jax.experimental.pallas import tpu as pltpu
from jax.experimental.pallas import tpu_sc as plsc

F32 = jnp.float32
BF16 = jnp.bfloat16
I32 = jnp.int32

D_MODEL = 1024
D_POOL = 512
D_ATTN = 512
POOL_WINDOWS = (2, 4, 8, 16)
POOL_GROUP = 128
HEAD_DIM = 64
N_EXPERTS = 32
TOP_K = 4
SWIGLU_LIMIT = 7.0
SWIGLU_ALPHA = 1.702
EPS = 1e-5
NEG_INF = -1e30
LOG2_E = 1.4426950408889634

LANES = 128
QB = 128
DIL_MID = 4
DIL_FAR = 16
FAR_BATCH = 2
ROW_TILE = 1024
MOE_TILE = 1024
MOE_PIECE = 256
COMBINE_TILE = 1024
SC_WINDOW = 128
D_PACKED = D_MODEL // 2
N_PART = 2
D_PART = D_PACKED // N_PART
VMEM_LIMIT = 56 * 1024 * 1024


def _pack_row(x):
    hi = lax.bitcast_convert_type(x[:, :D_PACKED].astype(BF16).astype(F32), I32)
    lo = lax.bitcast_convert_type(x[:, D_PACKED:].astype(BF16).astype(F32), I32)
    return hi | lax.shift_right_logical(lo, 16)


def _unpack_row(w):
    hi = lax.bitcast_convert_type(w & jnp.int32(-65536), F32)
    lo = lax.bitcast_convert_type(lax.shift_left(w, 16), F32)
    return jnp.concatenate([hi, lo], axis=1)


def _rms(x, g):
    ms = jnp.mean(x * x, axis=-1, keepdims=True)
    return x * lax.rsqrt(ms + EPS) * g


POOL_PAD = 16


def _inproj_kernel(x_ref, g_ref, w_ref, wp_ref, sc_ref, yp_ref, qkv_ref, w_bf, hist, pad_ref,
                   *, tiles_per_seq):
    i = pl.program_id(0)
    tm = x_ref.shape[0]

    @pl.when(i == 0)
    def _():
        w_bf[...] = w_ref[...].astype(BF16)
        pad_ref[0:POOL_PAD, :] = jnp.zeros((POOL_PAD, POOL_GROUP), F32)

    @pl.when(i % tiles_per_seq == 0)
    def _():
        hist[...] = jnp.zeros_like(hist)

    h = _rms(x_ref[...], g_ref[...]).astype(BF16)
    proj = jnp.dot(h, w_bf[...], preferred_element_type=F32)
    q = proj[:, D_POOL:D_POOL + D_ATTN] * (HEAD_DIM ** -0.5 * LOG2_E)
    qkv_ref[:, :D_ATTN] = q.astype(BF16)
    qkv_ref[:, D_ATTN:] = proj[:, D_POOL + D_ATTN:].astype(BF16)

    pos = (i % tiles_per_seq) * tm + lax.broadcasted_iota(I32, (tm, 1), 0)
    for g, w in enumerate(POOL_WINDOWS):
        lo, hi = g * POOL_GROUP, (g + 1) * POOL_GROUP
        e = proj[:, lo:hi]
        acc = jnp.concatenate([hist[g], e], axis=0)
        hist[g] = e[tm - POOL_PAD:, :]
        span = 1
        while span < w:
            pad_ref[POOL_PAD:, :] = acc
            acc = acc + pad_ref[pl.ds(POOL_PAD - span, POOL_PAD + tm), :]
            span *= 2
        count = jnp.minimum(pos + 1, w).astype(F32)
        pooled = acc[POOL_PAD:, :] / count - e
        y = jnp.dot(pooled.astype(BF16), wp_ref[g].astype(BF16), preferred_element_type=F32)
        yp_ref[:, lo:hi] = (y * sc_ref[:, lo:hi]).astype(BF16)


def _inproj(x2, g, w, w_pool, pool_scale, seq):
    t = x2.shape[0]
    d_in = w.shape[1]
    n_grp = len(POOL_WINDOWS)
    return pl.pallas_call(
        functools.partial(_inproj_kernel, tiles_per_seq=seq // ROW_TILE),
        grid=(t // ROW_TILE,),
        in_specs=[
            pl.BlockSpec((ROW_TILE, D_MODEL), lambda i: (i, 0)),
            pl.BlockSpec((1, D_MODEL), lambda i: (0, 0)),
            pl.BlockSpec((D_MODEL, d_in), lambda i: (0, 0)),
            pl.BlockSpec((n_grp, POOL_GROUP, POOL_GROUP), lambda i: (0, 0, 0)),
            pl.BlockSpec((1, D_POOL), lambda i: (0, 0)),
        ],
        out_specs=[
            pl.BlockSpec((ROW_TILE, D_POOL), lambda i: (i, 0)),
            pl.BlockSpec((ROW_TILE, 3 * D_ATTN), lambda i: (i, 0)),
        ],
        out_shape=[
            jax.ShapeDtypeStruct((t, D_POOL), BF16),
            jax.ShapeDtypeStruct((t, 3 * D_ATTN), BF16),
        ],
        scratch_shapes=[pltpu.VMEM((D_MODEL, d_in), BF16),
                        pltpu.VMEM((n_grp, POOL_PAD, POOL_GROUP), F32),
                        pltpu.VMEM((2 * POOL_PAD + ROW_TILE, POOL_GROUP), F32)],
        compiler_params=pltpu.CompilerParams(
            dimension_semantics=("arbitrary",), vmem_limit_bytes=VMEM_LIMIT),
        name="inproj_pool",
    )(x2, g, w, w_pool, pool_scale)


def _pair_block(q, k, v_ext, mask2, lane_h0):
    nq = q.shape[0]
    zero = jnp.zeros_like(q)
    q2 = jnp.concatenate([jnp.where(lane_h0, q, zero), jnp.where(lane_h0, zero, q)], axis=0)
    s = lax.dot_general(q2, k, (((1,), (1,)), ((), ())), preferred_element_type=F32)
    s = jnp.where(mask2, s, NEG_INF)
    m = jnp.max(s, axis=-1, keepdims=True)
    p = jnp.exp2(s - m).astype(BF16)
    ol = jnp.dot(p, v_ext, preferred_element_type=F32)
    o = jnp.where(lane_h0, ol[:nq, :LANES], ol[nq:, :LANES])
    l = jnp.where(lane_h0, ol[:nq, LANES:], ol[nq:, LANES:])
    mb = jnp.where(lane_h0, m[:nq], m[nq:])
    return o, mb, l


def _merge(o_a, m_a, l_a, o_b, m_b, l_b):
    m = jnp.maximum(m_a, m_b)
    ea = jnp.exp2(m_a - m)
    eb = jnp.exp2(m_b - m)
    return o_a * ea + o_b * eb, m, l_a * ea + l_b * eb


def _both_heads(mask):
    return jnp.concatenate([mask, mask], axis=0)


def _attn_kernel(q_ref, k_ref, v_ref, o_ref, qp, kp, vpx, vnx, o23, m23, l23, o2, m2, l2,
                 on, mn):
    s = q_ref.shape[0]
    n_chunk = DIL_FAR
    rows = s // n_chunk
    sub = rows // DIL_MID
    grp = n_chunk * n_chunk
    lane_h0 = lax.broadcasted_iota(I32, (1, LANES), 1) < HEAD_DIM

    @pl.when((pl.program_id(0) == 0) & (pl.program_id(1) == 0))
    def _():
        ones = jnp.ones((s, LANES), BF16)
        vnx[:, LANES:] = ones
        vpx[:, LANES:] = ones

    vnx[:, :LANES] = v_ref[...]

    pi = lax.broadcasted_iota(I32, (grp, grp), 0)
    pj = lax.broadcasted_iota(I32, (grp, grp), 1)
    swap = (pj == (pi % n_chunk) * n_chunk + pi // n_chunk).astype(BF16)
    n_grp = s // grp
    wide = jnp.concatenate([ref[g * grp:(g + 1) * grp, :]
                            for ref in (q_ref, k_ref, v_ref) for g in range(n_grp)], axis=1)
    moved = jnp.dot(swap, wide, preferred_element_type=F32).astype(BF16)
    for which, dst_ref in enumerate((qp, kp, vpx)):
        for g in range(n_grp):
            col = (which * n_grp + g) * LANES
            for r in range(n_chunk):
                dst = slice(r * rows + g * n_chunk, r * rows + (g + 1) * n_chunk)
                dst_ref[dst, 0:LANES] = moved[r * n_chunk:(r + 1) * n_chunk, col:col + LANES]

    qi = lax.broadcasted_iota(I32, (QB, QB), 0)
    kj = lax.broadcasted_iota(I32, (QB, QB), 1)
    causal = _both_heads(qi >= kj)

    def mid_index(n_key_sub):
        i_q = lax.broadcasted_iota(I32, (QB, DIL_MID * n_key_sub), 0)
        i_k = lax.broadcasted_iota(I32, (QB, DIL_MID * n_key_sub), 1)
        return i_q // sub, i_q % sub, i_k // n_key_sub, i_k % n_key_sub

    jq, aq, jk, ak = mid_index(sub)
    d0 = DIL_FAR * (aq - ak) + DIL_MID * (jq - jk)
    mask_mid0 = _both_heads(d0 >= 0)
    jq, aq, jk, ak = mid_index(2 * sub)
    d1 = DIL_FAR * (aq + sub - ak) + DIL_MID * (jq - jk)
    mask_mid = _both_heads((d1 >= 0) & (d1 <= DIL_MID * QB))

    def mid_tile(r4, a0, k0, nk, mask):
        def at(j, off, n):
            return pl.ds((DIL_MID * j + r4) * rows + off, n)

        q = jnp.concatenate([qp[at(j, a0, sub), :] for j in range(DIL_MID)], axis=0)
        k = jnp.concatenate([kp[at(j, k0, nk), :] for j in range(DIL_MID)], axis=0)
        v = jnp.concatenate([vpx[at(j, k0, nk), :] for j in range(DIL_MID)], axis=0)
        o, m, l = _pair_block(q, k, v, mask, lane_h0)
        for j in range(DIL_MID):
            dst = at(j, a0, sub)
            src = slice(j * sub, (j + 1) * sub)
            o2[dst, :] = o[src]
            m2[dst, :] = m[src]
            l2[dst, :] = l[src]

    fq = lax.broadcasted_iota(I32, (FAR_BATCH * rows, FAR_BATCH * rows), 0)
    fk = lax.broadcasted_iota(I32, (FAR_BATCH * rows, FAR_BATCH * rows), 1)
    far_mask = _both_heads((fq // rows == fk // rows) & (fq >= fk))
    for r in range(0, n_chunk, FAR_BATCH):
        blk = slice(r * rows, (r + FAR_BATCH) * rows)
        o, m, l = _pair_block(qp[blk, :], kp[blk, :], vpx[blk, :], far_mask, lane_h0)
        o23[blk, :] = o
        m23[blk, :] = m
        l23[blk, :] = l

    for r4 in range(DIL_MID):
        mid_tile(r4, 0, 0, sub, mask_mid0)
        for a_blk in range(1, rows // sub):
            mid_tile(r4, a_blk * sub, (a_blk - 1) * sub, 2 * sub, mask_mid)

    for r in range(n_chunk):
        blk = slice(r * rows, (r + 1) * rows)
        oo, mm, ll = _merge(o23[blk, :], m23[blk, :], l23[blk, :],
                            o2[blk, :], m2[blk, :], l2[blk, :])
        o23[blk, :] = oo / ll
        m23[blk, :] = mm + jnp.log2(ll)

    back = jnp.concatenate(
        [jnp.concatenate([o23[r * rows + g * n_chunk:r * rows + (g + 1) * n_chunk, :].astype(BF16)
                          for r in range(n_chunk)], axis=0) for g in range(n_grp)], axis=1)
    restored = jnp.dot(swap, back, preferred_element_type=F32)
    for g in range(n_grp):
        on[g * grp:(g + 1) * grp, :] = restored[:, g * LANES:(g + 1) * LANES]
    for r in range(n_chunk):
        mn[pl.ds(r, rows, stride=n_chunk), :] = m23[r * rows:(r + 1) * rows, :]

    qi2 = lax.broadcasted_iota(I32, (QB, 2 * QB), 0)
    kj2 = lax.broadcasted_iota(I32, (QB, 2 * QB), 1)
    dn = qi2 + QB - kj2
    mask_near = _both_heads((dn >= 0) & (dn <= QB))

    def near_finish(dst, o, m, l):
        oo, _, ll = _merge(on[dst, :], mn[dst, :], 1.0, o, m, l)
        o_ref[dst, :] = (oo / ll).astype(o_ref.dtype)

    first = pl.ds(0, QB)
    o, m, l = _pair_block(q_ref[first, :], k_ref[first, :], vnx[first, :], causal, lane_h0)
    near_finish(first, o, m, l)
    for n in range(1, s // QB):
        keys = pl.ds((n - 1) * QB, 2 * QB)
        o, m, l = _pair_block(q_ref[pl.ds(n * QB, QB), :], k_ref[keys, :], vnx[keys, :],
                              mask_near, lane_h0)
        near_finish(pl.ds(n * QB, QB), o, m, l)


def _attention(qkv, batch, seq):
    n_pair = D_ATTN // LANES
    blk = (seq, LANES)
    f32_scr = pltpu.VMEM(blk, F32)
    bf_scr = pltpu.VMEM(blk, BF16)
    bfx_scr = pltpu.VMEM((seq, 2 * LANES), BF16)
    return pl.pallas_call(
        _attn_kernel,
        grid=(batch, n_pair),
        in_specs=[
            pl.BlockSpec(blk, lambda b, h: (b, h)),
            pl.BlockSpec(blk, lambda b, h: (b, n_pair + h)),
            pl.BlockSpec(blk, lambda b, h: (b, 2 * n_pair + h)),
        ],
        out_specs=pl.BlockSpec(blk, lambda b, h: (b, h)),
        out_shape=jax.ShapeDtypeStruct((batch * seq, D_ATTN), BF16),
        scratch_shapes=[bf_scr, bf_scr, bfx_scr, bfx_scr] + [f32_scr] * 8,
        compiler_params=pltpu.CompilerParams(
            dimension_semantics=("arbitrary", "arbitrary"), vmem_limit_bytes=VMEM_LIMIT),
        name="dilated_attn",
    )(qkv, qkv, qkv)


META_ROWS = 16
META_EIDX, META_GATE, META_RANK = 0, 4, 8


def _outproj_kernel(x_ref, yp_ref, ya_ref, wo_ref, g_ref, wrt_ref, brt_ref,
                    x1_ref, h2_ref, meta_ref, cnt_ref, wo_bf, before, carry):
    tm = x_ref.shape[0]

    @pl.when(pl.program_id(0) == 0)
    def _():
        wo_bf[...] = wo_ref[...].astype(BF16)
        carry[...] = jnp.zeros_like(carry)
        ti = lax.broadcasted_iota(I32, (tm, tm), 0)
        tj = lax.broadcasted_iota(I32, (tm, tm), 1)
        before[...] = (ti < tj).astype(BF16)

    x1 = (x_ref[...]
          + jnp.dot(yp_ref[...], wo_bf[:D_POOL, :], preferred_element_type=F32)
          + jnp.dot(ya_ref[...], wo_bf[D_POOL:, :], preferred_element_type=F32))
    x1_ref[...] = x1
    h2 = _rms(x1, g_ref[...])
    h2_ref[...] = _pack_row(h2)

    logits_t = lax.dot_general(wrt_ref[...].astype(BF16), h2.astype(BF16),
                               (((1,), (1,)), ((), ())), preferred_element_type=F32)
    logits_t = logits_t + brt_ref[:, 0:1]
    eid = lax.broadcasted_iota(I32, (N_EXPERTS, tm), 0)
    work = logits_t
    idxs, vals = [], []
    for _ in range(TOP_K):
        mx = jnp.max(work, axis=0, keepdims=True)
        idx = jnp.min(jnp.where(work == mx, eid, N_EXPERTS), axis=0, keepdims=True)
        idxs.append(idx)
        vals.append(mx)
        work = jnp.where(eid == idx, -jnp.inf, work)
    exps = [jnp.exp(v - vals[0]) for v in vals]
    den = exps[0] + exps[1] + exps[2] + exps[3]

    onehot = jnp.zeros((N_EXPERTS, tm), F32)
    for idx in idxs:
        onehot = onehot + (eid == idx).astype(F32)
    rank_e = carry[:, 0:1] + jnp.dot(onehot.astype(BF16), before[...],
                                     preferred_element_type=F32)
    carry[...] = carry[...] + jnp.sum(onehot, axis=1, keepdims=True)
    cnt_ref[...] = carry[...]

    mrow = lax.broadcasted_iota(I32, (META_ROWS, tm), 0)
    meta = jnp.zeros((META_ROWS, tm), F32)
    for k in range(TOP_K):
        rank_k = jnp.sum(jnp.where(eid == idxs[k], rank_e, 0.0), axis=0, keepdims=True)
        meta = jnp.where(mrow == META_EIDX + k, idxs[k].astype(F32), meta)
        meta = jnp.where(mrow == META_GATE + k, exps[k] / den, meta)
        meta = jnp.where(mrow == META_RANK + k, rank_k, meta)
    meta_ref[...] = meta


def _outproj(x2, y_pool, y_attn, w_out, g, w_router_t, b_router_t):
    t = x2.shape[0]
    row = lambda i: (i, 0)
    const = lambda i: (0, 0)
    return pl.pallas_call(
        _outproj_kernel,
        grid=(t // ROW_TILE,),
        in_specs=[
            pl.BlockSpec((ROW_TILE, D_MODEL), row),
            pl.BlockSpec((ROW_TILE, D_POOL), row),
            pl.BlockSpec((ROW_TILE, D_ATTN), row),
            pl.BlockSpec((D_MODEL, D_MODEL), const),
            pl.BlockSpec((1, D_MODEL), const),
            pl.BlockSpec((N_EXPERTS, D_MODEL), const),
            pl.BlockSpec((N_EXPERTS, LANES), const),
        ],
        out_specs=[
            pl.BlockSpec((ROW_TILE, D_MODEL), row),
            pl.BlockSpec((ROW_TILE, D_PACKED), row),
            pl.BlockSpec((META_ROWS, ROW_TILE), lambda i: (0, i)),
            pl.BlockSpec((N_EXPERTS, LANES), const),
        ],
        out_shape=[
            jax.ShapeDtypeStruct((t, D_MODEL), F32),
            jax.ShapeDtypeStruct((t, D_PACKED), I32),
            jax.ShapeDtypeStruct((META_ROWS, t), F32),
            jax.ShapeDtypeStruct((N_EXPERTS, LANES), F32),
        ],
        scratch_shapes=[pltpu.VMEM((D_MODEL, D_MODEL), BF16),
                        pltpu.VMEM((ROW_TILE, ROW_TILE), BF16),
                        pltpu.VMEM((N_EXPERTS, LANES), F32)],
        compiler_params=pltpu.CompilerParams(
            dimension_semantics=("arbitrary",), vmem_limit_bytes=VMEM_LIMIT),
        name="outproj_router",
    )(x2, y_pool, y_attn, w_out, g, w_router_t, b_router_t)


def _sc_mesh():
    return plsc.VectorSubcoreMesh(core_axis_name="core", subcore_axis_name="subcore")


def _sc_dispatch(h2, pos_rows, p_max):
    t = h2.shape[0]

    @functools.partial(
        pl.kernel, mesh=_sc_mesh(),
        out_type=[jax.ShapeDtypeStruct((p_max, D_PART), h2.dtype)] * N_PART)
    def run(h_hbm, *refs):
        pos_hbm, xs_hbm = refs[:TOP_K], refs[TOP_K:]
        for c in range(N_PART):
            def body(x_vmem, *idx_vmem, dst=xs_hbm[c]):
                for iv in idx_vmem:
                    pltpu.sync_copy(x_vmem, dst.at[iv.at[0]])

            pltpu.emit_pipeline(
                body,
                grid=(t // SC_WINDOW,),
                in_specs=[pl.BlockSpec((SC_WINDOW, D_PART), lambda i, c=c: (i, c))]
                + [pl.BlockSpec((1, SC_WINDOW), lambda i: (0, i))] * TOP_K,
                out_specs=[],
                core_axis_name=("core", "subcore"),
                dimension_semantics=(pltpu.PARALLEL,),
            )(h_hbm, *pos_hbm)

    return run(h2, *pos_rows)


def _sc_unpermute(y_parts, idx_row):
    n = idx_row.shape[1]

    @functools.partial(
        pl.kernel, mesh=_sc_mesh(),
        out_type=[jax.ShapeDtypeStruct((n, D_PART), y_parts[0].dtype)] * N_PART)
    def run(*refs):
        y_hbm, i_hbm, o_hbm = refs[:N_PART], refs[N_PART], refs[N_PART + 1:]
        for c in range(N_PART):
            def body(i_vmem, o_vmem, src=y_hbm[c]):
                pltpu.sync_copy(src.at[i_vmem.at[0]], o_vmem)

            pltpu.emit_pipeline(
                body,
                grid=(n // SC_WINDOW,),
                in_specs=[pl.BlockSpec((1, SC_WINDOW), lambda i: (0, i))],
                out_specs=[pl.BlockSpec((SC_WINDOW, D_PART), lambda i: (i, 0))],
                core_axis_name=("core", "subcore"),
                dimension_semantics=(pltpu.PARALLEL,),
            )(i_hbm, o_hbm[c])

    return run(*y_parts, idx_row)


def _moe_kernel(te_ref, nv_ref, rows_ref, nne_ref, *refs):
    xs_refs = refs[:N_PART]
    bgu_ref, bd_ref, wgu_hbm, wd_hbm = refs[N_PART:N_PART + 4]
    y_refs = refs[N_PART + 4:2 * N_PART + 4]
    wgu_f32, wd_f32, wgu_bf, wd_bf, slot_ref, sem = refs[2 * N_PART + 4:]
    i = pl.program_id(0)

    def weight_copies(e):
        return (pltpu.make_async_copy(wgu_hbm.at[e], wgu_f32, sem.at[0]),
                pltpu.make_async_copy(wd_hbm.at[e], wd_f32, sem.at[1]))

    def cast_weights(slot):
        wgu_bf[slot] = wgu_f32[...].astype(BF16)
        wd_bf[slot] = wd_f32[...].astype(BF16)

    @pl.when(i < nv_ref[0])
    def _():
        prev = te_ref[jnp.maximum(i - 1, 0)]
        new_expert = jnp.logical_or(i == 0, te_ref[i] != prev)
        nxt = nne_ref[te_ref[i]]
        after_next = nne_ref[jnp.maximum(nxt, 0)]

        @pl.when(i == 0)
        def _():
            for cp in weight_copies(te_ref[0]):
                cp.start()
            for cp in weight_copies(te_ref[0]):
                cp.wait()
            cast_weights(0)
            slot_ref[0] = 0

            @pl.when(nxt >= 0)
            def _():
                for cp in weight_copies(nxt):
                    cp.start()

        @pl.when(jnp.logical_and(new_expert, i > 0))
        def _():
            slot_ref[0] = 1 - slot_ref[0]

        slot = slot_ref[0]
        d_e = wd_bf.shape[1]

        def ffn(row0, n_rows, s):
            rows = pl.ds(row0, n_rows)
            x = _unpack_row(jnp.concatenate([r[rows, :] for r in xs_refs], axis=1))
            rid = row0 + lax.broadcasted_iota(I32, (n_rows, 1), 0)
            x = jnp.where(rid < rows_ref[i], x, 0.0).astype(BF16)
            gu = jnp.dot(x, wgu_bf[s], preferred_element_type=F32) + bgu_ref[0]
            gate = jnp.minimum(gu[:, :d_e], SWIGLU_LIMIT)
            lin = jnp.clip(gu[:, d_e:], -SWIGLU_LIMIT, SWIGLU_LIMIT)
            act = gate * jax.nn.sigmoid(SWIGLU_ALPHA * gate) * (lin + 1.0)
            y = jnp.dot(act.astype(BF16), wd_bf[s], preferred_element_type=F32) + bd_ref[0]
            yp = _pack_row(y)
            for c, y_ref in enumerate(y_refs):
                y_ref[rows, :] = yp[:, c * D_PART:(c + 1) * D_PART]

        full = rows_ref[i] > MOE_TILE - MOE_PIECE
        first_with_next = jnp.logical_and(new_expert, nxt >= 0)
        fused = jnp.logical_and(first_with_next, full)

        @pl.when(first_with_next)
        def _():
            for cp in weight_copies(nxt):
                cp.wait()

        for s in range(2):
            @pl.when(jnp.logical_and(fused, slot == s))
            def _(s=s):
                ffn(0, MOE_TILE, s)
                cast_weights(1 - s)

        @pl.when(jnp.logical_and(first_with_next, jnp.logical_not(full)))
        def _():
            cast_weights(1 - slot)

        @pl.when(jnp.logical_and(first_with_next, after_next >= 0))
        def _():
            for cp in weight_copies(after_next):
                cp.start()

        @pl.when(jnp.logical_and(full, jnp.logical_not(first_with_next)))
        def _():
            ffn(0, MOE_TILE, slot)

        @pl.when(jnp.logical_not(full))
        def _():
            for y_ref in y_refs:
                y_ref[...] = jnp.zeros_like(y_ref)

            def piece(j, carry):
                ffn(pl.multiple_of(j * MOE_PIECE, MOE_PIECE), MOE_PIECE, slot)
                return carry

            lax.fori_loop(0, (rows_ref[i] + MOE_PIECE - 1) // MOE_PIECE, piece, 0)


def _moe(tile_expert, n_valid, tile_rows, next_nonempty, xs_parts, w_gu, b_gu, w_down, b_down):
    p_max = xs_parts[0].shape[0]
    n_tiles = p_max // MOE_TILE
    d_e = w_down.shape[1]

    def row(i, te, nv, tr, nx):
        return (jnp.minimum(i, nv[0] - 1), 0)

    def expert(i, te, nv, tr, nx):
        return (te[jnp.minimum(i, nv[0] - 1)], 0, 0)

    grid_spec = pltpu.PrefetchScalarGridSpec(
        num_scalar_prefetch=4,
        grid=(n_tiles,),
        in_specs=[pl.BlockSpec((MOE_TILE, D_PART), row)] * N_PART + [
            pl.BlockSpec((1, 1, 2 * d_e), expert),
            pl.BlockSpec((1, 1, D_MODEL), expert),
            pl.BlockSpec(memory_space=pl.ANY),
            pl.BlockSpec(memory_space=pl.ANY),
        ],
        out_specs=[pl.BlockSpec((MOE_TILE, D_PART), row)] * N_PART,
        scratch_shapes=[pltpu.VMEM((D_MODEL, 2 * d_e), F32), pltpu.VMEM((d_e, D_MODEL), F32),
                        pltpu.VMEM((2, D_MODEL, 2 * d_e), BF16), pltpu.VMEM((2, d_e, D_MODEL), BF16),
                        pltpu.SMEM((1,), I32), pltpu.SemaphoreType.DMA((2,))],
    )
    return pl.pallas_call(
        _moe_kernel,
        grid_spec=grid_spec,
        out_shape=[jax.ShapeDtypeStruct((p_max, D_PART), I32)] * N_PART,
        compiler_params=pltpu.CompilerParams(
            dimension_semantics=("arbitrary",), vmem_limit_bytes=VMEM_LIMIT),
        name="moe_ffn",
    )(tile_expert, n_valid, tile_rows, next_nonempty, *xs_parts, b_gu, b_down, w_gu, w_down)


def _combine_kernel(x1_ref, gate_ref, g_ref, *refs):
    yk_refs, o_ref = refs[:N_PART], refs[-1]
    acc = x1_ref[...]
    for k in range(TOP_K):
        y_k = _unpack_row(jnp.concatenate([r[k] for r in yk_refs], axis=1))
        acc = acc + gate_ref[:, k:k + 1] * y_k
    o_ref[...] = _rms(acc, g_ref[...])


def _combine(x1, gates, g, yk_parts):
    t, d = x1.shape
    tq = COMBINE_TILE
    row = lambda i: (i, 0)
    return pl.pallas_call(
        _combine_kernel,
        grid=(t // tq,),
        in_specs=[
            pl.BlockSpec((tq, d), row),
            pl.BlockSpec((tq, TOP_K), row),
            pl.BlockSpec((1, d), lambda i: (0, 0)),
        ] + [pl.BlockSpec((TOP_K, tq, D_PART), lambda i: (0, i, 0))] * N_PART,
        out_specs=pl.BlockSpec((tq, d), row),
        out_shape=jax.ShapeDtypeStruct((t, d), F32),
        compiler_params=pltpu.CompilerParams(
            dimension_semantics=("arbitrary",), vmem_limit_bytes=VMEM_LIMIT),
        name="combine_final",
    )(x1, gates, g, *yk_parts)


def kernel(x, g_mix, w_in, w_pool, pool_scale, w_out, g_ffn, w_router, b_router,
           w_gu, b_gu, w_down, b_down, g_final):
    batch, seq, d = x.shape
    t = batch * seq
    assert d == D_MODEL and seq == DIL_FAR * QB, (x.shape, "unsupported shape")
    assert seq % ROW_TILE == 0 and t % COMBINE_TILE == 0 and t % SC_WINDOW == 0
    assert w_in.shape[0] == 1, "one layer"
    x2 = x.reshape(t, d)

    y_pool, qkv = _inproj(x2, g_mix[0].reshape(1, d), w_in[0], w_pool[0],
                          pool_scale[0].reshape(1, D_POOL), seq)
    y_attn = _attention(qkv, batch, seq)

    wr_t = w_router[0].T
    br_t = jnp.broadcast_to(b_router[0].reshape(N_EXPERTS, 1), (N_EXPERTS, LANES))
    x1, h2, meta, cnt = _outproj(x2, y_pool, y_attn, w_out[0], g_ffn[0].reshape(1, d), wr_t, br_t)

    eidx = meta[META_EIDX:META_EIDX + TOP_K].astype(I32)
    rank = meta[META_RANK:META_RANK + TOP_K].astype(I32)
    gates = meta[META_GATE:META_GATE + TOP_K].T
    counts = cnt[:, 0].astype(I32)
    padded = ((counts + MOE_TILE - 1) // MOE_TILE) * MOE_TILE
    ends = jnp.cumsum(padded)
    offsets = ends - padded
    e_ids = jnp.arange(N_EXPERTS, dtype=I32)
    hit = eidx[None] == e_ids[:, None, None]
    pos = rank + jnp.sum(jnp.where(hit, offsets[:, None, None], 0), axis=0)
    p_max = t * TOP_K + N_EXPERTS * MOE_TILE
    n_tiles = p_max // MOE_TILE
    tile_start = jnp.arange(n_tiles, dtype=I32) * MOE_TILE
    tile_expert = jnp.minimum(
        jnp.sum((tile_start[:, None] >= ends[None, :]).astype(I32), axis=1), N_EXPERTS - 1)
    mine = tile_expert[:, None] == e_ids[None, :]
    group_end = jnp.sum(jnp.where(mine, (offsets + counts)[None, :], 0), axis=1)
    tile_rows = jnp.clip(group_end - tile_start, 0, MOE_TILE)
    n_valid = (ends[-1] // MOE_TILE).reshape(1).astype(I32)
    later = (e_ids[None, :] > e_ids[:, None]) & (counts[None, :] > 0)
    next_nonempty = jnp.min(jnp.where(later, e_ids[None, :], N_EXPERTS), axis=1)
    next_nonempty = jnp.where(next_nonempty < N_EXPERTS, next_nonempty, -1).astype(I32)

    xs = _sc_dispatch(h2, [pos[k:k + 1] for k in range(TOP_K)], p_max)
    ys = _moe(tile_expert, n_valid, tile_rows, next_nonempty, xs, w_gu[0],
              b_gu[0].reshape(N_EXPERTS, 1, -1), w_down[0], b_down[0].reshape(N_EXPERTS, 1, -1))
    yk = _sc_unpermute(ys, pos.reshape(1, TOP_K * t))
    yk = [q.reshape(TOP_K, t, D_PART) for q in yk]
    out = _combine(x1, gates, g_final.reshape(1, d), yk)
    return out.reshape(batch, seq, d)
```

```python
import functools

import jax
import jax.numpy as jnp
from jax import lax
from jax.experimental import pallas as pl
from jax.experimental.pallas import tpu as pltpu
from jax.experimental.pallas import tpu_sc as plsc

F32 = jnp.float32
BF16 = jnp.bfloat16
I32 = jnp.int32

D_MODEL = 1024
D_POOL = 512
D_ATTN = 512
POOL_WINDOWS = (2, 4, 8, 16)
POOL_GROUP = 128
HEAD_DIM = 64
N_EXPERTS = 32
TOP_K = 4
SWIGLU_LIMIT = 7.0
SWIGLU_ALPHA = 1.702
EPS = 1e-5
NEG_INF = -1e30
LOG2_E = 1.4426950408889634

LANES = 128
QB = 128
DIL_MID = 4
DIL_FAR = 16
FAR_BATCH = 2
ROW_TILE = 1024
MOE_TILE = 1024
MOE_PIECE = 256
COMBINE_TILE = 1024
SC_WINDOW = 128
D_PACKED = D_MODEL // 2
N_PART = 2
D_PART = D_PACKED // N_PART
VMEM_LIMIT = 56 * 1024 * 1024


def _pack_row(x):
    hi = lax.bitcast_convert_type(x[:, :D_PACKED].astype(BF16).astype(F32), I32)
    lo = lax.bitcast_convert_type(x[:, D_PACKED:].astype(BF16).astype(F32), I32)
    return hi | lax.shift_right_logical(lo, 16)


def _unpack_row(w):
    hi = lax.bitcast_convert_type(w & jnp.int32(-65536), F32)
    lo = lax.bitcast_convert_type(lax.shift_left(w, 16), F32)
    return jnp.concatenate([hi, lo], axis=1)


def _rms(x, g):
    ms = jnp.mean(x * x, axis=-1, keepdims=True)
    return x * lax.rsqrt(ms + EPS) * g


POOL_PAD = 16


def _inproj_kernel(x_ref, g_ref, w_ref, wp_ref, sc_ref, yp_ref, qkv_ref, w_bf, hist, pad_ref,
                   *, tiles_per_seq):
    i = pl.program_id(0)
    tm = x_ref.shape[0]

    @pl.when(i == 0)
    def _():
        w_bf[...] = w_ref[...].astype(BF16)
        pad_ref[0:POOL_PAD, :] = jnp.zeros((POOL_PAD, POOL_GROUP), F32)

    @pl.when(i % tiles_per_seq == 0)
    def _():
        hist[...] = jnp.zeros_like(hist)

    h = _rms(x_ref[...], g_ref[...]).astype(BF16)
    proj = jnp.dot(h, w_bf[...], preferred_element_type=F32)
    q = proj[:, D_POOL:D_POOL + D_ATTN] * (HEAD_DIM ** -0.5 * LOG2_E)
    qkv_ref[:, :D_ATTN] = q.astype(BF16)
    qkv_ref[:, D_ATTN:] = proj[:, D_POOL + D_ATTN:].astype(BF16)

    pos = (i % tiles_per_seq) * tm + lax.broadcasted_iota(I32, (tm, 1), 0)
    for g, w in enumerate(POOL_WINDOWS):
        lo, hi = g * POOL_GROUP, (g + 1) * POOL_GROUP
        e = proj[:, lo:hi]
        acc = jnp.concatenate([hist[g], e], axis=0)
        hist[g] = e[tm - POOL_PAD:, :]
        span = 1
        while span < w:
            pad_ref[POOL_PAD:, :] = acc
            acc = acc + pad_ref[pl.ds(POOL_PAD - span, POOL_PAD + tm), :]
            span *= 2
        count = jnp.minimum(pos + 1, w).astype(F32)
        pooled = acc[POOL_PAD:, :] / count - e
        y = jnp.dot(pooled.astype(BF16), wp_ref[g].astype(BF16), preferred_element_type=F32)
        yp_ref[:, lo:hi] = (y * sc_ref[:, lo:hi]).astype(BF16)


def _inproj(x2, g, w, w_pool, pool_scale, seq):
    t = x2.shape[0]
    d_in = w.shape[1]
    n_grp = len(POOL_WINDOWS)
    return pl.pallas_call(
        functools.partial(_inproj_kernel, tiles_per_seq=seq // ROW_TILE),
        grid=(t // ROW_TILE,),
        in_specs=[
            pl.BlockSpec((ROW_TILE, D_MODEL), lambda i: (i, 0)),
            pl.BlockSpec((1, D_MODEL), lambda i: (0, 0)),
            pl.BlockSpec((D_MODEL, d_in), lambda i: (0, 0)),
            pl.BlockSpec((n_grp, POOL_GROUP, POOL_GROUP), lambda i: (0, 0, 0)),
            pl.BlockSpec((1, D_POOL), lambda i: (0, 0)),
        ],
        out_specs=[
            pl.BlockSpec((ROW_TILE, D_POOL), lambda i: (i, 0)),
            pl.BlockSpec((ROW_TILE, 3 * D_ATTN), lambda i: (i, 0)),
        ],
        out_shape=[
            jax.ShapeDtypeStruct((t, D_POOL), BF16),
            jax.ShapeDtypeStruct((t, 3 * D_ATTN), BF16),
        ],
        scratch_shapes=[pltpu.VMEM((D_MODEL, d_in), BF16),
                        pltpu.VMEM((n_grp, POOL_PAD, POOL_GROUP), F32),
                        pltpu.VMEM((2 * POOL_PAD + ROW_TILE, POOL_GROUP), F32)],
        compiler_params=pltpu.CompilerParams(
            dimension_semantics=("arbitrary",), vmem_limit_bytes=VMEM_LIMIT),
        name="inproj_pool",
    )(x2, g, w, w_pool, pool_scale)


def _pair_block(q, k, v_ext, mask2, lane_h0):
    nq = q.shape[0]
    zero = jnp.zeros_like(q)
    q2 = jnp.concatenate([jnp.where(lane_h0, q, zero), jnp.where(lane_h0, zero, q)], axis=0)
    s = lax.dot_general(q2, k, (((1,), (1,)), ((), ())), preferred_element_type=F32)
    s = jnp.where(mask2, s, NEG_INF)
    m = jnp.max(s, axis=-1, keepdims=True)
    p = jnp.exp2(s - m).astype(BF16)
    ol = jnp.dot(p, v_ext, preferred_element_type=F32)
    o = jnp.where(lane_h0, ol[:nq, :LANES], ol[nq:, :LANES])
    l = jnp.where(lane_h0, ol[:nq, LANES:], ol[nq:, LANES:])
    mb = jnp.where(lane_h0, m[:nq], m[nq:])
    return o, mb, l


def _merge(o_a, m_a, l_a, o_b, m_b, l_b):
    m = jnp.maximum(m_a, m_b)
    ea = jnp.exp2(m_a - m)
    eb = jnp.exp2(m_b - m)
    return o_a * ea + o_b * eb, m, l_a * ea + l_b * eb


def _both_heads(mask):
    return jnp.concatenate([mask, mask], axis=0)


def _attn_kernel(q_ref, k_ref, v_ref, o_ref, qp, kp, vpx, vnx, o23, m23, l23, o2, m2, l2,
                 on, mn):
    s = q_ref.shape[0]
    n_chunk = DIL_FAR
    rows = s // n_chunk
    sub = rows // DIL_MID
    grp = n_chunk * n_chunk
    lane_h0 = lax.broadcasted_iota(I32, (1, LANES), 1) < HEAD_DIM

    @pl.when((pl.program_id(0) == 0) & (pl.program_id(1) == 0))
    def _():
        ones = jnp.ones((s, LANES), BF16)
        vnx[:, LANES:] = ones
        vpx[:, LANES:] = ones

    vnx[:, :LANES] = v_ref[...]

    pi = lax.broadcasted_iota(I32, (grp, grp), 0)
    pj = lax.broadcasted_iota(I32, (grp, grp), 1)
    swap = (pj == (pi % n_chunk) * n_chunk + pi // n_chunk).astype(BF16)
    n_grp = s // grp
    wide = jnp.concatenate([ref[g * grp:(g + 1) * grp, :]
                            for ref in (q_ref, k_ref, v_ref) for g in range(n_grp)], axis=1)
    moved = jnp.dot(swap, wide, preferred_element_type=F32).astype(BF16)
    for which, dst_ref in enumerate((qp, kp, vpx)):
        for g in range(n_grp):
            col = (which * n_grp + g) * LANES
            for r in range(n_chunk):
                dst = slice(r * rows + g * n_chunk, r * rows + (g + 1) * n_chunk)
                dst_ref[dst, 0:LANES] = moved[r * n_chunk:(r + 1) * n_chunk, col:col + LANES]

    qi = lax.broadcasted_iota(I32, (QB, QB), 0)
    kj = lax.broadcasted_iota(I32, (QB, QB), 1)
    causal = _both_heads(qi >= kj)

    def mid_index(n_key_sub):
        i_q = lax.broadcasted_iota(I32, (QB, DIL_MID * n_key_sub), 0)
        i_k = lax.broadcasted_iota(I32, (QB, DIL_MID * n_key_sub), 1)
        return i_q // sub, i_q % sub, i_k // n_key_sub, i_k % n_key_sub

    jq, aq, jk, ak = mid_index(sub)
    d0 = DIL_FAR * (aq - ak) + DIL_MID * (jq - jk)
    mask_mid0 = _both_heads(d0 >= 0)
    jq, aq, jk, ak = mid_index(2 * sub)
    d1 = DIL_FAR * (aq + sub - ak) + DIL_MID * (jq - jk)
    mask_mid = _both_heads((d1 >= 0) & (d1 <= DIL_MID * QB))

    def mid_tile(r4, a0, k0, nk, mask):
        def at(j, off, n):
            return pl.ds((DIL_MID * j + r4) * rows + off, n)

        q = jnp.concatenate([qp[at(j, a0, sub), :] for j in range(DIL_MID)], axis=0)
        k = jnp.concatenate([kp[at(j, k0, nk), :] for j in range(DIL_MID)], axis=0)
        v = jnp.concatenate([vpx[at(j, k0, nk), :] for j in range(DIL_MID)], axis=0)
        o, m, l = _pair_block(q, k, v, mask, lane_h0)
        for j in range(DIL_MID):
            dst = at(j, a0, sub)
            src = slice(j * sub, (j + 1) * sub)
            o2[dst, :] = o[src]
            m2[dst, :] = m[src]
            l2[dst, :] = l[src]

    fq = lax.broadcasted_iota(I32, (FAR_BATCH * rows, FAR_BATCH * rows), 0)
    fk = lax.broadcasted_iota(I32, (FAR_BATCH * rows, FAR_BATCH * rows), 1)
    far_mask = _both_heads((fq // rows == fk // rows) & (fq >= fk))
    for r in range(0, n_chunk, FAR_BATCH):
        blk = slice(r * rows, (r + FAR_BATCH) * rows)
        o, m, l = _pair_block(qp[blk, :], kp[blk, :], vpx[blk, :], far_mask, lane_h0)
        o23[blk, :] = o
        m23[blk, :] = m
        l23[blk, :] = l

    for r4 in range(DIL_MID):
        mid_tile(r4, 0, 0, sub, mask_mid0)
        for a_blk in range(1, rows // sub):
            mid_tile(r4, a_blk * sub, (a_blk - 1) * sub, 2 * sub, mask_mid)

    for r in range(n_chunk):
        blk = slice(r * rows, (r + 1) * rows)
        oo, mm, ll = _merge(o23[blk, :], m23[blk, :], l23[blk, :],
                            o2[blk, :], m2[blk, :], l2[blk, :])
        o23[blk, :] = oo / ll
        m23[blk, :] = mm + jnp.log2(ll)

    back = jnp.concatenate(
        [jnp.concatenate([o23[r * rows + g * n_chunk:r * rows + (g + 1) * n_chunk, :].astype(BF16)
                          for r in range(n_chunk)], axis=0) for g in range(n_grp)], axis=1)
    restored = jnp.dot(swap, back, preferred_element_type=F32)
    for g in range(n_grp):
        on[g * grp:(g + 1) * grp, :] = restored[:, g * LANES:(g + 1) * LANES]
    for r in range(n_chunk):
        mn[pl.ds(r, rows, stride=n_chunk), :] = m23[r * rows:(r + 1) * rows, :]

    qi2 = lax.broadcasted_iota(I32, (QB, 2 * QB), 0)
    kj2 = lax.broadcasted_iota(I32, (QB, 2 * QB), 1)
    dn = qi2 + QB - kj2
    mask_near = _both_heads((dn >= 0) & (dn <= QB))

    def near_finish(dst, o, m, l):
        oo, _, ll = _merge(on[dst, :], mn[dst, :], 1.0, o, m, l)
        o_ref[dst, :] = (oo / ll).astype(o_ref.dtype)

    first = pl.ds(0, QB)
    o, m, l = _pair_block(q_ref[first, :], k_ref[first, :], vnx[first, :], causal, lane_h0)
    near_finish(first, o, m, l)
    for n in range(1, s // QB):
        keys = pl.ds((n - 1) * QB, 2 * QB)
        o, m, l = _pair_block(q_ref[pl.ds(n * QB, QB), :], k_ref[keys, :], vnx[keys, :],
                              mask_near, lane_h0)
        near_finish(pl.ds(n * QB, QB), o, m, l)


def _attention(qkv, batch, seq):
    n_pair = D_ATTN // LANES
    blk = (seq, LANES)
    f32_scr = pltpu.VMEM(blk, F32)
    bf_scr = pltpu.VMEM(blk, BF16)
    bfx_scr = pltpu.VMEM((seq, 2 * LANES), BF16)
    return pl.pallas_call(
        _attn_kernel,
        grid=(batch, n_pair),
        in_specs=[
            pl.BlockSpec(blk, lambda b, h: (b, h)),
            pl.BlockSpec(blk, lambda b, h: (b, n_pair + h)),
            pl.BlockSpec(blk, lambda b, h: (b, 2 * n_pair + h)),
        ],
        out_specs=pl.BlockSpec(blk, lambda b, h: (b, h)),
        out_shape=jax.ShapeDtypeStruct((batch * seq, D_ATTN), BF16),
        scratch_shapes=[bf_scr, bf_scr, bfx_scr, bfx_scr] + [f32_scr] * 8,
        compiler_params=pltpu.CompilerParams(
            dimension_semantics=("arbitrary", "arbitrary"), vmem_limit_bytes=VMEM_LIMIT),
        name="dilated_attn",
    )(qkv, qkv, qkv)


META_ROWS = 16
META_EIDX, META_GATE, META_RANK = 0, 4, 8


def _outproj_kernel(x_ref, yp_ref, ya_ref, wo_ref, g_ref, wrt_ref, brt_ref,
                    x1_ref, h2_ref, meta_ref, cnt_ref, wo_bf, before, carry):
    tm = x_ref.shape[0]

    @pl.when(pl.program_id(0) == 0)
    def _():
        wo_bf[...] = wo_ref[...].astype(BF16)
        carry[...] = jnp.zeros_like(carry)
        ti = lax.broadcasted_iota(I32, (tm, tm), 0)
        tj = lax.broadcasted_iota(I32, (tm, tm), 1)
        before[...] = (ti < tj).astype(BF16)

    x1 = (x_ref[...]
          + jnp.dot(yp_ref[...], wo_bf[:D_POOL, :], preferred_element_type=F32)
          + jnp.dot(ya_ref[...], wo_bf[D_POOL:, :], preferred_element_type=F32))
    x1_ref[...] = x1
    h2 = _rms(x1, g_ref[...])
    h2_ref[...] = _pack_row(h2)

    logits_t = lax.dot_general(wrt_ref[...].astype(BF16), h2.astype(BF16),
                               (((1,), (1,)), ((), ())), preferred_element_type=F32)
    logits_t = logits_t + brt_ref[:, 0:1]
    eid = lax.broadcasted_iota(I32, (N_EXPERTS, tm), 0)
    work = logits_t
    idxs, vals = [], []
    for _ in range(TOP_K):
        mx = jnp.max(work, axis=0, keepdims=True)
        idx = jnp.min(jnp.where(work == mx, eid, N_EXPERTS), axis=0, keepdims=True)
        idxs.append(idx)
        vals.append(mx)
        work = jnp.where(eid == idx, -jnp.inf, work)
    exps = [jnp.exp(v - vals[0]) for v in vals]
    den = exps[0] + exps[1] + exps[2] + exps[3]

    onehot = jnp.zeros((N_EXPERTS, tm), F32)
    for idx in idxs:
        onehot = onehot + (eid == idx).astype(F32)
    rank_e = carry[:, 0:1] + jnp.dot(onehot.astype(BF16), before[...],
                                     preferred_element_type=F32)
    carry[...] = carry[...] + jnp.sum(onehot, axis=1, keepdims=True)
    cnt_ref[...] = carry[...]

    mrow = lax.broadcasted_iota(I32, (META_ROWS, tm), 0)
    meta = jnp.zeros((META_ROWS, tm), F32)
    for k in range(TOP_K):
        rank_k = jnp.sum(jnp.where(eid == idxs[k], rank_e, 0.0), axis=0, keepdims=True)
        meta = jnp.where(mrow == META_EIDX + k, idxs[k].astype(F32), meta)
        meta = jnp.where(mrow == META_GATE + k, exps[k] / den, meta)
        meta = jnp.where(mrow == META_RANK + k, rank_k, meta)
    meta_ref[...] = meta


def _outproj(x2, y_pool, y_attn, w_out, g, w_router_t, b_router_t):
    t = x2.shape[0]
    row = lambda i: (i, 0)
    const = lambda i: (0, 0)
    return pl.pallas_call(
        _outproj_kernel,
        grid=(t // ROW_TILE,),
        in_specs=[
            pl.BlockSpec((ROW_TILE, D_MODEL), row),
            pl.BlockSpec((ROW_TILE, D_POOL), row),
            pl.BlockSpec((ROW_TILE, D_ATTN), row),
            pl.BlockSpec((D_MODEL, D_MODEL), const),
            pl.BlockSpec((1, D_MODEL), const),
            pl.BlockSpec((N_EXPERTS, D_MODEL), const),
            pl.BlockSpec((N_EXPERTS, LANES), const),
        ],
        out_specs=[
            pl.BlockSpec((ROW_TILE, D_MODEL), row),
            pl.BlockSpec((ROW_TILE, D_PACKED), row),
            pl.BlockSpec((META_ROWS, ROW_TILE), lambda i: (0, i)),
            pl.BlockSpec((N_EXPERTS, LANES), const),
        ],
        out_shape=[
            jax.ShapeDtypeStruct((t, D_MODEL), F32),
            jax.ShapeDtypeStruct((t, D_PACKED), I32),
            jax.ShapeDtypeStruct((META_ROWS, t), F32),
            jax.ShapeDtypeStruct((N_EXPERTS, LANES), F32),
        ],
        scratch_shapes=[pltpu.VMEM((D_MODEL, D_MODEL), BF16),
                        pltpu.VMEM((ROW_TILE, ROW_TILE), BF16),
                        pltpu.VMEM((N_EXPERTS, LANES), F32)],
        compiler_params=pltpu.CompilerParams(
            dimension_semantics=("arbitrary",), vmem_limit_bytes=VMEM_LIMIT),
        name="outproj_router",
    )(x2, y_pool, y_attn, w_out, g, w_router_t, b_router_t)


def _sc_mesh():
    return plsc.VectorSubcoreMesh(core_axis_name="core", subcore_axis_name="subcore")


def _sc_dispatch(h2, pos_rows, p_max):
    t = h2.shape[0]

    @functools.partial(
        pl.kernel, mesh=_sc_mesh(),
        out_type=[jax.ShapeDtypeStruct((p_max, D_PART), h2.dtype)] * N_PART)
    def run(h_hbm, *refs):
        pos_hbm, xs_hbm = refs[:TOP_K], refs[TOP_K:]
        for c in range(N_PART):
            def body(x_vmem, *idx_vmem, dst=xs_hbm[c]):
                for iv in idx_vmem:
                    pltpu.sync_copy(x_vmem, dst.at[iv.at[0]])

            pltpu.emit_pipeline(
                body,
                grid=(t // SC_WINDOW,),
                in_specs=[pl.BlockSpec((SC_WINDOW, D_PART), lambda i, c=c: (i, c))]
                + [pl.BlockSpec((1, SC_WINDOW), lambda i: (0, i))] * TOP_K,
                out_specs=[],
                core_axis_name=("core", "subcore"),
                dimension_semantics=(pltpu.PARALLEL,),
            )(h_hbm, *pos_hbm)

    return run(h2, *pos_rows)


def _sc_unpermute(y_parts, idx_row):
    n = idx_row.shape[1]

    @functools.partial(
        pl.kernel, mesh=_sc_mesh(),
        out_type=[jax.ShapeDtypeStruct((n, D_PART), y_parts[0].dtype)] * N_PART)
    def run(*refs):
        y_hbm, i_hbm, o_hbm = refs[:N_PART], refs[N_PART], refs[N_PART + 1:]
        for c in range(N_PART):
            def body(i_vmem, o_vmem, src=y_hbm[c]):
                pltpu.sync_copy(src.at[i_vmem.at[0]], o_vmem)

            pltpu.emit_pipeline(
                body,
                grid=(n // SC_WINDOW,),
                in_specs=[pl.BlockSpec((1, SC_WINDOW), lambda i: (0, i))],
                out_specs=[pl.BlockSpec((SC_WINDOW, D_PART), lambda i: (i, 0))],
                core_axis_name=("core", "subcore"),
                dimension_semantics=(pltpu.PARALLEL,),
            )(i_hbm, o_hbm[c])

    return run(*y_parts, idx_row)


def _moe_kernel(te_ref, nv_ref, rows_ref, nxt_ref, *refs):
    xs_refs = refs[:N_PART]
    bgu_ref, bd_ref, wgu_hbm, wd_hbm = refs[N_PART:N_PART + 4]
    y_refs = refs[N_PART + 4:2 * N_PART + 4]
    wgu_f32, wd_f32, wgu_bf, wd_bf, sem = refs[2 * N_PART + 4:]
    i = pl.program_id(0)

    def weight_copies(e):
        return (pltpu.make_async_copy(wgu_hbm.at[e], wgu_f32, sem.at[0]),
                pltpu.make_async_copy(wd_hbm.at[e], wd_f32, sem.at[1]))

    @pl.when(i < nv_ref[0])
    def _():
        prev = te_ref[jnp.maximum(i - 1, 0)]
        new_expert = jnp.logical_or(i == 0, te_ref[i] != prev)

        @pl.when(i == 0)
        def _():
            for cp in weight_copies(te_ref[0]):
                cp.start()

        @pl.when(new_expert)
        def _():
            for cp in weight_copies(te_ref[i]):
                cp.wait()
            wgu_bf[...] = wgu_f32[...].astype(BF16)
            wd_bf[...] = wd_f32[...].astype(BF16)

            @pl.when(nxt_ref[i] >= 0)
            def _():
                for cp in weight_copies(nxt_ref[i]):
                    cp.start()

        d_e = wd_bf.shape[0]

        def ffn(row0, n_rows):
            rows = pl.ds(row0, n_rows)
            x = _unpack_row(jnp.concatenate([r[rows, :] for r in xs_refs], axis=1))
            rid = row0 + lax.broadcasted_iota(I32, (n_rows, 1), 0)
            x = jnp.where(rid < rows_ref[i], x, 0.0).astype(BF16)
            gu = jnp.dot(x, wgu_bf[...], preferred_element_type=F32) + bgu_ref[0]
            gate = jnp.minimum(gu[:, :d_e], SWIGLU_LIMIT)
            lin = jnp.clip(gu[:, d_e:], -SWIGLU_LIMIT, SWIGLU_LIMIT)
            act = gate * jax.nn.sigmoid(SWIGLU_ALPHA * gate) * (lin + 1.0)
            y = jnp.dot(act.astype(BF16), wd_bf[...], preferred_element_type=F32) + bd_ref[0]
            yp = _pack_row(y)
            for c, y_ref in enumerate(y_refs):
                y_ref[rows, :] = yp[:, c * D_PART:(c + 1) * D_PART]

        full = rows_ref[i] > MOE_TILE - MOE_PIECE

        @pl.when(full)
        def _():
            ffn(0, MOE_TILE)

        @pl.when(jnp.logical_not(full))
        def _():
            for y_ref in y_refs:
                y_ref[...] = jnp.zeros_like(y_ref)

            def piece(j, carry):
                ffn(pl.multiple_of(j * MOE_PIECE, MOE_PIECE), MOE_PIECE)
                return carry

            lax.fori_loop(0, (rows_ref[i] + MOE_PIECE - 1) // MOE_PIECE, piece, 0)


def _moe(tile_expert, n_valid, tile_rows, next_expert, xs_parts, w_gu, b_gu, w_down, b_down):
    p_max = xs_parts[0].shape[0]
    n_tiles = p_max // MOE_TILE
    d_e = w_down.shape[1]

    def row(i, te, nv, tr, nx):
        return (jnp.minimum(i, nv[0] - 1), 0)

    def expert(i, te, nv, tr, nx):
        return (te[jnp.minimum(i, nv[0] - 1)], 0, 0)

    grid_spec = pltpu.PrefetchScalarGridSpec(
        num_scalar_prefetch=4,
        grid=(n_tiles,),
        in_specs=[pl.BlockSpec((MOE_TILE, D_PART), row)] * N_PART + [
            pl.BlockSpec((1, 1, 2 * d_e), expert),
            pl.BlockSpec((1, 1, D_MODEL), expert),
            pl.BlockSpec(memory_space=pl.ANY),
            pl.BlockSpec(memory_space=pl.ANY),
        ],
        out_specs=[pl.BlockSpec((MOE_TILE, D_PART), row)] * N_PART,
        scratch_shapes=[pltpu.VMEM((D_MODEL, 2 * d_e), F32), pltpu.VMEM((d_e, D_MODEL), F32),
                        pltpu.VMEM((D_MODEL, 2 * d_e), BF16), pltpu.VMEM((d_e, D_MODEL), BF16),
                        pltpu.SemaphoreType.DMA((2,))],
    )
    return pl.pallas_call(
        _moe_kernel,
        grid_spec=grid_spec,
        out_shape=[jax.ShapeDtypeStruct((p_max, D_PART), I32)] * N_PART,
        compiler_params=pltpu.CompilerParams(
            dimension_semantics=("arbitrary",), vmem_limit_bytes=VMEM_LIMIT),
        name="moe_ffn",
    )(tile_expert, n_valid, tile_rows, next_expert, *xs_parts, b_gu, b_down, w_gu, w_down)


def _combine_kernel(x1_ref, gate_ref, g_ref, *refs):
    yk_refs, o_ref = refs[:N_PART], refs[-1]
    acc = x1_ref[...]
    for k in range(TOP_K):
        y_k = _unpack_row(jnp.concatenate([r[k] for r in yk_refs], axis=1))
        acc = acc + gate_ref[:, k:k + 1] * y_k
    o_ref[...] = _rms(acc, g_ref[...])


def _combine(x1, gates, g, yk_parts):
    t, d = x1.shape
    tq = COMBINE_TILE
    row = lambda i: (i, 0)
    deep = pl.Buffered(3)

    def stream(*hbm_refs):
        pltpu.emit_pipeline(
            _combine_kernel,
            grid=(t // tq,),
            in_specs=[
                pl.BlockSpec((tq, d), row, pipeline_mode=deep),
                pl.BlockSpec((tq, TOP_K), row),
                pl.BlockSpec((1, d), lambda i: (0, 0)),
            ] + [pl.BlockSpec((TOP_K, tq, D_PART), lambda i: (0, i, 0),
                              pipeline_mode=deep)] * N_PART,
            out_specs=[pl.BlockSpec((tq, d), row)],
        )(*hbm_refs)

    return pl.pallas_call(
        stream,
        in_specs=[pl.BlockSpec(memory_space=pl.ANY)] * (3 + N_PART),
        out_specs=pl.BlockSpec(memory_space=pl.ANY),
        out_shape=jax.ShapeDtypeStruct((t, d), F32),
        compiler_params=pltpu.CompilerParams(vmem_limit_bytes=VMEM_LIMIT),
        name="combine_final",
    )(x1, gates, g, *yk_parts)


def kernel(x, g_mix, w_in, w_pool, pool_scale, w_out, g_ffn, w_router, b_router,
           w_gu, b_gu, w_down, b_down, g_final):
    batch, seq, d = x.shape
    t = batch * seq
    assert d == D_MODEL and seq == DIL_FAR * QB, (x.shape, "unsupported shape")
    assert seq % ROW_TILE == 0 and t % COMBINE_TILE == 0 and t % SC_WINDOW == 0
    assert w_in.shape[0] == 1, "one layer"
    x2 = x.reshape(t, d)

    y_pool, qkv = _inproj(x2, g_mix[0].reshape(1, d), w_in[0], w_pool[0],
                          pool_scale[0].reshape(1, D_POOL), seq)
    y_attn = _attention(qkv, batch, seq)

    wr_t = w_router[0].T
    br_t = jnp.broadcast_to(b_router[0].reshape(N_EXPERTS, 1), (N_EXPERTS, LANES))
    x1, h2, meta, cnt = _outproj(x2, y_pool, y_attn, w_out[0], g_ffn[0].reshape(1, d), wr_t, br_t)

    eidx = meta[META_EIDX:META_EIDX + TOP_K].astype(I32)
    rank = meta[META_RANK:META_RANK + TOP_K].astype(I32)
    gates = meta[META_GATE:META_GATE + TOP_K].T
    counts = cnt[:, 0].astype(I32)
    padded = ((counts + MOE_TILE - 1) // MOE_TILE) * MOE_TILE
    ends = jnp.cumsum(padded)
    offsets = ends - padded
    e_ids = jnp.arange(N_EXPERTS, dtype=I32)
    hit = eidx[None] == e_ids[:, None, None]
    pos = rank + jnp.sum(jnp.where(hit, offsets[:, None, None], 0), axis=0)
    p_max = t * TOP_K + N_EXPERTS * MOE_TILE
    n_tiles = p_max // MOE_TILE
    tile_start = jnp.arange(n_tiles, dtype=I32) * MOE_TILE
    tile_expert = jnp.minimum(
        jnp.sum((tile_start[:, None] >= ends[None, :]).astype(I32), axis=1), N_EXPERTS - 1)
    mine = tile_expert[:, None] == e_ids[None, :]
    group_end = jnp.sum(jnp.where(mine, (offsets + counts)[None, :], 0), axis=1)
    tile_rows = jnp.clip(group_end - tile_start, 0, MOE_TILE)
    n_valid = (ends[-1] // MOE_TILE).reshape(1).astype(I32)
    later = (e_ids[None, :] > e_ids[:, None]) & (counts[None, :] > 0)
    next_nonempty = jnp.min(jnp.where(later, e_ids[None, :], N_EXPERTS), axis=1)
    next_nonempty = jnp.where(next_nonempty < N_EXPERTS, next_nonempty, -1)
    next_expert = jnp.sum(jnp.where(mine, next_nonempty[None, :], 0), axis=1)

    xs = _sc_dispatch(h2, [pos[k:k + 1] for k in range(TOP_K)], p_max)
    ys = _moe(tile_expert, n_valid, tile_rows, next_expert, xs, w_gu[0],
              b_gu[0].reshape(N_EXPERTS, 1, -1), w_down[0], b_down[0].reshape(N_EXPERTS, 1, -1))
    yk = _sc_unpermute(ys, pos.reshape(1, TOP_K * t))
    yk = [q.reshape(TOP_K, t, D_PART) for q in yk]
    out = _combine(x1, gates, g_final.reshape(1, d), yk)
    return out.reshape(batch, seq, d)
```

```python
import functools

import jax
import jax.numpy as jnp
from jax import lax
from jax.experimental import pallas as pl
from jax.experimental.pallas import tpu as pltpu
from jax.experimental.pallas import tpu_sc as plsc

F32 = jnp.float32
BF16 = jnp.bfloat16
I32 = jnp.int32

D_MODEL = 1024
D_POOL = 512
D_ATTN = 512
POOL_WINDOWS = (2, 4, 8, 16)
POOL_GROUP = 128
HEAD_DIM = 64
N_EXPERTS = 32
TOP_K = 4
SWIGLU_LIMIT = 7.0
SWIGLU_ALPHA = 1.702
EPS = 1e-5
NEG_INF = -1e30
LOG2_E = 1.4426950408889634

LANES = 128
QB = 128
DIL_MID = 4
DIL_FAR = 16
FAR_BATCH = 2
ROW_TILE = 1024
MOE_TILE = 1024
MOE_PIECE = 256
COMBINE_TILE = 1024
SC_WINDOW = 128
D_PACKED = D_MODEL // 2
N_PART = 2
D_PART = D_PACKED // N_PART
VMEM_LIMIT = 56 * 1024 * 1024


def _pack_row(x):
    hi = lax.bitcast_convert_type(x[:, :D_PACKED].astype(BF16).astype(F32), I32)
    lo = lax.bitcast_convert_type(x[:, D_PACKED:].astype(BF16).astype(F32), I32)
    return hi | lax.shift_right_logical(lo, 16)


def _unpack_row(w):
    hi = lax.bitcast_convert_type(w & jnp.int32(-65536), F32)
    lo = lax.bitcast_convert_type(lax.shift_left(w, 16), F32)
    return jnp.concatenate([hi, lo], axis=1)


def _rms(x, g):
    ms = jnp.mean(x * x, axis=-1, keepdims=True)
    return x * lax.rsqrt(ms + EPS) * g


POOL_PAD = 16


def _inproj_kernel(x_ref, g_ref, w_ref, wp_ref, sc_ref, yp_ref, qkv_ref, w_bf, hist, pad_ref,
                   *, tiles_per_seq):
    i = pl.program_id(0)
    tm = x_ref.shape[0]

    @pl.when(i == 0)
    def _():
        w_bf[...] = w_ref[...].astype(BF16)
        pad_ref[0:POOL_PAD, :] = jnp.zeros((POOL_PAD, POOL_GROUP), F32)

    @pl.when(i % tiles_per_seq == 0)
    def _():
        hist[...] = jnp.zeros_like(hist)

    h = _rms(x_ref[...], g_ref[...]).astype(BF16)
    proj = jnp.dot(h, w_bf[...], preferred_element_type=F32)
    q = proj[:, D_POOL:D_POOL + D_ATTN] * (HEAD_DIM ** -0.5 * LOG2_E)
    qkv_ref[:, :D_ATTN] = q.astype(BF16)
    qkv_ref[:, D_ATTN:] = proj[:, D_POOL + D_ATTN:].astype(BF16)

    pos = (i % tiles_per_seq) * tm + lax.broadcasted_iota(I32, (tm, 1), 0)
    for g, w in enumerate(POOL_WINDOWS):
        lo, hi = g * POOL_GROUP, (g + 1) * POOL_GROUP
        e = proj[:, lo:hi]
        acc = jnp.concatenate([hist[g], e], axis=0)
        hist[g] = e[tm - POOL_PAD:, :]
        span = 1
        while span < w:
            pad_ref[POOL_PAD:, :] = acc
            acc = acc + pad_ref[pl.ds(POOL_PAD - span, POOL_PAD + tm), :]
            span *= 2
        count = jnp.minimum(pos + 1, w).astype(F32)
        pooled = acc[POOL_PAD:, :] / count - e
        y = jnp.dot(pooled.astype(BF16), wp_ref[g].astype(BF16), preferred_element_type=F32)
        yp_ref[:, lo:hi] = (y * sc_ref[:, lo:hi]).astype(BF16)


def _inproj(x2, g, w, w_pool, pool_scale, seq):
    t = x2.shape[0]
    d_in = w.shape[1]
    n_grp = len(POOL_WINDOWS)
    return pl.pallas_call(
        functools.partial(_inproj_kernel, tiles_per_seq=seq // ROW_TILE),
        grid=(t // ROW_TILE,),
        in_specs=[
            pl.BlockSpec((ROW_TILE, D_MODEL), lambda i: (i, 0)),
            pl.BlockSpec((1, D_MODEL), lambda i: (0, 0)),
            pl.BlockSpec((D_MODEL, d_in), lambda i: (0, 0)),
            pl.BlockSpec((n_grp, POOL_GROUP, POOL_GROUP), lambda i: (0, 0, 0)),
            pl.BlockSpec((1, D_POOL), lambda i: (0, 0)),
        ],
        out_specs=[
            pl.BlockSpec((ROW_TILE, D_POOL), lambda i: (i, 0)),
            pl.BlockSpec((ROW_TILE, 3 * D_ATTN), lambda i: (i, 0)),
        ],
        out_shape=[
            jax.ShapeDtypeStruct((t, D_POOL), BF16),
            jax.ShapeDtypeStruct((t, 3 * D_ATTN), BF16),
        ],
        scratch_shapes=[pltpu.VMEM((D_MODEL, d_in), BF16),
                        pltpu.VMEM((n_grp, POOL_PAD, POOL_GROUP), F32),
                        pltpu.VMEM((2 * POOL_PAD + ROW_TILE, POOL_GROUP), F32)],
        compiler_params=pltpu.CompilerParams(
            dimension_semantics=("arbitrary",), vmem_limit_bytes=VMEM_LIMIT),
        name="inproj_pool",
    )(x2, g, w, w_pool, pool_scale)


def _pair_block(q, k, v_ext, mask2, lane_h0):
    nq = q.shape[0]
    zero = jnp.zeros_like(q)
    q2 = jnp.concatenate([jnp.where(lane_h0, q, zero), jnp.where(lane_h0, zero, q)], axis=0)
    s = lax.dot_general(q2, k, (((1,), (1,)), ((), ())), preferred_element_type=F32)
    s = jnp.where(mask2, s, NEG_INF)
    m = jnp.max(s, axis=-1, keepdims=True)
    p = jnp.exp2(s - m).astype(BF16)
    ol = jnp.dot(p, v_ext, preferred_element_type=F32)
    o = jnp.where(lane_h0, ol[:nq, :LANES], ol[nq:, :LANES])
    l = jnp.where(lane_h0, ol[:nq, LANES:], ol[nq:, LANES:])
    mb = jnp.where(lane_h0, m[:nq], m[nq:])
    return o, mb, l


def _merge(o_a, m_a, l_a, o_b, m_b, l_b):
    m = jnp.maximum(m_a, m_b)
    ea = jnp.exp2(m_a - m)
    eb = jnp.exp2(m_b - m)
    return o_a * ea + o_b * eb, m, l_a * ea + l_b * eb


def _both_heads(mask):
    return jnp.concatenate([mask, mask], axis=0)


def _attn_kernel(q_ref, k_ref, v_ref, o_ref, qp, kp, vpx, vnx, o23, m23, l23, o2, m2, l2,
                 on, mn):
    s = q_ref.shape[0]
    n_chunk = DIL_FAR
    rows = s // n_chunk
    sub = rows // DIL_MID
    grp = n_chunk * n_chunk
    lane_h0 = lax.broadcasted_iota(I32, (1, LANES), 1) < HEAD_DIM

    @pl.when((pl.program_id(0) == 0) & (pl.program_id(1) == 0))
    def _():
        ones = jnp.ones((s, LANES), BF16)
        vnx[:, LANES:] = ones
        vpx[:, LANES:] = ones

    vnx[:, :LANES] = v_ref[...]

    pi = lax.broadcasted_iota(I32, (grp, grp), 0)
    pj = lax.broadcasted_iota(I32, (grp, grp), 1)
    swap = (pj == (pi % n_chunk) * n_chunk + pi // n_chunk).astype(BF16)
    n_grp = s // grp
    wide = jnp.concatenate([ref[g * grp:(g + 1) * grp, :]
                            for ref in (q_ref, k_ref, v_ref) for g in range(n_grp)], axis=1)
    moved = jnp.dot(swap, wide, preferred_element_type=F32).astype(BF16)
    for which, dst_ref in enumerate((qp, kp, vpx)):
        for g in range(n_grp):
            col = (which * n_grp + g) * LANES
            for r in range(n_chunk):
                dst = slice(r * rows + g * n_chunk, r * rows + (g + 1) * n_chunk)
                dst_ref[dst, 0:LANES] = moved[r * n_chunk:(r + 1) * n_chunk, col:col + LANES]

    qi = lax.broadcasted_iota(I32, (QB, QB), 0)
    kj = lax.broadcasted_iota(I32, (QB, QB), 1)
    causal = _both_heads(qi >= kj)

    def mid_index(n_key_sub):
        i_q = lax.broadcasted_iota(I32, (QB, DIL_MID * n_key_sub), 0)
        i_k = lax.broadcasted_iota(I32, (QB, DIL_MID * n_key_sub), 1)
        return i_q // sub, i_q % sub, i_k // n_key_sub, i_k % n_key_sub

    jq, aq, jk, ak = mid_index(sub)
    d0 = DIL_FAR * (aq - ak) + DIL_MID * (jq - jk)
    mask_mid0 = _both_heads(d0 >= 0)
    jq, aq, jk, ak = mid_index(2 * sub)
    d1 = DIL_FAR * (aq + sub - ak) + DIL_MID * (jq - jk)
    mask_mid = _both_heads((d1 >= 0) & (d1 <= DIL_MID * QB))

    def mid_tile(r4, a0, k0, nk, mask):
        def at(j, off, n):
            return pl.ds((DIL_MID * j + r4) * rows + off, n)

        q = jnp.concatenate([qp[at(j, a0, sub), :] for j in range(DIL_MID)], axis=0)
        k = jnp.concatenate([kp[at(j, k0, nk), :] for j in range(DIL_MID)], axis=0)
        v = jnp.concatenate([vpx[at(j, k0, nk), :] for j in range(DIL_MID)], axis=0)
        o, m, l = _pair_block(q, k, v, mask, lane_h0)
        for j in range(DIL_MID):
            dst = at(j, a0, sub)
            src = slice(j * sub, (j + 1) * sub)
            o2[dst, :] = o[src]
            m2[dst, :] = m[src]
            l2[dst, :] = l[src]

    fq = lax.broadcasted_iota(I32, (FAR_BATCH * rows, FAR_BATCH * rows), 0)
    fk = lax.broadcasted_iota(I32, (FAR_BATCH * rows, FAR_BATCH * rows), 1)
    far_mask = _both_heads((fq // rows == fk // rows) & (fq >= fk))
    for r in range(0, n_chunk, FAR_BATCH):
        blk = slice(r * rows, (r + FAR_BATCH) * rows)
        o, m, l = _pair_block(qp[blk, :], kp[blk, :], vpx[blk, :], far_mask, lane_h0)
        o23[blk, :] = o
        m23[blk, :] = m
        l23[blk, :] = l

    for r4 in range(DIL_MID):
        mid_tile(r4, 0, 0, sub, mask_mid0)
        for a_blk in range(1, rows // sub):
            mid_tile(r4, a_blk * sub, (a_blk - 1) * sub, 2 * sub, mask_mid)

    for r in range(n_chunk):
        blk = slice(r * rows, (r + 1) * rows)
        oo, mm, ll = _merge(o23[blk, :], m23[blk, :], l23[blk, :],
                            o2[blk, :], m2[blk, :], l2[blk, :])
        o23[blk, :] = oo / ll
        m23[blk, :] = mm + jnp.log2(ll)

    back = jnp.concatenate(
        [jnp.concatenate([o23[r * rows + g * n_chunk:r * rows + (g + 1) * n_chunk, :].astype(BF16)
                          for r in range(n_chunk)], axis=0) for g in range(n_grp)], axis=1)
    restored = jnp.dot(swap, back, preferred_element_type=F32)
    for g in range(n_grp):
        on[g * grp:(g + 1) * grp, :] = restored[:, g * LANES:(g + 1) * LANES]
    for r in range(n_chunk):
        mn[pl.ds(r, rows, stride=n_chunk), :] = m23[r * rows:(r + 1) * rows, :]

    qi2 = lax.broadcasted_iota(I32, (QB, 2 * QB), 0)
    kj2 = lax.broadcasted_iota(I32, (QB, 2 * QB), 1)
    dn = qi2 + QB - kj2
    mask_near = _both_heads((dn >= 0) & (dn <= QB))

    def near_finish(dst, o, m, l):
        oo, _, ll = _merge(on[dst, :], mn[dst, :], 1.0, o, m, l)
        o_ref[dst, :] = (oo / ll).astype(o_ref.dtype)

    first = pl.ds(0, QB)
    o, m, l = _pair_block(q_ref[first, :], k_ref[first, :], vnx[first, :], causal, lane_h0)
    near_finish(first, o, m, l)
    for n in range(1, s // QB):
        keys = pl.ds((n - 1) * QB, 2 * QB)
        o, m, l = _pair_block(q_ref[pl.ds(n * QB, QB), :], k_ref[keys, :], vnx[keys, :],
                              mask_near, lane_h0)
        near_finish(pl.ds(n * QB, QB), o, m, l)


def _attention(qkv, batch, seq):
    n_pair = D_ATTN // LANES
    blk = (seq, LANES)
    f32_scr = pltpu.VMEM(blk, F32)
    bf_scr = pltpu.VMEM(blk, BF16)
    bfx_scr = pltpu.VMEM((seq, 2 * LANES), BF16)
    return pl.pallas_call(
        _attn_kernel,
        grid=(batch, n_pair),
        in_specs=[
            pl.BlockSpec(blk, lambda b, h: (b, h)),
            pl.BlockSpec(blk, lambda b, h: (b, n_pair + h)),
            pl.BlockSpec(blk, lambda b, h: (b, 2 * n_pair + h)),
        ],
        out_specs=pl.BlockSpec(blk, lambda b, h: (b, h)),
        out_shape=jax.ShapeDtypeStruct((batch * seq, D_ATTN), BF16),
        scratch_shapes=[bf_scr, bf_scr, bfx_scr, bfx_scr] + [f32_scr] * 8,
        compiler_params=pltpu.CompilerParams(
            dimension_semantics=("arbitrary", "arbitrary"), vmem_limit_bytes=VMEM_LIMIT),
        name="dilated_attn",
    )(qkv, qkv, qkv)


META_ROWS = 16
META_EIDX, META_GATE, META_RANK = 0, 4, 8


def _outproj_kernel(x_ref, yp_ref, ya_ref, wo_ref, g_ref, wrt_ref, brt_ref,
                    x1_ref, h2_ref, meta_ref, cnt_ref, wo_bf, before, carry):
    tm = x_ref.shape[0]

    @pl.when(pl.program_id(0) == 0)
    def _():
        wo_bf[...] = wo_ref[...].astype(BF16)
        carry[...] = jnp.zeros_like(carry)
        ti = lax.broadcasted_iota(I32, (tm, tm), 0)
        tj = lax.broadcasted_iota(I32, (tm, tm), 1)
        before[...] = (ti < tj).astype(BF16)

    x1 = (x_ref[...]
          + jnp.dot(yp_ref[...], wo_bf[:D_POOL, :], preferred_element_type=F32)
          + jnp.dot(ya_ref[...], wo_bf[D_POOL:, :], preferred_element_type=F32))
    x1_ref[...] = x1
    h2 = _rms(x1, g_ref[...])
    h2_ref[...] = _pack_row(h2)

    logits_t = lax.dot_general(wrt_ref[...].astype(BF16), h2.astype(BF16),
                               (((1,), (1,)), ((), ())), preferred_element_type=F32)
    logits_t = logits_t + brt_ref[:, 0:1]
    eid = lax.broadcasted_iota(I32, (N_EXPERTS, tm), 0)
    work = logits_t
    idxs, vals = [], []
    for _ in range(TOP_K):
        mx = jnp.max(work, axis=0, keepdims=True)
        idx = jnp.min(jnp.where(work == mx, eid, N_EXPERTS), axis=0, keepdims=True)
        idxs.append(idx)
        vals.append(mx)
        work = jnp.where(eid == idx, -jnp.inf, work)
    exps = [jnp.exp(v - vals[0]) for v in vals]
    den = exps[0] + exps[1] + exps[2] + exps[3]

    onehot = jnp.zeros((N_EXPERTS, tm), F32)
    for idx in idxs:
        onehot = onehot + (eid == idx).astype(F32)
    rank_e = carry[:, 0:1] + jnp.dot(onehot.astype(BF16), before[...],
                                     preferred_element_type=F32)
    carry[...] = carry[...] + jnp.sum(onehot, axis=1, keepdims=True)
    cnt_ref[...] = carry[...]

    mrow = lax.broadcasted_iota(I32, (META_ROWS, tm), 0)
    meta = jnp.zeros((META_ROWS, tm), F32)
    for k in range(TOP_K):
        rank_k = jnp.sum(jnp.where(eid == idxs[k], rank_e, 0.0), axis=0, keepdims=True)
        meta = jnp.where(mrow == META_EIDX + k, idxs[k].astype(F32), meta)
        meta = jnp.where(mrow == META_GATE + k, exps[k] / den, meta)
        meta = jnp.where(mrow == META_RANK + k, rank_k, meta)
    meta_ref[...] = meta


def _outproj(x2, y_pool, y_attn, w_out, g, w_router_t, b_router_t):
    t = x2.shape[0]
    row = lambda i: (i, 0)
    const = lambda i: (0, 0)
    return pl.pallas_call(
        _outproj_kernel,
        grid=(t // ROW_TILE,),
        in_specs=[
            pl.BlockSpec((ROW_TILE, D_MODEL), row),
            pl.BlockSpec((ROW_TILE, D_POOL), row),
            pl.BlockSpec((ROW_TILE, D_ATTN), row),
            pl.BlockSpec((D_MODEL, D_MODEL), const),
            pl.BlockSpec((1, D_MODEL), const),
            pl.BlockSpec((N_EXPERTS, D_MODEL), const),
            pl.BlockSpec((N_EXPERTS, LANES), const),
        ],
        out_specs=[
            pl.BlockSpec((ROW_TILE, D_MODEL), row),
            pl.BlockSpec((ROW_TILE, D_PACKED), row),
            pl.BlockSpec((META_ROWS, ROW_TILE), lambda i: (0, i)),
            pl.BlockSpec((N_EXPERTS, LANES), const),
        ],
        out_shape=[
            jax.ShapeDtypeStruct((t, D_MODEL), F32),
            jax.ShapeDtypeStruct((t, D_PACKED), I32),
            jax.ShapeDtypeStruct((META_ROWS, t), F32),
            jax.ShapeDtypeStruct((N_EXPERTS, LANES), F32),
        ],
        scratch_shapes=[pltpu.VMEM((D_MODEL, D_MODEL), BF16),
                        pltpu.VMEM((ROW_TILE, ROW_TILE), BF16),
                        pltpu.VMEM((N_EXPERTS, LANES), F32)],
        compiler_params=pltpu.CompilerParams(
            dimension_semantics=("arbitrary",), vmem_limit_bytes=VMEM_LIMIT),
        name="outproj_router",
    )(x2, y_pool, y_attn, w_out, g, w_router_t, b_router_t)


def _sc_mesh():
    return plsc.VectorSubcoreMesh(core_axis_name="core", subcore_axis_name="subcore")


def _sc_dispatch(h2, pos_rows, p_max):
    t = h2.shape[0]

    @functools.partial(
        pl.kernel, mesh=_sc_mesh(),
        out_type=[jax.ShapeDtypeStruct((p_max, D_PART), h2.dtype)] * N_PART)
    def run(h_hbm, *refs):
        pos_hbm, xs_hbm = refs[:TOP_K], refs[TOP_K:]
        for c in range(N_PART):
            def body(x_vmem, *idx_vmem, dst=xs_hbm[c]):
                for iv in idx_vmem:
                    pltpu.sync_copy(x_vmem, dst.at[iv.at[0]])

            pltpu.emit_pipeline(
                body,
                grid=(t // SC_WINDOW,),
                in_specs=[pl.BlockSpec((SC_WINDOW, D_PART), lambda i, c=c: (i, c))]
                + [pl.BlockSpec((1, SC_WINDOW), lambda i: (0, i))] * TOP_K,
                out_specs=[],
                core_axis_name=("core", "subcore"),
                dimension_semantics=(pltpu.PARALLEL,),
            )(h_hbm, *pos_hbm)

    return run(h2, *pos_rows)


def _sc_unpermute(y_parts, idx_row):
    n = idx_row.shape[1]

    @functools.partial(
        pl.kernel, mesh=_sc_mesh(),
        out_type=[jax.ShapeDtypeStruct((n, D_PART), y_parts[0].dtype)] * N_PART)
    def run(*refs):
        y_hbm, i_hbm, o_hbm = refs[:N_PART], refs[N_PART], refs[N_PART + 1:]
        for c in range(N_PART):
            def body(i_vmem, o_vmem, src=y_hbm[c]):
                pltpu.sync_copy(src.at[i_vmem.at[0]], o_vmem)

            pltpu.emit_pipeline(
                body,
                grid=(n // SC_WINDOW,),
                in_specs=[pl.BlockSpec((1, SC_WINDOW), lambda i: (0, i))],
                out_specs=[pl.BlockSpec((SC_WINDOW, D_PART), lambda i: (i, 0))],
                core_axis_name=("core", "subcore"),
                dimension_semantics=(pltpu.PARALLEL,),
            )(i_hbm, o_hbm[c])

    return run(*y_parts, idx_row)


def _moe_kernel(te_ref, nv_ref, rows_ref, nxt_ref, *refs):
    xs_refs = refs[:N_PART]
    bgu_ref, bd_ref, wgu_hbm, wd_hbm = refs[N_PART:N_PART + 4]
    y_refs = refs[N_PART + 4:2 * N_PART + 4]
    wgu_f32, wd_f32, wgu_bf, wd_bf, sem = refs[2 * N_PART + 4:]
    i = pl.program_id(0)

    def weight_copies(e):
        return (pltpu.make_async_copy(wgu_hbm.at[e], wgu_f32, sem.at[0]),
                pltpu.make_async_copy(wd_hbm.at[e], wd_f32, sem.at[1]))

    @pl.when(i < nv_ref[0])
    def _():
        prev = te_ref[jnp.maximum(i - 1, 0)]
        new_expert = jnp.logical_or(i == 0, te_ref[i] != prev)

        @pl.when(i == 0)
        def _():
            for cp in weight_copies(te_ref[0]):
                cp.start()

        @pl.when(new_expert)
        def _():
            for cp in weight_copies(te_ref[i]):
                cp.wait()
            wgu_bf[...] = wgu_f32[...].astype(BF16)
            wd_bf[...] = wd_f32[...].astype(BF16)

            @pl.when(nxt_ref[i] >= 0)
            def _():
                for cp in weight_copies(nxt_ref[i]):
                    cp.start()

        d_e = wd_bf.shape[0]

        def ffn(row0, n_rows):
            rows = pl.ds(row0, n_rows)
            x = _unpack_row(jnp.concatenate([r[rows, :] for r in xs_refs], axis=1))
            rid = row0 + lax.broadcasted_iota(I32, (n_rows, 1), 0)
            x = jnp.where(rid < rows_ref[i], x, 0.0).astype(BF16)
            gu = jnp.dot(x, wgu_bf[...], preferred_element_type=F32) + bgu_ref[0]
            gate = jnp.minimum(gu[:, :d_e], SWIGLU_LIMIT)
            lin = jnp.clip(gu[:, d_e:], -SWIGLU_LIMIT, SWIGLU_LIMIT)
            act = gate * jax.nn.sigmoid(SWIGLU_ALPHA * gate) * (lin + 1.0)
            y = jnp.dot(act.astype(BF16), wd_bf[...], preferred_element_type=F32) + bd_ref[0]
            yp = _pack_row(y)
            for c, y_ref in enumerate(y_refs):
                y_ref[rows, :] = yp[:, c * D_PART:(c + 1) * D_PART]

        full = rows_ref[i] > MOE_TILE - MOE_PIECE

        @pl.when(full)
        def _():
            ffn(0, MOE_TILE)

        @pl.when(jnp.logical_not(full))
        def _():
            def piece(j, carry):
                ffn(pl.multiple_of(j * MOE_PIECE, MOE_PIECE), MOE_PIECE)
                return carry

            lax.fori_loop(0, (rows_ref[i] + MOE_PIECE - 1) // MOE_PIECE, piece, 0)


def _moe(tile_expert, n_valid, tile_rows, next_expert, xs_parts, w_gu, b_gu, w_down, b_down):
    p_max = xs_parts[0].shape[0]
    n_tiles = p_max // MOE_TILE
    d_e = w_down.shape[1]

    def row(i, te, nv, tr, nx):
        return (jnp.minimum(i, nv[0] - 1), 0)

    def expert(i, te, nv, tr, nx):
        return (te[jnp.minimum(i, nv[0] - 1)], 0, 0)

    grid_spec = pltpu.PrefetchScalarGridSpec(
        num_scalar_prefetch=4,
        grid=(n_tiles,),
        in_specs=[pl.BlockSpec((MOE_TILE, D_PART), row)] * N_PART + [
            pl.BlockSpec((1, 1, 2 * d_e), expert),
            pl.BlockSpec((1, 1, D_MODEL), expert),
            pl.BlockSpec(memory_space=pl.ANY),
            pl.BlockSpec(memory_space=pl.ANY),
        ],
        out_specs=[pl.BlockSpec((MOE_TILE, D_PART), row)] * N_PART,
        scratch_shapes=[pltpu.VMEM((D_MODEL, 2 * d_e), F32), pltpu.VMEM((d_e, D_MODEL), F32),
                        pltpu.VMEM((D_MODEL, 2 * d_e), BF16), pltpu.VMEM((d_e, D_MODEL), BF16),
                        pltpu.SemaphoreType.DMA((2,))],
    )
    return pl.pallas_call(
        _moe_kernel,
        grid_spec=grid_spec,
        out_shape=[jax.ShapeDtypeStruct((p_max, D_PART), I32)] * N_PART,
        compiler_params=pltpu.CompilerParams(
            dimension_semantics=("arbitrary",), vmem_limit_bytes=VMEM_LIMIT),
        name="moe_ffn",
    )(tile_expert, n_valid, tile_rows, next_expert, *xs_parts, b_gu, b_down, w_gu, w_down)


def _combine_kernel(x1_ref, gate_ref, g_ref, *refs):
    yk_refs, o_ref = refs[:N_PART], refs[-1]
    acc = x1_ref[...]
    for k in range(TOP_K):
        y_k = _unpack_row(jnp.concatenate([r[k] for r in yk_refs], axis=1))
        acc = acc + gate_ref[:, k:k + 1] * y_k
    o_ref[...] = _rms(acc, g_ref[...])


def _combine(x1, gates, g, yk_parts):
    t, d = x1.shape
    tq = COMBINE_TILE
    row = lambda i: (i, 0)
    return pl.pallas_call(
        _combine_kernel,
        grid=(t // tq,),
        in_specs=[
            pl.BlockSpec((tq, d), row),
            pl.BlockSpec((tq, TOP_K), row),
            pl.BlockSpec((1, d), lambda i: (0, 0)),
        ] + [pl.BlockSpec((TOP_K, tq, D_PART), lambda i: (0, i, 0))] * N_PART,
        out_specs=pl.BlockSpec((tq, d), row),
        out_shape=jax.ShapeDtypeStruct((t, d), F32),
        compiler_params=pltpu.CompilerParams(
            dimension_semantics=("arbitrary",), vmem_limit_bytes=VMEM_LIMIT),
        name="combine_final",
    )(x1, gates, g, *yk_parts)


def kernel(x, g_mix, w_in, w_pool, pool_scale, w_out, g_ffn, w_router, b_router,
           w_gu, b_gu, w_down, b_down, g_final):
    batch, seq, d = x.shape
    t = batch * seq
    assert d == D_MODEL and seq == DIL_FAR * QB, (x.shape, "unsupported shape")
    assert seq % ROW_TILE == 0 and t % COMBINE_TILE == 0 and t % SC_WINDOW == 0
    assert w_in.shape[0] == 1, "one layer"
    x2 = x.reshape(t, d)

    y_pool, qkv = _inproj(x2, g_mix[0].reshape(1, d), w_in[0], w_pool[0],
                          pool_scale[0].reshape(1, D_POOL), seq)
    y_attn = _attention(qkv, batch, seq)

    wr_t = w_router[0].T
    br_t = jnp.broadcast_to(b_router[0].reshape(N_EXPERTS, 1), (N_EXPERTS, LANES))
    x1, h2, meta, cnt = _outproj(x2, y_pool, y_attn, w_out[0], g_ffn[0].reshape(1, d), wr_t, br_t)

    eidx = meta[META_EIDX:META_EIDX + TOP_K].astype(I32)
    rank = meta[META_RANK:META_RANK + TOP_K].astype(I32)
    gates = meta[META_GATE:META_GATE + TOP_K].T
    counts = cnt[:, 0].astype(I32)
    padded = ((counts + MOE_TILE - 1) // MOE_TILE) * MOE_TILE
    ends = jnp.cumsum(padded)
    offsets = ends - padded
    e_ids = jnp.arange(N_EXPERTS, dtype=I32)
    hit = eidx[None] == e_ids[:, None, None]
    pos = rank + jnp.sum(jnp.where(hit, offsets[:, None, None], 0), axis=0)
    p_max = t * TOP_K + N_EXPERTS * MOE_TILE
    n_tiles = p_max // MOE_TILE
    tile_start = jnp.arange(n_tiles, dtype=I32) * MOE_TILE
    tile_expert = jnp.minimum(
        jnp.sum((tile_start[:, None] >= ends[None, :]).astype(I32), axis=1), N_EXPERTS - 1)
    mine = tile_expert[:, None] == e_ids[None, :]
    group_end = jnp.sum(jnp.where(mine, (offsets + counts)[None, :], 0), axis=1)
    tile_rows = jnp.clip(group_end - tile_start, 0, MOE_TILE)
    n_valid = (ends[-1] // MOE_TILE).reshape(1).astype(I32)
    later = (e_ids[None, :] > e_ids[:, None]) & (counts[None, :] > 0)
    next_nonempty = jnp.min(jnp.where(later, e_ids[None, :], N_EXPERTS), axis=1)
    next_nonempty = jnp.where(next_nonempty < N_EXPERTS, next_nonempty, -1)
    next_expert = jnp.sum(jnp.where(mine, next_nonempty[None, :], 0), axis=1)

    xs = _sc_dispatch(h2, [pos[k:k + 1] for k in range(TOP_K)], p_max)
    ys = _moe(tile_expert, n_valid, tile_rows, next_expert, xs, w_gu[0],
              b_gu[0].reshape(N_EXPERTS, 1, -1), w_down[0], b_down[0].reshape(N_EXPERTS, 1, -1))
    yk = _sc_unpermute(ys, pos.reshape(1, TOP_K * t))
    yk = [q.reshape(TOP_K, t, D_PART) for q in yk]
    out = _combine(x1, gates, g_final.reshape(1, d), yk)
    return out.reshape(batch, seq, d)
```

```python
import functools

import jax
import jax.numpy as jnp
from jax import lax
from jax.experimental import pallas as pl
from jax.experimental.pallas import tpu as pltpu
from jax.experimental.pallas import tpu_sc as plsc

F32 = jnp.float32
BF16 = jnp.bfloat16
I32 = jnp.int32

D_MODEL = 1024
D_POOL = 512
D_ATTN = 512
POOL_WINDOWS = (2, 4, 8, 16)
POOL_GROUP = 128
HEAD_DIM = 64
N_EXPERTS = 32
TOP_K = 4
SWIGLU_LIMIT = 7.0
SWIGLU_ALPHA = 1.702
EPS = 1e-5
NEG_INF = -1e30
LOG2_E = 1.4426950408889634

LANES = 128
QB = 128
DIL_MID = 4
DIL_FAR = 16
FAR_BATCH = 2
ROW_TILE = 1024
MOE_TILE = 1024
MOE_PIECE = 256
COMBINE_TILE = 1024
SC_WINDOW = 128
D_PACKED = D_MODEL // 2
N_PART = 2
D_PART = D_PACKED // N_PART
VMEM_LIMIT = 56 * 1024 * 1024


def _pack_row(x):
    hi = lax.bitcast_convert_type(x[:, :D_PACKED].astype(BF16).astype(F32), I32)
    lo = lax.bitcast_convert_type(x[:, D_PACKED:].astype(BF16).astype(F32), I32)
    return hi | lax.shift_right_logical(lo, 16)


def _unpack_row(w):
    hi = lax.bitcast_convert_type(w & jnp.int32(-65536), F32)
    lo = lax.bitcast_convert_type(lax.shift_left(w, 16), F32)
    return jnp.concatenate([hi, lo], axis=1)


def _rms(x, g):
    ms = jnp.mean(x * x, axis=-1, keepdims=True)
    return x * lax.rsqrt(ms + EPS) * g


POOL_PAD = 16


def _inproj_kernel(x_ref, g_ref, w_ref, wp_ref, sc_ref, yp_ref, qkv_ref, w_bf, hist, pad_ref,
                   *, tiles_per_seq):
    i = pl.program_id(0)
    tm = x_ref.shape[0]

    @pl.when(i == 0)
    def _():
        w_bf[...] = w_ref[...].astype(BF16)
        pad_ref[0:POOL_PAD, :] = jnp.zeros((POOL_PAD, POOL_GROUP), F32)

    @pl.when(i % tiles_per_seq == 0)
    def _():
        hist[...] = jnp.zeros_like(hist)

    h = _rms(x_ref[...], g_ref[...]).astype(BF16)
    proj = jnp.dot(h, w_bf[...], preferred_element_type=F32)
    q = proj[:, D_POOL:D_POOL + D_ATTN] * (HEAD_DIM ** -0.5 * LOG2_E)
    qkv_ref[:, :D_ATTN] = q.astype(BF16)
    qkv_ref[:, D_ATTN:] = proj[:, D_POOL + D_ATTN:].astype(BF16)

    pos = (i % tiles_per_seq) * tm + lax.broadcasted_iota(I32, (tm, 1), 0)
    for g, w in enumerate(POOL_WINDOWS):
        lo, hi = g * POOL_GROUP, (g + 1) * POOL_GROUP
        e = proj[:, lo:hi]
        acc = jnp.concatenate([hist[g], e], axis=0)
        hist[g] = e[tm - POOL_PAD:, :]
        span = 1
        while span < w:
            pad_ref[POOL_PAD:, :] = acc
            acc = acc + pad_ref[pl.ds(POOL_PAD - span, POOL_PAD + tm), :]
            span *= 2
        count = jnp.minimum(pos + 1, w).astype(F32)
        pooled = acc[POOL_PAD:, :] / count - e
        y = jnp.dot(pooled.astype(BF16), wp_ref[g].astype(BF16), preferred_element_type=F32)
        yp_ref[:, lo:hi] = (y * sc_ref[:, lo:hi]).astype(BF16)


def _inproj(x2, g, w, w_pool, pool_scale, seq):
    t = x2.shape[0]
    d_in = w.shape[1]
    n_grp = len(POOL_WINDOWS)
    return pl.pallas_call(
        functools.partial(_inproj_kernel, tiles_per_seq=seq // ROW_TILE),
        grid=(t // ROW_TILE,),
        in_specs=[
            pl.BlockSpec((ROW_TILE, D_MODEL), lambda i: (i, 0)),
            pl.BlockSpec((1, D_MODEL), lambda i: (0, 0)),
            pl.BlockSpec((D_MODEL, d_in), lambda i: (0, 0)),
            pl.BlockSpec((n_grp, POOL_GROUP, POOL_GROUP), lambda i: (0, 0, 0)),
            pl.BlockSpec((1, D_POOL), lambda i: (0, 0)),
        ],
        out_specs=[
            pl.BlockSpec((ROW_TILE, D_POOL), lambda i: (i, 0)),
            pl.BlockSpec((ROW_TILE, 3 * D_ATTN), lambda i: (i, 0)),
        ],
        out_shape=[
            jax.ShapeDtypeStruct((t, D_POOL), BF16),
            jax.ShapeDtypeStruct((t, 3 * D_ATTN), BF16),
        ],
        scratch_shapes=[pltpu.VMEM((D_MODEL, d_in), BF16),
                        pltpu.VMEM((n_grp, POOL_PAD, POOL_GROUP), F32),
                        pltpu.VMEM((2 * POOL_PAD + ROW_TILE, POOL_GROUP), F32)],
        compiler_params=pltpu.CompilerParams(
            dimension_semantics=("arbitrary",), vmem_limit_bytes=VMEM_LIMIT),
        name="inproj_pool",
    )(x2, g, w, w_pool, pool_scale)


def _pair_block(q, k, v_ext, mask2, lane_h0):
    nq = q.shape[0]
    zero = jnp.zeros_like(q)
    q2 = jnp.concatenate([jnp.where(lane_h0, q, zero), jnp.where(lane_h0, zero, q)], axis=0)
    s = lax.dot_general(q2, k, (((1,), (1,)), ((), ())), preferred_element_type=F32)
    s = jnp.where(mask2, s, NEG_INF)
    m = jnp.max(s, axis=-1, keepdims=True)
    p = jnp.exp2(s - m).astype(BF16)
    ol = jnp.dot(p, v_ext, preferred_element_type=F32)
    o = jnp.where(lane_h0, ol[:nq, :LANES], ol[nq:, :LANES])
    l = jnp.where(lane_h0, ol[:nq, LANES:], ol[nq:, LANES:])
    mb = jnp.where(lane_h0, m[:nq], m[nq:])
    return o, mb, l


def _merge(o_a, m_a, l_a, o_b, m_b, l_b):
    m = jnp.maximum(m_a, m_b)
    ea = jnp.exp2(m_a - m)
    eb = jnp.exp2(m_b - m)
    return o_a * ea + o_b * eb, m, l_a * ea + l_b * eb


def _both_heads(mask):
    return jnp.concatenate([mask, mask], axis=0)


def _attn_kernel(q_ref, k_ref, v_ref, o_ref, qp, kp, vpx, vnx, o23, m23, l23, o2, m2, l2,
                 on, mn):
    s = q_ref.shape[0]
    n_chunk = DIL_FAR
    rows = s // n_chunk
    sub = rows // DIL_MID
    grp = n_chunk * n_chunk
    lane_h0 = lax.broadcasted_iota(I32, (1, LANES), 1) < HEAD_DIM

    @pl.when((pl.program_id(0) == 0) & (pl.program_id(1) == 0))
    def _():
        ones = jnp.ones((s, LANES), BF16)
        vnx[:, LANES:] = ones
        vpx[:, LANES:] = ones

    vnx[:, :LANES] = v_ref[...]

    pi = lax.broadcasted_iota(I32, (grp, grp), 0)
    pj = lax.broadcasted_iota(I32, (grp, grp), 1)
    swap = (pj == (pi % n_chunk) * n_chunk + pi // n_chunk).astype(BF16)
    n_grp = s // grp
    wide = jnp.concatenate([ref[g * grp:(g + 1) * grp, :]
                            for ref in (q_ref, k_ref, v_ref) for g in range(n_grp)], axis=1)
    moved = jnp.dot(swap, wide, preferred_element_type=F32).astype(BF16)
    for which, dst_ref in enumerate((qp, kp, vpx)):
        for g in range(n_grp):
            col = (which * n_grp + g) * LANES
            for r in range(n_chunk):
                dst = slice(r * rows + g * n_chunk, r * rows + (g + 1) * n_chunk)
                dst_ref[dst, 0:LANES] = moved[r * n_chunk:(r + 1) * n_chunk, col:col + LANES]

    qi = lax.broadcasted_iota(I32, (QB, QB), 0)
    kj = lax.broadcasted_iota(I32, (QB, QB), 1)
    causal = _both_heads(qi >= kj)

    def mid_index(n_key_sub):
        i_q = lax.broadcasted_iota(I32, (QB, DIL_MID * n_key_sub), 0)
        i_k = lax.broadcasted_iota(I32, (QB, DIL_MID * n_key_sub), 1)
        return i_q // sub, i_q % sub, i_k // n_key_sub, i_k % n_key_sub

    jq, aq, jk, ak = mid_index(sub)
    d0 = DIL_FAR * (aq - ak) + DIL_MID * (jq - jk)
    mask_mid0 = _both_heads(d0 >= 0)
    jq, aq, jk, ak = mid_index(2 * sub)
    d1 = DIL_FAR * (aq + sub - ak) + DIL_MID * (jq - jk)
    mask_mid = _both_heads((d1 >= 0) & (d1 <= DIL_MID * QB))

    def mid_tile(r4, a0, k0, nk, mask):
        def at(j, off, n):
            return pl.ds((DIL_MID * j + r4) * rows + off, n)

        q = jnp.concatenate([qp[at(j, a0, sub), :] for j in range(DIL_MID)], axis=0)
        k = jnp.concatenate([kp[at(j, k0, nk), :] for j in range(DIL_MID)], axis=0)
        v = jnp.concatenate([vpx[at(j, k0, nk), :] for j in range(DIL_MID)], axis=0)
        o, m, l = _pair_block(q, k, v, mask, lane_h0)
        for j in range(DIL_MID):
            dst = at(j, a0, sub)
            src = slice(j * sub, (j + 1) * sub)
            o2[dst, :] = o[src]
            m2[dst, :] = m[src]
            l2[dst, :] = l[src]

    fq = lax.broadcasted_iota(I32, (FAR_BATCH * rows, FAR_BATCH * rows), 0)
    fk = lax.broadcasted_iota(I32, (FAR_BATCH * rows, FAR_BATCH * rows), 1)
    far_mask = _both_heads((fq // rows == fk // rows) & (fq >= fk))
    for r in range(0, n_chunk, FAR_BATCH):
        blk = slice(r * rows, (r + FAR_BATCH) * rows)
        o, m, l = _pair_block(qp[blk, :], kp[blk, :], vpx[blk, :], far_mask, lane_h0)
        o23[blk, :] = o
        m23[blk, :] = m
        l23[blk, :] = l

    for r4 in range(DIL_MID):
        mid_tile(r4, 0, 0, sub, mask_mid0)
        for a_blk in range(1, rows // sub):
            mid_tile(r4, a_blk * sub, (a_blk - 1) * sub, 2 * sub, mask_mid)

    for r in range(n_chunk):
        blk = slice(r * rows, (r + 1) * rows)
        oo, mm, ll = _merge(o23[blk, :], m23[blk, :], l23[blk, :],
                            o2[blk, :], m2[blk, :], l2[blk, :])
        o23[blk, :] = oo / ll
        m23[blk, :] = mm + jnp.log2(ll)

    back = jnp.concatenate(
        [jnp.concatenate([o23[r * rows + g * n_chunk:r * rows + (g + 1) * n_chunk, :].astype(BF16)
                          for r in range(n_chunk)], axis=0) for g in range(n_grp)], axis=1)
    restored = jnp.dot(swap, back, preferred_element_type=F32)
    for g in range(n_grp):
        on[g * grp:(g + 1) * grp, :] = restored[:, g * LANES:(g + 1) * LANES]
    for r in range(n_chunk):
        mn[pl.ds(r, rows, stride=n_chunk), :] = m23[r * rows:(r + 1) * rows, :]

    qi2 = lax.broadcasted_iota(I32, (QB, 2 * QB), 0)
    kj2 = lax.broadcasted_iota(I32, (QB, 2 * QB), 1)
    dn = qi2 + QB - kj2
    mask_near = _both_heads((dn >= 0) & (dn <= QB))

    def near_finish(dst, o, m, l):
        oo, _, ll = _merge(on[dst, :], mn[dst, :], 1.0, o, m, l)
        o_ref[dst, :] = (oo / ll).astype(o_ref.dtype)

    first = pl.ds(0, QB)
    o, m, l = _pair_block(q_ref[first, :], k_ref[first, :], vnx[first, :], causal, lane_h0)
    near_finish(first, o, m, l)
    for n in range(1, s // QB):
        keys = pl.ds((n - 1) * QB, 2 * QB)
        o, m, l = _pair_block(q_ref[pl.ds(n * QB, QB), :], k_ref[keys, :], vnx[keys, :],
                              mask_near, lane_h0)
        near_finish(pl.ds(n * QB, QB), o, m, l)


def _attention(qkv, batch, seq):
    n_pair = D_ATTN // LANES
    blk = (seq, LANES)
    f32_scr = pltpu.VMEM(blk, F32)
    bf_scr = pltpu.VMEM(blk, BF16)
    bfx_scr = pltpu.VMEM((seq, 2 * LANES), BF16)
    return pl.pallas_call(
        _attn_kernel,
        grid=(batch, n_pair),
        in_specs=[
            pl.BlockSpec(blk, lambda b, h: (b, h)),
            pl.BlockSpec(blk, lambda b, h: (b, n_pair + h)),
            pl.BlockSpec(blk, lambda b, h: (b, 2 * n_pair + h)),
        ],
        out_specs=pl.BlockSpec(blk, lambda b, h: (b, h)),
        out_shape=jax.ShapeDtypeStruct((batch * seq, D_ATTN), BF16),
        scratch_shapes=[bf_scr, bf_scr, bfx_scr, bfx_scr] + [f32_scr] * 8,
        compiler_params=pltpu.CompilerParams(
            dimension_semantics=("arbitrary", "arbitrary"), vmem_limit_bytes=VMEM_LIMIT),
        name="dilated_attn",
    )(qkv, qkv, qkv)


META_ROWS = 16
META_EIDX, META_GATE, META_RANK = 0, 4, 8


def _outproj_kernel(x_ref, yp_ref, ya_ref, wo_ref, g_ref, wrt_ref, brt_ref,
                    x1_ref, h2_ref, meta_ref, cnt_ref, wo_bf, before, carry):
    tm = x_ref.shape[0]

    @pl.when(pl.program_id(0) == 0)
    def _():
        wo_bf[...] = wo_ref[...].astype(BF16)
        carry[...] = jnp.zeros_like(carry)
        ti = lax.broadcasted_iota(I32, (tm, tm), 0)
        tj = lax.broadcasted_iota(I32, (tm, tm), 1)
        before[...] = (ti < tj).astype(BF16)

    x1 = (x_ref[...]
          + jnp.dot(yp_ref[...], wo_bf[:D_POOL, :], preferred_element_type=F32)
          + jnp.dot(ya_ref[...], wo_bf[D_POOL:, :], preferred_element_type=F32))
    x1_ref[...] = x1
    h2 = _rms(x1, g_ref[...])
    h2_ref[...] = _pack_row(h2)

    logits_t = lax.dot_general(wrt_ref[...].astype(BF16), h2.astype(BF16),
                               (((1,), (1,)), ((), ())), preferred_element_type=F32)
    logits_t = logits_t + brt_ref[:, 0:1]
    eid = lax.broadcasted_iota(I32, (N_EXPERTS, tm), 0)
    work = logits_t
    idxs, vals = [], []
    for _ in range(TOP_K):
        mx = jnp.max(work, axis=0, keepdims=True)
        idx = jnp.min(jnp.where(work == mx, eid, N_EXPERTS), axis=0, keepdims=True)
        idxs.append(idx)
        vals.append(mx)
        work = jnp.where(eid == idx, -jnp.inf, work)
    exps = [jnp.exp(v - vals[0]) for v in vals]
    den = exps[0] + exps[1] + exps[2] + exps[3]

    onehot = jnp.zeros((N_EXPERTS, tm), F32)
    for idx in idxs:
        onehot = onehot + (eid == idx).astype(F32)
    rank_e = carry[:, 0:1] + jnp.dot(onehot.astype(BF16), before[...],
                                     preferred_element_type=F32)
    carry[...] = carry[...] + jnp.sum(onehot, axis=1, keepdims=True)
    cnt_ref[...] = carry[...]

    mrow = lax.broadcasted_iota(I32, (META_ROWS, tm), 0)
    meta = jnp.zeros((META_ROWS, tm), F32)
    for k in range(TOP_K):
        rank_k = jnp.sum(jnp.where(eid == idxs[k], rank_e, 0.0), axis=0, keepdims=True)
        meta = jnp.where(mrow == META_EIDX + k, idxs[k].astype(F32), meta)
        meta = jnp.where(mrow == META_GATE + k, exps[k] / den, meta)
        meta = jnp.where(mrow == META_RANK + k, rank_k, meta)
    meta_ref[...] = meta


def _outproj(x2, y_pool, y_attn, w_out, g, w_router_t, b_router_t):
    t = x2.shape[0]
    row = lambda i: (i, 0)
    const = lambda i: (0, 0)
    return pl.pallas_call(
        _outproj_kernel,
        grid=(t // ROW_TILE,),
        in_specs=[
            pl.BlockSpec((ROW_TILE, D_MODEL), row),
            pl.BlockSpec((ROW_TILE, D_POOL), row),
            pl.BlockSpec((ROW_TILE, D_ATTN), row),
            pl.BlockSpec((D_MODEL, D_MODEL), const),
            pl.BlockSpec((1, D_MODEL), const),
            pl.BlockSpec((N_EXPERTS, D_MODEL), const),
            pl.BlockSpec((N_EXPERTS, LANES), const),
        ],
        out_specs=[
            pl.BlockSpec((ROW_TILE, D_MODEL), row),
            pl.BlockSpec((ROW_TILE, D_PACKED), row),
            pl.BlockSpec((META_ROWS, ROW_TILE), lambda i: (0, i)),
            pl.BlockSpec((N_EXPERTS, LANES), const),
        ],
        out_shape=[
            jax.ShapeDtypeStruct((t, D_MODEL), F32),
            jax.ShapeDtypeStruct((t, D_PACKED), I32),
            jax.ShapeDtypeStruct((META_ROWS, t), F32),
            jax.ShapeDtypeStruct((N_EXPERTS, LANES), F32),
        ],
        scratch_shapes=[pltpu.VMEM((D_MODEL, D_MODEL), BF16),
                        pltpu.VMEM((ROW_TILE, ROW_TILE), BF16),
                        pltpu.VMEM((N_EXPERTS, LANES), F32)],
        compiler_params=pltpu.CompilerParams(
            dimension_semantics=("arbitrary",), vmem_limit_bytes=VMEM_LIMIT),
        name="outproj_router",
    )(x2, y_pool, y_attn, w_out, g, w_router_t, b_router_t)


def _sc_mesh():
    return plsc.VectorSubcoreMesh(core_axis_name="core", subcore_axis_name="subcore")


def _sc_dispatch(h2, pos_rows, p_max):
    t = h2.shape[0]

    @functools.partial(
        pl.kernel, mesh=_sc_mesh(),
        out_type=[jax.ShapeDtypeStruct((p_max, D_PART), h2.dtype)] * N_PART)
    def run(h_hbm, *refs):
        pos_hbm, xs_hbm = refs[:TOP_K], refs[TOP_K:]
        for c in range(N_PART):
            def body(x_vmem, *idx_vmem, dst=xs_hbm[c]):
                for iv in idx_vmem:
                    pltpu.sync_copy(x_vmem, dst.at[iv.at[0]])

            pltpu.emit_pipeline(
                body,
                grid=(t // SC_WINDOW,),
                in_specs=[pl.BlockSpec((SC_WINDOW, D_PART), lambda i, c=c: (i, c))]
                + [pl.BlockSpec((1, SC_WINDOW), lambda i: (0, i))] * TOP_K,
                out_specs=[],
                core_axis_name=("core", "subcore"),
                dimension_semantics=(pltpu.PARALLEL,),
            )(h_hbm, *pos_hbm)

    return run(h2, *pos_rows)


def _sc_unpermute(y_parts, idx_row):
    n = idx_row.shape[1]

    @functools.partial(
        pl.kernel, mesh=_sc_mesh(),
        out_type=[jax.ShapeDtypeStruct((n, D_PART), y_parts[0].dtype)] * N_PART)
    def run(*refs):
        y_hbm, i_hbm, o_hbm = refs[:N_PART], refs[N_PART], refs[N_PART + 1:]
        for c in range(N_PART):
            def body(i_vmem, o_vmem, src=y_hbm[c]):
                pltpu.sync_copy(src.at[i_vmem.at[0]], o_vmem)

            pltpu.emit_pipeline(
                body,
                grid=(n // SC_WINDOW,),
                in_specs=[pl.BlockSpec((1, SC_WINDOW), lambda i: (0, i))],
                out_specs=[pl.BlockSpec((SC_WINDOW, D_PART), lambda i: (i, 0))],
                core_axis_name=("core", "subcore"),
                dimension_semantics=(pltpu.PARALLEL,),
            )(i_hbm, o_hbm[c])

    return run(*y_parts, idx_row)


def _moe_kernel(te_ref, nv_ref, rows_ref, nxt_ref, *refs):
    xs_refs = refs[:N_PART]
    bgu_ref, bd_ref, wgu_hbm, wd_hbm = refs[N_PART:N_PART + 4]
    y_refs = refs[N_PART + 4:2 * N_PART + 4]
    wgu_f32, wd_f32, wgu_bf, wd_bf, sem = refs[2 * N_PART + 4:]
    i = pl.program_id(0)

    def weight_copies(e):
        return (pltpu.make_async_copy(wgu_hbm.at[e], wgu_f32, sem.at[0]),
                pltpu.make_async_copy(wd_hbm.at[e], wd_f32, sem.at[1]))

    @pl.when(i < nv_ref[0])
    def _():
        prev = te_ref[jnp.maximum(i - 1, 0)]
        new_expert = jnp.logical_or(i == 0, te_ref[i] != prev)

        @pl.when(i == 0)
        def _():
            for cp in weight_copies(te_ref[0]):
                cp.start()

        @pl.when(new_expert)
        def _():
            for cp in weight_copies(te_ref[i]):
                cp.wait()
            wgu_bf[...] = wgu_f32[...].astype(BF16)
            wd_bf[...] = wd_f32[...].astype(BF16)

            @pl.when(nxt_ref[i] >= 0)
            def _():
                for cp in weight_copies(nxt_ref[i]):
                    cp.start()

        d_e = wd_bf.shape[0]

        def ffn(row0, n_rows):
            rows = pl.ds(row0, n_rows)
            x = _unpack_row(jnp.concatenate([r[rows, :] for r in xs_refs], axis=1))
            rid = row0 + lax.broadcasted_iota(I32, (n_rows, 1), 0)
            x = jnp.where(rid < rows_ref[i], x, 0.0).astype(BF16)
            gu = jnp.dot(x, wgu_bf[...], preferred_element_type=F32) + bgu_ref[0]
            gate = jnp.minimum(gu[:, :d_e], SWIGLU_LIMIT)
            lin = jnp.clip(gu[:, d_e:], -SWIGLU_LIMIT, SWIGLU_LIMIT)
            act = gate * jax.nn.sigmoid(SWIGLU_ALPHA * gate) * (lin + 1.0)
            y = jnp.dot(act.astype(BF16), wd_bf[...], preferred_element_type=F32) + bd_ref[0]
            yp = _pack_row(y)
            for c, y_ref in enumerate(y_refs):
                y_ref[rows, :] = yp[:, c * D_PART:(c + 1) * D_PART]

        full = rows_ref[i] > MOE_TILE - MOE_PIECE

        @pl.when(full)
        def _():
            ffn(0, MOE_TILE)

        @pl.when(jnp.logical_not(full))
        def _():
            for y_ref in y_refs:
                y_ref[...] = jnp.zeros_like(y_ref)

            def piece(j, carry):
                ffn(pl.multiple_of(j * MOE_PIECE, MOE_PIECE), MOE_PIECE)
                return carry

            lax.fori_loop(0, (rows_ref[i] + MOE_PIECE - 1) // MOE_PIECE, piece, 0)


def _moe(tile_expert, n_valid, tile_rows, next_expert, xs_parts, w_gu, b_gu, w_down, b_down):
    p_max = xs_parts[0].shape[0]
    n_tiles = p_max // MOE_TILE
    d_e = w_down.shape[1]

    def row(i, te, nv, tr, nx):
        return (jnp.minimum(i, nv[0] - 1), 0)

    def expert(i, te, nv, tr, nx):
        return (te[jnp.minimum(i, nv[0] - 1)], 0, 0)

    grid_spec = pltpu.PrefetchScalarGridSpec(
        num_scalar_prefetch=4,
        grid=(n_tiles,),
        in_specs=[pl.BlockSpec((MOE_TILE, D_PART), row)] * N_PART + [
            pl.BlockSpec((1, 1, 2 * d_e), expert),
            pl.BlockSpec((1, 1, D_MODEL), expert),
            pl.BlockSpec(memory_space=pl.ANY),
            pl.BlockSpec(memory_space=pl.ANY),
        ],
        out_specs=[pl.BlockSpec((MOE_TILE, D_PART), row)] * N_PART,
        scratch_shapes=[pltpu.VMEM((D_MODEL, 2 * d_e), F32), pltpu.VMEM((d_e, D_MODEL), F32),
                        pltpu.VMEM((D_MODEL, 2 * d_e), BF16), pltpu.VMEM((d_e, D_MODEL), BF16),
                        pltpu.SemaphoreType.DMA((2,))],
    )
    return pl.pallas_call(
        _moe_kernel,
        grid_spec=grid_spec,
        out_shape=[jax.ShapeDtypeStruct((p_max, D_PART), I32)] * N_PART,
        compiler_params=pltpu.CompilerParams(
            dimension_semantics=("arbitrary",), vmem_limit_bytes=VMEM_LIMIT),
        name="moe_ffn",
    )(tile_expert, n_valid, tile_rows, next_expert, *xs_parts, b_gu, b_down, w_gu, w_down)


def _combine_kernel(x1_ref, meta_ref, g_ref, *refs):
    yk_refs, o_ref = refs[:N_PART], refs[-1]
    acc = x1_ref[...]
    gates = meta_ref[...].T
    for k in range(TOP_K):
        y_k = _unpack_row(jnp.concatenate([r[k] for r in yk_refs], axis=1))
        acc = acc + gates[:, META_GATE + k:META_GATE + k + 1] * y_k
    o_ref[...] = _rms(acc, g_ref[...])


def _combine(x1, meta, g, yk_parts):
    t, d = x1.shape
    tq = COMBINE_TILE
    row = lambda i: (i, 0)
    return pl.pallas_call(
        _combine_kernel,
        grid=(t // tq,),
        in_specs=[
            pl.BlockSpec((tq, d), row),
            pl.BlockSpec((META_ROWS, tq), lambda i: (0, i)),
            pl.BlockSpec((1, d), lambda i: (0, 0)),
        ] + [pl.BlockSpec((TOP_K, tq, D_PART), lambda i: (0, i, 0))] * N_PART,
        out_specs=pl.BlockSpec((tq, d), row),
        out_shape=jax.ShapeDtypeStruct((t, d), F32),
        compiler_params=pltpu.CompilerParams(
            dimension_semantics=("arbitrary",), vmem_limit_bytes=VMEM_LIMIT),
        name="combine_final",
    )(x1, meta, g, *yk_parts)


def kernel(x, g_mix, w_in, w_pool, pool_scale, w_out, g_ffn, w_router, b_router,
           w_gu, b_gu, w_down, b_down, g_final):
    batch, seq, d = x.shape
    t = batch * seq
    assert d == D_MODEL and seq == DIL_FAR * QB, (x.shape, "unsupported shape")
    assert seq % ROW_TILE == 0 and t % COMBINE_TILE == 0 and t % SC_WINDOW == 0
    assert w_in.shape[0] == 1, "one layer"
    x2 = x.reshape(t, d)

    y_pool, qkv = _inproj(x2, g_mix[0].reshape(1, d), w_in[0], w_pool[0],
                          pool_scale[0].reshape(1, D_POOL), seq)
    y_attn = _attention(qkv, batch, seq)

    wr_t = w_router[0].T
    br_t = jnp.broadcast_to(b_router[0].reshape(N_EXPERTS, 1), (N_EXPERTS, LANES))
    x1, h2, meta, cnt = _outproj(x2, y_pool, y_attn, w_out[0], g_ffn[0].reshape(1, d), wr_t, br_t)

    eidx = meta[META_EIDX:META_EIDX + TOP_K].astype(I32)
    rank = meta[META_RANK:META_RANK + TOP_K].astype(I32)
    counts = cnt[:, 0].astype(I32)
    padded = ((counts + MOE_TILE - 1) // MOE_TILE) * MOE_TILE
    ends = jnp.cumsum(padded)
    offsets = ends - padded
    e_ids = jnp.arange(N_EXPERTS, dtype=I32)
    hit = eidx[None] == e_ids[:, None, None]
    pos = rank + jnp.sum(jnp.where(hit, offsets[:, None, None], 0), axis=0)
    p_max = t * TOP_K + N_EXPERTS * MOE_TILE
    n_tiles = p_max // MOE_TILE
    tile_start = jnp.arange(n_tiles, dtype=I32) * MOE_TILE
    tile_expert = jnp.minimum(
        jnp.sum((tile_start[:, None] >= ends[None, :]).astype(I32), axis=1), N_EXPERTS - 1)
    mine = tile_expert[:, None] == e_ids[None, :]
    group_end = jnp.sum(jnp.where(mine, (offsets + counts)[None, :], 0), axis=1)
    tile_rows = jnp.clip(group_end - tile_start, 0, MOE_TILE)
    n_valid = (ends[-1] // MOE_TILE).reshape(1).astype(I32)
    later = (e_ids[None, :] > e_ids[:, None]) & (counts[None, :] > 0)
    next_nonempty = jnp.min(jnp.where(later, e_ids[None, :], N_EXPERTS), axis=1)
    next_nonempty = jnp.where(next_nonempty < N_EXPERTS, next_nonempty, -1)
    next_expert = jnp.sum(jnp.where(mine, next_nonempty[None, :], 0), axis=1)

    xs = _sc_dispatch(h2, [pos[k:k + 1] for k in range(TOP_K)], p_max)
    ys = _moe(tile_expert, n_valid, tile_rows, next_expert, xs, w_gu[0],
              b_gu[0].reshape(N_EXPERTS, 1, -1), w_down[0], b_down[0].reshape(N_EXPERTS, 1, -1))
    yk = _sc_unpermute(ys, pos.reshape(1, TOP_K * t))
    yk = [q.reshape(TOP_K, t, D_PART) for q in yk]
    out = _combine(x1, meta, g_final.reshape(1, d), yk)
    return out.reshape(batch, seq, d)
```
